```python
import jax, jax.numpy as jnp
from jax import lax
import numpy as np

D_MODEL = 1024
BATCH = 32
SEQ = 256
DEPTH = 4
DEC_BATCH = 2
DEC_SEQ = 1024
PAST_LEN = 512

GRID_W = 64
N_MIXERS = 2
EPS = 1e-6
ROPE_BASE = 10000.0
RET_HEADS = 4
RET_DK = D_MODEL // RET_HEADS
RET_DV = 2 * RET_DK
RET_QK_W = RET_HEADS * RET_DK
RET_V_W = RET_HEADS * RET_DV
RET_CHUNK = 128
NA_HEADS = 16
NA_HD = D_MODEL // NA_HEADS
NA_KH = 8
NA_KW = 16
NA_QROWS = 2
ATTN_QBLOCK = 128
N_EXPERTS = 32
TOP_K = 4
D_EXPERT = D_MODEL
SWIGLU_LIMIT = 7.0
SWIGLU_ALPHA = 1.702
N_RET_LAYERS = (DEPTH + 1) // 2
N_NA_LAYERS = DEPTH // 2

kernel_name = 'hybrid_retention_natten_moe_diffusion_step'


def rmsnorm(x, g):
    xf = x.astype(jnp.float32)
    y = xf * lax.rsqrt(jnp.mean(xf * xf, axis=-1, keepdims=True) + EPS)
    return (y * g.astype(jnp.float32)).astype(x.dtype)


def adaln(cond, w, b):
    m = jax.nn.silu(cond) @ w + b
    return [t[:, None, :] for t in jnp.split(m, 6, axis=-1)]


def axial_rope(n_tokens, head_dim, dtype):
    t = jnp.arange(n_tokens)
    row = (t // GRID_W).astype(jnp.float32)
    col = (t % GRID_W).astype(jnp.float32)
    d_axis = head_dim // 2
    inv = ROPE_BASE ** (-jnp.arange(0, d_axis, 2, dtype=jnp.float32) / d_axis)
    ang = jnp.concatenate([row[:, None] * inv, col[:, None] * inv], axis=-1)
    return jnp.cos(ang).astype(dtype), jnp.sin(ang).astype(dtype)


def apply_rope(x, cos, sin):
    x2 = x.reshape(x.shape[:-1] + (x.shape[-1] // 2, 2))
    a, b = x2[..., 0], x2[..., 1]
    return jnp.stack([a * cos - b * sin, a * sin + b * cos], axis=-1).reshape(x.shape)


def retention_scan(q, k, v, log_gamma, s0):
    bsz, nh, t, _ = q.shape
    dv = v.shape[-1]
    n_chunks = t // RET_CHUNK
    dt = q.dtype
    lg = log_gamma.astype(jnp.float32)
    pos = jnp.arange(RET_CHUNK, dtype=jnp.float32)
    diff = pos[:, None] - pos[None, :]
    lower = diff >= 0
    intra = jnp.where(lower, jnp.exp(jnp.where(lower, diff, 0.0) * lg[:, None, None]), 0.0).astype(dt)
    q_dec = jnp.exp((pos + 1.0) * lg[:, None]).astype(dt)[..., None]
    k_dec = jnp.exp((RET_CHUNK - 1.0 - pos) * lg[:, None]).astype(dt)[..., None]
    c_dec = jnp.exp(RET_CHUNK * lg).astype(dt)[:, None, None]

    def chunks(a):
        return jnp.moveaxis(a.reshape(bsz, nh, n_chunks, RET_CHUNK, a.shape[-1]), 2, 0)

    def step(s, inp):
        qc, kc, vc = inp
        scores = jnp.einsum('bhqd,bhkd->bhqk', qc, kc) * intra
        o = jnp.einsum('bhqk,bhkv->bhqv', scores, vc) + jnp.einsum('bhqd,bhdv->bhqv', qc * q_dec, s)
        s = s * c_dec + jnp.einsum('bhkd,bhkv->bhdv', kc * k_dec, vc)
        return s, o

    s_final, o = lax.scan(step, s0.astype(dt), (chunks(q), chunks(k), chunks(v)))
    return jnp.moveaxis(o, 0, 2).reshape(bsz, nh, t, dv), s_final


def retention_mixer(h, w_in, w_out, lg_f, lg_b, s0_f, s0_b, rope):
    bsz, t, _ = h.shape
    q, k, v, g = jnp.split(h @ w_in, [RET_QK_W, 2 * RET_QK_W, 2 * RET_QK_W + RET_V_W], axis=-1)

    def heads(a, hd):
        return a.reshape(bsz, t, RET_HEADS, hd).transpose(0, 2, 1, 3)

    q = heads(q, RET_DK)
    k = heads(k, RET_DK) * (RET_DK ** -0.5)
    v = heads(v, RET_DV)
    if rope is not None:
        cos, sin = rope
        q = apply_rope(q, cos, sin)
        k = apply_rope(k, cos, sin)
    o_f, s_f = retention_scan(q, k, v, lg_f, s0_f)
    o_b, s_b = retention_scan(jnp.flip(q, 2), jnp.flip(k, 2), jnp.flip(v, 2), lg_b, s0_b)
    of = (o_f + jnp.flip(o_b, 2)).astype(jnp.float32)
    of = of * lax.rsqrt(jnp.mean(of * of, axis=-1, keepdims=True) + EPS)
    o = of.astype(h.dtype).transpose(0, 2, 1, 3).reshape(bsz, t, RET_V_W)
    return (jax.nn.silu(g) * o) @ w_out, s_f, s_b


def dense_ctx_attention(q, k, v):
    bsz, l, nh, d = q.shape
    qb = jnp.moveaxis(q.reshape(bsz, l // ATTN_QBLOCK, ATTN_QBLOCK, nh, d), 1, 0)

    def block(qi):
        s = jnp.einsum('bqhd,bkhd->bhqk', qi, k).astype(jnp.float32)
        p = jax.nn.softmax(s, axis=-1).astype(v.dtype)
        return jnp.einsum('bhqk,bkhd->bqhd', p, v)

    o = lax.map(block, qb)
    return jnp.moveaxis(o, 0, 1).reshape(bsz, l, nh * d)


def neighbourhood_attention(q, k, v, ck, cv, rpb):
    bsz, t, nh, d = q.shape
    rows = t // GRID_W
    kh = min(NA_KH, rows)
    n_cb = GRID_W // NA_KW
    kbw = 2 * NA_KW
    qcol = np.arange(GRID_W).reshape(n_cb, NA_KW)
    cstart = np.clip(qcol - NA_KW // 2, 0, GRID_W - NA_KW)
    kb = np.clip(np.arange(n_cb) * NA_KW - NA_KW // 2, 0, GRID_W - kbw)
    kcol = kb[:, None] + np.arange(kbw)[None, :]
    kc3 = kcol[:, None, :]
    col_valid = (kc3 >= cstart[..., None]) & (kc3 < cstart[..., None] + NA_KW)
    col_off = np.clip(kc3 - qcol[..., None], -(NA_KW - 1), NA_KW - 1) + (NA_KW - 1)
    kgrid = k.reshape(bsz, rows, GRID_W, nh, d)
    vgrid = v.reshape(bsz, rows, GRID_W, nh, d)
    qb = jnp.moveaxis(q.reshape(bsz, rows // NA_QROWS, NA_QROWS, n_cb, NA_KW, nh, d), 1, 0)
    row_ids = jnp.arange(rows, dtype=jnp.int32).reshape(rows // NA_QROWS, NA_QROWS)
    n_win = kh * kbw

    def block(args):
        qi, r = args
        rs = jnp.clip(r - kh // 2, 0, rows - kh)
        krow = rs[:, None] + jnp.arange(kh, dtype=jnp.int32)[None, :]
        kblk = jnp.take(jnp.take(kgrid, krow, axis=1), kcol, axis=3)
        vblk = jnp.take(jnp.take(vgrid, krow, axis=1), kcol, axis=3)
        s_win = jnp.einsum('brjqhd,brkjchd->bhrjqkc', qi, kblk).astype(jnp.float32)
        row_off = krow - r[:, None] + (NA_KH - 1)
        bias = rpb[:, row_off[:, None, None, :, None], col_off[None, :, :, None, :]]
        s_win = jnp.where(col_valid[:, :, None, :], s_win + bias.astype(jnp.float32)[None], -jnp.inf)
        s_ctx = jnp.einsum('brjqhd,bkhd->bhrjqk', qi, ck).astype(jnp.float32)
        s_all = jnp.concatenate([s_win.reshape(s_win.shape[:5] + (n_win,)), s_ctx], axis=-1)
        p = jax.nn.softmax(s_all, axis=-1).astype(v.dtype)
        p_win = p[..., :n_win].reshape(s_win.shape)
        p_ctx = p[..., n_win:]
        return (jnp.einsum('bhrjqkc,brkjchd->brjqhd', p_win, vblk)
                + jnp.einsum('bhrjqk,bkhd->brjqhd', p_ctx, cv))

    o = lax.map(block, (qb, row_ids))
    return jnp.moveaxis(o, 0, 1).reshape(bsz, t, nh * d)


def na_qkv(h, w_in):
    bsz, l, _ = h.shape
    qkv = (h @ w_in).reshape(bsz, l, 3, NA_HEADS, NA_HD)
    return qkv[:, :, 0] * (NA_HD ** -0.5), qkv[:, :, 1], qkv[:, :, 2]


def moe(x, w_router, b_router, w1, b1, w2, b2):
    logits = (x @ w_router + b_router).astype(jnp.float32)
    top_val, top_idx = lax.top_k(logits, TOP_K)
    gates = jax.nn.softmax(top_val, axis=-1)
    gate_full = jnp.sum(jax.nn.one_hot(top_idx, N_EXPERTS, dtype=jnp.float32) * gates[..., None], axis=1).astype(x.dtype)
    y = jnp.zeros_like(x)
    for e in range(N_EXPERTS):
        hh = x @ w1[e] + b1[e]
        h_glu = jnp.minimum(hh[:, 0::2], SWIGLU_LIMIT)
        h_lin = jnp.clip(hh[:, 1::2], -SWIGLU_LIMIT, SWIGLU_LIMIT)
        a = h_glu * jax.nn.sigmoid(SWIGLU_ALPHA * h_glu) * (h_lin + 1.0)
        y = y + gate_full[:, e:e + 1] * (a @ w2[e] + b2[e])
    return y


def setup_inputs(seed: int = 0) -> dict:
    key = jax.random.key(seed)
    ks = jax.random.split(key, 32)
    d = D_MODEL

    def nrm(k, shape, scale):
        return jax.random.normal(k, shape, jnp.float32) * scale

    base_logit = jnp.log(2.0 ** (5.0 + jnp.arange(RET_HEADS, dtype=jnp.float32)) - 1.0)
    return {
        'x_prompt': nrm(ks[0], (BATCH, SEQ, d), 1.0),
        'x_sample': nrm(ks[1], (DEC_BATCH, DEC_SEQ, d), 1.0),
        'state_ret_fwd': nrm(ks[2], (DEC_BATCH, N_RET_LAYERS, RET_HEADS, RET_DK, RET_DV), 1.0),
        'state_ret_bwd': nrm(ks[3], (DEC_BATCH, N_RET_LAYERS, RET_HEADS, RET_DK, RET_DV), 1.0),
        'cache_na_k': nrm(ks[4], (DEC_BATCH, N_NA_LAYERS, PAST_LEN, NA_HEADS, NA_HD), 1.0),
        'cache_na_v': nrm(ks[5], (DEC_BATCH, N_NA_LAYERS, PAST_LEN, NA_HEADS, NA_HD), 1.0),
        'c': nrm(ks[6], (DEC_BATCH, d), 1.0),
        'c_ctx': nrm(ks[7], (d,), 1.0),
        'ada_w': nrm(ks[8], (DEPTH, d, 6 * d), 0.5 * d ** -0.5),
        'ada_b': nrm(ks[9], (DEPTH, 6 * d), 0.02),
        'norm_mix': 1.0 + nrm(ks[10], (DEPTH, d), 0.05),
        'norm_ffn': 1.0 + nrm(ks[11], (DEPTH, d), 0.05),
        'norm_final': 1.0 + nrm(ks[12], (d,), 0.05),
        'ret_w_in': nrm(ks[13], (N_RET_LAYERS, d, 2 * RET_QK_W + 2 * RET_V_W), d ** -0.5),
        'ret_w_out': nrm(ks[14], (N_RET_LAYERS, RET_V_W, d), RET_V_W ** -0.5),
        'ret_decay_fwd': base_logit + nrm(ks[15], (N_RET_LAYERS, RET_HEADS), 0.1),
        'ret_decay_bwd': base_logit + nrm(ks[16], (N_RET_LAYERS, RET_HEADS), 0.1),
        'na_w_in': nrm(ks[17], (N_NA_LAYERS, d, 3 * d), d ** -0.5),
        'na_w_out': nrm(ks[18], (N_NA_LAYERS, d, d), d ** -0.5),
        'na_rpb': nrm(ks[19], (N_NA_LAYERS, NA_HEADS, 2 * NA_KH - 1, 2 * NA_KW - 1), 0.1),
        'moe_w_router': nrm(ks[20], (DEPTH, d, N_EXPERTS), d ** -0.5),
        'moe_b_router': nrm(ks[21], (DEPTH, N_EXPERTS), 0.01),
        'moe_w1': nrm(ks[22], (DEPTH, N_EXPERTS, d, 2 * D_EXPERT), d ** -0.5),
        'moe_b1': nrm(ks[23], (DEPTH, N_EXPERTS, 2 * D_EXPERT), 0.01),
        'moe_w2': nrm(ks[24], (DEPTH, N_EXPERTS, D_EXPERT, d), D_EXPERT ** -0.5),
        'moe_b2': nrm(ks[25], (DEPTH, N_EXPERTS, d), 0.01),
    }


def reference(x_prompt, x_sample, state_ret_fwd, state_ret_bwd, cache_na_k, cache_na_v, c, c_ctx,
              ada_w, ada_b, norm_mix, norm_ffn, norm_final,
              ret_w_in, ret_w_out, ret_decay_fwd, ret_decay_bwd,
              na_w_in, na_w_out, na_rpb,
              moe_w_router, moe_b_router, moe_w1, moe_b1, moe_w2, moe_b2):
    bp, lp, _ = x_prompt.shape
    bs, ls, _ = x_sample.shape
    rope = axial_rope(ls, RET_DK, x_sample.dtype)
    xp, xs = x_prompt, x_sample
    ret_f, ret_b, na_k, na_v = [], [], [], []
    for layer in range(DEPTH):
        mp = adaln(c_ctx[None, :], ada_w[layer], ada_b[layer])
        ms = adaln(c, ada_w[layer], ada_b[layer])
        hp = rmsnorm(xp, norm_mix[layer]) * (1.0 + mp[1]) + mp[0]
        hs = rmsnorm(xs, norm_mix[layer]) * (1.0 + ms[1]) + ms[0]
        j = layer // N_MIXERS
        if layer % N_MIXERS == 0:
            lg_f = jax.nn.log_sigmoid(ret_decay_fwd[j])
            lg_b = jax.nn.log_sigmoid(ret_decay_bwd[j])
            zeros = jnp.zeros((bp, RET_HEADS, RET_DK, RET_DV), xp.dtype)
            op, sf, sb = retention_mixer(hp, ret_w_in[j], ret_w_out[j], lg_f, lg_b, zeros, zeros, None)
            os_, _, _ = retention_mixer(hs, ret_w_in[j], ret_w_out[j], lg_f, lg_b,
                                        state_ret_fwd[:, j], state_ret_bwd[:, j], rope)
            ret_f.append(sf)
            ret_b.append(sb)
        else:
            qp, kp, vp = na_qkv(hp, na_w_in[j])
            op = dense_ctx_attention(qp, kp, vp) @ na_w_out[j]
            qs, ks_, vs = na_qkv(hs, na_w_in[j])
            os_ = neighbourhood_attention(qs, ks_, vs, cache_na_k[:, j], cache_na_v[:, j], na_rpb[j]) @ na_w_out[j]
            na_k.append(kp)
            na_v.append(vp)
        xp = xp + mp[2] * op
        xs = xs + ms[2] * os_
        hp = rmsnorm(xp, norm_ffn[layer]) * (1.0 + mp[4]) + mp[3]
        hs = rmsnorm(xs, norm_ffn[layer]) * (1.0 + ms[4]) + ms[3]
        tok = jnp.concatenate([hp.reshape(-1, D_MODEL), hs.reshape(-1, D_MODEL)], axis=0)
        y = moe(tok, moe_w_router[layer], moe_b_router[layer], moe_w1[layer], moe_b1[layer],
                moe_w2[layer], moe_b2[layer])
        xp = xp + mp[5] * y[:bp * lp].reshape(bp, lp, D_MODEL)
        xs = xs + ms[5] * y[bp * lp:].reshape(bs, ls, D_MODEL)
    y_prompt = rmsnorm(xp, norm_final)
    y_sample = rmsnorm(xs, norm_final)
    new_state_ret_fwd = jnp.stack(ret_f, axis=1)
    new_state_ret_bwd = jnp.stack(ret_b, axis=1)
    new_cache_na_k = jnp.stack(na_k, axis=1)
    new_cache_na_v = jnp.stack(na_v, axis=1)
    return (y_prompt, y_sample, new_state_ret_fwd, new_state_ret_bwd, new_cache_na_k, new_cache_na_v)
```

```python
import functools

import numpy as np
import jax
import jax.numpy as jnp
from jax import lax
from jax.experimental import pallas as pl
from jax.experimental.pallas import tpu as pltpu

F32 = jnp.float32
BF16 = jnp.bfloat16
U32 = jnp.uint32
I32 = jnp.int32

D_MODEL = 1024
BATCH = 32
SEQ = 256
DEPTH = 4
DEC_BATCH = 2
DEC_SEQ = 1024
PAST_LEN = 512
GRID_W = 64
EPS = 1e-6
ROPE_BASE = 10000.0
RET_HEADS = 4
RET_DK = D_MODEL // RET_HEADS
RET_DV = 2 * RET_DK
RET_QK_W = RET_HEADS * RET_DK
RET_V_W = RET_HEADS * RET_DV
NA_HEADS = 16
NA_HD = D_MODEL // NA_HEADS
NA_KH = 8
NA_KW = 16
N_EXPERTS = 32
TOP_K = 4
D_EXPERT = D_MODEL
SWIGLU_LIMIT = 7.0
SWIGLU_ALPHA = 1.702

N_PROMPT = BATCH * SEQ
N_SAMPLE = DEC_BATCH * DEC_SEQ
N_TOK = N_PROMPT + N_SAMPLE
N_GROUPS = 1 + DEC_BATCH
MOD_ROWS = 8
LANES = 128
SUBLANES = 8

TM = 256
RET_CHUNK = 256
MOE_TM = 256
MOE_ROWS = N_TOK * TOP_K + N_EXPERTS * MOE_TM
MOE_TILES = MOE_ROWS // MOE_TM
ROUTER_TM = 512
PERM_W = 256
VMEM_LIMIT = 56 * 1024 * 1024


def _cparams(n_axes, vmem=VMEM_LIMIT):
    return pltpu.CompilerParams(dimension_semantics=("arbitrary",) * n_axes, vmem_limit_bytes=vmem)


def _group_of_tile(i, tm):
    return jnp.maximum((i * tm) // DEC_SEQ - (N_PROMPT // DEC_SEQ - 1), 0)


def _sigmoid(x):
    return 1.0 / (1.0 + jnp.exp(-x))


def _silu(x):
    return x * _sigmoid(x)


def _log_sigmoid(x):
    return jnp.minimum(x, 0.0) - jnp.log(1.0 + jnp.exp(-jnp.abs(x)))


def _norm_mod(x, g, shift, scale):
    y = x * lax.rsqrt(jnp.mean(x * x, axis=-1, keepdims=True) + EPS)
    return (y * g) * (1.0 + scale) + shift


def _dot(a, b):
    return jnp.dot(a, b, preferred_element_type=F32)


def _dot_nt(a, b):
    return lax.dot_general(a, b, (((1,), (1,)), ((), ())), preferred_element_type=F32)


def _dot_tn(a, b):
    return lax.dot_general(a, b, (((0,), (0,)), ((), ())), preferred_element_type=F32)


ADA_TN = 1536


def _adaln_body(cond_ref, w_ref, b_ref, o_ref):
    s = _silu(cond_ref[...]).astype(BF16)
    o_ref[...] = _dot(s, w_ref[...].astype(BF16)) + b_ref[...]


def _adaln(cond, ada_w, ada_b):
    n_out = 6 * D_MODEL
    return pl.pallas_call(
        _adaln_body,
        grid=(DEPTH, n_out // ADA_TN),
        in_specs=[
            pl.BlockSpec((MOD_ROWS, D_MODEL), lambda l, j: (0, 0)),
            pl.BlockSpec((None, D_MODEL, ADA_TN), lambda l, j: (l, 0, j)),
            pl.BlockSpec((None, 1, ADA_TN), lambda l, j: (l, 0, j)),
        ],
        out_specs=pl.BlockSpec((None, MOD_ROWS, ADA_TN), lambda l, j: (l, 0, j)),
        out_shape=jax.ShapeDtypeStruct((DEPTH, MOD_ROWS, n_out), F32),
        compiler_params=_cparams(2),
        name="adaln",
    )(cond, ada_w, ada_b.reshape(DEPTH, 1, n_out))


def _inproj_body(x_ref, g_ref, mod_ref, w_ref, o_ref, *, n_chunk):
    h = _norm_mod(x_ref[...], g_ref[...], mod_ref[0:1, :], mod_ref[1:2, :]).astype(BF16)
    for c in range(o_ref.shape[1] // n_chunk):
        sl = slice(c * n_chunk, (c + 1) * n_chunk)
        o_ref[:, sl] = _dot(h, w_ref[:, sl]).astype(o_ref.dtype)


def _inproj(x, g, mod, w_bf16):
    n_out = w_bf16.shape[1]
    return pl.pallas_call(
        functools.partial(_inproj_body, n_chunk=1024),
        grid=(N_TOK // TM,),
        in_specs=[
            pl.BlockSpec((TM, D_MODEL), lambda i: (i, 0)),
            pl.BlockSpec((1, D_MODEL), lambda i: (0, 0)),
            pl.BlockSpec((None, MOD_ROWS, D_MODEL), lambda i: (_group_of_tile(i, TM), 0, 0)),
            pl.BlockSpec((D_MODEL, n_out), lambda i: (0, 0)),
        ],
        out_specs=pl.BlockSpec((TM, n_out), lambda i: (i, 0)),
        out_shape=jax.ShapeDtypeStruct((N_TOK, n_out), F32),
        compiler_params=_cparams(1),
        name="inproj",
    )(x, g.reshape(1, D_MODEL), mod, w_bf16)


def _decay_terms(dec_ref, h, c):
    lgf = _log_sigmoid(jnp.zeros((c, c), F32) + dec_ref[0, h])
    lgb = _log_sigmoid(jnp.zeros((c, c), F32) + dec_ref[1, h])
    row = lax.broadcasted_iota(I32, (c, c), 0)
    col = lax.broadcasted_iota(I32, (c, c), 1)
    diff = (row - col).astype(F32)
    lower = diff >= 0
    upper = diff <= 0
    mask = (jnp.where(lower, jnp.exp(jnp.where(lower, diff, 0.0) * lgf), 0.0)
            + jnp.where(upper, jnp.exp(jnp.where(upper, -diff, 0.0) * lgb), 0.0))
    pos = lax.broadcasted_iota(I32, (c, 1), 0).astype(F32)
    lgf1 = _log_sigmoid(jnp.zeros((c, 1), F32) + dec_ref[0, h])
    lgb1 = _log_sigmoid(jnp.zeros((c, 1), F32) + dec_ref[1, h])
    lgf0 = _log_sigmoid(jnp.zeros((1, 1), F32) + dec_ref[0, h])
    lgb0 = _log_sigmoid(jnp.zeros((1, 1), F32) + dec_ref[1, h])
    terms = dict(
        mask=mask,
        q_f=jnp.exp((pos + 1.0) * lgf1), k_f=jnp.exp((c - 1.0 - pos) * lgf1), c_f=jnp.exp(c * lgf0),
        q_b=jnp.exp((c - pos) * lgb1), k_b=jnp.exp(pos * lgb1), c_b=jnp.exp(c * lgb0),
    )
    return terms


def _group_norm_gate(o, g):
    o = o * lax.rsqrt(jnp.mean(o * o, axis=-1, keepdims=True) + EPS)
    return (_silu(g) * o).astype(BF16)


def _ret_prompt_body(dec_ref, q_ref, k_ref, v_ref, g_ref, o_ref, sf_ref, sb_ref):
    h = pl.program_id(1)
    t = _decay_terms(dec_ref, h, SEQ)
    q = q_ref[...]
    k = k_ref[...] * (RET_DK ** -0.5)
    v = v_ref[...].astype(BF16)
    scores = _dot_nt(q.astype(BF16), k.astype(BF16)) * t["mask"]
    o = _dot(scores.astype(BF16), v)
    o_ref[...] = _group_norm_gate(o, g_ref[...])
    sf_ref[...] = _dot_tn((k * t["k_f"]).astype(BF16), v)
    sb_ref[...] = _dot_tn((k * t["k_b"]).astype(BF16), v)


def _ret_prompt(proj, decay):
    kq = RET_QK_W // RET_DK
    kv = 2 * RET_QK_W // RET_DV
    kg = kv + RET_HEADS
    state = jax.ShapeDtypeStruct((BATCH, RET_HEADS, RET_DK, RET_DV), F32)
    return pl.pallas_call(
        _ret_prompt_body,
        grid=(BATCH, RET_HEADS),
        in_specs=[
            pl.BlockSpec(memory_space=pltpu.SMEM),
            pl.BlockSpec((SEQ, RET_DK), lambda b, h: (b, h)),
            pl.BlockSpec((SEQ, RET_DK), lambda b, h: (b, kq + h)),
            pl.BlockSpec((SEQ, RET_DV), lambda b, h: (b, kv + h)),
            pl.BlockSpec((SEQ, RET_DV), lambda b, h: (b, kg + h)),
        ],
        out_specs=[
            pl.BlockSpec((SEQ, RET_DV), lambda b, h: (b, h)),
            pl.BlockSpec((None, None, RET_DK, RET_DV), lambda b, h: (b, h, 0, 0)),
            pl.BlockSpec((None, None, RET_DK, RET_DV), lambda b, h: (b, h, 0, 0)),
        ],
        out_shape=[jax.ShapeDtypeStruct((N_PROMPT, RET_V_W), BF16), state, state],
        compiler_params=_cparams(2),
        name="ret_prompt",
    )(decay, proj, proj, proj, proj)


def _rope(x, cos, sin_signed):
    n = x.shape[-1]
    lane = lax.broadcasted_iota(I32, x.shape, 1)
    swapped = jnp.where(lane % 2 == 0, pltpu.roll(x, n - 1, 1), pltpu.roll(x, 1, 1))
    return x * cos + swapped * sin_signed


def _ret_sample_body(dec_ref, q_ref, k_ref, v_ref, g_ref, cos_ref, sin_ref, s0f_ref, s0b_ref, o_ref,
                     sf_scr, sb_scr):
    h = pl.program_id(1)
    c = RET_CHUNK
    n_chunks = DEC_SEQ // c
    t = _decay_terms(dec_ref, h, c)

    def chunk(ref, i):
        return ref[i * c:(i + 1) * c, :]

    def qk(i):
        q = _rope(chunk(q_ref, i), chunk(cos_ref, i), chunk(sin_ref, i))
        k = _rope(chunk(k_ref, i) * (RET_DK ** -0.5), chunk(cos_ref, i), chunk(sin_ref, i))
        return q, k

    s = s0f_ref[...]
    for i in range(n_chunks):
        sf_scr[i] = s
        if i + 1 < n_chunks:
            _, k = qk(i)
            s = s * t["c_f"] + _dot_tn((k * t["k_f"]).astype(BF16), chunk(v_ref, i).astype(BF16))
    s = s0b_ref[...]
    for i in reversed(range(n_chunks)):
        sb_scr[i] = s
        if i > 0:
            _, k = qk(i)
            s = s * t["c_b"] + _dot_tn((k * t["k_b"]).astype(BF16), chunk(v_ref, i).astype(BF16))

    for i in range(n_chunks):
        q, k = qk(i)
        scores = _dot_nt(q.astype(BF16), k.astype(BF16)) * t["mask"]
        o = _dot(scores.astype(BF16), chunk(v_ref, i).astype(BF16))
        o = o + _dot((q * t["q_f"]).astype(BF16), sf_scr[i].astype(BF16))
        o = o + _dot((q * t["q_b"]).astype(BF16), sb_scr[i].astype(BF16))
        o_ref[i * c:(i + 1) * c, :] = _group_norm_gate(o, chunk(g_ref, i))


def _ret_sample(proj, decay, cos, sin_signed, s0f, s0b, j):
    kq = RET_QK_W // RET_DK
    kv = 2 * RET_QK_W // RET_DV
    kg = kv + RET_HEADS
    rb = N_PROMPT // DEC_SEQ
    n_chunks = DEC_SEQ // RET_CHUNK
    return pl.pallas_call(
        _ret_sample_body,
        grid=(DEC_BATCH, RET_HEADS),
        in_specs=[
            pl.BlockSpec(memory_space=pltpu.SMEM),
            pl.BlockSpec((DEC_SEQ, RET_DK), lambda b, h: (rb + b, h)),
            pl.BlockSpec((DEC_SEQ, RET_DK), lambda b, h: (rb + b, kq + h)),
            pl.BlockSpec((DEC_SEQ, RET_DV), lambda b, h: (rb + b, kv + h)),
            pl.BlockSpec((DEC_SEQ, RET_DV), lambda b, h: (rb + b, kg + h)),
            pl.BlockSpec((DEC_SEQ, RET_DK), lambda b, h: (0, 0)),
            pl.BlockSpec((DEC_SEQ, RET_DK), lambda b, h: (0, 0)),
            pl.BlockSpec((None, None, None, RET_DK, RET_DV), lambda b, h: (b, j, h, 0, 0)),
            pl.BlockSpec((None, None, None, RET_DK, RET_DV), lambda b, h: (b, j, h, 0, 0)),
        ],
        out_specs=pl.BlockSpec((DEC_SEQ, RET_DV), lambda b, h: (b, h)),
        out_shape=jax.ShapeDtypeStruct((N_SAMPLE, RET_V_W), BF16),
        scratch_shapes=[pltpu.VMEM((n_chunks, RET_DK, RET_DV), F32),
                        pltpu.VMEM((n_chunks, RET_DK, RET_DV), F32)],
        compiler_params=_cparams(2),
        name="ret_sample",
    )(decay, proj, proj, proj, proj, cos, sin_signed, s0f, s0b)


def _softmax_pv(parts):
    m = parts[0][0].max(axis=-1, keepdims=True)
    for s, _ in parts[1:]:
        m = jnp.maximum(m, s.max(axis=-1, keepdims=True))
    es = [jnp.exp(s - m) for s, _ in parts]
    denom = es[0].sum(axis=-1, keepdims=True)
    for e in es[1:]:
        denom = denom + e.sum(axis=-1, keepdims=True)
    inv = 1.0 / denom
    o = _dot((es[0] * inv).astype(BF16), parts[0][1])
    for e, (_, v) in zip(es[1:], parts[1:]):
        o = o + _dot((e * inv).astype(BF16), v)
    return o


def _ctx_attn_body(q_ref, k_ref, v_ref, o_ref):
    outs = []
    for h in range(NA_HEADS):
        sl = slice(h * NA_HD, (h + 1) * NA_HD)
        q = (q_ref[:, sl] * (NA_HD ** -0.5)).astype(BF16)
        s = _dot_nt(q, k_ref[:, sl].astype(BF16))
        outs.append(_softmax_pv([(s, v_ref[:, sl].astype(BF16))]))
    o_ref[...] = jnp.concatenate(outs, axis=-1).astype(BF16)


def _ctx_attn(proj):
    nb = D_MODEL // D_MODEL
    return pl.pallas_call(
        _ctx_attn_body,
        grid=(BATCH,),
        in_specs=[
            pl.BlockSpec((SEQ, D_MODEL), lambda b: (b, 0)),
            pl.BlockSpec((SEQ, D_MODEL), lambda b: (b, nb)),
            pl.BlockSpec((SEQ, D_MODEL), lambda b: (b, 2 * nb)),
        ],
        out_specs=pl.BlockSpec((SEQ, D_MODEL), lambda b: (b, 0)),
        out_shape=jax.ShapeDtypeStruct((N_PROMPT, D_MODEL), BF16),
        compiler_params=_cparams(1),
        name="ctx_attn",
    )(proj, proj, proj)


NA_QT = 256
NA_HPB = LANES // NA_HD


def _na_attn_body(q_ref, k_ref, v_ref, ck_ref, cv_ref, bias_ref, o_ref):
    outs = []
    for hh in range(NA_HPB):
        sl = slice(hh * NA_HD, (hh + 1) * NA_HD)
        q = (q_ref[:, sl] * (NA_HD ** -0.5)).astype(BF16)
        s_win = _dot_nt(q, k_ref[:, sl].astype(BF16)) + bias_ref[hh]
        s_ctx = _dot_nt(q, ck_ref[:, sl].astype(BF16))
        outs.append(_softmax_pv([(s_win, v_ref[:, sl].astype(BF16)), (s_ctx, cv_ref[:, sl].astype(BF16))]))
    o_ref[...] = jnp.concatenate(outs, axis=-1).astype(BF16)


def _na_attn(proj, cache_k, cache_v, bias, j):
    n_qt = DEC_SEQ // NA_QT
    rb_q = N_PROMPT // NA_QT
    rb_kv = N_PROMPT // DEC_SEQ
    cb = D_MODEL // LANES
    ck = cache_k.reshape(DEC_BATCH, -1, PAST_LEN, D_MODEL)
    cv = cache_v.reshape(DEC_BATCH, -1, PAST_LEN, D_MODEL)
    return pl.pallas_call(
        _na_attn_body,
        grid=(cb, n_qt, DEC_BATCH),
        in_specs=[
            pl.BlockSpec((NA_QT, LANES), lambda p, t, b: (rb_q + b * n_qt + t, p)),
            pl.BlockSpec((DEC_SEQ, LANES), lambda p, t, b: (rb_kv + b, cb + p)),
            pl.BlockSpec((DEC_SEQ, LANES), lambda p, t, b: (rb_kv + b, 2 * cb + p)),
            pl.BlockSpec((None, None, PAST_LEN, LANES), lambda p, t, b: (b, j, 0, p)),
            pl.BlockSpec((None, None, PAST_LEN, LANES), lambda p, t, b: (b, j, 0, p)),
            pl.BlockSpec((NA_HPB, NA_QT, DEC_SEQ), lambda p, t, b: (p, t, 0)),
        ],
        out_specs=pl.BlockSpec((NA_QT, LANES), lambda p, t, b: (b * n_qt + t, p)),
        out_shape=jax.ShapeDtypeStruct((N_SAMPLE, D_MODEL), BF16),
        compiler_params=_cparams(3),
        name="na_attn",
    )(proj, proj, proj, ck, cv, bias)


def _na_bias_table(rpb):
    rows = DEC_SEQ // GRID_W
    r = np.arange(rows)
    c = np.arange(GRID_W)
    rs = np.clip(r - NA_KH // 2, 0, rows - NA_KH)
    cs = np.clip(c - NA_KW // 2, 0, GRID_W - NA_KW)
    row_valid = (r[None, :] >= rs[:, None]) & (r[None, :] < rs[:, None] + NA_KH)
    col_valid = (c[None, :] >= cs[:, None]) & (c[None, :] < cs[:, None] + NA_KW)
    row_off = np.clip(r[None, :] - r[:, None] + NA_KH - 1, 0, 2 * NA_KH - 2)
    col_off = np.clip(c[None, :] - c[:, None] + NA_KW - 1, 0, 2 * NA_KW - 2)
    sel_r = (row_off[..., None] == np.arange(2 * NA_KH - 1)).astype(np.float32)
    sel_c = (col_off[..., None] == np.arange(2 * NA_KW - 1)).astype(np.float32)
    bias = jnp.einsum("hij,rki,cdj->hrckd", rpb, sel_r, sel_c, precision=lax.Precision.HIGHEST)
    valid = row_valid[:, None, :, None] & col_valid[None, :, None, :]
    bias = jnp.where(valid[None], bias, -jnp.inf)
    return bias.reshape(NA_HEADS, DEC_SEQ, DEC_SEQ)


def _pack_bf16_pair(lo, hi):
    ulo = pltpu.bitcast(lo.astype(BF16).astype(F32), U32)
    uhi = pltpu.bitcast(hi.astype(BF16).astype(F32), U32)
    return (ulo >> 16) | (uhi & jnp.uint32(0xFFFF0000))


def _unpack_bf16_pair(p):
    lo = pltpu.bitcast(p << 16, F32).astype(BF16)
    hi = pltpu.bitcast(p & jnp.uint32(0xFFFF0000), F32).astype(BF16)
    return lo, hi


def _outproj_body(a_ref, w_ref, x_ref, g_ref, mod_ref, wrh_ref, wrl_ref, br_ref, xo_ref, tok_ref, lg_ref):
    x = x_ref[...] + mod_ref[2:3, :] * _dot(a_ref[...], w_ref[...])
    xo_ref[...] = x
    tok = _norm_mod(x, g_ref[...], mod_ref[3:4, :], mod_ref[4:5, :])
    half = D_MODEL // 2
    tok_ref[...] = _pack_bf16_pair(tok[:, :half], tok[:, half:])
    hi = tok.astype(BF16)
    lo = (tok - hi.astype(F32)).astype(BF16)
    lg_ref[...] = _dot(hi, wrh_ref[...]) + _dot(lo, wrh_ref[...]) + _dot(hi, wrl_ref[...]) + br_ref[...]


def _outproj(a, w_bf16, x, g, mod, wr_hi, wr_lo, br):
    k_in = a.shape[1]
    return pl.pallas_call(
        _outproj_body,
        grid=(N_TOK // TM,),
        in_specs=[
            pl.BlockSpec((TM, k_in), lambda i: (i, 0)),
            pl.BlockSpec((k_in, D_MODEL), lambda i: (0, 0)),
            pl.BlockSpec((TM, D_MODEL), lambda i: (i, 0)),
            pl.BlockSpec((1, D_MODEL), lambda i: (0, 0)),
            pl.BlockSpec((None, MOD_ROWS, D_MODEL), lambda i: (_group_of_tile(i, TM), 0, 0)),
            pl.BlockSpec((D_MODEL, LANES), lambda i: (0, 0)),
            pl.BlockSpec((D_MODEL, LANES), lambda i: (0, 0)),
            pl.BlockSpec((1, LANES), lambda i: (0, 0)),
        ],
        out_specs=[
            pl.BlockSpec((TM, D_MODEL), lambda i: (i, 0)),
            pl.BlockSpec((TM, D_MODEL // 2), lambda i: (i, 0)),
            pl.BlockSpec((TM, LANES), lambda i: (i, 0)),
        ],
        out_shape=[
            jax.ShapeDtypeStruct((N_TOK, D_MODEL), F32),
            jax.ShapeDtypeStruct((N_TOK, D_MODEL // 2), U32),
            jax.ShapeDtypeStruct((N_TOK, LANES), F32),
        ],
        compiler_params=_cparams(1),
        name="outproj",
    )(a, w_bf16, x, g.reshape(1, D_MODEL), mod, wr_hi, wr_lo, br)


def _router_body(lg_ref, idx_ref, gate_ref, rank_ref, cnt_ref, carry):
    i = pl.program_id(0)

    @pl.when(i == 0)
    def _():
        carry[...] = jnp.zeros_like(carry)

    lane = lax.broadcasted_iota(I32, (ROUTER_TM, LANES), 1)
    l = jnp.where(lane < N_EXPERTS, lg_ref[...], -jnp.inf)
    vals, idxs, hots = [], [], []
    for _ in range(TOP_K):
        m = l.max(axis=-1, keepdims=True)
        idx = jnp.where(l == m, lane, LANES).min(axis=-1, keepdims=True)
        hot = lane == idx
        l = jnp.where(hot, -jnp.inf, l)
        vals.append(m)
        idxs.append(idx)
        hots.append(hot)
    es = [jnp.exp(v - vals[0]) for v in vals]
    denom = es[0] + es[1] + es[2] + es[3]
    hot_all = (hots[0] | hots[1] | hots[2] | hots[3]).astype(F32)
    r = lax.broadcasted_iota(I32, (ROUTER_TM, ROUTER_TM), 0)
    c = lax.broadcasted_iota(I32, (ROUTER_TM, ROUTER_TM), 1)
    before = _dot((c < r).astype(BF16), hot_all.astype(BF16)) + carry[...]
    idx_out = jnp.zeros((ROUTER_TM, LANES), I32)
    gate_out = jnp.zeros((ROUTER_TM, LANES), F32)
    rank_out = jnp.zeros((ROUTER_TM, LANES), I32)
    for k in range(TOP_K):
        rank_k = jnp.where(hots[k], before, 0.0).sum(axis=-1, keepdims=True).astype(I32)
        idx_out = jnp.where(lane == k, idxs[k], idx_out)
        gate_out = jnp.where(lane == k, es[k] / denom, gate_out)
        rank_out = jnp.where(lane == k, rank_k, rank_out)
    idx_ref[...] = idx_out
    gate_ref[...] = gate_out
    rank_ref[...] = rank_out
    carry[...] = carry[...] + hot_all.sum(axis=0, keepdims=True)
    cnt_ref[...] = carry[...].astype(I32)


def _router(logits):
    spec = pl.BlockSpec((ROUTER_TM, LANES), lambda i: (i, 0))
    return pl.pallas_call(
        _router_body,
        grid=(N_TOK // ROUTER_TM,),
        in_specs=[spec],
        out_specs=[spec, spec, spec, pl.BlockSpec((1, LANES), lambda i: (0, 0))],
        out_shape=[
            jax.ShapeDtypeStruct((N_TOK, LANES), I32),
            jax.ShapeDtypeStruct((N_TOK, LANES), F32),
            jax.ShapeDtypeStruct((N_TOK, LANES), I32),
            jax.ShapeDtypeStruct((1, LANES), I32),
        ],
        scratch_shapes=[pltpu.VMEM((1, LANES), F32)],
        compiler_params=_cparams(1),
        name="router",
    )(logits)


DISPATCH_TM = 256


def _dispatch_body(src_ref, tok_hbm, o_ref, sem):
    def issue(r, _):
        pltpu.make_async_copy(tok_hbm.at[pl.ds(src_ref[r], 1)], o_ref.at[pl.ds(r, 1)], sem).start()
        return 0

    lax.fori_loop(0, DISPATCH_TM, issue, 0)

    def drain(r, _):
        pltpu.make_async_copy(tok_hbm.at[pl.ds(0, 1)], o_ref.at[pl.ds(r, 1)], sem).wait()
        return 0

    lax.fori_loop(0, DISPATCH_TM, drain, 0)


def _dispatch(tok_packed, src_rows):
    w = tok_packed.shape[1]
    return pl.pallas_call(
        _dispatch_body,
        grid=(MOE_ROWS // DISPATCH_TM,),
        in_specs=[
            pl.BlockSpec((DISPATCH_TM,), lambda i: (i,), memory_space=pltpu.SMEM),
            pl.BlockSpec(memory_space=pl.ANY),
        ],
        out_specs=pl.BlockSpec((DISPATCH_TM, w), lambda i: (i, 0)),
        out_shape=jax.ShapeDtypeStruct((MOE_ROWS, w), tok_packed.dtype),
        scratch_shapes=[pltpu.SemaphoreType.DMA],
        compiler_params=_cparams(1),
        name="dispatch",
    )(src_rows, tok_packed)


def _moe_body(te_ref, tv_ref, x_ref, w1_ref, b1_ref, w2_ref, b2_ref, perm_ref, o_ref, w1_scr, w2_scr):
    t = pl.program_id(0)
    first = jnp.logical_or(t == 0, te_ref[t] != te_ref[jnp.maximum(t - 1, 0)])

    @pl.when(jnp.logical_and(tv_ref[t] == 1, first))
    def _():
        w2_scr[...] = w2_ref[...].astype(BF16)
        half = PERM_W // 2
        for c in range(2 * D_EXPERT // PERM_W):
            blk = _dot(w1_ref[:, c * PERM_W:(c + 1) * PERM_W].astype(BF16), perm_ref[...]).astype(BF16)
            w1_scr[:, c * half:(c + 1) * half] = blk[:, :half]
            w1_scr[:, D_EXPERT + c * half:D_EXPERT + (c + 1) * half] = blk[:, half:]

    @pl.when(tv_ref[t] == 0)
    def _():
        o_ref[...] = jnp.zeros_like(o_ref)

    @pl.when(tv_ref[t] == 1)
    def _():
        lo, hi = _unpack_bf16_pair(x_ref[...])
        x = jnp.concatenate([lo, hi], axis=-1)
        hh = _dot(x, w1_scr[...]) + b1_ref[...]
        h_glu = jnp.minimum(hh[:, :D_EXPERT], SWIGLU_LIMIT)
        h_lin = jnp.clip(hh[:, D_EXPERT:], -SWIGLU_LIMIT, SWIGLU_LIMIT)
        a = h_glu * _sigmoid(SWIGLU_ALPHA * h_glu) * (h_lin + 1.0)
        o_ref[...] = _dot(a.astype(BF16), w2_scr[...]) + b2_ref[...]


def _moe_experts(tile_expert, tile_valid, xs, w1, b1_split, w2, b2, perm):
    grid_spec = pltpu.PrefetchScalarGridSpec(
        num_scalar_prefetch=2,
        grid=(MOE_TILES,),
        in_specs=[
            pl.BlockSpec((MOE_TM, D_MODEL // 2), lambda t, te, tv: (t, 0)),
            pl.BlockSpec((None, D_MODEL, 2 * D_EXPERT), lambda t, te, tv: (te[t], 0, 0)),
            pl.BlockSpec((None, 1, 2 * D_EXPERT), lambda t, te, tv: (te[t], 0, 0)),
            pl.BlockSpec((None, D_EXPERT, D_MODEL), lambda t, te, tv: (te[t], 0, 0)),
            pl.BlockSpec((None, 1, D_MODEL), lambda t, te, tv: (te[t], 0, 0)),
            pl.BlockSpec((PERM_W, PERM_W), lambda t, te, tv: (0, 0)),
        ],
        out_specs=pl.BlockSpec((MOE_TM, D_MODEL), lambda t, te, tv: (t, 0)),
        scratch_shapes=[pltpu.VMEM((D_MODEL, 2 * D_EXPERT), BF16), pltpu.VMEM((D_EXPERT, D_MODEL), BF16)],
    )
    return pl.pallas_call(
        _moe_body,
        grid_spec=grid_spec,
        out_shape=jax.ShapeDtypeStruct((MOE_ROWS, D_MODEL), F32),
        compiler_params=_cparams(1),
        name="moe_experts",
    )(tile_expert, tile_valid, xs, w1, b1_split, w2, b2, perm)


COMBINE_TM = 128


def _combine_body(pos_ref, y_hbm, gate_ref, x_ref, mod_ref, g_ref, o_ref, buf, sem, *, final):
    def issue(r, _):
        for k in range(TOP_K):
            pltpu.make_async_copy(y_hbm.at[pl.ds(pos_ref[r * TOP_K + k], 1)], buf.at[k, pl.ds(r, 1)], sem).start()
        return 0

    lax.fori_loop(0, COMBINE_TM, issue, 0)

    def drain(r, _):
        for k in range(TOP_K):
            pltpu.make_async_copy(y_hbm.at[pl.ds(0, 1)], buf.at[k, pl.ds(r, 1)], sem).wait()
        return 0

    lax.fori_loop(0, COMBINE_TM, drain, 0)
    gates = gate_ref[...]
    y = gates[:, 0:1] * buf[0]
    for k in range(1, TOP_K):
        y = y + gates[:, k:k + 1] * buf[k]
    x = x_ref[...] + mod_ref[5:6, :] * y
    if final:
        x = (x * lax.rsqrt(jnp.mean(x * x, axis=-1, keepdims=True) + EPS)) * g_ref[...]
    o_ref[...] = x


def _combine(pos_flat, y_sorted, gates, x, mod, g_final, final):
    return pl.pallas_call(
        functools.partial(_combine_body, final=final),
        grid=(N_TOK // COMBINE_TM,),
        in_specs=[
            pl.BlockSpec((COMBINE_TM * TOP_K,), lambda i: (i,), memory_space=pltpu.SMEM),
            pl.BlockSpec(memory_space=pl.ANY),
            pl.BlockSpec((COMBINE_TM, LANES), lambda i: (i, 0)),
            pl.BlockSpec((COMBINE_TM, D_MODEL), lambda i: (i, 0)),
            pl.BlockSpec((None, MOD_ROWS, D_MODEL), lambda i: (_group_of_tile(i, COMBINE_TM), 0, 0)),
            pl.BlockSpec((1, D_MODEL), lambda i: (0, 0)),
        ],
        out_specs=pl.BlockSpec((COMBINE_TM, D_MODEL), lambda i: (i, 0)),
        out_shape=jax.ShapeDtypeStruct((N_TOK, D_MODEL), F32),
        scratch_shapes=[pltpu.VMEM((TOP_K, COMBINE_TM, D_MODEL), F32), pltpu.SemaphoreType.DMA],
        compiler_params=_cparams(1),
        name="combine",
    )(pos_flat, y_sorted, gates, x, mod, g_final.reshape(1, D_MODEL))


def _routing_tables(idx, rank, counts):
    counts = counts[0, :N_EXPERTS]
    padded = ((counts + MOE_TM - 1) // MOE_TM) * MOE_TM
    ends = jnp.cumsum(padded)
    offsets = ends - padded
    idx = idx[:, :TOP_K]
    pos = rank[:, :TOP_K] + jnp.sum(
        jnp.where(idx[..., None] == jnp.arange(N_EXPERTS, dtype=I32), offsets, 0), axis=-1)
    pos_flat = pos.reshape(-1).astype(I32)
    token = jnp.repeat(jnp.arange(N_TOK, dtype=I32), TOP_K)
    src = jnp.zeros((MOE_ROWS,), I32).at[pos_flat].set(token)
    tile_start = jnp.arange(MOE_TILES, dtype=I32) * MOE_TM
    tile_valid = (tile_start < ends[-1]).astype(I32)
    tile_expert = jnp.sum((tile_start[:, None] >= ends[None, :]).astype(I32), axis=-1)
    last_expert = jnp.sum((ends[-1] - 1 >= ends).astype(I32))
    tile_expert = jnp.where(tile_valid == 1, tile_expert, last_expert).astype(I32)
    return pos_flat, src, tile_expert, tile_valid


def _rope_tables():
    t = np.arange(DEC_SEQ)
    row = (t // GRID_W).astype(np.float32)
    col = (t % GRID_W).astype(np.float32)
    d_axis = RET_DK // 2
    inv = jnp.asarray(ROPE_BASE, F32) ** (-jnp.arange(0, d_axis, 2, dtype=F32) / d_axis)
    ang = jnp.concatenate([row[:, None] * inv, col[:, None] * inv], axis=-1)
    cos = jnp.repeat(jnp.cos(ang), 2, axis=-1)
    sin = jnp.repeat(jnp.sin(ang), 2, axis=-1)
    sign = jnp.asarray(np.tile(np.array([-1.0, 1.0], np.float32), RET_DK // 2))
    return cos, sin * sign


def _deinterleave_matrix():
    p = np.zeros((PERM_W, PERM_W), np.float32)
    half = PERM_W // 2
    for j in range(half):
        p[2 * j, j] = 1.0
        p[2 * j + 1, half + j] = 1.0
    return jnp.asarray(p, BF16)


def _split_hi_lo(w):
    hi = w.astype(BF16)
    return hi, (w - hi.astype(F32)).astype(BF16)


def kernel(x_prompt, x_sample, state_ret_fwd, state_ret_bwd, cache_na_k, cache_na_v, c, c_ctx, ada_w, ada_b, norm_mix, norm_ffn, norm_final, ret_w_in, ret_w_out, ret_decay_fwd, ret_decay_bwd, na_w_in, na_w_out, na_rpb, moe_w_router, moe_b_router, moe_w1, moe_b1, moe_w2, moe_b2):
    x = jnp.concatenate([x_prompt.reshape(N_PROMPT, D_MODEL), x_sample.reshape(N_SAMPLE, D_MODEL)], axis=0)

    cond = jnp.zeros((MOD_ROWS, D_MODEL), F32).at[0].set(c_ctx).at[1:1 + DEC_BATCH].set(c)
    mod_all = _adaln(cond, ada_w, ada_b)
    mod_all = mod_all[:, :N_GROUPS].reshape(DEPTH, N_GROUPS, 6, D_MODEL)
    mod_all = jnp.pad(mod_all, ((0, 0), (0, 0), (0, MOD_ROWS - 6), (0, 0)))

    cos, sin_signed = _rope_tables()
    perm = _deinterleave_matrix()
    ret_f, ret_b, na_k, na_v = [], [], [], []
    for layer in range(DEPTH):
        mod = mod_all[layer]
        j = layer // 2
        if layer % 2 == 0:
            proj = _inproj(x, norm_mix[layer], mod, ret_w_in[j].astype(BF16))
            decay = jnp.stack([ret_decay_fwd[j], ret_decay_bwd[j]])
            a_p, s_f, s_b = _ret_prompt(proj, decay)
            a_s = _ret_sample(proj, decay, cos, sin_signed, state_ret_fwd, state_ret_bwd, j)
            ret_f.append(s_f)
            ret_b.append(s_b)
            w_out = ret_w_out[j]
        else:
            proj = _inproj(x, norm_mix[layer], mod, na_w_in[j].astype(BF16))
            a_p = _ctx_attn(proj)
            a_s = _na_attn(proj, cache_na_k, cache_na_v, _na_bias_table(na_rpb[j]), j)
            na_k.append(proj[:N_PROMPT, D_MODEL:2 * D_MODEL].reshape(BATCH, SEQ, NA_HEADS, NA_HD))
            na_v.append(proj[:N_PROMPT, 2 * D_MODEL:].reshape(BATCH, SEQ, NA_HEADS, NA_HD))
            w_out = na_w_out[j]
        a = jnp.concatenate([a_p, a_s], axis=0)

        wr = jnp.pad(moe_w_router[layer], ((0, 0), (0, LANES - N_EXPERTS)))
        wr_hi, wr_lo = _split_hi_lo(wr)
        br = jnp.pad(moe_b_router[layer], (0, LANES - N_EXPERTS)).reshape(1, LANES)
        x, tok, logits = _outproj(a, w_out.astype(BF16), x, norm_ffn[layer], mod, wr_hi, wr_lo, br)

        idx, gates, rank, counts = _router(logits)
        pos_flat, src, tile_expert, tile_valid = _routing_tables(idx, rank, counts)
        xs = _dispatch(tok, src)
        b1 = moe_b1[layer]
        b1_split = jnp.concatenate([b1[:, 0::2], b1[:, 1::2]], axis=-1).reshape(N_EXPERTS, 1, 2 * D_EXPERT)
        ys = _moe_experts(tile_expert, tile_valid, xs, moe_w1[layer], b1_split, moe_w2[layer],
                          moe_b2[layer].reshape(N_EXPERTS, 1, D_MODEL), perm)
        x = _combine(pos_flat, ys, gates, x, mod, norm_final, final=(layer == DEPTH - 1))

    y_prompt = x[:N_PROMPT].reshape(BATCH, SEQ, D_MODEL)
    y_sample = x[N_PROMPT:].reshape(DEC_BATCH, DEC_SEQ, D_MODEL)
    return (y_prompt, y_sample, jnp.stack(ret_f, axis=1), jnp.stack(ret_b, axis=1),
            jnp.stack(na_k, axis=1), jnp.stack(na_v, axis=1))
```

```python
import functools

import numpy as np
import jax
import jax.numpy as jnp
from jax import lax
from jax.experimental import pallas as pl
from jax.experimental.pallas import tpu as pltpu

F32 = jnp.float32
BF16 = jnp.bfloat16
I32 = jnp.int32

D_MODEL = 1024
BATCH = 32
SEQ = 256
DEPTH = 4
DEC_BATCH = 2
DEC_SEQ = 1024
PAST_LEN = 512
GRID_W = 64
EPS = 1e-6
ROPE_BASE = 10000.0
RET_HEADS = 4
RET_DK = D_MODEL // RET_HEADS
RET_DV = 2 * RET_DK
RET_QK_W = RET_HEADS * RET_DK
RET_V_W = RET_HEADS * RET_DV
NA_HEADS = 16
NA_HD = D_MODEL // NA_HEADS
NA_KH = 8
NA_KW = 16
N_EXPERTS = 32
TOP_K = 4
D_EXPERT = D_MODEL
SWIGLU_LIMIT = 7.0
SWIGLU_ALPHA = 1.702
N_RET_LAYERS = (DEPTH + 1) // 2
N_NA_LAYERS = DEPTH // 2

N_PROMPT = BATCH * SEQ
N_SAMPLE = DEC_BATCH * DEC_SEQ
N_TOK = N_PROMPT + N_SAMPLE
N_GROUPS = 1 + DEC_BATCH
MOD_ROWS = 8
LANES = 128

TM = 256
RET_CHUNK = 256
MOE_TM = 256
MOE_ROWS = N_TOK * TOP_K + N_EXPERTS * MOE_TM
MOE_TILES = MOE_ROWS // MOE_TM
ROUTER_TM = 512
PERM_W = 256
VMEM_LIMIT = 56 * 1024 * 1024


def _cparams(n_axes, vmem=VMEM_LIMIT):
    return pltpu.CompilerParams(dimension_semantics=("arbitrary",) * n_axes, vmem_limit_bytes=vmem)


def _group_of_tile(i, tm):
    return jnp.maximum((i * tm) // DEC_SEQ - (N_PROMPT // DEC_SEQ - 1), 0)


def _sigmoid(x):
    return 1.0 / (1.0 + jnp.exp(-x))


def _silu(x):
    return x * _sigmoid(x)


def _log_sigmoid(x):
    return jnp.minimum(x, 0.0) - jnp.log(1.0 + jnp.exp(-jnp.abs(x)))


def _norm_mod(x, g, shift, scale):
    y = x * lax.rsqrt(jnp.mean(x * x, axis=-1, keepdims=True) + EPS)
    return (y * g) * (1.0 + scale) + shift


def _dot(a, b):
    return jnp.dot(a, b, preferred_element_type=F32)


def _dot_nt(a, b):
    return lax.dot_general(a, b, (((1,), (1,)), ((), ())), preferred_element_type=F32)


def _dot_tn(a, b):
    return lax.dot_general(a, b, (((0,), (0,)), ((), ())), preferred_element_type=F32)


ADA_TN = 1536


def _adaln_body(cond_ref, w_ref, b_ref, o_ref):
    s = _silu(cond_ref[...]).astype(BF16)
    o_ref[...] = _dot(s, w_ref[...].astype(BF16)) + b_ref[...]


def _adaln(cond, ada_w, ada_b):
    n_out = 6 * D_MODEL
    return pl.pallas_call(
        _adaln_body,
        grid=(DEPTH, n_out // ADA_TN),
        in_specs=[
            pl.BlockSpec((MOD_ROWS, D_MODEL), lambda l, j: (0, 0)),
            pl.BlockSpec((None, D_MODEL, ADA_TN), lambda l, j: (l, 0, j)),
            pl.BlockSpec((None, 1, ADA_TN), lambda l, j: (l, 0, j)),
        ],
        out_specs=pl.BlockSpec((None, MOD_ROWS, ADA_TN), lambda l, j: (l, 0, j)),
        out_shape=jax.ShapeDtypeStruct((DEPTH, MOD_ROWS, n_out), F32),
        compiler_params=_cparams(2),
        name="adaln",
    )(cond, ada_w, ada_b.reshape(DEPTH, 1, n_out))


CAST_TR = 256


def _cast_body(w_ref, o_ref):
    o_ref[...] = w_ref[...].astype(BF16)


def _cast_layer_bf16(w, j):
    _, r, c = w.shape
    return pl.pallas_call(
        _cast_body,
        grid=(r // CAST_TR,),
        in_specs=[pl.BlockSpec((None, CAST_TR, c), lambda i: (j, i, 0))],
        out_specs=pl.BlockSpec((CAST_TR, c), lambda i: (i, 0)),
        out_shape=jax.ShapeDtypeStruct((r, c), BF16),
        compiler_params=_cparams(1),
        name="cast_bf16",
    )(w)


def _inproj_body(x_ref, g_ref, mod_ref, w_ref, *rest, cache_layer, first_cache):
    o_ref = rest[-3] if cache_layer is not None else rest[-1]
    i = pl.program_id(0)
    h = _norm_mod(x_ref[...], g_ref[...], mod_ref[0:1, :], mod_ref[1:2, :]).astype(BF16)
    for c in range(o_ref.shape[1] // D_MODEL):
        sl = slice(c * D_MODEL, (c + 1) * D_MODEL)
        r = _dot(h, w_ref[:, sl])
        o_ref[:, sl] = r
        if cache_layer is not None and c >= 1:
            cache_ref = rest[-3 + c]

            @pl.when(i < BATCH)
            def _(r=r, cache_ref=cache_ref):
                if first_cache:
                    for l in range(N_NA_LAYERS):
                        cache_ref[l] = r if l == cache_layer else jnp.zeros_like(r)
                else:
                    cache_ref[...] = r


def _inproj(x, g, mod, w_bf16, cache_layer=None, prev_caches=None):
    n_out = w_bf16.shape[1]
    in_specs = [
        pl.BlockSpec((TM, D_MODEL), lambda i: (i, 0)),
        pl.BlockSpec((1, D_MODEL), lambda i: (0, 0)),
        pl.BlockSpec((None, MOD_ROWS, D_MODEL), lambda i: (_group_of_tile(i, TM), 0, 0)),
        pl.BlockSpec((D_MODEL, n_out), lambda i: (0, 0)),
    ]
    out_specs = [pl.BlockSpec((TM, n_out), lambda i: (i, 0))]
    out_shape = [jax.ShapeDtypeStruct((N_TOK, n_out), F32)]
    args = [x, g.reshape(1, D_MODEL), mod, w_bf16]
    aliases = {}
    first_cache = prev_caches is None
    if cache_layer is not None:
        assert TM == SEQ and n_out == 3 * D_MODEL
        cache = jax.ShapeDtypeStruct((BATCH, N_NA_LAYERS, SEQ, D_MODEL), F32)
        if first_cache:
            spec = pl.BlockSpec((None, N_NA_LAYERS, SEQ, D_MODEL), lambda i: (jnp.minimum(i, BATCH - 1), 0, 0, 0))
        else:
            spec = pl.BlockSpec((None, None, SEQ, D_MODEL),
                                lambda i: (jnp.minimum(i, BATCH - 1), cache_layer, 0, 0))
            in_specs += [pl.BlockSpec(memory_space=pl.ANY)] * 2
            aliases = {len(args): 1, len(args) + 1: 2}
            args += list(prev_caches)
        out_specs += [spec, spec]
        out_shape += [cache, cache]
    outs = pl.pallas_call(
        functools.partial(_inproj_body, cache_layer=cache_layer, first_cache=first_cache),
        grid=(N_TOK // TM,),
        in_specs=in_specs,
        out_specs=out_specs,
        out_shape=out_shape,
        input_output_aliases=aliases,
        compiler_params=_cparams(1),
        name="inproj",
    )(*args)
    return outs[0] if cache_layer is None else outs


def _decay_terms(dec_ref, h, c):
    lgf = _log_sigmoid(jnp.zeros((c, c), F32) + dec_ref[0, h])
    lgb = _log_sigmoid(jnp.zeros((c, c), F32) + dec_ref[1, h])
    row = lax.broadcasted_iota(I32, (c, c), 0)
    col = lax.broadcasted_iota(I32, (c, c), 1)
    diff = (row - col).astype(F32)
    lower = diff >= 0
    upper = diff <= 0
    mask = (jnp.where(lower, jnp.exp(jnp.where(lower, diff, 0.0) * lgf), 0.0)
            + jnp.where(upper, jnp.exp(jnp.where(upper, -diff, 0.0) * lgb), 0.0))
    pos = lax.broadcasted_iota(I32, (c, 1), 0).astype(F32)
    lgf1 = _log_sigmoid(jnp.zeros((c, 1), F32) + dec_ref[0, h])
    lgb1 = _log_sigmoid(jnp.zeros((c, 1), F32) + dec_ref[1, h])
    lgf0 = _log_sigmoid(jnp.zeros((1, 1), F32) + dec_ref[0, h])
    lgb0 = _log_sigmoid(jnp.zeros((1, 1), F32) + dec_ref[1, h])
    terms = dict(
        mask=mask,
        q_f=jnp.exp((pos + 1.0) * lgf1), k_f=jnp.exp((c - 1.0 - pos) * lgf1), c_f=jnp.exp(c * lgf0),
        q_b=jnp.exp((c - pos) * lgb1), k_b=jnp.exp(pos * lgb1), c_b=jnp.exp(c * lgb0),
    )
    return terms


def _group_norm_gate(o, g):
    o = o * lax.rsqrt(jnp.mean(o * o, axis=-1, keepdims=True) + EPS)
    return (_silu(g) * o).astype(BF16)


def _ret_prompt_body(dec_ref, q_ref, k_ref, v_ref, g_ref, *rest, layer, first):
    o_ref, sf_ref, sb_ref = rest[-3:]
    h = pl.program_id(1)
    t = _decay_terms(dec_ref, h, SEQ)
    q = q_ref[...]
    k = k_ref[...] * (RET_DK ** -0.5)
    v = v_ref[...].astype(BF16)
    scores = _dot_nt(q.astype(BF16), k.astype(BF16)) * t["mask"]
    o = _dot(scores.astype(BF16), v)
    o_ref[...] = _group_norm_gate(o, g_ref[...])
    for ref, k_dec in ((sf_ref, t["k_f"]), (sb_ref, t["k_b"])):
        s = _dot_tn((k * k_dec).astype(BF16), v)
        if first:
            for l in range(N_RET_LAYERS):
                ref[l] = s if l == layer else jnp.zeros_like(s)
        else:
            ref[...] = s


def _ret_prompt(proj, decay, layer, prev_states=None):
    kq = RET_QK_W // RET_DK
    kv = 2 * RET_QK_W // RET_DV
    kg = kv + RET_HEADS
    state = jax.ShapeDtypeStruct((BATCH, N_RET_LAYERS, RET_HEADS, RET_DK, RET_DV), F32)
    in_specs = [
        pl.BlockSpec(memory_space=pltpu.SMEM),
        pl.BlockSpec((SEQ, RET_DK), lambda b, h: (b, h)),
        pl.BlockSpec((SEQ, RET_DK), lambda b, h: (b, kq + h)),
        pl.BlockSpec((SEQ, RET_DV), lambda b, h: (b, kv + h)),
        pl.BlockSpec((SEQ, RET_DV), lambda b, h: (b, kg + h)),
    ]
    args = [decay, proj, proj, proj, proj]
    first = prev_states is None
    aliases = {}
    if first:
        s_spec = pl.BlockSpec((None, N_RET_LAYERS, None, RET_DK, RET_DV), lambda b, h: (b, 0, h, 0, 0))
    else:
        s_spec = pl.BlockSpec((None, None, None, RET_DK, RET_DV), lambda b, h: (b, layer, h, 0, 0))
        in_specs += [pl.BlockSpec(memory_space=pl.ANY)] * 2
        aliases = {len(args): 1, len(args) + 1: 2}
        args += list(prev_states)
    return pl.pallas_call(
        functools.partial(_ret_prompt_body, layer=layer, first=first),
        grid=(BATCH, RET_HEADS),
        in_specs=in_specs,
        out_specs=[pl.BlockSpec((SEQ, RET_DV), lambda b, h: (b, h)), s_spec, s_spec],
        out_shape=[jax.ShapeDtypeStruct((N_PROMPT, RET_V_W), BF16), state, state],
        input_output_aliases=aliases,
        compiler_params=_cparams(2),
        name="ret_prompt",
    )(*args)


def _rope(x, cos, sin_signed):
    n = x.shape[-1]
    lane = lax.broadcasted_iota(I32, x.shape, 1)
    swapped = jnp.where(lane % 2 == 0, pltpu.roll(x, n - 1, 1), pltpu.roll(x, 1, 1))
    return x * cos + swapped * sin_signed


def _ret_sample_body(dec_ref, q_ref, k_ref, v_ref, g_ref, cos_ref, sin_ref, s0f_ref, s0b_ref, o_ref,
                     sf_scr, sb_scr):
    h = pl.program_id(1)
    c = RET_CHUNK
    n_chunks = DEC_SEQ // c
    t = _decay_terms(dec_ref, h, c)

    def chunk(ref, i):
        return ref[i * c:(i + 1) * c, :]

    def qk(i):
        q = _rope(chunk(q_ref, i), chunk(cos_ref, i), chunk(sin_ref, i))
        k = _rope(chunk(k_ref, i) * (RET_DK ** -0.5), chunk(cos_ref, i), chunk(sin_ref, i))
        return q, k

    s = s0f_ref[...]
    for i in range(n_chunks):
        sf_scr[i] = s
        if i + 1 < n_chunks:
            _, k = qk(i)
            s = s * t["c_f"] + _dot_tn((k * t["k_f"]).astype(BF16), chunk(v_ref, i).astype(BF16))
    s = s0b_ref[...]
    for i in reversed(range(n_chunks)):
        sb_scr[i] = s
        if i > 0:
            _, k = qk(i)
            s = s * t["c_b"] + _dot_tn((k * t["k_b"]).astype(BF16), chunk(v_ref, i).astype(BF16))

    for i in range(n_chunks):
        q, k = qk(i)
        scores = _dot_nt(q.astype(BF16), k.astype(BF16)) * t["mask"]
        o = _dot(scores.astype(BF16), chunk(v_ref, i).astype(BF16))
        o = o + _dot((q * t["q_f"]).astype(BF16), sf_scr[i].astype(BF16))
        o = o + _dot((q * t["q_b"]).astype(BF16), sb_scr[i].astype(BF16))
        o_ref[i * c:(i + 1) * c, :] = _group_norm_gate(o, chunk(g_ref, i))


def _ret_sample(proj, decay, cos, sin_signed, s0f, s0b, j):
    kq = RET_QK_W // RET_DK
    kv = 2 * RET_QK_W // RET_DV
    kg = kv + RET_HEADS
    rb = N_PROMPT // DEC_SEQ
    n_chunks = DEC_SEQ // RET_CHUNK
    return pl.pallas_call(
        _ret_sample_body,
        grid=(DEC_BATCH, RET_HEADS),
        in_specs=[
            pl.BlockSpec(memory_space=pltpu.SMEM),
            pl.BlockSpec((DEC_SEQ, RET_DK), lambda b, h: (rb + b, h)),
            pl.BlockSpec((DEC_SEQ, RET_DK), lambda b, h: (rb + b, kq + h)),
            pl.BlockSpec((DEC_SEQ, RET_DV), lambda b, h: (rb + b, kv + h)),
            pl.BlockSpec((DEC_SEQ, RET_DV), lambda b, h: (rb + b, kg + h)),
            pl.BlockSpec((DEC_SEQ, RET_DK), lambda b, h: (0, 0)),
            pl.BlockSpec((DEC_SEQ, RET_DK), lambda b, h: (0, 0)),
            pl.BlockSpec((None, None, None, RET_DK, RET_DV), lambda b, h: (b, j, h, 0, 0)),
            pl.BlockSpec((None, None, None, RET_DK, RET_DV), lambda b, h: (b, j, h, 0, 0)),
        ],
        out_specs=pl.BlockSpec((DEC_SEQ, RET_DV), lambda b, h: (b, h)),
        out_shape=jax.ShapeDtypeStruct((N_SAMPLE, RET_V_W), BF16),
        scratch_shapes=[pltpu.VMEM((n_chunks, RET_DK, RET_DV), F32),
                        pltpu.VMEM((n_chunks, RET_DK, RET_DV), F32)],
        compiler_params=_cparams(2),
        name="ret_sample",
    )(decay, proj, proj, proj, proj, cos, sin_signed, s0f, s0b)


def _softmax_pv(parts):
    m = parts[0][0].max(axis=-1, keepdims=True)
    for s, _ in parts[1:]:
        m = jnp.maximum(m, s.max(axis=-1, keepdims=True))
    es = [jnp.exp(s - m) for s, _ in parts]
    denom = es[0].sum(axis=-1, keepdims=True)
    for e in es[1:]:
        denom = denom + e.sum(axis=-1, keepdims=True)
    inv = 1.0 / denom
    o = _dot((es[0] * inv).astype(BF16), parts[0][1])
    for e, (_, v) in zip(es[1:], parts[1:]):
        o = o + _dot((e * inv).astype(BF16), v)
    return o


def _ctx_attn_body(q_ref, k_ref, v_ref, o_ref):
    outs = []
    for h in range(NA_HEADS):
        sl = slice(h * NA_HD, (h + 1) * NA_HD)
        q = (q_ref[:, sl] * (NA_HD ** -0.5)).astype(BF16)
        s = _dot_nt(q, k_ref[:, sl].astype(BF16))
        outs.append(_softmax_pv([(s, v_ref[:, sl].astype(BF16))]))
    o_ref[...] = jnp.concatenate(outs, axis=-1).astype(BF16)


def _ctx_attn(proj):
    return pl.pallas_call(
        _ctx_attn_body,
        grid=(BATCH,),
        in_specs=[
            pl.BlockSpec((SEQ, D_MODEL), lambda b: (b, 0)),
            pl.BlockSpec((SEQ, D_MODEL), lambda b: (b, 1)),
            pl.BlockSpec((SEQ, D_MODEL), lambda b: (b, 2)),
        ],
        out_specs=pl.BlockSpec((SEQ, D_MODEL), lambda b: (b, 0)),
        out_shape=jax.ShapeDtypeStruct((N_PROMPT, D_MODEL), BF16),
        compiler_params=_cparams(1),
        name="ctx_attn",
    )(proj, proj, proj)


NA_QT = 256
NA_HPB = LANES // NA_HD
NA_WIN = NA_KH * GRID_W


def _na_attn_body(q_ref, k_ref, v_ref, ck_ref, cv_ref, tbl_ref, o_ref):
    t = pl.program_id(1)
    rows_per_tile = NA_QT // GRID_W
    n_rows = DEC_SEQ // GRID_W
    heads = [slice(hh * NA_HD, (hh + 1) * NA_HD) for hh in range(NA_HPB)]
    ck = [ck_ref[:, sl].astype(BF16) for sl in heads]
    cv = [cv_ref[:, sl].astype(BF16) for sl in heads]
    for u in range(rows_per_tile):
        qr = t * rows_per_tile + u
        rs = jnp.clip(qr - NA_KH // 2, 0, n_rows - NA_KH)
        i0 = rs - qr + (NA_KH - 1)
        start = pl.multiple_of(rs * GRID_W, GRID_W)
        kw = k_ref[pl.ds(start, NA_WIN), :].astype(BF16)
        vw = v_ref[pl.ds(start, NA_WIN), :].astype(BF16)
        rows = slice(u * GRID_W, (u + 1) * GRID_W)
        outs = []
        for hh, sl in enumerate(heads):
            q = (q_ref[rows, sl] * (NA_HD ** -0.5)).astype(BF16)
            s_win = _dot_nt(q, kw[:, sl]) + tbl_ref[hh, i0]
            s_ctx = _dot_nt(q, ck[hh])
            outs.append(_softmax_pv([(s_win, vw[:, sl]), (s_ctx, cv[hh])]))
        o_ref[rows, :] = jnp.concatenate(outs, axis=-1).astype(BF16)


def _na_attn(proj, cache_k, cache_v, tbl, j):
    n_qt = DEC_SEQ // NA_QT
    rb_q = N_PROMPT // NA_QT
    rb_kv = N_PROMPT // DEC_SEQ
    cb = D_MODEL // LANES
    ck = cache_k.reshape(DEC_BATCH, -1, PAST_LEN, D_MODEL)
    cv = cache_v.reshape(DEC_BATCH, -1, PAST_LEN, D_MODEL)
    return pl.pallas_call(
        _na_attn_body,
        grid=(cb, n_qt, DEC_BATCH),
        in_specs=[
            pl.BlockSpec((NA_QT, LANES), lambda p, t, b: (rb_q + b * n_qt + t, p)),
            pl.BlockSpec((DEC_SEQ, LANES), lambda p, t, b: (rb_kv + b, cb + p)),
            pl.BlockSpec((DEC_SEQ, LANES), lambda p, t, b: (rb_kv + b, 2 * cb + p)),
            pl.BlockSpec((None, None, PAST_LEN, LANES), lambda p, t, b: (b, j, 0, p)),
            pl.BlockSpec((None, None, PAST_LEN, LANES), lambda p, t, b: (b, j, 0, p)),
            pl.BlockSpec((NA_HPB, NA_KH, GRID_W, NA_WIN), lambda p, t, b: (p, 0, 0, 0)),
        ],
        out_specs=pl.BlockSpec((NA_QT, LANES), lambda p, t, b: (b * n_qt + t, p)),
        out_shape=jax.ShapeDtypeStruct((N_SAMPLE, D_MODEL), BF16),
        compiler_params=_cparams(3),
        name="na_attn",
    )(proj, proj, proj, ck, cv, tbl)


def _na_bias_table(rpb):
    c = np.arange(GRID_W)
    cs = np.clip(c - NA_KW // 2, 0, GRID_W - NA_KW)
    col_valid = (c[None, :] >= cs[:, None]) & (c[None, :] < cs[:, None] + NA_KW)
    col_off = np.clip(c[None, :] - c[:, None] + NA_KW - 1, 0, 2 * NA_KW - 2)
    sel_c = (col_off[..., None] == np.arange(2 * NA_KW - 1)).astype(np.float32)
    t = jnp.einsum("hij,qkj->hqik", rpb, sel_c, precision=lax.Precision.HIGHEST)
    t = jnp.where(col_valid[None, :, None, :], t, -jnp.inf)
    return jnp.stack([t[:, :, i0:i0 + NA_KH, :].reshape(NA_HEADS, GRID_W, NA_WIN) for i0 in range(NA_KH)], axis=1)


def _outproj_body(ap_ref, as_ref, w_ref, x_ref, g_ref, mod_ref, wr_ref, br_ref, xo_ref, tok_ref, lg_ref):
    i = pl.program_id(0)
    a = jnp.where(i < N_PROMPT // TM, ap_ref[...], as_ref[...])
    x = x_ref[...] + mod_ref[2:3, :] * _dot(a, w_ref[...])
    xo_ref[...] = x
    tok = _norm_mod(x, g_ref[...], mod_ref[3:4, :], mod_ref[4:5, :])
    tok_ref[...] = tok
    hi = tok.astype(BF16)
    lo = (tok - hi.astype(F32)).astype(BF16)
    wr = wr_ref[...]
    wr_hi = wr.astype(BF16)
    wr_lo = (wr - wr_hi.astype(F32)).astype(BF16)
    lg_ref[...] = _dot(hi, wr_hi) + _dot(lo, wr_hi) + _dot(hi, wr_lo) + br_ref[...]


def _outproj(a_prompt, a_sample, w_bf16, x, g, mod, wr, br):
    k_in = a_prompt.shape[1]
    n_pt = N_PROMPT // TM
    return pl.pallas_call(
        _outproj_body,
        grid=(N_TOK // TM,),
        in_specs=[
            pl.BlockSpec((TM, k_in), lambda i: (jnp.minimum(i, n_pt - 1), 0)),
            pl.BlockSpec((TM, k_in), lambda i: (jnp.maximum(i - n_pt, 0), 0)),
            pl.BlockSpec((k_in, D_MODEL), lambda i: (0, 0)),
            pl.BlockSpec((TM, D_MODEL), lambda i: (i, 0)),
            pl.BlockSpec((1, D_MODEL), lambda i: (0, 0)),
            pl.BlockSpec((None, MOD_ROWS, D_MODEL), lambda i: (_group_of_tile(i, TM), 0, 0)),
            pl.BlockSpec((D_MODEL, LANES), lambda i: (0, 0)),
            pl.BlockSpec((1, LANES), lambda i: (0, 0)),
        ],
        out_specs=[
            pl.BlockSpec((TM, D_MODEL), lambda i: (i, 0)),
            pl.BlockSpec((TM, D_MODEL), lambda i: (i, 0)),
            pl.BlockSpec((TM, LANES), lambda i: (i, 0)),
        ],
        out_shape=[
            jax.ShapeDtypeStruct((N_TOK, D_MODEL), F32),
            jax.ShapeDtypeStruct((N_TOK, D_MODEL), F32),
            jax.ShapeDtypeStruct((N_TOK, LANES), F32),
        ],
        compiler_params=_cparams(1),
        name="outproj",
    )(a_prompt, a_sample, w_bf16, x, g.reshape(1, D_MODEL), mod, wr, br)


def _router_body(lg_ref, idx_ref, gate_ref, rank_ref, cnt_ref, carry):
    i = pl.program_id(0)

    @pl.when(i == 0)
    def _():
        carry[...] = jnp.zeros_like(carry)

    lane = lax.broadcasted_iota(I32, (ROUTER_TM, LANES), 1)
    l = jnp.where(lane < N_EXPERTS, lg_ref[...], -jnp.inf)
    vals, idxs, hots = [], [], []
    for _ in range(TOP_K):
        m = l.max(axis=-1, keepdims=True)
        idx = jnp.where(l == m, lane, LANES).min(axis=-1, keepdims=True)
        hot = lane == idx
        l = jnp.where(hot, -jnp.inf, l)
        vals.append(m)
        idxs.append(idx)
        hots.append(hot)
    es = [jnp.exp(v - vals[0]) for v in vals]
    denom = es[0] + es[1] + es[2] + es[3]
    hot_all = (hots[0] | hots[1] | hots[2] | hots[3]).astype(F32)
    r = lax.broadcasted_iota(I32, (ROUTER_TM, ROUTER_TM), 0)
    c = lax.broadcasted_iota(I32, (ROUTER_TM, ROUTER_TM), 1)
    before = _dot((c < r).astype(BF16), hot_all.astype(BF16)) + carry[...]
    idx_out = jnp.zeros((ROUTER_TM, LANES), I32)
    gate_out = jnp.zeros((ROUTER_TM, LANES), F32)
    rank_out = jnp.zeros((ROUTER_TM, LANES), I32)
    for k in range(TOP_K):
        rank_k = jnp.where(hots[k], before, 0.0).sum(axis=-1, keepdims=True).astype(I32)
        idx_out = jnp.where(lane == k, idxs[k], idx_out)
        gate_out = jnp.where(lane == k, es[k] / denom, gate_out)
        rank_out = jnp.where(lane == k, rank_k, rank_out)
    idx_ref[...] = idx_out
    gate_ref[...] = gate_out
    rank_ref[...] = rank_out
    carry[...] = carry[...] + hot_all.sum(axis=0, keepdims=True)
    cnt_ref[...] = carry[...].astype(I32)


def _router(logits):
    spec = pl.BlockSpec((ROUTER_TM, LANES), lambda i: (i, 0))
    return pl.pallas_call(
        _router_body,
        grid=(N_TOK // ROUTER_TM,),
        in_specs=[spec],
        out_specs=[spec, spec, spec, pl.BlockSpec((1, LANES), lambda i: (0, 0))],
        out_shape=[
            jax.ShapeDtypeStruct((N_TOK, LANES), I32),
            jax.ShapeDtypeStruct((N_TOK, LANES), F32),
            jax.ShapeDtypeStruct((N_TOK, LANES), I32),
            jax.ShapeDtypeStruct((1, LANES), I32),
        ],
        scratch_shapes=[pltpu.VMEM((1, LANES), F32)],
        compiler_params=_cparams(1),
        name="router",
    )(logits)


DISPATCH_TM = 256


def _dispatch_body(src_ref, tok_hbm, o_ref, sem):
    def issue(r, _):
        pltpu.make_async_copy(tok_hbm.at[pl.ds(src_ref[r], 1)], o_ref.at[pl.ds(r, 1)], sem).start()
        return 0

    lax.fori_loop(0, DISPATCH_TM, issue, 0, unroll=8)

    def drain(r, _):
        pltpu.make_async_copy(tok_hbm.at[pl.ds(0, 1)], o_ref.at[pl.ds(r, 1)], sem).wait()
        return 0

    lax.fori_loop(0, DISPATCH_TM, drain, 0, unroll=8)


def _dispatch(tok, src_rows):
    w = tok.shape[1]
    return pl.pallas_call(
        _dispatch_body,
        grid=(MOE_ROWS // DISPATCH_TM,),
        in_specs=[
            pl.BlockSpec((DISPATCH_TM,), lambda i: (i,), memory_space=pltpu.SMEM),
            pl.BlockSpec(memory_space=pl.ANY),
        ],
        out_specs=pl.BlockSpec((DISPATCH_TM, w), lambda i: (i, 0)),
        out_shape=jax.ShapeDtypeStruct((MOE_ROWS, w), tok.dtype),
        scratch_shapes=[pltpu.SemaphoreType.DMA],
        compiler_params=_cparams(1),
        name="dispatch",
    )(src_rows, tok)


def _moe_body(te_ref, tv_ref, x_ref, w1_ref, b1_ref, w2_ref, b2_ref, perm_ref, o_ref, w1_scr, w2_scr):
    t = pl.program_id(0)
    first = jnp.logical_or(t == 0, te_ref[t] != te_ref[jnp.maximum(t - 1, 0)])

    @pl.when(jnp.logical_and(tv_ref[t] == 1, first))
    def _():
        w2_scr[...] = w2_ref[...].astype(BF16)
        half = PERM_W // 2
        for c in range(2 * D_EXPERT // PERM_W):
            blk = _dot(w1_ref[:, c * PERM_W:(c + 1) * PERM_W].astype(BF16), perm_ref[...]).astype(BF16)
            w1_scr[:, c * half:(c + 1) * half] = blk[:, :half]
            w1_scr[:, D_EXPERT + c * half:D_EXPERT + (c + 1) * half] = blk[:, half:]

    @pl.when(tv_ref[t] == 0)
    def _():
        o_ref[...] = jnp.zeros_like(o_ref)

    @pl.when(tv_ref[t] == 1)
    def _():
        hh = _dot(x_ref[...].astype(BF16), w1_scr[...]) + b1_ref[...]
        h_glu = jnp.minimum(hh[:, :D_EXPERT], SWIGLU_LIMIT)
        h_lin = jnp.clip(hh[:, D_EXPERT:], -SWIGLU_LIMIT, SWIGLU_LIMIT)
        a = h_glu * _sigmoid(SWIGLU_ALPHA * h_glu) * (h_lin + 1.0)
        o_ref[...] = _dot(a.astype(BF16), w2_scr[...]) + b2_ref[...]


def _moe_experts(tile_expert, tile_valid, xs, w1, b1_split, w2, b2, perm, layer):
    grid_spec = pltpu.PrefetchScalarGridSpec(
        num_scalar_prefetch=2,
        grid=(MOE_TILES,),
        in_specs=[
            pl.BlockSpec((MOE_TM, D_MODEL), lambda t, te, tv: (t, 0)),
            pl.BlockSpec((None, None, D_MODEL, 2 * D_EXPERT), lambda t, te, tv: (layer, te[t], 0, 0)),
            pl.BlockSpec((None, None, 1, 2 * D_EXPERT), lambda t, te, tv: (layer, te[t], 0, 0)),
            pl.BlockSpec((None, None, D_EXPERT, D_MODEL), lambda t, te, tv: (layer, te[t], 0, 0)),
            pl.BlockSpec((None, None, 1, D_MODEL), lambda t, te, tv: (layer, te[t], 0, 0)),
            pl.BlockSpec((PERM_W, PERM_W), lambda t, te, tv: (0, 0)),
        ],
        out_specs=pl.BlockSpec((MOE_TM, D_MODEL), lambda t, te, tv: (t, 0)),
        scratch_shapes=[pltpu.VMEM((D_MODEL, 2 * D_EXPERT), BF16), pltpu.VMEM((D_EXPERT, D_MODEL), BF16)],
    )
    return pl.pallas_call(
        _moe_body,
        grid_spec=grid_spec,
        out_shape=jax.ShapeDtypeStruct((MOE_ROWS, D_MODEL), F32),
        compiler_params=_cparams(1),
        name="moe_experts",
    )(tile_expert, tile_valid, xs, w1, b1_split, w2, b2, perm)


COMBINE_TM = 128


def _combine_body(pos_ref, y_hbm, gate_ref, x_ref, mod_ref, g_ref, o_ref, buf, sem, *, final):
    def issue(r, _):
        for k in range(TOP_K):
            pltpu.make_async_copy(y_hbm.at[pl.ds(pos_ref[r * TOP_K + k], 1)], buf.at[k, pl.ds(r, 1)], sem).start()
        return 0

    lax.fori_loop(0, COMBINE_TM, issue, 0)

    def drain(r, _):
        for k in range(TOP_K):
            pltpu.make_async_copy(y_hbm.at[pl.ds(0, 1)], buf.at[k, pl.ds(r, 1)], sem).wait()
        return 0

    lax.fori_loop(0, COMBINE_TM, drain, 0)
    gates = gate_ref[...]
    y = gates[:, 0:1] * buf[0]
    for k in range(1, TOP_K):
        y = y + gates[:, k:k + 1] * buf[k]
    x = x_ref[...] + mod_ref[5:6, :] * y
    if final:
        x = (x * lax.rsqrt(jnp.mean(x * x, axis=-1, keepdims=True) + EPS)) * g_ref[...]
    o_ref[...] = x


def _combine(pos_flat, y_sorted, gates, x, mod, g_final, final):
    return pl.pallas_call(
        functools.partial(_combine_body, final=final),
        grid=(N_TOK // COMBINE_TM,),
        in_specs=[
            pl.BlockSpec((COMBINE_TM * TOP_K,), lambda i: (i,), memory_space=pltpu.SMEM),
            pl.BlockSpec(memory_space=pl.ANY),
            pl.BlockSpec((COMBINE_TM, LANES), lambda i: (i, 0)),
            pl.BlockSpec((COMBINE_TM, D_MODEL), lambda i: (i, 0)),
            pl.BlockSpec((None, MOD_ROWS, D_MODEL), lambda i: (_group_of_tile(i, COMBINE_TM), 0, 0)),
            pl.BlockSpec((1, D_MODEL), lambda i: (0, 0)),
        ],
        out_specs=pl.BlockSpec((COMBINE_TM, D_MODEL), lambda i: (i, 0)),
        out_shape=jax.ShapeDtypeStruct((N_TOK, D_MODEL), F32),
        scratch_shapes=[pltpu.VMEM((TOP_K, COMBINE_TM, D_MODEL), F32), pltpu.SemaphoreType.DMA],
        compiler_params=_cparams(1),
        name="combine",
    )(pos_flat, y_sorted, gates, x, mod, g_final.reshape(1, D_MODEL))


def _routing_tables(idx, rank, counts):
    counts = counts[0, :N_EXPERTS]
    padded = ((counts + MOE_TM - 1) // MOE_TM) * MOE_TM
    ends = jnp.cumsum(padded)
    offsets = ends - padded
    idx = idx[:, :TOP_K]
    pos = rank[:, :TOP_K] + jnp.sum(
        jnp.where(idx[..., None] == jnp.arange(N_EXPERTS, dtype=I32), offsets, 0), axis=-1)
    pos_flat = pos.reshape(-1).astype(I32)
    token = jnp.repeat(jnp.arange(N_TOK, dtype=I32), TOP_K)
    src = jnp.zeros((MOE_ROWS,), I32).at[pos_flat].set(token)
    tile_start = jnp.arange(MOE_TILES, dtype=I32) * MOE_TM
    tile_valid = (tile_start < ends[-1]).astype(I32)
    tile_expert = jnp.sum((tile_start[:, None] >= ends[None, :]).astype(I32), axis=-1)
    last_expert = jnp.sum((ends[-1] - 1 >= ends).astype(I32))
    tile_expert = jnp.where(tile_valid == 1, tile_expert, last_expert).astype(I32)
    return pos_flat, src, tile_expert, tile_valid


def _rope_tables():
    t = np.arange(DEC_SEQ)
    row = (t // GRID_W).astype(np.float32)
    col = (t % GRID_W).astype(np.float32)
    d_axis = RET_DK // 2
    inv = jnp.asarray(ROPE_BASE, F32) ** (-jnp.arange(0, d_axis, 2, dtype=F32) / d_axis)
    ang = jnp.concatenate([row[:, None] * inv, col[:, None] * inv], axis=-1)
    cos = jnp.repeat(jnp.cos(ang), 2, axis=-1)
    sin = jnp.repeat(jnp.sin(ang), 2, axis=-1)
    sign = jnp.asarray(np.tile(np.array([-1.0, 1.0], np.float32), RET_DK // 2))
    return cos, sin * sign


def _deinterleave_matrix():
    p = np.zeros((PERM_W, PERM_W), np.float32)
    half = PERM_W // 2
    for j in range(half):
        p[2 * j, j] = 1.0
        p[2 * j + 1, half + j] = 1.0
    return jnp.asarray(p, BF16)


def kernel(x_prompt, x_sample, state_ret_fwd, state_ret_bwd, cache_na_k, cache_na_v, c, c_ctx, ada_w, ada_b, norm_mix, norm_ffn, norm_final, ret_w_in, ret_w_out, ret_decay_fwd, ret_decay_bwd, na_w_in, na_w_out, na_rpb, moe_w_router, moe_b_router, moe_w1, moe_b1, moe_w2, moe_b2):
    x = jnp.concatenate([x_prompt.reshape(N_PROMPT, D_MODEL), x_sample.reshape(N_SAMPLE, D_MODEL)], axis=0)

    cond = jnp.zeros((MOD_ROWS, D_MODEL), F32).at[0].set(c_ctx).at[1:1 + DEC_BATCH].set(c)
    mod_all = _adaln(cond, ada_w, ada_b)
    mod_all = mod_all[:, :N_GROUPS].reshape(DEPTH, N_GROUPS, 6, D_MODEL)
    mod_all = jnp.pad(mod_all, ((0, 0), (0, 0), (0, MOD_ROWS - 6), (0, 0)))

    cos, sin_signed = _rope_tables()
    perm = _deinterleave_matrix()
    wr_all = jnp.pad(moe_w_router, ((0, 0), (0, 0), (0, LANES - N_EXPERTS)))
    br_all = jnp.pad(moe_b_router, ((0, 0), (0, LANES - N_EXPERTS))).reshape(DEPTH, 1, LANES)
    b1_split = jnp.concatenate([moe_b1[..., 0::2], moe_b1[..., 1::2]], axis=-1).reshape(DEPTH, N_EXPERTS, 1, 2 * D_EXPERT)
    b2_all = moe_b2.reshape(DEPTH, N_EXPERTS, 1, D_MODEL)
    ret_states = None
    na_caches = None
    for layer in range(DEPTH):
        mod = mod_all[layer]
        j = layer // 2
        if layer % 2 == 0:
            proj = _inproj(x, norm_mix[layer], mod, _cast_layer_bf16(ret_w_in, j))
            decay = jnp.stack([ret_decay_fwd[j], ret_decay_bwd[j]])
            a_p, s_f, s_b = _ret_prompt(proj, decay, j, ret_states)
            ret_states = (s_f, s_b)
            a_s = _ret_sample(proj, decay, cos, sin_signed, state_ret_fwd, state_ret_bwd, j)
            w_out = _cast_layer_bf16(ret_w_out, j)
        else:
            proj, ck, cv = _inproj(x, norm_mix[layer], mod, _cast_layer_bf16(na_w_in, j), j, na_caches)
            na_caches = (ck, cv)
            a_p = _ctx_attn(proj)
            a_s = _na_attn(proj, cache_na_k, cache_na_v, _na_bias_table(na_rpb[j]), j)
            w_out = _cast_layer_bf16(na_w_out, j)

        x, tok, logits = _outproj(a_p, a_s, w_out, x, norm_ffn[layer], mod, wr_all[layer], br_all[layer])

        idx, gates, rank, counts = _router(logits)
        pos_flat, src, tile_expert, tile_valid = _routing_tables(idx, rank, counts)
        xs = _dispatch(tok, src)
        ys = _moe_experts(tile_expert, tile_valid, xs, moe_w1, b1_split, moe_w2, b2_all, perm, layer)
        x = _combine(pos_flat, ys, gates, x, mod, norm_final, final=(layer == DEPTH - 1))

    y_prompt = x[:N_PROMPT].reshape(BATCH, SEQ, D_MODEL)
    y_sample = x[N_PROMPT:].reshape(DEC_BATCH, DEC_SEQ, D_MODEL)
    return (y_prompt, y_sample, ret_states[0], ret_states[1],
            na_caches[0].reshape(BATCH, N_NA_LAYERS, SEQ, NA_HEADS, NA_HD),
            na_caches[1].reshape(BATCH, N_NA_LAYERS, SEQ, NA_HEADS, NA_HD))
```

```python
import functools

import numpy as np
import jax
import jax.numpy as jnp
from jax import lax
from jax.experimental import pallas as pl
from jax.experimental.pallas import tpu as pltpu
from jax.experimental.pallas import tpu_sc as plsc

F32 = jnp.float32
BF16 = jnp.bfloat16
I32 = jnp.int32

D_MODEL = 1024
BATCH = 32
SEQ = 256
DEPTH = 4
DEC_BATCH = 2
DEC_SEQ = 1024
PAST_LEN = 512
GRID_W = 64
EPS = 1e-6
ROPE_BASE = 10000.0
RET_HEADS = 4
RET_DK = D_MODEL // RET_HEADS
RET_DV = 2 * RET_DK
RET_QK_W = RET_HEADS * RET_DK
RET_V_W = RET_HEADS * RET_DV
NA_HEADS = 16
NA_HD = D_MODEL // NA_HEADS
NA_KH = 8
NA_KW = 16
N_EXPERTS = 32
TOP_K = 4
D_EXPERT = D_MODEL
SWIGLU_LIMIT = 7.0
SWIGLU_ALPHA = 1.702
N_RET_LAYERS = (DEPTH + 1) // 2
N_NA_LAYERS = DEPTH // 2

N_PROMPT = BATCH * SEQ
N_SAMPLE = DEC_BATCH * DEC_SEQ
N_TOK = N_PROMPT + N_SAMPLE
N_GROUPS = 1 + DEC_BATCH
MOD_ROWS = 8
LANES = 128

TM = 256
RET_CHUNK = 256
MOE_TM = 256
MOE_ROWS = N_TOK * TOP_K + N_EXPERTS * MOE_TM
MOE_TILES = MOE_ROWS // MOE_TM
ROUTER_TM = 512
PERM_W = 256
VMEM_LIMIT = 56 * 1024 * 1024


def _cparams(n_axes, vmem=VMEM_LIMIT):
    return pltpu.CompilerParams(dimension_semantics=("arbitrary",) * n_axes, vmem_limit_bytes=vmem)


def _group_of_tile(i, tm):
    return jnp.maximum((i * tm) // DEC_SEQ - (N_PROMPT // DEC_SEQ - 1), 0)


def _sigmoid(x):
    return 1.0 / (1.0 + jnp.exp(-x))


def _silu(x):
    return x * _sigmoid(x)


def _log_sigmoid(x):
    return jnp.minimum(x, 0.0) - jnp.log(1.0 + jnp.exp(-jnp.abs(x)))


def _norm_mod(x, g, shift, scale):
    y = x * lax.rsqrt(jnp.mean(x * x, axis=-1, keepdims=True) + EPS)
    return (y * g) * (1.0 + scale) + shift


def _dot(a, b):
    return jnp.dot(a, b, preferred_element_type=F32)


def _dot_nt(a, b):
    return lax.dot_general(a, b, (((1,), (1,)), ((), ())), preferred_element_type=F32)


def _dot_tn(a, b):
    return lax.dot_general(a, b, (((0,), (0,)), ((), ())), preferred_element_type=F32)


ADA_TN = 1536


def _adaln_body(cond_ref, w_ref, b_ref, o_ref):
    s = _silu(cond_ref[...]).astype(BF16)
    o_ref[...] = _dot(s, w_ref[...].astype(BF16)) + b_ref[...]


def _adaln(cond, ada_w, ada_b):
    n_out = 6 * D_MODEL
    return pl.pallas_call(
        _adaln_body,
        grid=(DEPTH, n_out // ADA_TN),
        in_specs=[
            pl.BlockSpec((MOD_ROWS, D_MODEL), lambda l, j: (0, 0)),
            pl.BlockSpec((None, D_MODEL, ADA_TN), lambda l, j: (l, 0, j)),
            pl.BlockSpec((None, 1, ADA_TN), lambda l, j: (l, 0, j)),
        ],
        out_specs=pl.BlockSpec((None, MOD_ROWS, ADA_TN), lambda l, j: (l, 0, j)),
        out_shape=jax.ShapeDtypeStruct((DEPTH, MOD_ROWS, n_out), F32),
        compiler_params=_cparams(2),
        name="adaln",
    )(cond, ada_w, ada_b.reshape(DEPTH, 1, n_out))


CAST_TR = 256


def _cast_body(w_ref, o_ref):
    o_ref[...] = w_ref[...].astype(BF16)


def _cast_layer_bf16(w, j):
    _, r, c = w.shape
    return pl.pallas_call(
        _cast_body,
        grid=(r // CAST_TR,),
        in_specs=[pl.BlockSpec((None, CAST_TR, c), lambda i: (j, i, 0))],
        out_specs=pl.BlockSpec((CAST_TR, c), lambda i: (i, 0)),
        out_shape=jax.ShapeDtypeStruct((r, c), BF16),
        compiler_params=_cparams(1),
        name="cast_bf16",
    )(w)


def _inproj_body(x_ref, g_ref, mod_ref, w_ref, *rest, cache_layer, first_cache):
    o_ref = rest[-3] if cache_layer is not None else rest[-1]
    i = pl.program_id(0)
    h = _norm_mod(x_ref[...], g_ref[...], mod_ref[0:1, :], mod_ref[1:2, :]).astype(BF16)
    for c in range(o_ref.shape[1] // D_MODEL):
        sl = slice(c * D_MODEL, (c + 1) * D_MODEL)
        r = _dot(h, w_ref[:, sl])
        o_ref[:, sl] = r
        if cache_layer is not None and c >= 1:
            cache_ref = rest[-3 + c]

            @pl.when(i < BATCH)
            def _(r=r, cache_ref=cache_ref):
                if first_cache:
                    for l in range(N_NA_LAYERS):
                        cache_ref[l] = r if l == cache_layer else jnp.zeros_like(r)
                else:
                    cache_ref[...] = r


def _inproj(x, g, mod, w_bf16, cache_layer=None, prev_caches=None):
    n_out = w_bf16.shape[1]
    in_specs = [
        pl.BlockSpec((TM, D_MODEL), lambda i: (i, 0)),
        pl.BlockSpec((1, D_MODEL), lambda i: (0, 0)),
        pl.BlockSpec((None, MOD_ROWS, D_MODEL), lambda i: (_group_of_tile(i, TM), 0, 0)),
        pl.BlockSpec((D_MODEL, n_out), lambda i: (0, 0)),
    ]
    out_specs = [pl.BlockSpec((TM, n_out), lambda i: (i, 0))]
    out_shape = [jax.ShapeDtypeStruct((N_TOK, n_out), F32)]
    args = [x, g.reshape(1, D_MODEL), mod, w_bf16]
    aliases = {}
    first_cache = prev_caches is None
    if cache_layer is not None:
        assert TM == SEQ and n_out == 3 * D_MODEL
        cache = jax.ShapeDtypeStruct((BATCH, N_NA_LAYERS, SEQ, D_MODEL), F32)
        if first_cache:
            spec = pl.BlockSpec((None, N_NA_LAYERS, SEQ, D_MODEL), lambda i: (jnp.minimum(i, BATCH - 1), 0, 0, 0))
        else:
            spec = pl.BlockSpec((None, None, SEQ, D_MODEL),
                                lambda i: (jnp.minimum(i, BATCH - 1), cache_layer, 0, 0))
            in_specs += [pl.BlockSpec(memory_space=pl.ANY)] * 2
            aliases = {len(args): 1, len(args) + 1: 2}
            args += list(prev_caches)
        out_specs += [spec, spec]
        out_shape += [cache, cache]
    outs = pl.pallas_call(
        functools.partial(_inproj_body, cache_layer=cache_layer, first_cache=first_cache),
        grid=(N_TOK // TM,),
        in_specs=in_specs,
        out_specs=out_specs,
        out_shape=out_shape,
        input_output_aliases=aliases,
        compiler_params=_cparams(1),
        name="inproj",
    )(*args)
    return outs[0] if cache_layer is None else outs


def _decay_terms(dec_ref, h, c):
    lgf = _log_sigmoid(jnp.zeros((c, c), F32) + dec_ref[0, h])
    lgb = _log_sigmoid(jnp.zeros((c, c), F32) + dec_ref[1, h])
    row = lax.broadcasted_iota(I32, (c, c), 0)
    col = lax.broadcasted_iota(I32, (c, c), 1)
    diff = (row - col).astype(F32)
    lower = diff >= 0
    upper = diff <= 0
    mask = (jnp.where(lower, jnp.exp(jnp.where(lower, diff, 0.0) * lgf), 0.0)
            + jnp.where(upper, jnp.exp(jnp.where(upper, -diff, 0.0) * lgb), 0.0))
    pos = lax.broadcasted_iota(I32, (c, 1), 0).astype(F32)
    lgf1 = _log_sigmoid(jnp.zeros((c, 1), F32) + dec_ref[0, h])
    lgb1 = _log_sigmoid(jnp.zeros((c, 1), F32) + dec_ref[1, h])
    lgf0 = _log_sigmoid(jnp.zeros((1, 1), F32) + dec_ref[0, h])
    lgb0 = _log_sigmoid(jnp.zeros((1, 1), F32) + dec_ref[1, h])
    terms = dict(
        mask=mask,
        q_f=jnp.exp((pos + 1.0) * lgf1), k_f=jnp.exp((c - 1.0 - pos) * lgf1), c_f=jnp.exp(c * lgf0),
        q_b=jnp.exp((c - pos) * lgb1), k_b=jnp.exp(pos * lgb1), c_b=jnp.exp(c * lgb0),
    )
    return terms


def _group_norm_gate(o, g):
    o = o * lax.rsqrt(jnp.mean(o * o, axis=-1, keepdims=True) + EPS)
    return (_silu(g) * o).astype(BF16)


def _ret_prompt_body(dec_ref, q_ref, k_ref, v_ref, g_ref, *rest, layer, first):
    o_ref, sf_ref, sb_ref = rest[-3:]
    h = pl.program_id(1)
    t = _decay_terms(dec_ref, h, SEQ)
    q = q_ref[...]
    k = k_ref[...] * (RET_DK ** -0.5)
    v = v_ref[...].astype(BF16)
    scores = _dot_nt(q.astype(BF16), k.astype(BF16)) * t["mask"]
    o = _dot(scores.astype(BF16), v)
    o_ref[...] = _group_norm_gate(o, g_ref[...])
    for ref, k_dec in ((sf_ref, t["k_f"]), (sb_ref, t["k_b"])):
        s = _dot_tn((k * k_dec).astype(BF16), v)
        if first:
            for l in range(N_RET_LAYERS):
                ref[l] = s if l == layer else jnp.zeros_like(s)
        else:
            ref[...] = s


def _ret_prompt(proj, decay, layer, prev_states=None):
    kq = RET_QK_W // RET_DK
    kv = 2 * RET_QK_W // RET_DV
    kg = kv + RET_HEADS
    state = jax.ShapeDtypeStruct((BATCH, N_RET_LAYERS, RET_HEADS, RET_DK, RET_DV), F32)
    in_specs = [
        pl.BlockSpec(memory_space=pltpu.SMEM),
        pl.BlockSpec((SEQ, RET_DK), lambda b, h: (b, h)),
        pl.BlockSpec((SEQ, RET_DK), lambda b, h: (b, kq + h)),
        pl.BlockSpec((SEQ, RET_DV), lambda b, h: (b, kv + h)),
        pl.BlockSpec((SEQ, RET_DV), lambda b, h: (b, kg + h)),
    ]
    args = [decay, proj, proj, proj, proj]
    first = prev_states is None
    aliases = {}
    if first:
        s_spec = pl.BlockSpec((None, N_RET_LAYERS, None, RET_DK, RET_DV), lambda b, h: (b, 0, h, 0, 0))
    else:
        s_spec = pl.BlockSpec((None, None, None, RET_DK, RET_DV), lambda b, h: (b, layer, h, 0, 0))
        in_specs += [pl.BlockSpec(memory_space=pl.ANY)] * 2
        aliases = {len(args): 1, len(args) + 1: 2}
        args += list(prev_states)
    return pl.pallas_call(
        functools.partial(_ret_prompt_body, layer=layer, first=first),
        grid=(BATCH, RET_HEADS),
        in_specs=in_specs,
        out_specs=[pl.BlockSpec((SEQ, RET_DV), lambda b, h: (b, h)), s_spec, s_spec],
        out_shape=[jax.ShapeDtypeStruct((N_PROMPT, RET_V_W), BF16), state, state],
        input_output_aliases=aliases,
        compiler_params=_cparams(2),
        name="ret_prompt",
    )(*args)


def _rope(x, cos, sin_signed):
    n = x.shape[-1]
    lane = lax.broadcasted_iota(I32, x.shape, 1)
    swapped = jnp.where(lane % 2 == 0, pltpu.roll(x, n - 1, 1), pltpu.roll(x, 1, 1))
    return x * cos + swapped * sin_signed


def _ret_sample_body(dec_ref, q_ref, k_ref, v_ref, g_ref, cos_ref, sin_ref, s0f_ref, s0b_ref, o_ref,
                     sf_scr, sb_scr):
    h = pl.program_id(1)
    c = RET_CHUNK
    n_chunks = DEC_SEQ // c
    t = _decay_terms(dec_ref, h, c)

    def chunk(ref, i):
        return ref[i * c:(i + 1) * c, :]

    def qk(i):
        q = _rope(chunk(q_ref, i), chunk(cos_ref, i), chunk(sin_ref, i))
        k = _rope(chunk(k_ref, i) * (RET_DK ** -0.5), chunk(cos_ref, i), chunk(sin_ref, i))
        return q, k

    s = s0f_ref[...]
    for i in range(n_chunks):
        sf_scr[i] = s
        if i + 1 < n_chunks:
            _, k = qk(i)
            s = s * t["c_f"] + _dot_tn((k * t["k_f"]).astype(BF16), chunk(v_ref, i).astype(BF16))
    s = s0b_ref[...]
    for i in reversed(range(n_chunks)):
        sb_scr[i] = s
        if i > 0:
            _, k = qk(i)
            s = s * t["c_b"] + _dot_tn((k * t["k_b"]).astype(BF16), chunk(v_ref, i).astype(BF16))

    for i in range(n_chunks):
        q, k = qk(i)
        scores = _dot_nt(q.astype(BF16), k.astype(BF16)) * t["mask"]
        o = _dot(scores.astype(BF16), chunk(v_ref, i).astype(BF16))
        o = o + _dot((q * t["q_f"]).astype(BF16), sf_scr[i].astype(BF16))
        o = o + _dot((q * t["q_b"]).astype(BF16), sb_scr[i].astype(BF16))
        o_ref[i * c:(i + 1) * c, :] = _group_norm_gate(o, chunk(g_ref, i))


def _ret_sample(proj, decay, cos, sin_signed, s0f, s0b, j):
    kq = RET_QK_W // RET_DK
    kv = 2 * RET_QK_W // RET_DV
    kg = kv + RET_HEADS
    rb = N_PROMPT // DEC_SEQ
    n_chunks = DEC_SEQ // RET_CHUNK
    return pl.pallas_call(
        _ret_sample_body,
        grid=(DEC_BATCH, RET_HEADS),
        in_specs=[
            pl.BlockSpec(memory_space=pltpu.SMEM),
            pl.BlockSpec((DEC_SEQ, RET_DK), lambda b, h: (rb + b, h)),
            pl.BlockSpec((DEC_SEQ, RET_DK), lambda b, h: (rb + b, kq + h)),
            pl.BlockSpec((DEC_SEQ, RET_DV), lambda b, h: (rb + b, kv + h)),
            pl.BlockSpec((DEC_SEQ, RET_DV), lambda b, h: (rb + b, kg + h)),
            pl.BlockSpec((DEC_SEQ, RET_DK), lambda b, h: (0, 0)),
            pl.BlockSpec((DEC_SEQ, RET_DK), lambda b, h: (0, 0)),
            pl.BlockSpec((None, None, None, RET_DK, RET_DV), lambda b, h: (b, j, h, 0, 0)),
            pl.BlockSpec((None, None, None, RET_DK, RET_DV), lambda b, h: (b, j, h, 0, 0)),
        ],
        out_specs=pl.BlockSpec((DEC_SEQ, RET_DV), lambda b, h: (b, h)),
        out_shape=jax.ShapeDtypeStruct((N_SAMPLE, RET_V_W), BF16),
        scratch_shapes=[pltpu.VMEM((n_chunks, RET_DK, RET_DV), F32),
                        pltpu.VMEM((n_chunks, RET_DK, RET_DV), F32)],
        compiler_params=_cparams(2),
        name="ret_sample",
    )(decay, proj, proj, proj, proj, cos, sin_signed, s0f, s0b)


def _softmax_pv(parts):
    m = parts[0][0].max(axis=-1, keepdims=True)
    for s, _ in parts[1:]:
        m = jnp.maximum(m, s.max(axis=-1, keepdims=True))
    es = [jnp.exp(s - m) for s, _ in parts]
    denom = es[0].sum(axis=-1, keepdims=True)
    for e in es[1:]:
        denom = denom + e.sum(axis=-1, keepdims=True)
    inv = 1.0 / denom
    o = _dot((es[0] * inv).astype(BF16), parts[0][1])
    for e, (_, v) in zip(es[1:], parts[1:]):
        o = o + _dot((e * inv).astype(BF16), v)
    return o


def _ctx_attn_body(q_ref, k_ref, v_ref, o_ref):
    outs = []
    for h in range(NA_HEADS):
        sl = slice(h * NA_HD, (h + 1) * NA_HD)
        q = (q_ref[:, sl] * (NA_HD ** -0.5)).astype(BF16)
        s = _dot_nt(q, k_ref[:, sl].astype(BF16))
        outs.append(_softmax_pv([(s, v_ref[:, sl].astype(BF16))]))
    o_ref[...] = jnp.concatenate(outs, axis=-1).astype(BF16)


def _ctx_attn(proj):
    return pl.pallas_call(
        _ctx_attn_body,
        grid=(BATCH,),
        in_specs=[
            pl.BlockSpec((SEQ, D_MODEL), lambda b: (b, 0)),
            pl.BlockSpec((SEQ, D_MODEL), lambda b: (b, 1)),
            pl.BlockSpec((SEQ, D_MODEL), lambda b: (b, 2)),
        ],
        out_specs=pl.BlockSpec((SEQ, D_MODEL), lambda b: (b, 0)),
        out_shape=jax.ShapeDtypeStruct((N_PROMPT, D_MODEL), BF16),
        compiler_params=_cparams(1),
        name="ctx_attn",
    )(proj, proj, proj)


NA_QT = 256
NA_HPB = LANES // NA_HD
NA_WIN = NA_KH * GRID_W


def _na_attn_body(q_ref, k_ref, v_ref, ck_ref, cv_ref, tbl_ref, o_ref):
    t = pl.program_id(1)
    rows_per_tile = NA_QT // GRID_W
    n_rows = DEC_SEQ // GRID_W
    heads = [slice(hh * NA_HD, (hh + 1) * NA_HD) for hh in range(NA_HPB)]
    ck = [ck_ref[:, sl].astype(BF16) for sl in heads]
    cv = [cv_ref[:, sl].astype(BF16) for sl in heads]
    for u in range(rows_per_tile):
        qr = t * rows_per_tile + u
        rs = jnp.clip(qr - NA_KH // 2, 0, n_rows - NA_KH)
        i0 = rs - qr + (NA_KH - 1)
        start = pl.multiple_of(rs * GRID_W, GRID_W)
        kw = k_ref[pl.ds(start, NA_WIN), :].astype(BF16)
        vw = v_ref[pl.ds(start, NA_WIN), :].astype(BF16)
        rows = slice(u * GRID_W, (u + 1) * GRID_W)
        outs = []
        for hh, sl in enumerate(heads):
            q = (q_ref[rows, sl] * (NA_HD ** -0.5)).astype(BF16)
            s_win = _dot_nt(q, kw[:, sl]) + tbl_ref[hh, i0]
            s_ctx = _dot_nt(q, ck[hh])
            outs.append(_softmax_pv([(s_win, vw[:, sl]), (s_ctx, cv[hh])]))
        o_ref[rows, :] = jnp.concatenate(outs, axis=-1).astype(BF16)


def _na_attn(proj, cache_k, cache_v, tbl, j):
    n_qt = DEC_SEQ // NA_QT
    rb_q = N_PROMPT // NA_QT
    rb_kv = N_PROMPT // DEC_SEQ
    cb = D_MODEL // LANES
    ck = cache_k.reshape(DEC_BATCH, -1, PAST_LEN, D_MODEL)
    cv = cache_v.reshape(DEC_BATCH, -1, PAST_LEN, D_MODEL)
    return pl.pallas_call(
        _na_attn_body,
        grid=(cb, n_qt, DEC_BATCH),
        in_specs=[
            pl.BlockSpec((NA_QT, LANES), lambda p, t, b: (rb_q + b * n_qt + t, p)),
            pl.BlockSpec((DEC_SEQ, LANES), lambda p, t, b: (rb_kv + b, cb + p)),
            pl.BlockSpec((DEC_SEQ, LANES), lambda p, t, b: (rb_kv + b, 2 * cb + p)),
            pl.BlockSpec((None, None, PAST_LEN, LANES), lambda p, t, b: (b, j, 0, p)),
            pl.BlockSpec((None, None, PAST_LEN, LANES), lambda p, t, b: (b, j, 0, p)),
            pl.BlockSpec((NA_HPB, NA_KH, GRID_W, NA_WIN), lambda p, t, b: (p, 0, 0, 0)),
        ],
        out_specs=pl.BlockSpec((NA_QT, LANES), lambda p, t, b: (b * n_qt + t, p)),
        out_shape=jax.ShapeDtypeStruct((N_SAMPLE, D_MODEL), BF16),
        compiler_params=_cparams(3),
        name="na_attn",
    )(proj, proj, proj, ck, cv, tbl)


def _na_bias_table(rpb):
    c = np.arange(GRID_W)
    cs = np.clip(c - NA_KW // 2, 0, GRID_W - NA_KW)
    col_valid = (c[None, :] >= cs[:, None]) & (c[None, :] < cs[:, None] + NA_KW)
    col_off = np.clip(c[None, :] - c[:, None] + NA_KW - 1, 0, 2 * NA_KW - 2)
    sel_c = (col_off[..., None] == np.arange(2 * NA_KW - 1)).astype(np.float32)
    t = jnp.einsum("hij,qkj->hqik", rpb, sel_c, precision=lax.Precision.HIGHEST)
    t = jnp.where(col_valid[None, :, None, :], t, -jnp.inf)
    return jnp.stack([t[:, :, i0:i0 + NA_KH, :].reshape(NA_HEADS, GRID_W, NA_WIN) for i0 in range(NA_KH)], axis=1)


def _outproj_body(ap_ref, as_ref, w_ref, x_ref, g_ref, mod_ref, wr_ref, br_ref, xo_ref, tok_ref, lg_ref):
    i = pl.program_id(0)
    a = jnp.where(i < N_PROMPT // TM, ap_ref[...], as_ref[...])
    x = x_ref[...] + mod_ref[2:3, :] * _dot(a, w_ref[...])
    xo_ref[...] = x
    tok = _norm_mod(x, g_ref[...], mod_ref[3:4, :], mod_ref[4:5, :])
    tok_ref[...] = tok
    hi = tok.astype(BF16)
    lo = (tok - hi.astype(F32)).astype(BF16)
    wr = wr_ref[...]
    wr_hi = wr.astype(BF16)
    wr_lo = (wr - wr_hi.astype(F32)).astype(BF16)
    lg_ref[...] = _dot(hi, wr_hi) + _dot(lo, wr_hi) + _dot(hi, wr_lo) + br_ref[...]


def _outproj(a_prompt, a_sample, w_bf16, x, g, mod, wr, br):
    k_in = a_prompt.shape[1]
    n_pt = N_PROMPT // TM
    return pl.pallas_call(
        _outproj_body,
        grid=(N_TOK // TM,),
        in_specs=[
            pl.BlockSpec((TM, k_in), lambda i: (jnp.minimum(i, n_pt - 1), 0)),
            pl.BlockSpec((TM, k_in), lambda i: (jnp.maximum(i - n_pt, 0), 0)),
            pl.BlockSpec((k_in, D_MODEL), lambda i: (0, 0)),
            pl.BlockSpec((TM, D_MODEL), lambda i: (i, 0)),
            pl.BlockSpec((1, D_MODEL), lambda i: (0, 0)),
            pl.BlockSpec((None, MOD_ROWS, D_MODEL), lambda i: (_group_of_tile(i, TM), 0, 0)),
            pl.BlockSpec((D_MODEL, LANES), lambda i: (0, 0)),
            pl.BlockSpec((1, LANES), lambda i: (0, 0)),
        ],
        out_specs=[
            pl.BlockSpec((TM, D_MODEL), lambda i: (i, 0)),
            pl.BlockSpec((TM, D_MODEL), lambda i: (i, 0)),
            pl.BlockSpec((TM, LANES), lambda i: (i, 0)),
        ],
        out_shape=[
            jax.ShapeDtypeStruct((N_TOK, D_MODEL), F32),
            jax.ShapeDtypeStruct((N_TOK, D_MODEL), F32),
            jax.ShapeDtypeStruct((N_TOK, LANES), F32),
        ],
        compiler_params=_cparams(1),
        name="outproj",
    )(a_prompt, a_sample, w_bf16, x, g.reshape(1, D_MODEL), mod, wr, br)


def _router_body(lg_ref, idx_ref, gate_ref, rank_ref, cnt_ref, carry):
    i = pl.program_id(0)

    @pl.when(i == 0)
    def _():
        carry[...] = jnp.zeros_like(carry)

    lane = lax.broadcasted_iota(I32, (ROUTER_TM, LANES), 1)
    l = jnp.where(lane < N_EXPERTS, lg_ref[...], -jnp.inf)
    vals, idxs, hots = [], [], []
    for _ in range(TOP_K):
        m = l.max(axis=-1, keepdims=True)
        idx = jnp.where(l == m, lane, LANES).min(axis=-1, keepdims=True)
        hot = lane == idx
        l = jnp.where(hot, -jnp.inf, l)
        vals.append(m)
        idxs.append(idx)
        hots.append(hot)
    es = [jnp.exp(v - vals[0]) for v in vals]
    denom = es[0] + es[1] + es[2] + es[3]
    hot_all = (hots[0] | hots[1] | hots[2] | hots[3]).astype(F32)
    r = lax.broadcasted_iota(I32, (ROUTER_TM, ROUTER_TM), 0)
    c = lax.broadcasted_iota(I32, (ROUTER_TM, ROUTER_TM), 1)
    before = _dot((c < r).astype(BF16), hot_all.astype(BF16)) + carry[...]
    idx_out = jnp.zeros((ROUTER_TM, LANES), I32)
    gate_out = jnp.zeros((ROUTER_TM, LANES), F32)
    rank_out = jnp.zeros((ROUTER_TM, LANES), I32)
    for k in range(TOP_K):
        rank_k = jnp.where(hots[k], before, 0.0).sum(axis=-1, keepdims=True).astype(I32)
        idx_out = jnp.where(lane == k, idxs[k], idx_out)
        gate_out = jnp.where(lane == k, es[k] / denom, gate_out)
        rank_out = jnp.where(lane == k, rank_k, rank_out)
    idx_ref[...] = idx_out
    gate_ref[...] = gate_out
    rank_ref[...] = rank_out
    carry[...] = carry[...] + hot_all.sum(axis=0, keepdims=True)
    cnt_ref[...] = carry[...].astype(I32)


def _router(logits):
    spec = pl.BlockSpec((ROUTER_TM, LANES), lambda i: (i, 0))
    return pl.pallas_call(
        _router_body,
        grid=(N_TOK // ROUTER_TM,),
        in_specs=[spec],
        out_specs=[spec, spec, spec, pl.BlockSpec((1, LANES), lambda i: (0, 0))],
        out_shape=[
            jax.ShapeDtypeStruct((N_TOK, LANES), I32),
            jax.ShapeDtypeStruct((N_TOK, LANES), F32),
            jax.ShapeDtypeStruct((N_TOK, LANES), I32),
            jax.ShapeDtypeStruct((1, LANES), I32),
        ],
        scratch_shapes=[pltpu.VMEM((1, LANES), F32)],
        compiler_params=_cparams(1),
        name="router",
    )(logits)


SC_CORES = 2
SC_SUBCORES = 16
SC_WORKERS = SC_CORES * SC_SUBCORES
SC_CH = 32


def _sc_mesh():
    return plsc.VectorSubcoreMesh(core_axis_name="c", subcore_axis_name="s",
                                  num_cores=SC_CORES, num_subcores=SC_SUBCORES)


def _sc_worker():
    return lax.axis_index("s") * SC_CORES + lax.axis_index("c")


def _dispatch(tok, pos_km):
    n, d = tok.shape
    n_ch = n // (SC_WORKERS * SC_CH)
    assert n_ch * SC_WORKERS * SC_CH == n and n_ch % 2 == 0

    @functools.partial(
        pl.kernel, mesh=_sc_mesh(), out_type=jax.ShapeDtypeStruct((MOE_ROWS, d), tok.dtype),
        scratch_types=[pltpu.VMEM((n_ch, TOP_K, SC_CH), I32), pltpu.VMEM((2, SC_CH, d), tok.dtype),
                       pltpu.SemaphoreType.DMA((2,)), pltpu.SemaphoreType.DMA((2,))])
    def scatter_rows(tok_hbm, pos_hbm, out_hbm, idx_v, rows_v, load_sem, scat_sem):
        wid = _sc_worker()
        base = wid * (n_ch * SC_CH)
        pltpu.sync_copy(pos_hbm.at[wid], idx_v)

        def load(g, slot):
            rows = pl.ds(pl.multiple_of(base + g * SC_CH, 8), SC_CH)
            return pltpu.make_async_copy(tok_hbm.at[rows], rows_v.at[slot], load_sem.at[slot])

        def scatter(g, k, slot):
            return pltpu.make_async_copy(rows_v.at[slot], out_hbm.at[idx_v.at[g, k]], scat_sem.at[slot])

        load(0, 0).start()

        @pl.loop(0, n_ch, step=2)
        def _(g0):
            for slot in range(2):
                g = g0 + slot
                load(g, slot).wait()

                @pl.when(g + 1 < n_ch)
                def _():
                    load(g + 1, 1 - slot).start()

                for k in range(TOP_K):
                    scatter(g, k, slot).start()
                for k in range(TOP_K):
                    scatter(g, k, slot).wait()

    pos = pos_km.reshape(TOP_K, SC_WORKERS, n_ch, SC_CH).transpose(1, 2, 0, 3)
    return scatter_rows(tok, pos)


def _gather_rows(table, idx):
    b, d = idx.shape[0], table.shape[1]
    n_ch = b // (SC_WORKERS * SC_CH)
    assert n_ch * SC_WORKERS * SC_CH == b and n_ch % 2 == 0

    @functools.partial(
        pl.kernel, mesh=_sc_mesh(), out_type=jax.ShapeDtypeStruct((b, d), table.dtype),
        scratch_types=[pltpu.VMEM((n_ch, SC_CH), I32), pltpu.VMEM((2, SC_CH, d), table.dtype),
                       pltpu.SemaphoreType.DMA((2,))])
    def gather_rows(table_hbm, idx_hbm, out_hbm, idx_v, rows_v, sems):
        wid = _sc_worker()
        base = wid * (n_ch * SC_CH)
        pltpu.sync_copy(idx_hbm.at[wid], idx_v)

        def gather(g, slot):
            return pltpu.make_async_copy(table_hbm.at[idx_v.at[g]], rows_v.at[slot], sems.at[slot])

        def write_out(g, slot):
            rows = pl.ds(pl.multiple_of(base + g * SC_CH, 8), SC_CH)
            pltpu.sync_copy(rows_v.at[slot], out_hbm.at[rows])

        gather(0, 0).start()

        @pl.loop(0, n_ch, step=2)
        def _(g):
            gather(g + 1, 1).start()
            gather(g, 0).wait()
            write_out(g, 0)

            @pl.when(g + 2 < n_ch)
            def _():
                gather(g + 2, 0).start()

            gather(g + 1, 1).wait()
            write_out(g + 1, 1)

    return gather_rows(table, idx.reshape(SC_WORKERS, n_ch, SC_CH))


def _moe_body(te_ref, tv_ref, x_ref, w1_ref, b1_ref, w2_ref, b2_ref, perm_ref, o_ref, w1_scr, w2_scr):
    t = pl.program_id(0)
    first = jnp.logical_or(t == 0, te_ref[t] != te_ref[jnp.maximum(t - 1, 0)])

    @pl.when(jnp.logical_and(tv_ref[t] == 1, first))
    def _():
        w2_scr[...] = w2_ref[...].astype(BF16)
        half = PERM_W // 2
        for c in range(2 * D_EXPERT // PERM_W):
            blk = _dot(w1_ref[:, c * PERM_W:(c + 1) * PERM_W].astype(BF16), perm_ref[...]).astype(BF16)
            w1_scr[:, c * half:(c + 1) * half] = blk[:, :half]
            w1_scr[:, D_EXPERT + c * half:D_EXPERT + (c + 1) * half] = blk[:, half:]

    @pl.when(tv_ref[t] == 0)
    def _():
        o_ref[...] = jnp.zeros_like(o_ref)

    @pl.when(tv_ref[t] == 1)
    def _():
        hh = _dot(x_ref[...].astype(BF16), w1_scr[...]) + b1_ref[...]
        h_glu = jnp.minimum(hh[:, :D_EXPERT], SWIGLU_LIMIT)
        h_lin = jnp.clip(hh[:, D_EXPERT:], -SWIGLU_LIMIT, SWIGLU_LIMIT)
        a = h_glu * _sigmoid(SWIGLU_ALPHA * h_glu) * (h_lin + 1.0)
        o_ref[...] = _dot(a.astype(BF16), w2_scr[...]) + b2_ref[...]


def _moe_experts(tile_expert, tile_valid, xs, w1, b1_split, w2, b2, perm, layer):
    grid_spec = pltpu.PrefetchScalarGridSpec(
        num_scalar_prefetch=2,
        grid=(MOE_TILES,),
        in_specs=[
            pl.BlockSpec((MOE_TM, D_MODEL), lambda t, te, tv: (t, 0)),
            pl.BlockSpec((None, None, D_MODEL, 2 * D_EXPERT), lambda t, te, tv: (layer, te[t], 0, 0)),
            pl.BlockSpec((None, None, 1, 2 * D_EXPERT), lambda t, te, tv: (layer, te[t], 0, 0)),
            pl.BlockSpec((None, None, D_EXPERT, D_MODEL), lambda t, te, tv: (layer, te[t], 0, 0)),
            pl.BlockSpec((None, None, 1, D_MODEL), lambda t, te, tv: (layer, te[t], 0, 0)),
            pl.BlockSpec((PERM_W, PERM_W), lambda t, te, tv: (0, 0)),
        ],
        out_specs=pl.BlockSpec((MOE_TM, D_MODEL), lambda t, te, tv: (t, 0)),
        scratch_shapes=[pltpu.VMEM((D_MODEL, 2 * D_EXPERT), BF16), pltpu.VMEM((D_EXPERT, D_MODEL), BF16)],
    )
    return pl.pallas_call(
        _moe_body,
        grid_spec=grid_spec,
        out_shape=jax.ShapeDtypeStruct((MOE_ROWS, D_MODEL), F32),
        compiler_params=_cparams(1),
        name="moe_experts",
    )(tile_expert, tile_valid, xs, w1, b1_split, w2, b2, perm)


COMBINE_TM = 256


def _combine_body(*refs, final):
    y_refs = refs[:TOP_K]
    gate_ref, x_ref, mod_ref, g_ref, o_ref = refs[TOP_K:]
    gates = gate_ref[...]
    y = gates[:, 0:1] * y_refs[0][...]
    for k in range(1, TOP_K):
        y = y + gates[:, k:k + 1] * y_refs[k][...]
    x = x_ref[...] + mod_ref[5:6, :] * y
    if final:
        x = (x * lax.rsqrt(jnp.mean(x * x, axis=-1, keepdims=True) + EPS)) * g_ref[...]
    o_ref[...] = x


def _combine(y_km, gates, x, mod, g_final, final):
    n_t = N_TOK // COMBINE_TM
    y_specs = [pl.BlockSpec((COMBINE_TM, D_MODEL), lambda i, k=k: (k * n_t + i, 0)) for k in range(TOP_K)]
    return pl.pallas_call(
        functools.partial(_combine_body, final=final),
        grid=(n_t,),
        in_specs=y_specs + [
            pl.BlockSpec((COMBINE_TM, LANES), lambda i: (i, 0)),
            pl.BlockSpec((COMBINE_TM, D_MODEL), lambda i: (i, 0)),
            pl.BlockSpec((None, MOD_ROWS, D_MODEL), lambda i: (_group_of_tile(i, COMBINE_TM), 0, 0)),
            pl.BlockSpec((1, D_MODEL), lambda i: (0, 0)),
        ],
        out_specs=pl.BlockSpec((COMBINE_TM, D_MODEL), lambda i: (i, 0)),
        out_shape=jax.ShapeDtypeStruct((N_TOK, D_MODEL), F32),
        compiler_params=_cparams(1),
        name="combine",
    )(*([y_km] * TOP_K), gates, x, mod, g_final.reshape(1, D_MODEL))


def _routing_tables(idx, rank, counts):
    counts = counts[0, :N_EXPERTS]
    padded = ((counts + MOE_TM - 1) // MOE_TM) * MOE_TM
    ends = jnp.cumsum(padded)
    offsets = ends - padded
    idx = idx[:, :TOP_K]
    pos = rank[:, :TOP_K] + jnp.sum(
        jnp.where(idx[..., None] == jnp.arange(N_EXPERTS, dtype=I32), offsets, 0), axis=-1)
    pos_km = pos.T.astype(I32)
    tile_start = jnp.arange(MOE_TILES, dtype=I32) * MOE_TM
    tile_valid = (tile_start < ends[-1]).astype(I32)
    tile_expert = jnp.sum((tile_start[:, None] >= ends[None, :]).astype(I32), axis=-1)
    last_expert = jnp.sum((ends[-1] - 1 >= ends).astype(I32))
    tile_expert = jnp.where(tile_valid == 1, tile_expert, last_expert).astype(I32)
    return pos_km, tile_expert, tile_valid


def _rope_tables():
    t = np.arange(DEC_SEQ)
    row = (t // GRID_W).astype(np.float32)
    col = (t % GRID_W).astype(np.float32)
    d_axis = RET_DK // 2
    inv = jnp.asarray(ROPE_BASE, F32) ** (-jnp.arange(0, d_axis, 2, dtype=F32) / d_axis)
    ang = jnp.concatenate([row[:, None] * inv, col[:, None] * inv], axis=-1)
    cos = jnp.repeat(jnp.cos(ang), 2, axis=-1)
    sin = jnp.repeat(jnp.sin(ang), 2, axis=-1)
    sign = jnp.asarray(np.tile(np.array([-1.0, 1.0], np.float32), RET_DK // 2))
    return cos, sin * sign


def _deinterleave_matrix():
    p = np.zeros((PERM_W, PERM_W), np.float32)
    half = PERM_W // 2
    for j in range(half):
        p[2 * j, j] = 1.0
        p[2 * j + 1, half + j] = 1.0
    return jnp.asarray(p, BF16)


def kernel(x_prompt, x_sample, state_ret_fwd, state_ret_bwd, cache_na_k, cache_na_v, c, c_ctx, ada_w, ada_b, norm_mix, norm_ffn, norm_final, ret_w_in, ret_w_out, ret_decay_fwd, ret_decay_bwd, na_w_in, na_w_out, na_rpb, moe_w_router, moe_b_router, moe_w1, moe_b1, moe_w2, moe_b2):
    x = jnp.concatenate([x_prompt.reshape(N_PROMPT, D_MODEL), x_sample.reshape(N_SAMPLE, D_MODEL)], axis=0)

    cond = jnp.zeros((MOD_ROWS, D_MODEL), F32).at[0].set(c_ctx).at[1:1 + DEC_BATCH].set(c)
    mod_all = _adaln(cond, ada_w, ada_b)
    mod_all = mod_all[:, :N_GROUPS].reshape(DEPTH, N_GROUPS, 6, D_MODEL)
    mod_all = jnp.pad(mod_all, ((0, 0), (0, 0), (0, MOD_ROWS - 6), (0, 0)))

    cos, sin_signed = _rope_tables()
    perm = _deinterleave_matrix()
    wr_all = jnp.pad(moe_w_router, ((0, 0), (0, 0), (0, LANES - N_EXPERTS)))
    br_all = jnp.pad(moe_b_router, ((0, 0), (0, LANES - N_EXPERTS))).reshape(DEPTH, 1, LANES)
    b1_split = jnp.concatenate([moe_b1[..., 0::2], moe_b1[..., 1::2]], axis=-1).reshape(DEPTH, N_EXPERTS, 1, 2 * D_EXPERT)
    b2_all = moe_b2.reshape(DEPTH, N_EXPERTS, 1, D_MODEL)
    ret_states = None
    na_caches = None
    for layer in range(DEPTH):
        mod = mod_all[layer]
        j = layer // 2
        if layer % 2 == 0:
            proj = _inproj(x, norm_mix[layer], mod, _cast_layer_bf16(ret_w_in, j))
            decay = jnp.stack([ret_decay_fwd[j], ret_decay_bwd[j]])
            a_p, s_f, s_b = _ret_prompt(proj, decay, j, ret_states)
            ret_states = (s_f, s_b)
            a_s = _ret_sample(proj, decay, cos, sin_signed, state_ret_fwd, state_ret_bwd, j)
            w_out = _cast_layer_bf16(ret_w_out, j)
        else:
            proj, ck, cv = _inproj(x, norm_mix[layer], mod, _cast_layer_bf16(na_w_in, j), j, na_caches)
            na_caches = (ck, cv)
            a_p = _ctx_attn(proj)
            a_s = _na_attn(proj, cache_na_k, cache_na_v, _na_bias_table(na_rpb[j]), j)
            w_out = _cast_layer_bf16(na_w_out, j)

        x, tok, logits = _outproj(a_p, a_s, w_out, x, norm_ffn[layer], mod, wr_all[layer], br_all[layer])

        idx, gates, rank, counts = _router(logits)
        pos_km, tile_expert, tile_valid = _routing_tables(idx, rank, counts)
        xs = _dispatch(tok, pos_km)
        ys = _moe_experts(tile_expert, tile_valid, xs, moe_w1, b1_split, moe_w2, b2_all, perm, layer)
        y_km = _gather_rows(ys, pos_km.reshape(-1))
        x = _combine(y_km, gates, x, mod, norm_final, final=(layer == DEPTH - 1))

    y_prompt = x[:N_PROMPT].reshape(BATCH, SEQ, D_MODEL)
    y_sample = x[N_PROMPT:].reshape(DEC_BATCH, DEC_SEQ, D_MODEL)
    return (y_prompt, y_sample, ret_states[0], ret_states[1],
            na_caches[0].reshape(BATCH, N_NA_LAYERS, SEQ, NA_HEADS, NA_HD),
            na_caches[1].reshape(BATCH, N_NA_LAYERS, SEQ, NA_HEADS, NA_HD))
```

```python
import functools

import numpy as np
import jax
import jax.numpy as jnp
from jax import lax
from jax.experimental import pallas as pl
from jax.experimental.pallas import tpu as pltpu
from jax.experimental.pallas import tpu_sc as plsc

F32 = jnp.float32
BF16 = jnp.bfloat16
I32 = jnp.int32

D_MODEL = 1024
BATCH = 32
SEQ = 256
DEPTH = 4
DEC_BATCH = 2
DEC_SEQ = 1024
PAST_LEN = 512
GRID_W = 64
EPS = 1e-6
ROPE_BASE = 10000.0
RET_HEADS = 4
RET_DK = D_MODEL // RET_HEADS
RET_DV = 2 * RET_DK
RET_QK_W = RET_HEADS * RET_DK
RET_V_W = RET_HEADS * RET_DV
NA_HEADS = 16
NA_HD = D_MODEL // NA_HEADS
NA_KH = 8
NA_KW = 16
N_EXPERTS = 32
TOP_K = 4
D_EXPERT = D_MODEL
SWIGLU_LIMIT = 7.0
SWIGLU_ALPHA = 1.702
N_RET_LAYERS = (DEPTH + 1) // 2
N_NA_LAYERS = DEPTH // 2

N_PROMPT = BATCH * SEQ
N_SAMPLE = DEC_BATCH * DEC_SEQ
N_TOK = N_PROMPT + N_SAMPLE
N_GROUPS = 1 + DEC_BATCH
MOD_ROWS = 8
LANES = 128

TM = 256
RET_CHUNK = 256
MOE_TM = 256
MOE_ROWS = N_TOK * TOP_K + N_EXPERTS * MOE_TM
MOE_TILES = MOE_ROWS // MOE_TM
ROUTER_TM = 512
PERM_W = 256
VMEM_LIMIT = 56 * 1024 * 1024


def _cparams(n_axes, vmem=VMEM_LIMIT):
    return pltpu.CompilerParams(dimension_semantics=("arbitrary",) * n_axes, vmem_limit_bytes=vmem)


def _group_of_tile(i, tm):
    return jnp.maximum((i * tm) // DEC_SEQ - (N_PROMPT // DEC_SEQ - 1), 0)


def _sigmoid(x):
    return 1.0 / (1.0 + jnp.exp(-x))


def _silu(x):
    return x * _sigmoid(x)


def _log_sigmoid(x):
    return jnp.minimum(x, 0.0) - jnp.log(1.0 + jnp.exp(-jnp.abs(x)))


def _norm_mod(x, g, shift, scale):
    y = x * lax.rsqrt(jnp.mean(x * x, axis=-1, keepdims=True) + EPS)
    return (y * g) * (1.0 + scale) + shift


def _dot(a, b):
    return jnp.dot(a, b, preferred_element_type=F32)


def _dot_nt(a, b):
    return lax.dot_general(a, b, (((1,), (1,)), ((), ())), preferred_element_type=F32)


def _dot_tn(a, b):
    return lax.dot_general(a, b, (((0,), (0,)), ((), ())), preferred_element_type=F32)


ADA_TN = 1536


def _adaln_body(cond_ref, w_ref, b_ref, o_ref):
    s = _silu(cond_ref[...]).astype(BF16)
    o_ref[...] = _dot(s, w_ref[...].astype(BF16)) + b_ref[...]


def _adaln(cond, ada_w, ada_b):
    n_out = 6 * D_MODEL
    return pl.pallas_call(
        _adaln_body,
        grid=(DEPTH, n_out // ADA_TN),
        in_specs=[
            pl.BlockSpec((MOD_ROWS, D_MODEL), lambda l, j: (0, 0)),
            pl.BlockSpec((None, D_MODEL, ADA_TN), lambda l, j: (l, 0, j)),
            pl.BlockSpec((None, 1, ADA_TN), lambda l, j: (l, 0, j)),
        ],
        out_specs=pl.BlockSpec((None, MOD_ROWS, ADA_TN), lambda l, j: (l, 0, j)),
        out_shape=jax.ShapeDtypeStruct((DEPTH, MOD_ROWS, n_out), F32),
        compiler_params=_cparams(2),
        name="adaln",
    )(cond, ada_w, ada_b.reshape(DEPTH, 1, n_out))


CAST_TR = 256


def _cast_body(w_ref, o_ref):
    o_ref[...] = w_ref[...].astype(BF16)


def _cast_layer_bf16(w, j):
    _, r, c = w.shape
    return pl.pallas_call(
        _cast_body,
        grid=(r // CAST_TR,),
        in_specs=[pl.BlockSpec((None, CAST_TR, c), lambda i: (j, i, 0))],
        out_specs=pl.BlockSpec((CAST_TR, c), lambda i: (i, 0)),
        out_shape=jax.ShapeDtypeStruct((r, c), BF16),
        compiler_params=_cparams(1),
        name="cast_bf16",
    )(w)


def _inproj_body(x_ref, g_ref, mod_ref, w_ref, *rest, cache_layer, first_cache):
    o_ref = rest[-3] if cache_layer is not None else rest[-1]
    i = pl.program_id(0)
    h = _norm_mod(x_ref[...], g_ref[...], mod_ref[0:1, :], mod_ref[1:2, :]).astype(BF16)
    for c in range(o_ref.shape[1] // D_MODEL):
        sl = slice(c * D_MODEL, (c + 1) * D_MODEL)
        r = _dot(h, w_ref[:, sl])
        o_ref[:, sl] = r
        if cache_layer is not None and c >= 1:
            cache_ref = rest[-3 + c]

            @pl.when(i < BATCH)
            def _(r=r, cache_ref=cache_ref):
                if first_cache:
                    for l in range(N_NA_LAYERS):
                        cache_ref[l] = r if l == cache_layer else jnp.zeros_like(r)
                else:
                    cache_ref[...] = r


def _inproj(x, g, mod, w_bf16, cache_layer=None, prev_caches=None):
    n_out = w_bf16.shape[1]
    in_specs = [
        pl.BlockSpec((TM, D_MODEL), lambda i: (i, 0)),
        pl.BlockSpec((1, D_MODEL), lambda i: (0, 0)),
        pl.BlockSpec((None, MOD_ROWS, D_MODEL), lambda i: (_group_of_tile(i, TM), 0, 0)),
        pl.BlockSpec((D_MODEL, n_out), lambda i: (0, 0)),
    ]
    out_specs = [pl.BlockSpec((TM, n_out), lambda i: (i, 0))]
    out_shape = [jax.ShapeDtypeStruct((N_TOK, n_out), F32)]
    args = [x, g.reshape(1, D_MODEL), mod, w_bf16]
    aliases = {}
    first_cache = prev_caches is None
    if cache_layer is not None:
        assert TM == SEQ and n_out == 3 * D_MODEL
        cache = jax.ShapeDtypeStruct((BATCH, N_NA_LAYERS, SEQ, D_MODEL), F32)
        if first_cache:
            spec = pl.BlockSpec((None, N_NA_LAYERS, SEQ, D_MODEL), lambda i: (jnp.minimum(i, BATCH - 1), 0, 0, 0))
        else:
            spec = pl.BlockSpec((None, None, SEQ, D_MODEL),
                                lambda i: (jnp.minimum(i, BATCH - 1), cache_layer, 0, 0))
            in_specs += [pl.BlockSpec(memory_space=pl.ANY)] * 2
            aliases = {len(args): 1, len(args) + 1: 2}
            args += list(prev_caches)
        out_specs += [spec, spec]
        out_shape += [cache, cache]
    outs = pl.pallas_call(
        functools.partial(_inproj_body, cache_layer=cache_layer, first_cache=first_cache),
        grid=(N_TOK // TM,),
        in_specs=in_specs,
        out_specs=out_specs,
        out_shape=out_shape,
        input_output_aliases=aliases,
        compiler_params=_cparams(1),
        name="inproj",
    )(*args)
    return outs[0] if cache_layer is None else outs


def _decay_terms(dec_ref, h, c):
    lgf = _log_sigmoid(jnp.zeros((c, c), F32) + dec_ref[0, h])
    lgb = _log_sigmoid(jnp.zeros((c, c), F32) + dec_ref[1, h])
    row = lax.broadcasted_iota(I32, (c, c), 0)
    col = lax.broadcasted_iota(I32, (c, c), 1)
    diff = (row - col).astype(F32)
    lower = diff >= 0
    upper = diff <= 0
    mask = (jnp.where(lower, jnp.exp(jnp.where(lower, diff, 0.0) * lgf), 0.0)
            + jnp.where(upper, jnp.exp(jnp.where(upper, -diff, 0.0) * lgb), 0.0))
    pos = lax.broadcasted_iota(I32, (c, 1), 0).astype(F32)
    lgf1 = _log_sigmoid(jnp.zeros((c, 1), F32) + dec_ref[0, h])
    lgb1 = _log_sigmoid(jnp.zeros((c, 1), F32) + dec_ref[1, h])
    lgf0 = _log_sigmoid(jnp.zeros((1, 1), F32) + dec_ref[0, h])
    lgb0 = _log_sigmoid(jnp.zeros((1, 1), F32) + dec_ref[1, h])
    terms = dict(
        mask=mask,
        q_f=jnp.exp((pos + 1.0) * lgf1), k_f=jnp.exp((c - 1.0 - pos) * lgf1), c_f=jnp.exp(c * lgf0),
        q_b=jnp.exp((c - pos) * lgb1), k_b=jnp.exp(pos * lgb1), c_b=jnp.exp(c * lgb0),
    )
    return terms


def _group_norm_gate(o, g):
    o = o * lax.rsqrt(jnp.mean(o * o, axis=-1, keepdims=True) + EPS)
    return (_silu(g) * o).astype(BF16)


def _ret_prompt_body(dec_ref, q_ref, k_ref, v_ref, g_ref, *rest, layer, first):
    o_ref, sf_ref, sb_ref = rest[-3:]
    h = pl.program_id(1)
    t = _decay_terms(dec_ref, h, SEQ)
    q = q_ref[...]
    k = k_ref[...] * (RET_DK ** -0.5)
    v = v_ref[...].astype(BF16)
    scores = _dot_nt(q.astype(BF16), k.astype(BF16)) * t["mask"]
    o = _dot(scores.astype(BF16), v)
    o_ref[...] = _group_norm_gate(o, g_ref[...])
    for ref, k_dec in ((sf_ref, t["k_f"]), (sb_ref, t["k_b"])):
        s = _dot_tn((k * k_dec).astype(BF16), v)
        if first:
            for l in range(N_RET_LAYERS):
                ref[l] = s if l == layer else jnp.zeros_like(s)
        else:
            ref[...] = s


def _ret_prompt(proj, decay, layer, prev_states=None):
    kq = RET_QK_W // RET_DK
    kv = 2 * RET_QK_W // RET_DV
    kg = kv + RET_HEADS
    state = jax.ShapeDtypeStruct((BATCH, N_RET_LAYERS, RET_HEADS, RET_DK, RET_DV), F32)
    in_specs = [
        pl.BlockSpec(memory_space=pltpu.SMEM),
        pl.BlockSpec((SEQ, RET_DK), lambda b, h: (b, h)),
        pl.BlockSpec((SEQ, RET_DK), lambda b, h: (b, kq + h)),
        pl.BlockSpec((SEQ, RET_DV), lambda b, h: (b, kv + h)),
        pl.BlockSpec((SEQ, RET_DV), lambda b, h: (b, kg + h)),
    ]
    args = [decay, proj, proj, proj, proj]
    first = prev_states is None
    aliases = {}
    if first:
        s_spec = pl.BlockSpec((None, N_RET_LAYERS, None, RET_DK, RET_DV), lambda b, h: (b, 0, h, 0, 0))
    else:
        s_spec = pl.BlockSpec((None, None, None, RET_DK, RET_DV), lambda b, h: (b, layer, h, 0, 0))
        in_specs += [pl.BlockSpec(memory_space=pl.ANY)] * 2
        aliases = {len(args): 1, len(args) + 1: 2}
        args += list(prev_states)
    return pl.pallas_call(
        functools.partial(_ret_prompt_body, layer=layer, first=first),
        grid=(BATCH, RET_HEADS),
        in_specs=in_specs,
        out_specs=[pl.BlockSpec((SEQ, RET_DV), lambda b, h: (b, h)), s_spec, s_spec],
        out_shape=[jax.ShapeDtypeStruct((N_PROMPT, RET_V_W), BF16), state, state],
        input_output_aliases=aliases,
        compiler_params=_cparams(2),
        name="ret_prompt",
    )(*args)


def _rope(x, cos, sin_signed):
    n = x.shape[-1]
    lane = lax.broadcasted_iota(I32, x.shape, 1)
    swapped = jnp.where(lane % 2 == 0, pltpu.roll(x, n - 1, 1), pltpu.roll(x, 1, 1))
    return x * cos + swapped * sin_signed


def _ret_sample_body(dec_ref, q_ref, k_ref, v_ref, g_ref, cos_ref, sin_ref, s0f_ref, s0b_ref, o_ref,
                     sf_scr, sb_scr):
    h = pl.program_id(1)
    c = RET_CHUNK
    n_chunks = DEC_SEQ // c
    t = _decay_terms(dec_ref, h, c)

    def chunk(ref, i):
        return ref[i * c:(i + 1) * c, :]

    def qk(i):
        q = _rope(chunk(q_ref, i), chunk(cos_ref, i), chunk(sin_ref, i))
        k = _rope(chunk(k_ref, i) * (RET_DK ** -0.5), chunk(cos_ref, i), chunk(sin_ref, i))
        return q, k

    s = s0f_ref[...]
    for i in range(n_chunks):
        sf_scr[i] = s
        if i + 1 < n_chunks:
            _, k = qk(i)
            s = s * t["c_f"] + _dot_tn((k * t["k_f"]).astype(BF16), chunk(v_ref, i).astype(BF16))
    s = s0b_ref[...]
    for i in reversed(range(n_chunks)):
        sb_scr[i] = s
        if i > 0:
            _, k = qk(i)
            s = s * t["c_b"] + _dot_tn((k * t["k_b"]).astype(BF16), chunk(v_ref, i).astype(BF16))

    for i in range(n_chunks):
        q, k = qk(i)
        scores = _dot_nt(q.astype(BF16), k.astype(BF16)) * t["mask"]
        o = _dot(scores.astype(BF16), chunk(v_ref, i).astype(BF16))
        o = o + _dot((q * t["q_f"]).astype(BF16), sf_scr[i].astype(BF16))
        o = o + _dot((q * t["q_b"]).astype(BF16), sb_scr[i].astype(BF16))
        o_ref[i * c:(i + 1) * c, :] = _group_norm_gate(o, chunk(g_ref, i))


def _ret_sample(proj, decay, cos, sin_signed, s0f, s0b, j):
    kq = RET_QK_W // RET_DK
    kv = 2 * RET_QK_W // RET_DV
    kg = kv + RET_HEADS
    rb = N_PROMPT // DEC_SEQ
    n_chunks = DEC_SEQ // RET_CHUNK
    return pl.pallas_call(
        _ret_sample_body,
        grid=(DEC_BATCH, RET_HEADS),
        in_specs=[
            pl.BlockSpec(memory_space=pltpu.SMEM),
            pl.BlockSpec((DEC_SEQ, RET_DK), lambda b, h: (rb + b, h)),
            pl.BlockSpec((DEC_SEQ, RET_DK), lambda b, h: (rb + b, kq + h)),
            pl.BlockSpec((DEC_SEQ, RET_DV), lambda b, h: (rb + b, kv + h)),
            pl.BlockSpec((DEC_SEQ, RET_DV), lambda b, h: (rb + b, kg + h)),
            pl.BlockSpec((DEC_SEQ, RET_DK), lambda b, h: (0, 0)),
            pl.BlockSpec((DEC_SEQ, RET_DK), lambda b, h: (0, 0)),
            pl.BlockSpec((None, None, None, RET_DK, RET_DV), lambda b, h: (b, j, h, 0, 0)),
            pl.BlockSpec((None, None, None, RET_DK, RET_DV), lambda b, h: (b, j, h, 0, 0)),
        ],
        out_specs=pl.BlockSpec((DEC_SEQ, RET_DV), lambda b, h: (b, h)),
        out_shape=jax.ShapeDtypeStruct((N_SAMPLE, RET_V_W), BF16),
        scratch_shapes=[pltpu.VMEM((n_chunks, RET_DK, RET_DV), F32),
                        pltpu.VMEM((n_chunks, RET_DK, RET_DV), F32)],
        compiler_params=_cparams(2),
        name="ret_sample",
    )(decay, proj, proj, proj, proj, cos, sin_signed, s0f, s0b)


def _softmax_pv(parts):
    m = parts[0][0].max(axis=-1, keepdims=True)
    for s, _ in parts[1:]:
        m = jnp.maximum(m, s.max(axis=-1, keepdims=True))
    es = [jnp.exp(s - m) for s, _ in parts]
    denom = es[0].sum(axis=-1, keepdims=True)
    for e in es[1:]:
        denom = denom + e.sum(axis=-1, keepdims=True)
    inv = 1.0 / denom
    o = _dot((es[0] * inv).astype(BF16), parts[0][1])
    for e, (_, v) in zip(es[1:], parts[1:]):
        o = o + _dot((e * inv).astype(BF16), v)
    return o


def _ctx_attn_body(q_ref, k_ref, v_ref, o_ref):
    outs = []
    for h in range(NA_HEADS):
        sl = slice(h * NA_HD, (h + 1) * NA_HD)
        q = (q_ref[:, sl] * (NA_HD ** -0.5)).astype(BF16)
        s = _dot_nt(q, k_ref[:, sl].astype(BF16))
        outs.append(_softmax_pv([(s, v_ref[:, sl].astype(BF16))]))
    o_ref[...] = jnp.concatenate(outs, axis=-1).astype(BF16)


def _ctx_attn(proj):
    return pl.pallas_call(
        _ctx_attn_body,
        grid=(BATCH,),
        in_specs=[
            pl.BlockSpec((SEQ, D_MODEL), lambda b: (b, 0)),
            pl.BlockSpec((SEQ, D_MODEL), lambda b: (b, 1)),
            pl.BlockSpec((SEQ, D_MODEL), lambda b: (b, 2)),
        ],
        out_specs=pl.BlockSpec((SEQ, D_MODEL), lambda b: (b, 0)),
        out_shape=jax.ShapeDtypeStruct((N_PROMPT, D_MODEL), BF16),
        compiler_params=_cparams(1),
        name="ctx_attn",
    )(proj, proj, proj)


NA_QT = 256
NA_HPB = LANES // NA_HD
NA_WIN = NA_KH * GRID_W


def _na_attn_body(q_ref, k_ref, v_ref, ck_ref, cv_ref, tbl_ref, o_ref):
    t = pl.program_id(1)
    rows_per_tile = NA_QT // GRID_W
    n_rows = DEC_SEQ // GRID_W
    heads = [slice(hh * NA_HD, (hh + 1) * NA_HD) for hh in range(NA_HPB)]
    q = [(q_ref[:, sl] * (NA_HD ** -0.5)).astype(BF16) for sl in heads]
    s_ctx = [_dot_nt(q[hh], ck_ref[:, sl].astype(BF16)) for hh, sl in enumerate(heads)]
    o_win = [[] for _ in heads]
    p_ctx = [[] for _ in heads]
    for u in range(rows_per_tile):
        qr = t * rows_per_tile + u
        rs = jnp.clip(qr - NA_KH // 2, 0, n_rows - NA_KH)
        i0 = rs - qr + (NA_KH - 1)
        start = pl.multiple_of(rs * GRID_W, GRID_W)
        kw = k_ref[pl.ds(start, NA_WIN), :].astype(BF16)
        vw = v_ref[pl.ds(start, NA_WIN), :].astype(BF16)
        rows = slice(u * GRID_W, (u + 1) * GRID_W)
        for hh, sl in enumerate(heads):
            sw = _dot_nt(q[hh][rows], kw[:, sl]) + tbl_ref[hh, i0]
            sc = s_ctx[hh][rows]
            m = jnp.maximum(sw.max(axis=-1, keepdims=True), sc.max(axis=-1, keepdims=True))
            ew = jnp.exp(sw - m)
            ec = jnp.exp(sc - m)
            inv = 1.0 / (ew.sum(axis=-1, keepdims=True) + ec.sum(axis=-1, keepdims=True))
            o_win[hh].append(_dot((ew * inv).astype(BF16), vw[:, sl]))
            p_ctx[hh].append((ec * inv).astype(BF16))
    outs = []
    for hh, sl in enumerate(heads):
        o_ctx = _dot(jnp.concatenate(p_ctx[hh], axis=0), cv_ref[:, sl].astype(BF16))
        outs.append(jnp.concatenate(o_win[hh], axis=0) + o_ctx)
    o_ref[...] = jnp.concatenate(outs, axis=-1).astype(BF16)


def _na_attn(proj, cache_k, cache_v, tbl, j):
    n_qt = DEC_SEQ // NA_QT
    rb_q = N_PROMPT // NA_QT
    rb_kv = N_PROMPT // DEC_SEQ
    cb = D_MODEL // LANES
    ck = cache_k.reshape(DEC_BATCH, -1, PAST_LEN, D_MODEL)
    cv = cache_v.reshape(DEC_BATCH, -1, PAST_LEN, D_MODEL)
    return pl.pallas_call(
        _na_attn_body,
        grid=(cb, n_qt, DEC_BATCH),
        in_specs=[
            pl.BlockSpec((NA_QT, LANES), lambda p, t, b: (rb_q + b * n_qt + t, p)),
            pl.BlockSpec((DEC_SEQ, LANES), lambda p, t, b: (rb_kv + b, cb + p)),
            pl.BlockSpec((DEC_SEQ, LANES), lambda p, t, b: (rb_kv + b, 2 * cb + p)),
            pl.BlockSpec((None, None, PAST_LEN, LANES), lambda p, t, b: (b, j, 0, p)),
            pl.BlockSpec((None, None, PAST_LEN, LANES), lambda p, t, b: (b, j, 0, p)),
            pl.BlockSpec((NA_HPB, NA_KH, GRID_W, NA_WIN), lambda p, t, b: (p, 0, 0, 0)),
        ],
        out_specs=pl.BlockSpec((NA_QT, LANES), lambda p, t, b: (b * n_qt + t, p)),
        out_shape=jax.ShapeDtypeStruct((N_SAMPLE, D_MODEL), BF16),
        compiler_params=_cparams(3),
        name="na_attn",
    )(proj, proj, proj, ck, cv, tbl)


def _na_bias_table(rpb):
    c = np.arange(GRID_W)
    cs = np.clip(c - NA_KW // 2, 0, GRID_W - NA_KW)
    col_valid = (c[None, :] >= cs[:, None]) & (c[None, :] < cs[:, None] + NA_KW)
    col_off = np.clip(c[None, :] - c[:, None] + NA_KW - 1, 0, 2 * NA_KW - 2)
    sel_c = (col_off[..., None] == np.arange(2 * NA_KW - 1)).astype(np.float32)
    t = jnp.einsum("hij,qkj->hqik", rpb, sel_c, precision=lax.Precision.HIGHEST)
    t = jnp.where(col_valid[None, :, None, :], t, -jnp.inf)
    return jnp.stack([t[:, :, i0:i0 + NA_KH, :].reshape(NA_HEADS, GRID_W, NA_WIN) for i0 in range(NA_KH)], axis=1)


def _outproj_body(ap_ref, as_ref, w_ref, x_ref, g_ref, mod_ref, wr_ref, br_ref, xo_ref, tok_ref, lg_ref):
    i = pl.program_id(0)
    a = jnp.where(i < N_PROMPT // TM, ap_ref[...], as_ref[...])
    x = x_ref[...] + mod_ref[2:3, :] * _dot(a, w_ref[...])
    xo_ref[...] = x
    tok = _norm_mod(x, g_ref[...], mod_ref[3:4, :], mod_ref[4:5, :])
    tok_ref[...] = tok
    hi = tok.astype(BF16)
    lo = (tok - hi.astype(F32)).astype(BF16)
    wr = wr_ref[...]
    wr_hi = wr.astype(BF16)
    wr_lo = (wr - wr_hi.astype(F32)).astype(BF16)
    lg_ref[...] = _dot(hi, wr_hi) + _dot(lo, wr_hi) + _dot(hi, wr_lo) + br_ref[...]


def _outproj(a_prompt, a_sample, w_bf16, x, g, mod, wr, br):
    k_in = a_prompt.shape[1]
    n_pt = N_PROMPT // TM
    return pl.pallas_call(
        _outproj_body,
        grid=(N_TOK // TM,),
        in_specs=[
            pl.BlockSpec((TM, k_in), lambda i: (jnp.minimum(i, n_pt - 1), 0)),
            pl.BlockSpec((TM, k_in), lambda i: (jnp.maximum(i - n_pt, 0), 0)),
            pl.BlockSpec((k_in, D_MODEL), lambda i: (0, 0)),
            pl.BlockSpec((TM, D_MODEL), lambda i: (i, 0)),
            pl.BlockSpec((1, D_MODEL), lambda i: (0, 0)),
            pl.BlockSpec((None, MOD_ROWS, D_MODEL), lambda i: (_group_of_tile(i, TM), 0, 0)),
            pl.BlockSpec((D_MODEL, LANES), lambda i: (0, 0)),
            pl.BlockSpec((1, LANES), lambda i: (0, 0)),
        ],
        out_specs=[
            pl.BlockSpec((TM, D_MODEL), lambda i: (i, 0)),
            pl.BlockSpec((TM, D_MODEL), lambda i: (i, 0)),
            pl.BlockSpec((TM, LANES), lambda i: (i, 0)),
        ],
        out_shape=[
            jax.ShapeDtypeStruct((N_TOK, D_MODEL), F32),
            jax.ShapeDtypeStruct((N_TOK, D_MODEL), F32),
            jax.ShapeDtypeStruct((N_TOK, LANES), F32),
        ],
        compiler_params=_cparams(1),
        name="outproj",
    )(a_prompt, a_sample, w_bf16, x, g.reshape(1, D_MODEL), mod, wr, br)


def _router_body(lg_ref, idx_ref, gate_ref, rank_ref, cnt_ref, carry):
    i = pl.program_id(0)

    @pl.when(i == 0)
    def _():
        carry[...] = jnp.zeros_like(carry)

    lane = lax.broadcasted_iota(I32, (ROUTER_TM, LANES), 1)
    l = jnp.where(lane < N_EXPERTS, lg_ref[...], -jnp.inf)
    vals, idxs, hots = [], [], []
    for _ in range(TOP_K):
        m = l.max(axis=-1, keepdims=True)
        idx = jnp.where(l == m, lane, LANES).min(axis=-1, keepdims=True)
        hot = lane == idx
        l = jnp.where(hot, -jnp.inf, l)
        vals.append(m)
        idxs.append(idx)
        hots.append(hot)
    es = [jnp.exp(v - vals[0]) for v in vals]
    denom = es[0] + es[1] + es[2] + es[3]
    hot_all = (hots[0] | hots[1] | hots[2] | hots[3]).astype(F32)
    r = lax.broadcasted_iota(I32, (ROUTER_TM, ROUTER_TM), 0)
    c = lax.broadcasted_iota(I32, (ROUTER_TM, ROUTER_TM), 1)
    before = _dot((c < r).astype(BF16), hot_all.astype(BF16)) + carry[...]
    idx_out = jnp.zeros((ROUTER_TM, LANES), I32)
    gate_out = jnp.zeros((ROUTER_TM, LANES), F32)
    rank_out = jnp.zeros((ROUTER_TM, LANES), I32)
    for k in range(TOP_K):
        rank_k = jnp.where(hots[k], before, 0.0).sum(axis=-1, keepdims=True).astype(I32)
        idx_out = jnp.where(lane == k, idxs[k], idx_out)
        gate_out = jnp.where(lane == k, es[k] / denom, gate_out)
        rank_out = jnp.where(lane == k, rank_k, rank_out)
    idx_ref[...] = idx_out
    gate_ref[...] = gate_out
    rank_ref[...] = rank_out
    carry[...] = carry[...] + hot_all.sum(axis=0, keepdims=True)
    cnt_ref[...] = carry[...].astype(I32)


def _router(logits):
    spec = pl.BlockSpec((ROUTER_TM, LANES), lambda i: (i, 0))
    return pl.pallas_call(
        _router_body,
        grid=(N_TOK // ROUTER_TM,),
        in_specs=[spec],
        out_specs=[spec, spec, spec, pl.BlockSpec((1, LANES), lambda i: (0, 0))],
        out_shape=[
            jax.ShapeDtypeStruct((N_TOK, LANES), I32),
            jax.ShapeDtypeStruct((N_TOK, LANES), F32),
            jax.ShapeDtypeStruct((N_TOK, LANES), I32),
            jax.ShapeDtypeStruct((1, LANES), I32),
        ],
        scratch_shapes=[pltpu.VMEM((1, LANES), F32)],
        compiler_params=_cparams(1),
        name="router",
    )(logits)


SC_CORES = 2
SC_SUBCORES = 16
SC_WORKERS = SC_CORES * SC_SUBCORES
SC_CH = 32


def _sc_mesh():
    return plsc.VectorSubcoreMesh(core_axis_name="c", subcore_axis_name="s",
                                  num_cores=SC_CORES, num_subcores=SC_SUBCORES)


def _sc_worker():
    return lax.axis_index("s") * SC_CORES + lax.axis_index("c")


def _dispatch(tok, pos_km):
    n, d = tok.shape
    n_ch = n // (SC_WORKERS * SC_CH)
    assert n_ch * SC_WORKERS * SC_CH == n and n_ch % 2 == 0

    @functools.partial(
        pl.kernel, mesh=_sc_mesh(), out_type=jax.ShapeDtypeStruct((MOE_ROWS, d), tok.dtype),
        scratch_types=[pltpu.VMEM((n_ch, TOP_K, SC_CH), I32), pltpu.VMEM((2, SC_CH, d), tok.dtype),
                       pltpu.SemaphoreType.DMA((2,)), pltpu.SemaphoreType.DMA((2,))])
    def scatter_rows(tok_hbm, pos_hbm, out_hbm, idx_v, rows_v, load_sem, scat_sem):
        wid = _sc_worker()
        base = wid * (n_ch * SC_CH)
        pltpu.sync_copy(pos_hbm.at[wid], idx_v)

        def load(g, slot):
            rows = pl.ds(pl.multiple_of(base + g * SC_CH, 8), SC_CH)
            return pltpu.make_async_copy(tok_hbm.at[rows], rows_v.at[slot], load_sem.at[slot])

        def scatter(g, k, slot):
            return pltpu.make_async_copy(rows_v.at[slot], out_hbm.at[idx_v.at[g, k]], scat_sem.at[slot])

        load(0, 0).start()

        @pl.loop(0, n_ch, step=2)
        def _(g0):
            for slot in range(2):
                g = g0 + slot
                load(g, slot).wait()

                @pl.when(g + 1 < n_ch)
                def _():
                    load(g + 1, 1 - slot).start()

                for k in range(TOP_K):
                    scatter(g, k, slot).start()
                for k in range(TOP_K):
                    scatter(g, k, slot).wait()

    pos = pos_km.reshape(TOP_K, SC_WORKERS, n_ch, SC_CH).transpose(1, 2, 0, 3)
    return scatter_rows(tok, pos)


def _gather_rows(table, idx):
    b, d = idx.shape[0], table.shape[1]
    n_ch = b // (SC_WORKERS * SC_CH)
    assert n_ch * SC_WORKERS * SC_CH == b and n_ch % 2 == 0

    @functools.partial(
        pl.kernel, mesh=_sc_mesh(), out_type=jax.ShapeDtypeStruct((b, d), table.dtype),
        scratch_types=[pltpu.VMEM((n_ch, SC_CH), I32), pltpu.VMEM((2, SC_CH, d), table.dtype),
                       pltpu.SemaphoreType.DMA((2,))])
    def gather_rows(table_hbm, idx_hbm, out_hbm, idx_v, rows_v, sems):
        wid = _sc_worker()
        base = wid * (n_ch * SC_CH)
        pltpu.sync_copy(idx_hbm.at[wid], idx_v)

        def gather(g, slot):
            return pltpu.make_async_copy(table_hbm.at[idx_v.at[g]], rows_v.at[slot], sems.at[slot])

        def write_out(g, slot):
            rows = pl.ds(pl.multiple_of(base + g * SC_CH, 8), SC_CH)
            pltpu.sync_copy(rows_v.at[slot], out_hbm.at[rows])

        gather(0, 0).start()

        @pl.loop(0, n_ch, step=2)
        def _(g):
            gather(g + 1, 1).start()
            gather(g, 0).wait()
            write_out(g, 0)

            @pl.when(g + 2 < n_ch)
            def _():
                gather(g + 2, 0).start()

            gather(g + 1, 1).wait()
            write_out(g + 1, 1)

    return gather_rows(table, idx.reshape(SC_WORKERS, n_ch, SC_CH))


def _moe_body(te_ref, tv_ref, slot_ref, nxt_ref, x_ref, w1_hbm, b1_ref, w2_hbm, b2_ref, perm_ref, o_ref,
              w1_buf, w2_buf, w1_scr, w2_scr, sem, *, layer):
    t = pl.program_id(0)
    e = te_ref[t]
    slot = slot_ref[t]
    nxt = nxt_ref[t]
    first = jnp.logical_or(t == 0, e != te_ref[jnp.maximum(t - 1, 0)])

    def fetch(expert, s):
        return (pltpu.make_async_copy(w1_hbm.at[layer, expert], w1_buf.at[s], sem.at[s, 0]),
                pltpu.make_async_copy(w2_hbm.at[layer, expert], w2_buf.at[s], sem.at[s, 1]))

    @pl.when(t == 0)
    def _():
        for cp in fetch(e, slot):
            cp.start()

    @pl.when(jnp.logical_and(tv_ref[t] == 1, first))
    def _():
        for cp in fetch(e, slot):
            cp.wait()

        @pl.when(nxt != e)
        def _():
            for cp in fetch(nxt, 1 - slot):
                cp.start()

        w2_scr[...] = w2_buf[slot].astype(BF16)
        half = PERM_W // 2
        for c in range(2 * D_EXPERT // PERM_W):
            blk = _dot(w1_buf[slot, :, c * PERM_W:(c + 1) * PERM_W].astype(BF16), perm_ref[...]).astype(BF16)
            w1_scr[:, c * half:(c + 1) * half] = blk[:, :half]
            w1_scr[:, D_EXPERT + c * half:D_EXPERT + (c + 1) * half] = blk[:, half:]

    @pl.when(tv_ref[t] == 0)
    def _():
        o_ref[...] = jnp.zeros_like(o_ref)

    @pl.when(tv_ref[t] == 1)
    def _():
        hh = _dot(x_ref[...].astype(BF16), w1_scr[...]) + b1_ref[...]
        h_glu = jnp.minimum(hh[:, :D_EXPERT], SWIGLU_LIMIT)
        h_lin = jnp.clip(hh[:, D_EXPERT:], -SWIGLU_LIMIT, SWIGLU_LIMIT)
        a = h_glu * _sigmoid(SWIGLU_ALPHA * h_glu) * (h_lin + 1.0)
        o_ref[...] = _dot(a.astype(BF16), w2_scr[...]) + b2_ref[...]


def _moe_experts(tile_expert, tile_valid, tile_slot, tile_next, xs, w1, b1_split, w2, b2, perm, layer):
    grid_spec = pltpu.PrefetchScalarGridSpec(
        num_scalar_prefetch=4,
        grid=(MOE_TILES,),
        in_specs=[
            pl.BlockSpec((MOE_TM, D_MODEL), lambda t, te, *_: (t, 0)),
            pl.BlockSpec(memory_space=pl.ANY),
            pl.BlockSpec((None, None, 1, 2 * D_EXPERT), lambda t, te, *_: (layer, te[t], 0, 0)),
            pl.BlockSpec(memory_space=pl.ANY),
            pl.BlockSpec((None, None, 1, D_MODEL), lambda t, te, *_: (layer, te[t], 0, 0)),
            pl.BlockSpec((PERM_W, PERM_W), lambda t, te, *_: (0, 0)),
        ],
        out_specs=pl.BlockSpec((MOE_TM, D_MODEL), lambda t, te, *_: (t, 0)),
        scratch_shapes=[
            pltpu.VMEM((2, D_MODEL, 2 * D_EXPERT), F32),
            pltpu.VMEM((2, D_EXPERT, D_MODEL), F32),
            pltpu.VMEM((D_MODEL, 2 * D_EXPERT), BF16),
            pltpu.VMEM((D_EXPERT, D_MODEL), BF16),
            pltpu.SemaphoreType.DMA((2, 2)),
        ],
    )
    return pl.pallas_call(
        functools.partial(_moe_body, layer=layer),
        grid_spec=grid_spec,
        out_shape=jax.ShapeDtypeStruct((MOE_ROWS, D_MODEL), F32),
        compiler_params=_cparams(1),
        name="moe_experts",
    )(tile_expert, tile_valid, tile_slot, tile_next, xs, w1, b1_split, w2, b2, perm)


COMBINE_TM = 256


def _combine_body(*refs, final):
    y_refs = refs[:TOP_K]
    gate_ref, x_ref, mod_ref, g_ref, o_ref = refs[TOP_K:]
    gates = gate_ref[...]
    y = gates[:, 0:1] * y_refs[0][...]
    for k in range(1, TOP_K):
        y = y + gates[:, k:k + 1] * y_refs[k][...]
    x = x_ref[...] + mod_ref[5:6, :] * y
    if final:
        x = (x * lax.rsqrt(jnp.mean(x * x, axis=-1, keepdims=True) + EPS)) * g_ref[...]
    o_ref[...] = x


def _combine(y_km, gates, x, mod, g_final, final):
    n_t = N_TOK // COMBINE_TM
    y_specs = [pl.BlockSpec((COMBINE_TM, D_MODEL), lambda i, k=k: (k * n_t + i, 0)) for k in range(TOP_K)]
    return pl.pallas_call(
        functools.partial(_combine_body, final=final),
        grid=(n_t,),
        in_specs=y_specs + [
            pl.BlockSpec((COMBINE_TM, LANES), lambda i: (i, 0)),
            pl.BlockSpec((COMBINE_TM, D_MODEL), lambda i: (i, 0)),
            pl.BlockSpec((None, MOD_ROWS, D_MODEL), lambda i: (_group_of_tile(i, COMBINE_TM), 0, 0)),
            pl.BlockSpec((1, D_MODEL), lambda i: (0, 0)),
        ],
        out_specs=pl.BlockSpec((COMBINE_TM, D_MODEL), lambda i: (i, 0)),
        out_shape=jax.ShapeDtypeStruct((N_TOK, D_MODEL), F32),
        compiler_params=_cparams(1),
        name="combine",
    )(*([y_km] * TOP_K), gates, x, mod, g_final.reshape(1, D_MODEL))


def _routing_tables(idx, rank, counts):
    counts = counts[0, :N_EXPERTS]
    padded = ((counts + MOE_TM - 1) // MOE_TM) * MOE_TM
    ends = jnp.cumsum(padded)
    offsets = ends - padded
    idx = idx[:, :TOP_K]
    pos = rank[:, :TOP_K] + jnp.sum(
        jnp.where(idx[..., None] == jnp.arange(N_EXPERTS, dtype=I32), offsets, 0), axis=-1)
    pos_km = pos.T.astype(I32)
    tile_start = jnp.arange(MOE_TILES, dtype=I32) * MOE_TM
    tile_valid = (tile_start < ends[-1]).astype(I32)
    tile_expert = jnp.sum((tile_start[:, None] >= ends[None, :]).astype(I32), axis=-1)
    last_expert = jnp.sum((ends[-1] - 1 >= ends).astype(I32))
    tile_expert = jnp.where(tile_valid == 1, tile_expert, last_expert).astype(I32)
    e_ids = jnp.arange(N_EXPERTS, dtype=I32)
    owns = counts > 0
    ordinal = jnp.cumsum(owns.astype(I32)) - 1
    later = jnp.where(owns[None, :] & (e_ids[None, :] > e_ids[:, None]), e_ids[None, :], N_EXPERTS)
    nxt = jnp.min(later, axis=1)
    nxt = jnp.where(nxt == N_EXPERTS, e_ids, nxt)
    onehot = (tile_expert[:, None] == e_ids[None, :]).astype(I32)
    tile_slot = jnp.sum(onehot * (ordinal % 2)[None, :], axis=1).astype(I32)
    tile_next = jnp.sum(onehot * nxt[None, :], axis=1).astype(I32)
    return pos_km, tile_expert, tile_valid, tile_slot, tile_next


def _rope_tables():
    t = np.arange(DEC_SEQ)
    row = (t // GRID_W).astype(np.float32)
    col = (t % GRID_W).astype(np.float32)
    d_axis = RET_DK // 2
    inv = jnp.asarray(ROPE_BASE, F32) ** (-jnp.arange(0, d_axis, 2, dtype=F32) / d_axis)
    ang = jnp.concatenate([row[:, None] * inv, col[:, None] * inv], axis=-1)
    cos = jnp.repeat(jnp.cos(ang), 2, axis=-1)
    sin = jnp.repeat(jnp.sin(ang), 2, axis=-1)
    sign = jnp.asarray(np.tile(np.array([-1.0, 1.0], np.float32), RET_DK // 2))
    return cos, sin * sign


def _deinterleave_matrix():
    p = np.zeros((PERM_W, PERM_W), np.float32)
    half = PERM_W // 2
    for j in range(half):
        p[2 * j, j] = 1.0
        p[2 * j + 1, half + j] = 1.0
    return jnp.asarray(p, BF16)


def kernel(x_prompt, x_sample, state_ret_fwd, state_ret_bwd, cache_na_k, cache_na_v, c, c_ctx, ada_w, ada_b, norm_mix, norm_ffn, norm_final, ret_w_in, ret_w_out, ret_decay_fwd, ret_decay_bwd, na_w_in, na_w_out, na_rpb, moe_w_router, moe_b_router, moe_w1, moe_b1, moe_w2, moe_b2):
    x = jnp.concatenate([x_prompt.reshape(N_PROMPT, D_MODEL), x_sample.reshape(N_SAMPLE, D_MODEL)], axis=0)

    cond = jnp.zeros((MOD_ROWS, D_MODEL), F32).at[0].set(c_ctx).at[1:1 + DEC_BATCH].set(c)
    mod_all = _adaln(cond, ada_w, ada_b)
    mod_all = mod_all[:, :N_GROUPS].reshape(DEPTH, N_GROUPS, 6, D_MODEL)
    mod_all = jnp.pad(mod_all, ((0, 0), (0, 0), (0, MOD_ROWS - 6), (0, 0)))

    cos, sin_signed = _rope_tables()
    perm = _deinterleave_matrix()
    wr_all = jnp.pad(moe_w_router, ((0, 0), (0, 0), (0, LANES - N_EXPERTS)))
    br_all = jnp.pad(moe_b_router, ((0, 0), (0, LANES - N_EXPERTS))).reshape(DEPTH, 1, LANES)
    b1_split = jnp.concatenate([moe_b1[..., 0::2], moe_b1[..., 1::2]], axis=-1).reshape(DEPTH, N_EXPERTS, 1, 2 * D_EXPERT)
    b2_all = moe_b2.reshape(DEPTH, N_EXPERTS, 1, D_MODEL)
    ret_states = None
    na_caches = None
    for layer in range(DEPTH):
        mod = mod_all[layer]
        j = layer // 2
        if layer % 2 == 0:
            proj = _inproj(x, norm_mix[layer], mod, _cast_layer_bf16(ret_w_in, j))
            decay = jnp.stack([ret_decay_fwd[j], ret_decay_bwd[j]])
            a_p, s_f, s_b = _ret_prompt(proj, decay, j, ret_states)
            ret_states = (s_f, s_b)
            a_s = _ret_sample(proj, decay, cos, sin_signed, state_ret_fwd, state_ret_bwd, j)
            w_out = _cast_layer_bf16(ret_w_out, j)
        else:
            proj, ck, cv = _inproj(x, norm_mix[layer], mod, _cast_layer_bf16(na_w_in, j), j, na_caches)
            na_caches = (ck, cv)
            a_p = _ctx_attn(proj)
            a_s = _na_attn(proj, cache_na_k, cache_na_v, _na_bias_table(na_rpb[j]), j)
            w_out = _cast_layer_bf16(na_w_out, j)

        x, tok, logits = _outproj(a_p, a_s, w_out, x, norm_ffn[layer], mod, wr_all[layer], br_all[layer])

        idx, gates, rank, counts = _router(logits)
        pos_km, tile_expert, tile_valid, tile_slot, tile_next = _routing_tables(idx, rank, counts)
        xs = _dispatch(tok, pos_km)
        ys = _moe_experts(tile_expert, tile_valid, tile_slot, tile_next, xs, moe_w1, b1_split, moe_w2, b2_all,
                          perm, layer)
        y_km = _gather_rows(ys, pos_km.reshape(-1))
        x = _combine(y_km, gates, x, mod, norm_final, final=(layer == DEPTH - 1))

    y_prompt = x[:N_PROMPT].reshape(BATCH, SEQ, D_MODEL)
    y_sample = x[N_PROMPT:].reshape(DEC_BATCH, DEC_SEQ, D_MODEL)
    return (y_prompt, y_sample, ret_states[0], ret_states[1],
            na_caches[0].reshape(BATCH, N_NA_LAYERS, SEQ, NA_HEADS, NA_HD),
            na_caches[1].reshape(BATCH, N_NA_LAYERS, SEQ, NA_HEADS, NA_HD))
```

```python
import functools

import numpy as np
import jax
import jax.numpy as jnp
from jax import lax
from jax.experimental import pallas as pl
from jax.experimental.pallas import tpu as pltpu
from jax.experimental.pallas import tpu_sc as plsc

F32 = jnp.float32
BF16 = jnp.bfloat16
I32 = jnp.int32
U32 = jnp.uint32

D_MODEL = 1024
BATCH = 32
SEQ = 256
DEPTH = 4
DEC_BATCH = 2
DEC_SEQ = 1024
PAST_LEN = 512
GRID_W = 64
EPS = 1e-6
ROPE_BASE = 10000.0
RET_HEADS = 4
RET_DK = D_MODEL // RET_HEADS
RET_DV = 2 * RET_DK
RET_QK_W = RET_HEADS * RET_DK
RET_V_W = RET_HEADS * RET_DV
NA_HEADS = 16
NA_HD = D_MODEL // NA_HEADS
NA_KH = 8
NA_KW = 16
N_EXPERTS = 32
TOP_K = 4
D_EXPERT = D_MODEL
SWIGLU_LIMIT = 7.0
SWIGLU_ALPHA = 1.702
N_RET_LAYERS = (DEPTH + 1) // 2
N_NA_LAYERS = DEPTH // 2

N_PROMPT = BATCH * SEQ
N_SAMPLE = DEC_BATCH * DEC_SEQ
N_TOK = N_PROMPT + N_SAMPLE
N_GROUPS = 1 + DEC_BATCH
MOD_ROWS = 8
LANES = 128

TM = 256
RET_CHUNK = 256
MOE_TM = 256
MOE_ROWS = N_TOK * TOP_K + N_EXPERTS * MOE_TM
MOE_TILES = MOE_ROWS // MOE_TM
ROUTER_TM = 512
PERM_W = 256
VMEM_LIMIT = 56 * 1024 * 1024


def _cparams(n_axes, vmem=VMEM_LIMIT):
    return pltpu.CompilerParams(dimension_semantics=("arbitrary",) * n_axes, vmem_limit_bytes=vmem)


def _group_of_tile(i, tm):
    return jnp.maximum((i * tm) // DEC_SEQ - (N_PROMPT // DEC_SEQ - 1), 0)


def _sigmoid(x):
    return 1.0 / (1.0 + jnp.exp(-x))


def _silu(x):
    return x * _sigmoid(x)


def _log_sigmoid(x):
    return jnp.minimum(x, 0.0) - jnp.log(1.0 + jnp.exp(-jnp.abs(x)))


def _norm_mod(x, g, shift, scale):
    y = x * lax.rsqrt(jnp.mean(x * x, axis=-1, keepdims=True) + EPS)
    return (y * g) * (1.0 + scale) + shift


def _dot(a, b):
    return jnp.dot(a, b, preferred_element_type=F32)


HALF = D_MODEL // 2
HI_MASK = 0xFFFF0000


def _pack_rows(x):
    lo = pltpu.bitcast(x[:, :HALF].astype(BF16).astype(F32), U32)
    hi = pltpu.bitcast(x[:, HALF:].astype(BF16).astype(F32), U32)
    return (lo >> 16) | (hi & jnp.uint32(HI_MASK))


def _unpack_rows(p):
    return pltpu.bitcast(p << 16, F32), pltpu.bitcast(p & jnp.uint32(HI_MASK), F32)


def _dot_nt(a, b):
    return lax.dot_general(a, b, (((1,), (1,)), ((), ())), preferred_element_type=F32)


def _dot_tn(a, b):
    return lax.dot_general(a, b, (((0,), (0,)), ((), ())), preferred_element_type=F32)


ADA_TN = 1536


def _adaln_body(cond_ref, w_ref, b_ref, o_ref):
    s = _silu(cond_ref[...]).astype(BF16)
    o_ref[...] = _dot(s, w_ref[...].astype(BF16)) + b_ref[...]


def _adaln(cond, ada_w, ada_b):
    n_out = 6 * D_MODEL
    return pl.pallas_call(
        _adaln_body,
        grid=(DEPTH, n_out // ADA_TN),
        in_specs=[
            pl.BlockSpec((MOD_ROWS, D_MODEL), lambda l, j: (0, 0)),
            pl.BlockSpec((None, D_MODEL, ADA_TN), lambda l, j: (l, 0, j)),
            pl.BlockSpec((None, 1, ADA_TN), lambda l, j: (l, 0, j)),
        ],
        out_specs=pl.BlockSpec((None, MOD_ROWS, ADA_TN), lambda l, j: (l, 0, j)),
        out_shape=jax.ShapeDtypeStruct((DEPTH, MOD_ROWS, n_out), F32),
        compiler_params=_cparams(2),
        name="adaln",
    )(cond, ada_w, ada_b.reshape(DEPTH, 1, n_out))


CAST_TR = 256


def _cast_body(w_ref, o_ref):
    o_ref[...] = w_ref[...].astype(BF16)


def _cast_layer_bf16(w, j):
    _, r, c = w.shape
    return pl.pallas_call(
        _cast_body,
        grid=(r // CAST_TR,),
        in_specs=[pl.BlockSpec((None, CAST_TR, c), lambda i: (j, i, 0))],
        out_specs=pl.BlockSpec((CAST_TR, c), lambda i: (i, 0)),
        out_shape=jax.ShapeDtypeStruct((r, c), BF16),
        compiler_params=_cparams(1),
        name="cast_bf16",
    )(w)


def _inproj_body(x_ref, g_ref, mod_ref, w_ref, *rest, cache_layer, first_cache):
    o_ref = rest[-3] if cache_layer is not None else rest[-1]
    i = pl.program_id(0)
    h = _norm_mod(x_ref[...], g_ref[...], mod_ref[0:1, :], mod_ref[1:2, :]).astype(BF16)
    for c in range(o_ref.shape[1] // D_MODEL):
        sl = slice(c * D_MODEL, (c + 1) * D_MODEL)
        r = _dot(h, w_ref[:, sl])
        o_ref[:, sl] = r
        if cache_layer is not None and c >= 1:
            cache_ref = rest[-3 + c]

            @pl.when(i < BATCH)
            def _(r=r, cache_ref=cache_ref):
                if first_cache:
                    for l in range(N_NA_LAYERS):
                        cache_ref[l] = r if l == cache_layer else jnp.zeros_like(r)
                else:
                    cache_ref[...] = r


def _inproj(x, g, mod, w_bf16, cache_layer=None, prev_caches=None):
    n_out = w_bf16.shape[1]
    in_specs = [
        pl.BlockSpec((TM, D_MODEL), lambda i: (i, 0)),
        pl.BlockSpec((1, D_MODEL), lambda i: (0, 0)),
        pl.BlockSpec((None, MOD_ROWS, D_MODEL), lambda i: (_group_of_tile(i, TM), 0, 0)),
        pl.BlockSpec((D_MODEL, n_out), lambda i: (0, 0)),
    ]
    out_specs = [pl.BlockSpec((TM, n_out), lambda i: (i, 0))]
    out_shape = [jax.ShapeDtypeStruct((N_TOK, n_out), F32)]
    args = [x, g.reshape(1, D_MODEL), mod, w_bf16]
    aliases = {}
    first_cache = prev_caches is None
    if cache_layer is not None:
        assert TM == SEQ and n_out == 3 * D_MODEL
        cache = jax.ShapeDtypeStruct((BATCH, N_NA_LAYERS, SEQ, D_MODEL), F32)
        if first_cache:
            spec = pl.BlockSpec((None, N_NA_LAYERS, SEQ, D_MODEL), lambda i: (jnp.minimum(i, BATCH - 1), 0, 0, 0))
        else:
            spec = pl.BlockSpec((None, None, SEQ, D_MODEL),
                                lambda i: (jnp.minimum(i, BATCH - 1), cache_layer, 0, 0))
            in_specs += [pl.BlockSpec(memory_space=pl.ANY)] * 2
            aliases = {len(args): 1, len(args) + 1: 2}
            args += list(prev_caches)
        out_specs += [spec, spec]
        out_shape += [cache, cache]
    outs = pl.pallas_call(
        functools.partial(_inproj_body, cache_layer=cache_layer, first_cache=first_cache),
        grid=(N_TOK // TM,),
        in_specs=in_specs,
        out_specs=out_specs,
        out_shape=out_shape,
        input_output_aliases=aliases,
        compiler_params=_cparams(1),
        name="inproj",
    )(*args)
    return outs[0] if cache_layer is None else outs


def _decay_terms(dec_ref, h, c):
    lgf = _log_sigmoid(jnp.zeros((c, c), F32) + dec_ref[0, h])
    lgb = _log_sigmoid(jnp.zeros((c, c), F32) + dec_ref[1, h])
    row = lax.broadcasted_iota(I32, (c, c), 0)
    col = lax.broadcasted_iota(I32, (c, c), 1)
    diff = (row - col).astype(F32)
    lower = diff >= 0
    upper = diff <= 0
    mask = (jnp.where(lower, jnp.exp(jnp.where(lower, diff, 0.0) * lgf), 0.0)
            + jnp.where(upper, jnp.exp(jnp.where(upper, -diff, 0.0) * lgb), 0.0))
    pos = lax.broadcasted_iota(I32, (c, 1), 0).astype(F32)
    lgf1 = _log_sigmoid(jnp.zeros((c, 1), F32) + dec_ref[0, h])
    lgb1 = _log_sigmoid(jnp.zeros((c, 1), F32) + dec_ref[1, h])
    lgf0 = _log_sigmoid(jnp.zeros((1, 1), F32) + dec_ref[0, h])
    lgb0 = _log_sigmoid(jnp.zeros((1, 1), F32) + dec_ref[1, h])
    terms = dict(
        mask=mask,
        q_f=jnp.exp((pos + 1.0) * lgf1), k_f=jnp.exp((c - 1.0 - pos) * lgf1), c_f=jnp.exp(c * lgf0),
        q_b=jnp.exp((c - pos) * lgb1), k_b=jnp.exp(pos * lgb1), c_b=jnp.exp(c * lgb0),
    )
    return terms


def _group_norm_gate(o, g):
    o = o * lax.rsqrt(jnp.mean(o * o, axis=-1, keepdims=True) + EPS)
    return (_silu(g) * o).astype(BF16)


def _ret_prompt_body(dec_ref, q_ref, k_ref, v_ref, g_ref, *rest, layer, first):
    o_ref, sf_ref, sb_ref = rest[-3:]
    h = pl.program_id(1)
    t = _decay_terms(dec_ref, h, SEQ)
    q = q_ref[...]
    k = k_ref[...] * (RET_DK ** -0.5)
    v = v_ref[...].astype(BF16)
    scores = _dot_nt(q.astype(BF16), k.astype(BF16)) * t["mask"]
    o = _dot(scores.astype(BF16), v)
    o_ref[...] = _group_norm_gate(o, g_ref[...])
    for ref, k_dec in ((sf_ref, t["k_f"]), (sb_ref, t["k_b"])):
        s = _dot_tn((k * k_dec).astype(BF16), v)
        if first:
            for l in range(N_RET_LAYERS):
                ref[l] = s if l == layer else jnp.zeros_like(s)
        else:
            ref[...] = s


def _ret_prompt(proj, decay, layer, prev_states=None):
    kq = RET_QK_W // RET_DK
    kv = 2 * RET_QK_W // RET_DV
    kg = kv + RET_HEADS
    state = jax.ShapeDtypeStruct((BATCH, N_RET_LAYERS, RET_HEADS, RET_DK, RET_DV), F32)
    in_specs = [
        pl.BlockSpec(memory_space=pltpu.SMEM),
        pl.BlockSpec((SEQ, RET_DK), lambda b, h: (b, h)),
        pl.BlockSpec((SEQ, RET_DK), lambda b, h: (b, kq + h)),
        pl.BlockSpec((SEQ, RET_DV), lambda b, h: (b, kv + h)),
        pl.BlockSpec((SEQ, RET_DV), lambda b, h: (b, kg + h)),
    ]
    args = [decay, proj, proj, proj, proj]
    first = prev_states is None
    aliases = {}
    if first:
        s_spec = pl.BlockSpec((None, N_RET_LAYERS, None, RET_DK, RET_DV), lambda b, h: (b, 0, h, 0, 0))
    else:
        s_spec = pl.BlockSpec((None, None, None, RET_DK, RET_DV), lambda b, h: (b, layer, h, 0, 0))
        in_specs += [pl.BlockSpec(memory_space=pl.ANY)] * 2
        aliases = {len(args): 1, len(args) + 1: 2}
        args += list(prev_states)
    return pl.pallas_call(
        functools.partial(_ret_prompt_body, layer=layer, first=first),
        grid=(BATCH, RET_HEADS),
        in_specs=in_specs,
        out_specs=[pl.BlockSpec((SEQ, RET_DV), lambda b, h: (b, h)), s_spec, s_spec],
        out_shape=[jax.ShapeDtypeStruct((N_PROMPT, RET_V_W), BF16), state, state],
        input_output_aliases=aliases,
        compiler_params=_cparams(2),
        name="ret_prompt",
    )(*args)


def _rope(x, cos, sin_signed):
    n = x.shape[-1]
    lane = lax.broadcasted_iota(I32, x.shape, 1)
    swapped = jnp.where(lane % 2 == 0, pltpu.roll(x, n - 1, 1), pltpu.roll(x, 1, 1))
    return x * cos + swapped * sin_signed


def _ret_sample_body(dec_ref, q_ref, k_ref, v_ref, g_ref, cos_ref, sin_ref, s0f_ref, s0b_ref, o_ref,
                     sf_scr, sb_scr):
    h = pl.program_id(1)
    c = RET_CHUNK
    n_chunks = DEC_SEQ // c
    t = _decay_terms(dec_ref, h, c)

    def chunk(ref, i):
        return ref[i * c:(i + 1) * c, :]

    def qk(i):
        q = _rope(chunk(q_ref, i), chunk(cos_ref, i), chunk(sin_ref, i))
        k = _rope(chunk(k_ref, i) * (RET_DK ** -0.5), chunk(cos_ref, i), chunk(sin_ref, i))
        return q, k

    s = s0f_ref[...]
    for i in range(n_chunks):
        sf_scr[i] = s
        if i + 1 < n_chunks:
            _, k = qk(i)
            s = s * t["c_f"] + _dot_tn((k * t["k_f"]).astype(BF16), chunk(v_ref, i).astype(BF16))
    s = s0b_ref[...]
    for i in reversed(range(n_chunks)):
        sb_scr[i] = s
        if i > 0:
            _, k = qk(i)
            s = s * t["c_b"] + _dot_tn((k * t["k_b"]).astype(BF16), chunk(v_ref, i).astype(BF16))

    for i in range(n_chunks):
        q, k = qk(i)
        scores = _dot_nt(q.astype(BF16), k.astype(BF16)) * t["mask"]
        o = _dot(scores.astype(BF16), chunk(v_ref, i).astype(BF16))
        o = o + _dot((q * t["q_f"]).astype(BF16), sf_scr[i].astype(BF16))
        o = o + _dot((q * t["q_b"]).astype(BF16), sb_scr[i].astype(BF16))
        o_ref[i * c:(i + 1) * c, :] = _group_norm_gate(o, chunk(g_ref, i))


def _ret_sample(proj, decay, cos, sin_signed, s0f, s0b, j):
    kq = RET_QK_W // RET_DK
    kv = 2 * RET_QK_W // RET_DV
    kg = kv + RET_HEADS
    rb = N_PROMPT // DEC_SEQ
    n_chunks = DEC_SEQ // RET_CHUNK
    return pl.pallas_call(
        _ret_sample_body,
        grid=(DEC_BATCH, RET_HEADS),
        in_specs=[
            pl.BlockSpec(memory_space=pltpu.SMEM),
            pl.BlockSpec((DEC_SEQ, RET_DK), lambda b, h: (rb + b, h)),
            pl.BlockSpec((DEC_SEQ, RET_DK), lambda b, h: (rb + b, kq + h)),
            pl.BlockSpec((DEC_SEQ, RET_DV), lambda b, h: (rb + b, kv + h)),
            pl.BlockSpec((DEC_SEQ, RET_DV), lambda b, h: (rb + b, kg + h)),
            pl.BlockSpec((DEC_SEQ, RET_DK), lambda b, h: (0, 0)),
            pl.BlockSpec((DEC_SEQ, RET_DK), lambda b, h: (0, 0)),
            pl.BlockSpec((None, None, None, RET_DK, RET_DV), lambda b, h: (b, j, h, 0, 0)),
            pl.BlockSpec((None, None, None, RET_DK, RET_DV), lambda b, h: (b, j, h, 0, 0)),
        ],
        out_specs=pl.BlockSpec((DEC_SEQ, RET_DV), lambda b, h: (b, h)),
        out_shape=jax.ShapeDtypeStruct((N_SAMPLE, RET_V_W), BF16),
        scratch_shapes=[pltpu.VMEM((n_chunks, RET_DK, RET_DV), F32),
                        pltpu.VMEM((n_chunks, RET_DK, RET_DV), F32)],
        compiler_params=_cparams(2),
        name="ret_sample",
    )(decay, proj, proj, proj, proj, cos, sin_signed, s0f, s0b)


def _softmax_pv(parts):
    m = parts[0][0].max(axis=-1, keepdims=True)
    for s, _ in parts[1:]:
        m = jnp.maximum(m, s.max(axis=-1, keepdims=True))
    es = [jnp.exp(s - m) for s, _ in parts]
    denom = es[0].sum(axis=-1, keepdims=True)
    for e in es[1:]:
        denom = denom + e.sum(axis=-1, keepdims=True)
    inv = 1.0 / denom
    o = _dot((es[0] * inv).astype(BF16), parts[0][1])
    for e, (_, v) in zip(es[1:], parts[1:]):
        o = o + _dot((e * inv).astype(BF16), v)
    return o


def _ctx_attn_body(q_ref, k_ref, v_ref, o_ref):
    outs = []
    for h in range(NA_HEADS):
        sl = slice(h * NA_HD, (h + 1) * NA_HD)
        q = (q_ref[:, sl] * (NA_HD ** -0.5)).astype(BF16)
        s = _dot_nt(q, k_ref[:, sl].astype(BF16))
        outs.append(_softmax_pv([(s, v_ref[:, sl].astype(BF16))]))
    o_ref[...] = jnp.concatenate(outs, axis=-1).astype(BF16)


def _ctx_attn(proj):
    return pl.pallas_call(
        _ctx_attn_body,
        grid=(BATCH,),
        in_specs=[
            pl.BlockSpec((SEQ, D_MODEL), lambda b: (b, 0)),
            pl.BlockSpec((SEQ, D_MODEL), lambda b: (b, 1)),
            pl.BlockSpec((SEQ, D_MODEL), lambda b: (b, 2)),
        ],
        out_specs=pl.BlockSpec((SEQ, D_MODEL), lambda b: (b, 0)),
        out_shape=jax.ShapeDtypeStruct((N_PROMPT, D_MODEL), BF16),
        compiler_params=_cparams(1),
        name="ctx_attn",
    )(proj, proj, proj)


NA_QT = 256
NA_HPB = LANES // NA_HD
NA_WIN = NA_KH * GRID_W


def _na_attn_body(q_ref, k_ref, v_ref, ck_ref, cv_ref, tbl_ref, o_ref):
    t = pl.program_id(1)
    rows_per_tile = NA_QT // GRID_W
    n_rows = DEC_SEQ // GRID_W
    heads = [slice(hh * NA_HD, (hh + 1) * NA_HD) for hh in range(NA_HPB)]
    q = [(q_ref[:, sl] * (NA_HD ** -0.5)).astype(BF16) for sl in heads]
    s_ctx = [_dot_nt(q[hh], ck_ref[:, sl].astype(BF16)) for hh, sl in enumerate(heads)]
    o_win = [[] for _ in heads]
    p_ctx = [[] for _ in heads]
    for u in range(rows_per_tile):
        qr = t * rows_per_tile + u
        rs = jnp.clip(qr - NA_KH // 2, 0, n_rows - NA_KH)
        i0 = rs - qr + (NA_KH - 1)
        start = pl.multiple_of(rs * GRID_W, GRID_W)
        kw = k_ref[pl.ds(start, NA_WIN), :].astype(BF16)
        vw = v_ref[pl.ds(start, NA_WIN), :].astype(BF16)
        rows = slice(u * GRID_W, (u + 1) * GRID_W)
        for hh, sl in enumerate(heads):
            bias = jnp.concatenate([tbl_ref[hh, i0 + 2 * m] for m in range(NA_KH // 2)], axis=-1)
            sw = _dot_nt(q[hh][rows], kw[:, sl]) + bias
            sc = s_ctx[hh][rows]
            m = jnp.maximum(sw.max(axis=-1, keepdims=True), sc.max(axis=-1, keepdims=True))
            ew = jnp.exp(sw - m)
            ec = jnp.exp(sc - m)
            inv = 1.0 / (ew.sum(axis=-1, keepdims=True) + ec.sum(axis=-1, keepdims=True))
            o_win[hh].append(_dot((ew * inv).astype(BF16), vw[:, sl]))
            p_ctx[hh].append((ec * inv).astype(BF16))
    outs = []
    for hh, sl in enumerate(heads):
        o_ctx = _dot(jnp.concatenate(p_ctx[hh], axis=0), cv_ref[:, sl].astype(BF16))
        outs.append(jnp.concatenate(o_win[hh], axis=0) + o_ctx)
    o_ref[...] = jnp.concatenate(outs, axis=-1).astype(BF16)


def _na_attn(proj, cache_k, cache_v, tbl, j):
    n_qt = DEC_SEQ // NA_QT
    rb_q = N_PROMPT // NA_QT
    rb_kv = N_PROMPT // DEC_SEQ
    cb = D_MODEL // LANES
    ck = cache_k.reshape(DEC_BATCH, -1, PAST_LEN, D_MODEL)
    cv = cache_v.reshape(DEC_BATCH, -1, PAST_LEN, D_MODEL)
    return pl.pallas_call(
        _na_attn_body,
        grid=(cb, n_qt, DEC_BATCH),
        in_specs=[
            pl.BlockSpec((NA_QT, LANES), lambda p, t, b: (rb_q + b * n_qt + t, p)),
            pl.BlockSpec((DEC_SEQ, LANES), lambda p, t, b: (rb_kv + b, cb + p)),
            pl.BlockSpec((DEC_SEQ, LANES), lambda p, t, b: (rb_kv + b, 2 * cb + p)),
            pl.BlockSpec((None, None, PAST_LEN, LANES), lambda p, t, b: (b, j, 0, p)),
            pl.BlockSpec((None, None, PAST_LEN, LANES), lambda p, t, b: (b, j, 0, p)),
            pl.BlockSpec((NA_HPB, 2 * NA_KH - 2, GRID_W, 2 * GRID_W), lambda p, t, b: (p, 0, 0, 0)),
        ],
        out_specs=pl.BlockSpec((NA_QT, LANES), lambda p, t, b: (b * n_qt + t, p)),
        out_shape=jax.ShapeDtypeStruct((N_SAMPLE, D_MODEL), BF16),
        compiler_params=_cparams(3),
        name="na_attn",
    )(proj, proj, proj, ck, cv, tbl)


def _na_bias_table(rpb):
    c = np.arange(GRID_W)
    cs = np.clip(c - NA_KW // 2, 0, GRID_W - NA_KW)
    col_valid = (c[None, :] >= cs[:, None]) & (c[None, :] < cs[:, None] + NA_KW)
    col_off = np.clip(c[None, :] - c[:, None] + NA_KW - 1, 0, 2 * NA_KW - 2)
    sel_c = (col_off[..., None] == np.arange(2 * NA_KW - 1)).astype(np.float32)
    t = jnp.einsum("hij,qkj->hiqk", rpb, sel_c, precision=lax.Precision.HIGHEST)
    t = jnp.where(col_valid[None, None], t, -jnp.inf)
    return jnp.concatenate([t[:, :-1], t[:, 1:]], axis=-1)


def _outproj_body(ap_ref, as_ref, w_ref, x_ref, g_ref, mod_ref, wr_ref, br_ref, xo_ref, tok_ref, lg_ref):
    i = pl.program_id(0)
    a = jnp.where(i < N_PROMPT // TM, ap_ref[...], as_ref[...])
    x = x_ref[...] + mod_ref[2:3, :] * _dot(a, w_ref[...])
    xo_ref[...] = x
    tok = _norm_mod(x, g_ref[...], mod_ref[3:4, :], mod_ref[4:5, :])
    tok_ref[...] = _pack_rows(tok)
    hi = tok.astype(BF16)
    lo = (tok - hi.astype(F32)).astype(BF16)
    wr = wr_ref[...]
    wr_hi = wr.astype(BF16)
    wr_lo = (wr - wr_hi.astype(F32)).astype(BF16)
    lg_ref[...] = _dot(hi, wr_hi) + _dot(lo, wr_hi) + _dot(hi, wr_lo) + br_ref[...]


def _outproj(a_prompt, a_sample, w_bf16, x, g, mod, wr, br):
    k_in = a_prompt.shape[1]
    n_pt = N_PROMPT // TM
    return pl.pallas_call(
        _outproj_body,
        grid=(N_TOK // TM,),
        in_specs=[
            pl.BlockSpec((TM, k_in), lambda i: (jnp.minimum(i, n_pt - 1), 0)),
            pl.BlockSpec((TM, k_in), lambda i: (jnp.maximum(i - n_pt, 0), 0)),
            pl.BlockSpec((k_in, D_MODEL), lambda i: (0, 0)),
            pl.BlockSpec((TM, D_MODEL), lambda i: (i, 0)),
            pl.BlockSpec((1, D_MODEL), lambda i: (0, 0)),
            pl.BlockSpec((None, MOD_ROWS, D_MODEL), lambda i: (_group_of_tile(i, TM), 0, 0)),
            pl.BlockSpec((D_MODEL, LANES), lambda i: (0, 0)),
            pl.BlockSpec((1, LANES), lambda i: (0, 0)),
        ],
        out_specs=[
            pl.BlockSpec((TM, D_MODEL), lambda i: (i, 0)),
            pl.BlockSpec((TM, HALF), lambda i: (i, 0)),
            pl.BlockSpec((TM, LANES), lambda i: (i, 0)),
        ],
        out_shape=[
            jax.ShapeDtypeStruct((N_TOK, D_MODEL), F32),
            jax.ShapeDtypeStruct((N_TOK, HALF), U32),
            jax.ShapeDtypeStruct((N_TOK, LANES), F32),
        ],
        compiler_params=_cparams(1),
        name="outproj",
    )(a_prompt, a_sample, w_bf16, x, g.reshape(1, D_MODEL), mod, wr, br)


def _router_body(lg_ref, idx_ref, gate_ref, rank_ref, cnt_ref, carry):
    i = pl.program_id(0)

    @pl.when(i == 0)
    def _():
        carry[...] = jnp.zeros_like(carry)

    lane = lax.broadcasted_iota(I32, (ROUTER_TM, LANES), 1)
    l = jnp.where(lane < N_EXPERTS, lg_ref[...], -jnp.inf)
    vals, idxs, hots = [], [], []
    for _ in range(TOP_K):
        m = l.max(axis=-1, keepdims=True)
        idx = jnp.where(l == m, lane, LANES).min(axis=-1, keepdims=True)
        hot = lane == idx
        l = jnp.where(hot, -jnp.inf, l)
        vals.append(m)
        idxs.append(idx)
        hots.append(hot)
    es = [jnp.exp(v - vals[0]) for v in vals]
    denom = es[0] + es[1] + es[2] + es[3]
    hot_all = (hots[0] | hots[1] | hots[2] | hots[3]).astype(F32)
    r = lax.broadcasted_iota(I32, (ROUTER_TM, ROUTER_TM), 0)
    c = lax.broadcasted_iota(I32, (ROUTER_TM, ROUTER_TM), 1)
    before = _dot((c < r).astype(BF16), hot_all.astype(BF16)) + carry[...]
    idx_out = jnp.zeros((ROUTER_TM, LANES), I32)
    gate_out = jnp.zeros((ROUTER_TM, LANES), F32)
    rank_out = jnp.zeros((ROUTER_TM, LANES), I32)
    for k in range(TOP_K):
        rank_k = jnp.where(hots[k], before, 0.0).sum(axis=-1, keepdims=True).astype(I32)
        idx_out = jnp.where(lane == k, idxs[k], idx_out)
        gate_out = jnp.where(lane == k, es[k] / denom, gate_out)
        rank_out = jnp.where(lane == k, rank_k, rank_out)
    idx_ref[...] = idx_out
    gate_ref[...] = gate_out
    rank_ref[...] = rank_out
    carry[...] = carry[...] + hot_all.sum(axis=0, keepdims=True)
    cnt_ref[...] = carry[...].astype(I32)


def _router(logits):
    spec = pl.BlockSpec((ROUTER_TM, LANES), lambda i: (i, 0))
    return pl.pallas_call(
        _router_body,
        grid=(N_TOK // ROUTER_TM,),
        in_specs=[spec],
        out_specs=[spec, spec, spec, pl.BlockSpec((1, LANES), lambda i: (0, 0))],
        out_shape=[
            jax.ShapeDtypeStruct((N_TOK, LANES), I32),
            jax.ShapeDtypeStruct((N_TOK, LANES), F32),
            jax.ShapeDtypeStruct((N_TOK, LANES), I32),
            jax.ShapeDtypeStruct((1, LANES), I32),
        ],
        scratch_shapes=[pltpu.VMEM((1, LANES), F32)],
        compiler_params=_cparams(1),
        name="router",
    )(logits)


SC_CORES = 2
SC_SUBCORES = 16
SC_WORKERS = SC_CORES * SC_SUBCORES
SC_CH = 32


def _sc_mesh():
    return plsc.VectorSubcoreMesh(core_axis_name="c", subcore_axis_name="s",
                                  num_cores=SC_CORES, num_subcores=SC_SUBCORES)


def _sc_worker():
    return lax.axis_index("s") * SC_CORES + lax.axis_index("c")


def _dispatch(tok, pos_km):
    n, d = tok.shape
    n_ch = n // (SC_WORKERS * SC_CH)
    assert n_ch * SC_WORKERS * SC_CH == n and n_ch % 2 == 0

    @functools.partial(
        pl.kernel, mesh=_sc_mesh(), out_type=jax.ShapeDtypeStruct((MOE_ROWS, d), tok.dtype),
        scratch_types=[pltpu.VMEM((n_ch, TOP_K, SC_CH), I32), pltpu.VMEM((2, SC_CH, d), tok.dtype),
                       pltpu.SemaphoreType.DMA((2,)), pltpu.SemaphoreType.DMA((2,))])
    def scatter_rows(tok_hbm, pos_hbm, out_hbm, idx_v, rows_v, load_sem, scat_sem):
        wid = _sc_worker()
        base = wid * (n_ch * SC_CH)
        pltpu.sync_copy(pos_hbm.at[wid], idx_v)

        def load(g, slot):
            rows = pl.ds(pl.multiple_of(base + g * SC_CH, 8), SC_CH)
            return pltpu.make_async_copy(tok_hbm.at[rows], rows_v.at[slot], load_sem.at[slot])

        def scatter(g, k, slot):
            return pltpu.make_async_copy(rows_v.at[slot], out_hbm.at[idx_v.at[g, k]], scat_sem.at[slot])

        load(0, 0).start()

        @pl.loop(0, n_ch, step=2)
        def _(g0):
            for slot in range(2):
                g = g0 + slot
                load(g, slot).wait()

                @pl.when(g + 1 < n_ch)
                def _():
                    load(g + 1, 1 - slot).start()

                for k in range(TOP_K):
                    scatter(g, k, slot).start()
                for k in range(TOP_K):
                    scatter(g, k, slot).wait()

    pos = pos_km.reshape(TOP_K, SC_WORKERS, n_ch, SC_CH).transpose(1, 2, 0, 3)
    return scatter_rows(tok, pos)


def _gather_rows(table, idx):
    b, d = idx.shape[0], table.shape[1]
    n_ch = b // (SC_WORKERS * SC_CH)
    assert n_ch * SC_WORKERS * SC_CH == b and n_ch % 2 == 0

    @functools.partial(
        pl.kernel, mesh=_sc_mesh(), out_type=jax.ShapeDtypeStruct((b, d), table.dtype),
        scratch_types=[pltpu.VMEM((n_ch, SC_CH), I32), pltpu.VMEM((2, SC_CH, d), table.dtype),
                       pltpu.SemaphoreType.DMA((2,))])
    def gather_rows(table_hbm, idx_hbm, out_hbm, idx_v, rows_v, sems):
        wid = _sc_worker()
        base = wid * (n_ch * SC_CH)
        pltpu.sync_copy(idx_hbm.at[wid], idx_v)

        def gather(g, slot):
            return pltpu.make_async_copy(table_hbm.at[idx_v.at[g]], rows_v.at[slot], sems.at[slot])

        def write_out(g, slot):
            rows = pl.ds(pl.multiple_of(base + g * SC_CH, 8), SC_CH)
            pltpu.sync_copy(rows_v.at[slot], out_hbm.at[rows])

        gather(0, 0).start()

        @pl.loop(0, n_ch, step=2)
        def _(g):
            gather(g + 1, 1).start()
            gather(g, 0).wait()
            write_out(g, 0)

            @pl.when(g + 2 < n_ch)
            def _():
                gather(g + 2, 0).start()

            gather(g + 1, 1).wait()
            write_out(g + 1, 1)

    return gather_rows(table, idx.reshape(SC_WORKERS, n_ch, SC_CH))


def _moe_body(te_ref, tv_ref, slot_ref, nxt_ref, x_ref, w1_hbm, b1_ref, w2_hbm, b2_ref, perm_ref, o_ref,
              w1_buf, w2_buf, w1_scr, w2_scr, sem, *, layer):
    t = pl.program_id(0)
    e = te_ref[t]
    slot = slot_ref[t]
    nxt = nxt_ref[t]
    first = jnp.logical_or(t == 0, e != te_ref[jnp.maximum(t - 1, 0)])

    def fetch(expert, s):
        return (pltpu.make_async_copy(w1_hbm.at[layer, expert], w1_buf.at[s], sem.at[s, 0]),
                pltpu.make_async_copy(w2_hbm.at[layer, expert], w2_buf.at[s], sem.at[s, 1]))

    @pl.when(t == 0)
    def _():
        for cp in fetch(e, slot):
            cp.start()

    @pl.when(jnp.logical_and(tv_ref[t] == 1, first))
    def _():
        for cp in fetch(e, slot):
            cp.wait()

        @pl.when(nxt != e)
        def _():
            for cp in fetch(nxt, 1 - slot):
                cp.start()

        w2_scr[...] = w2_buf[slot].astype(BF16)
        half = PERM_W // 2
        for c in range(2 * D_EXPERT // PERM_W):
            blk = _dot(w1_buf[slot, :, c * PERM_W:(c + 1) * PERM_W].astype(BF16), perm_ref[...]).astype(BF16)
            w1_scr[:, c * half:(c + 1) * half] = blk[:, :half]
            w1_scr[:, D_EXPERT + c * half:D_EXPERT + (c + 1) * half] = blk[:, half:]

    @pl.when(tv_ref[t] == 0)
    def _():
        o_ref[...] = jnp.zeros_like(o_ref)

    @pl.when(tv_ref[t] == 1)
    def _():
        x_lo, x_hi = _unpack_rows(x_ref[...])
        x = jnp.concatenate([x_lo.astype(BF16), x_hi.astype(BF16)], axis=-1)
        hh = _dot(x, w1_scr[...]) + b1_ref[...]
        h_glu = jnp.minimum(hh[:, :D_EXPERT], SWIGLU_LIMIT)
        h_lin = jnp.clip(hh[:, D_EXPERT:], -SWIGLU_LIMIT, SWIGLU_LIMIT)
        a = h_glu * _sigmoid(SWIGLU_ALPHA * h_glu) * (h_lin + 1.0)
        o_ref[...] = _pack_rows(_dot(a.astype(BF16), w2_scr[...]) + b2_ref[...])


def _moe_experts(tile_expert, tile_valid, tile_slot, tile_next, xs, w1, b1_split, w2, b2, perm, layer):
    grid_spec = pltpu.PrefetchScalarGridSpec(
        num_scalar_prefetch=4,
        grid=(MOE_TILES,),
        in_specs=[
            pl.BlockSpec((MOE_TM, HALF), lambda t, te, *_: (t, 0)),
            pl.BlockSpec(memory_space=pl.ANY),
            pl.BlockSpec((None, None, 1, 2 * D_EXPERT), lambda t, te, *_: (layer, te[t], 0, 0)),
            pl.BlockSpec(memory_space=pl.ANY),
            pl.BlockSpec((None, None, 1, D_MODEL), lambda t, te, *_: (layer, te[t], 0, 0)),
            pl.BlockSpec((PERM_W, PERM_W), lambda t, te, *_: (0, 0)),
        ],
        out_specs=pl.BlockSpec((MOE_TM, HALF), lambda t, te, *_: (t, 0)),
        scratch_shapes=[
            pltpu.VMEM((2, D_MODEL, 2 * D_EXPERT), F32),
            pltpu.VMEM((2, D_EXPERT, D_MODEL), F32),
            pltpu.VMEM((D_MODEL, 2 * D_EXPERT), BF16),
            pltpu.VMEM((D_EXPERT, D_MODEL), BF16),
            pltpu.SemaphoreType.DMA((2, 2)),
        ],
    )
    return pl.pallas_call(
        functools.partial(_moe_body, layer=layer),
        grid_spec=grid_spec,
        out_shape=jax.ShapeDtypeStruct((MOE_ROWS, HALF), U32),
        compiler_params=_cparams(1),
        name="moe_experts",
    )(tile_expert, tile_valid, tile_slot, tile_next, xs, w1, b1_split, w2, b2, perm)


COMBINE_TM = 256


def _combine_body(*refs, final):
    y_refs = refs[:TOP_K]
    gate_ref, x_ref, mod_ref, g_ref, o_ref = refs[TOP_K:]
    gates = gate_ref[...]
    y_lo, y_hi = (gates[:, 0:1] * h for h in _unpack_rows(y_refs[0][...]))
    for k in range(1, TOP_K):
        lo, hi = _unpack_rows(y_refs[k][...])
        y_lo = y_lo + gates[:, k:k + 1] * lo
        y_hi = y_hi + gates[:, k:k + 1] * hi
    x = x_ref[...] + mod_ref[5:6, :] * jnp.concatenate([y_lo, y_hi], axis=-1)
    if final:
        x = (x * lax.rsqrt(jnp.mean(x * x, axis=-1, keepdims=True) + EPS)) * g_ref[...]
    o_ref[...] = x


def _combine(y_km, gates, x, mod, g_final, final):
    n_t = N_TOK // COMBINE_TM
    y_specs = [pl.BlockSpec((COMBINE_TM, HALF), lambda i, k=k: (k * n_t + i, 0)) for k in range(TOP_K)]
    return pl.pallas_call(
        functools.partial(_combine_body, final=final),
        grid=(n_t,),
        in_specs=y_specs + [
            pl.BlockSpec((COMBINE_TM, LANES), lambda i: (i, 0)),
            pl.BlockSpec((COMBINE_TM, D_MODEL), lambda i: (i, 0)),
            pl.BlockSpec((None, MOD_ROWS, D_MODEL), lambda i: (_group_of_tile(i, COMBINE_TM), 0, 0)),
            pl.BlockSpec((1, D_MODEL), lambda i: (0, 0)),
        ],
        out_specs=pl.BlockSpec((COMBINE_TM, D_MODEL), lambda i: (i, 0)),
        out_shape=jax.ShapeDtypeStruct((N_TOK, D_MODEL), F32),
        compiler_params=_cparams(1),
        name="combine",
    )(*([y_km] * TOP_K), gates, x, mod, g_final.reshape(1, D_MODEL))


def _routing_tables(idx, rank, counts):
    counts = counts[0, :N_EXPERTS]
    padded = ((counts + MOE_TM - 1) // MOE_TM) * MOE_TM
    ends = jnp.cumsum(padded)
    offsets = ends - padded
    idx = idx[:, :TOP_K]
    pos = rank[:, :TOP_K] + jnp.sum(
        jnp.where(idx[..., None] == jnp.arange(N_EXPERTS, dtype=I32), offsets, 0), axis=-1)
    pos_km = pos.T.astype(I32)
    tile_start = jnp.arange(MOE_TILES, dtype=I32) * MOE_TM
    tile_valid = (tile_start < ends[-1]).astype(I32)
    tile_expert = jnp.sum((tile_start[:, None] >= ends[None, :]).astype(I32), axis=-1)
    last_expert = jnp.sum((ends[-1] - 1 >= ends).astype(I32))
    tile_expert = jnp.where(tile_valid == 1, tile_expert, last_expert).astype(I32)
    e_ids = jnp.arange(N_EXPERTS, dtype=I32)
    owns = counts > 0
    ordinal = jnp.cumsum(owns.astype(I32)) - 1
    later = jnp.where(owns[None, :] & (e_ids[None, :] > e_ids[:, None]), e_ids[None, :], N_EXPERTS)
    nxt = jnp.min(later, axis=1)
    nxt = jnp.where(nxt == N_EXPERTS, e_ids, nxt)
    onehot = (tile_expert[:, None] == e_ids[None, :]).astype(I32)
    tile_slot = jnp.sum(onehot * (ordinal % 2)[None, :], axis=1).astype(I32)
    tile_next = jnp.sum(onehot * nxt[None, :], axis=1).astype(I32)
    return pos_km, tile_expert, tile_valid, tile_slot, tile_next


def _rope_tables():
    t = np.arange(DEC_SEQ)
    row = (t // GRID_W).astype(np.float32)
    col = (t % GRID_W).astype(np.float32)
    d_axis = RET_DK // 2
    inv = jnp.asarray(ROPE_BASE, F32) ** (-jnp.arange(0, d_axis, 2, dtype=F32) / d_axis)
    ang = jnp.concatenate([row[:, None] * inv, col[:, None] * inv], axis=-1)
    cos = jnp.repeat(jnp.cos(ang), 2, axis=-1)
    sin = jnp.repeat(jnp.sin(ang), 2, axis=-1)
    sign = jnp.asarray(np.tile(np.array([-1.0, 1.0], np.float32), RET_DK // 2))
    return cos, sin * sign


def _deinterleave_matrix():
    p = np.zeros((PERM_W, PERM_W), np.float32)
    half = PERM_W // 2
    for j in range(half):
        p[2 * j, j] = 1.0
        p[2 * j + 1, half + j] = 1.0
    return jnp.asarray(p, BF16)


def kernel(x_prompt, x_sample, state_ret_fwd, state_ret_bwd, cache_na_k, cache_na_v, c, c_ctx, ada_w, ada_b, norm_mix, norm_ffn, norm_final, ret_w_in, ret_w_out, ret_decay_fwd, ret_decay_bwd, na_w_in, na_w_out, na_rpb, moe_w_router, moe_b_router, moe_w1, moe_b1, moe_w2, moe_b2):
    x = jnp.concatenate([x_prompt.reshape(N_PROMPT, D_MODEL), x_sample.reshape(N_SAMPLE, D_MODEL)], axis=0)

    cond = jnp.zeros((MOD_ROWS, D_MODEL), F32).at[0].set(c_ctx).at[1:1 + DEC_BATCH].set(c)
    mod_all = _adaln(cond, ada_w, ada_b)
    mod_all = mod_all[:, :N_GROUPS].reshape(DEPTH, N_GROUPS, 6, D_MODEL)
    mod_all = jnp.pad(mod_all, ((0, 0), (0, 0), (0, MOD_ROWS - 6), (0, 0)))

    cos, sin_signed = _rope_tables()
    perm = _deinterleave_matrix()
    wr_all = jnp.pad(moe_w_router, ((0, 0), (0, 0), (0, LANES - N_EXPERTS)))
    br_all = jnp.pad(moe_b_router, ((0, 0), (0, LANES - N_EXPERTS))).reshape(DEPTH, 1, LANES)
    b1_split = jnp.concatenate([moe_b1[..., 0::2], moe_b1[..., 1::2]], axis=-1).reshape(DEPTH, N_EXPERTS, 1, 2 * D_EXPERT)
    b2_all = moe_b2.reshape(DEPTH, N_EXPERTS, 1, D_MODEL)
    ret_states = None
    na_caches = None
    for layer in range(DEPTH):
        mod = mod_all[layer]
        j = layer // 2
        if layer % 2 == 0:
            proj = _inproj(x, norm_mix[layer], mod, _cast_layer_bf16(ret_w_in, j))
            decay = jnp.stack([ret_decay_fwd[j], ret_decay_bwd[j]])
            a_p, s_f, s_b = _ret_prompt(proj, decay, j, ret_states)
            ret_states = (s_f, s_b)
            a_s = _ret_sample(proj, decay, cos, sin_signed, state_ret_fwd, state_ret_bwd, j)
            w_out = _cast_layer_bf16(ret_w_out, j)
        else:
            proj, ck, cv = _inproj(x, norm_mix[layer], mod, _cast_layer_bf16(na_w_in, j), j, na_caches)
            na_caches = (ck, cv)
            a_p = _ctx_attn(proj)
            a_s = _na_attn(proj, cache_na_k, cache_na_v, _na_bias_table(na_rpb[j]), j)
            w_out = _cast_layer_bf16(na_w_out, j)

        x, tok, logits = _outproj(a_p, a_s, w_out, x, norm_ffn[layer], mod, wr_all[layer], br_all[layer])

        idx, gates, rank, counts = _router(logits)
        pos_km, tile_expert, tile_valid, tile_slot, tile_next = _routing_tables(idx, rank, counts)
        xs = _dispatch(tok, pos_km)
        ys = _moe_experts(tile_expert, tile_valid, tile_slot, tile_next, xs, moe_w1, b1_split, moe_w2, b2_all,
                          perm, layer)
        y_km = _gather_rows(ys, pos_km.reshape(-1))
        x = _combine(y_km, gates, x, mod, norm_final, final=(layer == DEPTH - 1))

    y_prompt = x[:N_PROMPT].reshape(BATCH, SEQ, D_MODEL)
    y_sample = x[N_PROMPT:].reshape(DEC_BATCH, DEC_SEQ, D_MODEL)
    return (y_prompt, y_sample, ret_states[0], ret_states[1],
            na_caches[0].reshape(BATCH, N_NA_LAYERS, SEQ, NA_HEADS, NA_HD),
            na_caches[1].reshape(BATCH, N_NA_LAYERS, SEQ, NA_HEADS, NA_HD))
```

```python
import functools

import numpy as np
import jax
import jax.numpy as jnp
from jax import lax
from jax.experimental import pallas as pl
from jax.experimental.pallas import tpu as pltpu
from jax.experimental.pallas import tpu_sc as plsc

F32 = jnp.float32
BF16 = jnp.bfloat16
I32 = jnp.int32
U32 = jnp.uint32

D_MODEL = 1024
BATCH = 32
SEQ = 256
DEPTH = 4
DEC_BATCH = 2
DEC_SEQ = 1024
PAST_LEN = 512
GRID_W = 64
EPS = 1e-6
ROPE_BASE = 10000.0
RET_HEADS = 4
RET_DK = D_MODEL // RET_HEADS
RET_DV = 2 * RET_DK
RET_QK_W = RET_HEADS * RET_DK
RET_V_W = RET_HEADS * RET_DV
NA_HEADS = 16
NA_HD = D_MODEL // NA_HEADS
NA_KH = 8
NA_KW = 16
N_EXPERTS = 32
TOP_K = 4
D_EXPERT = D_MODEL
SWIGLU_LIMIT = 7.0
SWIGLU_ALPHA = 1.702
N_RET_LAYERS = (DEPTH + 1) // 2
N_NA_LAYERS = DEPTH // 2

N_PROMPT = BATCH * SEQ
N_SAMPLE = DEC_BATCH * DEC_SEQ
N_TOK = N_PROMPT + N_SAMPLE
N_GROUPS = 1 + DEC_BATCH
MOD_ROWS = 8
LANES = 128

TM = 256
RET_CHUNK = 256
MOE_TM = 256
MOE_ROWS = N_TOK * TOP_K + N_EXPERTS * MOE_TM
MOE_TILES = MOE_ROWS // MOE_TM
ROUTER_TM = 512
PERM_W = 256
VMEM_LIMIT = 56 * 1024 * 1024


def _cparams(n_axes, vmem=VMEM_LIMIT):
    return pltpu.CompilerParams(dimension_semantics=("arbitrary",) * n_axes, vmem_limit_bytes=vmem)


def _group_of_tile(i, tm):
    return jnp.maximum((i * tm) // DEC_SEQ - (N_PROMPT // DEC_SEQ - 1), 0)


def _sigmoid(x):
    return 1.0 / (1.0 + jnp.exp(-x))


def _silu(x):
    return x * _sigmoid(x)


def _log_sigmoid(x):
    return jnp.minimum(x, 0.0) - jnp.log(1.0 + jnp.exp(-jnp.abs(x)))


def _norm_mod(x, g, shift, scale):
    y = x * lax.rsqrt(jnp.mean(x * x, axis=-1, keepdims=True) + EPS)
    return (y * g) * (1.0 + scale) + shift


def _dot(a, b):
    return jnp.dot(a, b, preferred_element_type=F32)


HALF = D_MODEL // 2
HI_MASK = 0xFFFF0000


def _pack_rows(x):
    lo = pltpu.bitcast(x[:, :HALF].astype(BF16).astype(F32), U32)
    hi = pltpu.bitcast(x[:, HALF:].astype(BF16).astype(F32), U32)
    return (lo >> 16) | (hi & jnp.uint32(HI_MASK))


def _unpack_rows(p):
    return pltpu.bitcast(p << 16, F32), pltpu.bitcast(p & jnp.uint32(HI_MASK), F32)


def _dot_nt(a, b):
    return lax.dot_general(a, b, (((1,), (1,)), ((), ())), preferred_element_type=F32)


def _dot_tn(a, b):
    return lax.dot_general(a, b, (((0,), (0,)), ((), ())), preferred_element_type=F32)


ADA_TN = 1536


def _adaln_body(cond_ref, w_ref, b_ref, o_ref):
    s = _silu(cond_ref[...]).astype(BF16)
    o_ref[...] = _dot(s, w_ref[...].astype(BF16)) + b_ref[...]


def _adaln(cond, ada_w, ada_b):
    n_out = 6 * D_MODEL
    return pl.pallas_call(
        _adaln_body,
        grid=(DEPTH, n_out // ADA_TN),
        in_specs=[
            pl.BlockSpec((MOD_ROWS, D_MODEL), lambda l, j: (0, 0)),
            pl.BlockSpec((None, D_MODEL, ADA_TN), lambda l, j: (l, 0, j)),
            pl.BlockSpec((None, 1, ADA_TN), lambda l, j: (l, 0, j)),
        ],
        out_specs=pl.BlockSpec((None, MOD_ROWS, ADA_TN), lambda l, j: (l, 0, j)),
        out_shape=jax.ShapeDtypeStruct((DEPTH, MOD_ROWS, n_out), F32),
        compiler_params=_cparams(2),
        name="adaln",
    )(cond, ada_w, ada_b.reshape(DEPTH, 1, n_out))


CAST_TR = 256


def _cast_body(w_ref, o_ref):
    o_ref[...] = w_ref[...].astype(BF16)


def _cast_layer_bf16(w, j):
    _, r, c = w.shape
    return pl.pallas_call(
        _cast_body,
        grid=(r // CAST_TR,),
        in_specs=[pl.BlockSpec((None, CAST_TR, c), lambda i: (j, i, 0))],
        out_specs=pl.BlockSpec((CAST_TR, c), lambda i: (i, 0)),
        out_shape=jax.ShapeDtypeStruct((r, c), BF16),
        compiler_params=_cparams(1),
        name="cast_bf16",
    )(w)


def _inproj_body(x_ref, g_ref, mod_ref, w_ref, *rest, cache_layer, first_cache):
    o_ref = rest[-3] if cache_layer is not None else rest[-1]
    i = pl.program_id(0)
    h = _norm_mod(x_ref[...], g_ref[...], mod_ref[0:1, :], mod_ref[1:2, :]).astype(BF16)
    for c in range(o_ref.shape[1] // D_MODEL):
        sl = slice(c * D_MODEL, (c + 1) * D_MODEL)
        r = _dot(h, w_ref[:, sl])
        o_ref[:, sl] = r
        if cache_layer is not None and c >= 1:
            cache_ref = rest[-3 + c]

            @pl.when(i < BATCH)
            def _(r=r, cache_ref=cache_ref):
                if first_cache:
                    for l in range(N_NA_LAYERS):
                        cache_ref[l] = r if l == cache_layer else jnp.zeros_like(r)
                else:
                    cache_ref[...] = r


def _inproj(x, g, mod, w_bf16, cache_layer=None, prev_caches=None):
    n_out = w_bf16.shape[1]
    in_specs = [
        pl.BlockSpec((TM, D_MODEL), lambda i: (i, 0)),
        pl.BlockSpec((1, D_MODEL), lambda i: (0, 0)),
        pl.BlockSpec((None, MOD_ROWS, D_MODEL), lambda i: (_group_of_tile(i, TM), 0, 0)),
        pl.BlockSpec((D_MODEL, n_out), lambda i: (0, 0)),
    ]
    out_specs = [pl.BlockSpec((TM, n_out), lambda i: (i, 0))]
    out_shape = [jax.ShapeDtypeStruct((N_TOK, n_out), F32)]
    args = [x, g.reshape(1, D_MODEL), mod, w_bf16]
    aliases = {}
    first_cache = prev_caches is None
    if cache_layer is not None:
        assert TM == SEQ and n_out == 3 * D_MODEL
        cache = jax.ShapeDtypeStruct((BATCH, N_NA_LAYERS, SEQ, D_MODEL), F32)
        if first_cache:
            spec = pl.BlockSpec((None, N_NA_LAYERS, SEQ, D_MODEL), lambda i: (jnp.minimum(i, BATCH - 1), 0, 0, 0))
        else:
            spec = pl.BlockSpec((None, None, SEQ, D_MODEL),
                                lambda i: (jnp.minimum(i, BATCH - 1), cache_layer, 0, 0))
            in_specs += [pl.BlockSpec(memory_space=pl.ANY)] * 2
            aliases = {len(args): 1, len(args) + 1: 2}
            args += list(prev_caches)
        out_specs += [spec, spec]
        out_shape += [cache, cache]
    outs = pl.pallas_call(
        functools.partial(_inproj_body, cache_layer=cache_layer, first_cache=first_cache),
        grid=(N_TOK // TM,),
        in_specs=in_specs,
        out_specs=out_specs,
        out_shape=out_shape,
        input_output_aliases=aliases,
        compiler_params=_cparams(1),
        name="inproj",
    )(*args)
    return outs[0] if cache_layer is None else outs


def _decay_terms(dec_ref, h, c):
    lgf = _log_sigmoid(jnp.zeros((c, c), F32) + dec_ref[0, h])
    lgb = _log_sigmoid(jnp.zeros((c, c), F32) + dec_ref[1, h])
    row = lax.broadcasted_iota(I32, (c, c), 0)
    col = lax.broadcasted_iota(I32, (c, c), 1)
    diff = (row - col).astype(F32)
    lower = diff >= 0
    upper = diff <= 0
    mask = (jnp.where(lower, jnp.exp(jnp.where(lower, diff, 0.0) * lgf), 0.0)
            + jnp.where(upper, jnp.exp(jnp.where(upper, -diff, 0.0) * lgb), 0.0))
    pos = lax.broadcasted_iota(I32, (c, 1), 0).astype(F32)
    lgf1 = _log_sigmoid(jnp.zeros((c, 1), F32) + dec_ref[0, h])
    lgb1 = _log_sigmoid(jnp.zeros((c, 1), F32) + dec_ref[1, h])
    lgf0 = _log_sigmoid(jnp.zeros((1, 1), F32) + dec_ref[0, h])
    lgb0 = _log_sigmoid(jnp.zeros((1, 1), F32) + dec_ref[1, h])
    terms = dict(
        mask=mask,
        q_f=jnp.exp((pos + 1.0) * lgf1), k_f=jnp.exp((c - 1.0 - pos) * lgf1), c_f=jnp.exp(c * lgf0),
        q_b=jnp.exp((c - pos) * lgb1), k_b=jnp.exp(pos * lgb1), c_b=jnp.exp(c * lgb0),
    )
    return terms


def _group_norm_gate(o, g):
    o = o * lax.rsqrt(jnp.mean(o * o, axis=-1, keepdims=True) + EPS)
    return (_silu(g) * o).astype(BF16)


def _ret_prompt_body(dec_ref, q_ref, k_ref, v_ref, g_ref, *rest, layer, first):
    o_ref, sf_ref, sb_ref = rest[-3:]
    h = pl.program_id(1)
    t = _decay_terms(dec_ref, h, SEQ)
    q = q_ref[...]
    k = k_ref[...] * (RET_DK ** -0.5)
    v = v_ref[...].astype(BF16)
    scores = _dot_nt(q.astype(BF16), k.astype(BF16)) * t["mask"]
    o = _dot(scores.astype(BF16), v)
    o_ref[...] = _group_norm_gate(o, g_ref[...])
    for ref, k_dec in ((sf_ref, t["k_f"]), (sb_ref, t["k_b"])):
        s = _dot_tn((k * k_dec).astype(BF16), v)
        if first:
            for l in range(N_RET_LAYERS):
                ref[l] = s if l == layer else jnp.zeros_like(s)
        else:
            ref[...] = s


def _ret_prompt(proj, decay, layer, prev_states=None):
    kq = RET_QK_W // RET_DK
    kv = 2 * RET_QK_W // RET_DV
    kg = kv + RET_HEADS
    state = jax.ShapeDtypeStruct((BATCH, N_RET_LAYERS, RET_HEADS, RET_DK, RET_DV), F32)
    in_specs = [
        pl.BlockSpec(memory_space=pltpu.SMEM),
        pl.BlockSpec((SEQ, RET_DK), lambda b, h: (b, h)),
        pl.BlockSpec((SEQ, RET_DK), lambda b, h: (b, kq + h)),
        pl.BlockSpec((SEQ, RET_DV), lambda b, h: (b, kv + h)),
        pl.BlockSpec((SEQ, RET_DV), lambda b, h: (b, kg + h)),
    ]
    args = [decay, proj, proj, proj, proj]
    first = prev_states is None
    aliases = {}
    if first:
        s_spec = pl.BlockSpec((None, N_RET_LAYERS, None, RET_DK, RET_DV), lambda b, h: (b, 0, h, 0, 0))
    else:
        s_spec = pl.BlockSpec((None, None, None, RET_DK, RET_DV), lambda b, h: (b, layer, h, 0, 0))
        in_specs += [pl.BlockSpec(memory_space=pl.ANY)] * 2
        aliases = {len(args): 1, len(args) + 1: 2}
        args += list(prev_states)
    return pl.pallas_call(
        functools.partial(_ret_prompt_body, layer=layer, first=first),
        grid=(BATCH, RET_HEADS),
        in_specs=in_specs,
        out_specs=[pl.BlockSpec((SEQ, RET_DV), lambda b, h: (b, h)), s_spec, s_spec],
        out_shape=[jax.ShapeDtypeStruct((N_PROMPT, RET_V_W), BF16), state, state],
        input_output_aliases=aliases,
        compiler_params=_cparams(2),
        name="ret_prompt",
    )(*args)


def _rope(x, cos, sin_signed):
    n = x.shape[-1]
    lane = lax.broadcasted_iota(I32, x.shape, 1)
    swapped = jnp.where(lane % 2 == 0, pltpu.roll(x, n - 1, 1), pltpu.roll(x, 1, 1))
    return x * cos + swapped * sin_signed


def _ret_sample_body(dec_ref, q_ref, k_ref, v_ref, g_ref, cos_ref, sin_ref, s0f_ref, s0b_ref, o_ref,
                     sf_scr, sb_scr):
    h = pl.program_id(1)
    c = RET_CHUNK
    n_chunks = DEC_SEQ // c
    t = _decay_terms(dec_ref, h, c)

    def chunk(ref, i):
        return ref[i * c:(i + 1) * c, :]

    def qk(i):
        q = _rope(chunk(q_ref, i), chunk(cos_ref, i), chunk(sin_ref, i))
        k = _rope(chunk(k_ref, i) * (RET_DK ** -0.5), chunk(cos_ref, i), chunk(sin_ref, i))
        return q, k

    s = s0f_ref[...]
    for i in range(n_chunks):
        sf_scr[i] = s
        if i + 1 < n_chunks:
            _, k = qk(i)
            s = s * t["c_f"] + _dot_tn((k * t["k_f"]).astype(BF16), chunk(v_ref, i).astype(BF16))
    s = s0b_ref[...]
    for i in reversed(range(n_chunks)):
        sb_scr[i] = s
        if i > 0:
            _, k = qk(i)
            s = s * t["c_b"] + _dot_tn((k * t["k_b"]).astype(BF16), chunk(v_ref, i).astype(BF16))

    for i in range(n_chunks):
        q, k = qk(i)
        scores = _dot_nt(q.astype(BF16), k.astype(BF16)) * t["mask"]
        o = _dot(scores.astype(BF16), chunk(v_ref, i).astype(BF16))
        o = o + _dot((q * t["q_f"]).astype(BF16), sf_scr[i].astype(BF16))
        o = o + _dot((q * t["q_b"]).astype(BF16), sb_scr[i].astype(BF16))
        o_ref[i * c:(i + 1) * c, :] = _group_norm_gate(o, chunk(g_ref, i))


def _ret_sample(proj, decay, cos, sin_signed, s0f, s0b, j):
    kq = RET_QK_W // RET_DK
    kv = 2 * RET_QK_W // RET_DV
    kg = kv + RET_HEADS
    rb = N_PROMPT // DEC_SEQ
    n_chunks = DEC_SEQ // RET_CHUNK
    return pl.pallas_call(
        _ret_sample_body,
        grid=(DEC_BATCH, RET_HEADS),
        in_specs=[
            pl.BlockSpec(memory_space=pltpu.SMEM),
            pl.BlockSpec((DEC_SEQ, RET_DK), lambda b, h: (rb + b, h)),
            pl.BlockSpec((DEC_SEQ, RET_DK), lambda b, h: (rb + b, kq + h)),
            pl.BlockSpec((DEC_SEQ, RET_DV), lambda b, h: (rb + b, kv + h)),
            pl.BlockSpec((DEC_SEQ, RET_DV), lambda b, h: (rb + b, kg + h)),
            pl.BlockSpec((DEC_SEQ, RET_DK), lambda b, h: (0, 0)),
            pl.BlockSpec((DEC_SEQ, RET_DK), lambda b, h: (0, 0)),
            pl.BlockSpec((None, None, None, RET_DK, RET_DV), lambda b, h: (b, j, h, 0, 0)),
            pl.BlockSpec((None, None, None, RET_DK, RET_DV), lambda b, h: (b, j, h, 0, 0)),
        ],
        out_specs=pl.BlockSpec((DEC_SEQ, RET_DV), lambda b, h: (b, h)),
        out_shape=jax.ShapeDtypeStruct((N_SAMPLE, RET_V_W), BF16),
        scratch_shapes=[pltpu.VMEM((n_chunks, RET_DK, RET_DV), F32),
                        pltpu.VMEM((n_chunks, RET_DK, RET_DV), F32)],
        compiler_params=_cparams(2),
        name="ret_sample",
    )(decay, proj, proj, proj, proj, cos, sin_signed, s0f, s0b)


NA_HPB = LANES // NA_HD


def _head_lanes(rows, hh):
    lane = lax.broadcasted_iota(I32, (rows, LANES), 1)
    return jnp.logical_and(lane >= hh * NA_HD, lane < (hh + 1) * NA_HD)


def _with_ones(v):
    return jnp.concatenate([v, jnp.ones(v.shape, v.dtype)], axis=-1)


def _ctx_attn_body(q_ref, k_ref, v_ref, o_ref):
    outs = []
    for p in range(NA_HEADS // NA_HPB):
        sl = slice(p * LANES, (p + 1) * LANES)
        q2 = q_ref[:, sl] * (NA_HD ** -0.5)
        k2 = k_ref[:, sl].astype(BF16)
        v2 = _with_ones(v_ref[:, sl].astype(BF16))
        o_pair = None
        for hh in range(NA_HPB):
            mine = _head_lanes(SEQ, hh)
            s = _dot_nt(jnp.where(mine, q2, 0.0).astype(BF16), k2)
            e = jnp.exp(s - s.max(axis=-1, keepdims=True)).astype(BF16)
            acc = _dot(e, v2)
            o_h = acc[:, :LANES] / acc[:, LANES:]
            o_pair = o_h if o_pair is None else jnp.where(mine, o_h, o_pair)
        outs.append(o_pair)
    o_ref[...] = jnp.concatenate(outs, axis=-1).astype(BF16)


def _ctx_attn(proj):
    return pl.pallas_call(
        _ctx_attn_body,
        grid=(BATCH,),
        in_specs=[
            pl.BlockSpec((SEQ, D_MODEL), lambda b: (b, 0)),
            pl.BlockSpec((SEQ, D_MODEL), lambda b: (b, 1)),
            pl.BlockSpec((SEQ, D_MODEL), lambda b: (b, 2)),
        ],
        out_specs=pl.BlockSpec((SEQ, D_MODEL), lambda b: (b, 0)),
        out_shape=jax.ShapeDtypeStruct((N_PROMPT, D_MODEL), BF16),
        compiler_params=_cparams(1),
        name="ctx_attn",
    )(proj, proj, proj)


NA_QT = 256
NA_WIN = NA_KH * GRID_W


def _na_attn_body(q_ref, k_ref, v_ref, ck_ref, cv_ref, tbl_ref, o_ref):
    t = pl.program_id(1)
    rows_per_tile = NA_QT // GRID_W
    n_rows = DEC_SEQ // GRID_W
    q2 = q_ref[...] * (NA_HD ** -0.5)
    q = [jnp.where(_head_lanes(NA_QT, hh), q2, 0.0).astype(BF16) for hh in range(NA_HPB)]
    ck = ck_ref[...].astype(BF16)
    s_ctx = [_dot_nt(q[hh], ck) for hh in range(NA_HPB)]
    acc_win = [[] for _ in range(NA_HPB)]
    e_ctx = [[] for _ in range(NA_HPB)]
    for u in range(rows_per_tile):
        qr = t * rows_per_tile + u
        rs = jnp.clip(qr - NA_KH // 2, 0, n_rows - NA_KH)
        i0 = rs - qr + (NA_KH - 1)
        start = pl.multiple_of(rs * GRID_W, GRID_W)
        kw = k_ref[pl.ds(start, NA_WIN), :].astype(BF16)
        vw = _with_ones(v_ref[pl.ds(start, NA_WIN), :].astype(BF16))
        rows = slice(u * GRID_W, (u + 1) * GRID_W)
        for hh in range(NA_HPB):
            bias = jnp.concatenate([tbl_ref[hh, i0 + 2 * m] for m in range(NA_KH // 2)], axis=-1)
            sw = _dot_nt(q[hh][rows], kw) + bias
            sc = s_ctx[hh][rows]
            m = jnp.maximum(sw.max(axis=-1, keepdims=True), sc.max(axis=-1, keepdims=True))
            acc_win[hh].append(_dot(jnp.exp(sw - m).astype(BF16), vw))
            e_ctx[hh].append(jnp.exp(sc - m).astype(BF16))
    cv = _with_ones(cv_ref[...].astype(BF16))
    out = None
    for hh in range(NA_HPB):
        acc = jnp.concatenate(acc_win[hh], axis=0) + _dot(jnp.concatenate(e_ctx[hh], axis=0), cv)
        o_h = acc[:, :LANES] / acc[:, LANES:]
        out = o_h if out is None else jnp.where(_head_lanes(NA_QT, hh), o_h, out)
    o_ref[...] = out.astype(BF16)


def _na_attn(proj, cache_k, cache_v, tbl, j):
    n_qt = DEC_SEQ // NA_QT
    rb_q = N_PROMPT // NA_QT
    rb_kv = N_PROMPT // DEC_SEQ
    cb = D_MODEL // LANES
    ck = cache_k.reshape(DEC_BATCH, -1, PAST_LEN, D_MODEL)
    cv = cache_v.reshape(DEC_BATCH, -1, PAST_LEN, D_MODEL)
    return pl.pallas_call(
        _na_attn_body,
        grid=(cb, n_qt, DEC_BATCH),
        in_specs=[
            pl.BlockSpec((NA_QT, LANES), lambda p, t, b: (rb_q + b * n_qt + t, p)),
            pl.BlockSpec((DEC_SEQ, LANES), lambda p, t, b: (rb_kv + b, cb + p)),
            pl.BlockSpec((DEC_SEQ, LANES), lambda p, t, b: (rb_kv + b, 2 * cb + p)),
            pl.BlockSpec((None, None, PAST_LEN, LANES), lambda p, t, b: (b, j, 0, p)),
            pl.BlockSpec((None, None, PAST_LEN, LANES), lambda p, t, b: (b, j, 0, p)),
            pl.BlockSpec((NA_HPB, 2 * NA_KH - 2, GRID_W, 2 * GRID_W), lambda p, t, b: (p, 0, 0, 0)),
        ],
        out_specs=pl.BlockSpec((NA_QT, LANES), lambda p, t, b: (b * n_qt + t, p)),
        out_shape=jax.ShapeDtypeStruct((N_SAMPLE, D_MODEL), BF16),
        compiler_params=_cparams(3),
        name="na_attn",
    )(proj, proj, proj, ck, cv, tbl)


def _na_bias_table(rpb):
    c = np.arange(GRID_W)
    cs = np.clip(c - NA_KW // 2, 0, GRID_W - NA_KW)
    col_valid = (c[None, :] >= cs[:, None]) & (c[None, :] < cs[:, None] + NA_KW)
    col_off = np.clip(c[None, :] - c[:, None] + NA_KW - 1, 0, 2 * NA_KW - 2)
    sel_c = (col_off[..., None] == np.arange(2 * NA_KW - 1)).astype(np.float32)
    t = jnp.einsum("hij,qkj->hiqk", rpb, sel_c, precision=lax.Precision.HIGHEST)
    t = jnp.where(col_valid[None, None], t, -jnp.inf)
    return jnp.concatenate([t[:, :-1], t[:, 1:]], axis=-1)


def _outproj_body(ap_ref, as_ref, w_ref, x_ref, g_ref, mod_ref, wr_ref, br_ref, xo_ref, tok_ref, lg_ref):
    i = pl.program_id(0)
    a = jnp.where(i < N_PROMPT // TM, ap_ref[...], as_ref[...])
    x = x_ref[...] + mod_ref[2:3, :] * _dot(a, w_ref[...])
    xo_ref[...] = x
    tok = _norm_mod(x, g_ref[...], mod_ref[3:4, :], mod_ref[4:5, :])
    tok_ref[...] = _pack_rows(tok)
    hi = tok.astype(BF16)
    lo = (tok - hi.astype(F32)).astype(BF16)
    wr = wr_ref[...]
    wr_hi = wr.astype(BF16)
    wr_lo = (wr - wr_hi.astype(F32)).astype(BF16)
    lg_ref[...] = _dot(hi, wr_hi) + _dot(lo, wr_hi) + _dot(hi, wr_lo) + br_ref[...]


def _outproj(a_prompt, a_sample, w_bf16, x, g, mod, wr, br):
    k_in = a_prompt.shape[1]
    n_pt = N_PROMPT // TM
    return pl.pallas_call(
        _outproj_body,
        grid=(N_TOK // TM,),
        in_specs=[
            pl.BlockSpec((TM, k_in), lambda i: (jnp.minimum(i, n_pt - 1), 0)),
            pl.BlockSpec((TM, k_in), lambda i: (jnp.maximum(i - n_pt, 0), 0)),
            pl.BlockSpec((k_in, D_MODEL), lambda i: (0, 0)),
            pl.BlockSpec((TM, D_MODEL), lambda i: (i, 0)),
            pl.BlockSpec((1, D_MODEL), lambda i: (0, 0)),
            pl.BlockSpec((None, MOD_ROWS, D_MODEL), lambda i: (_group_of_tile(i, TM), 0, 0)),
            pl.BlockSpec((D_MODEL, LANES), lambda i: (0, 0)),
            pl.BlockSpec((1, LANES), lambda i: (0, 0)),
        ],
        out_specs=[
            pl.BlockSpec((TM, D_MODEL), lambda i: (i, 0)),
            pl.BlockSpec((TM, HALF), lambda i: (i, 0)),
            pl.BlockSpec((TM, LANES), lambda i: (i, 0)),
        ],
        out_shape=[
            jax.ShapeDtypeStruct((N_TOK, D_MODEL), F32),
            jax.ShapeDtypeStruct((N_TOK, HALF), U32),
            jax.ShapeDtypeStruct((N_TOK, LANES), F32),
        ],
        compiler_params=_cparams(1),
        name="outproj",
    )(a_prompt, a_sample, w_bf16, x, g.reshape(1, D_MODEL), mod, wr, br)


def _router_body(lg_ref, idx_ref, gate_ref, rank_ref, cnt_ref, carry):
    i = pl.program_id(0)

    @pl.when(i == 0)
    def _():
        carry[...] = jnp.zeros_like(carry)

    lane = lax.broadcasted_iota(I32, (ROUTER_TM, LANES), 1)
    l = jnp.where(lane < N_EXPERTS, lg_ref[...], -jnp.inf)
    vals, idxs, hots = [], [], []
    for _ in range(TOP_K):
        m = l.max(axis=-1, keepdims=True)
        idx = jnp.where(l == m, lane, LANES).min(axis=-1, keepdims=True)
        hot = lane == idx
        l = jnp.where(hot, -jnp.inf, l)
        vals.append(m)
        idxs.append(idx)
        hots.append(hot)
    es = [jnp.exp(v - vals[0]) for v in vals]
    denom = es[0] + es[1] + es[2] + es[3]
    hot_all = (hots[0] | hots[1] | hots[2] | hots[3]).astype(F32)
    r = lax.broadcasted_iota(I32, (ROUTER_TM, ROUTER_TM), 0)
    c = lax.broadcasted_iota(I32, (ROUTER_TM, ROUTER_TM), 1)
    before = _dot((c < r).astype(BF16), hot_all.astype(BF16)) + carry[...]
    idx_out = jnp.zeros((ROUTER_TM, LANES), I32)
    gate_out = jnp.zeros((ROUTER_TM, LANES), F32)
    rank_out = jnp.zeros((ROUTER_TM, LANES), I32)
    for k in range(TOP_K):
        rank_k = jnp.where(hots[k], before, 0.0).sum(axis=-1, keepdims=True).astype(I32)
        idx_out = jnp.where(lane == k, idxs[k], idx_out)
        gate_out = jnp.where(lane == k, es[k] / denom, gate_out)
        rank_out = jnp.where(lane == k, rank_k, rank_out)
    idx_ref[...] = idx_out
    gate_ref[...] = gate_out
    rank_ref[...] = rank_out
    carry[...] = carry[...] + hot_all.sum(axis=0, keepdims=True)
    cnt_ref[...] = carry[...].astype(I32)


def _router(logits):
    spec = pl.BlockSpec((ROUTER_TM, LANES), lambda i: (i, 0))
    return pl.pallas_call(
        _router_body,
        grid=(N_TOK // ROUTER_TM,),
        in_specs=[spec],
        out_specs=[spec, spec, spec, pl.BlockSpec((1, LANES), lambda i: (0, 0))],
        out_shape=[
            jax.ShapeDtypeStruct((N_TOK, LANES), I32),
            jax.ShapeDtypeStruct((N_TOK, LANES), F32),
            jax.ShapeDtypeStruct((N_TOK, LANES), I32),
            jax.ShapeDtypeStruct((1, LANES), I32),
        ],
        scratch_shapes=[pltpu.VMEM((1, LANES), F32)],
        compiler_params=_cparams(1),
        name="router",
    )(logits)


SC_CORES = 2
SC_SUBCORES = 16
SC_WORKERS = SC_CORES * SC_SUBCORES
SC_CH = 32


def _sc_mesh():
    return plsc.VectorSubcoreMesh(core_axis_name="c", subcore_axis_name="s",
                                  num_cores=SC_CORES, num_subcores=SC_SUBCORES)


def _sc_worker():
    return lax.axis_index("s") * SC_CORES + lax.axis_index("c")


def _dispatch(tok, pos_km):
    n, d = tok.shape
    n_ch = n // (SC_WORKERS * SC_CH)
    assert n_ch * SC_WORKERS * SC_CH == n and n_ch % 2 == 0

    @functools.partial(
        pl.kernel, mesh=_sc_mesh(), out_type=jax.ShapeDtypeStruct((MOE_ROWS, d), tok.dtype),
        scratch_types=[pltpu.VMEM((n_ch, TOP_K, SC_CH), I32), pltpu.VMEM((2, SC_CH, d), tok.dtype),
                       pltpu.SemaphoreType.DMA((2,)), pltpu.SemaphoreType.DMA((2,))])
    def scatter_rows(tok_hbm, pos_hbm, out_hbm, idx_v, rows_v, load_sem, scat_sem):
        wid = _sc_worker()
        base = wid * (n_ch * SC_CH)
        pltpu.sync_copy(pos_hbm.at[wid], idx_v)

        def load(g, slot):
            rows = pl.ds(pl.multiple_of(base + g * SC_CH, 8), SC_CH)
            return pltpu.make_async_copy(tok_hbm.at[rows], rows_v.at[slot], load_sem.at[slot])

        def scatter(g, k, slot):
            return pltpu.make_async_copy(rows_v.at[slot], out_hbm.at[idx_v.at[g, k]], scat_sem.at[slot])

        load(0, 0).start()

        @pl.loop(0, n_ch, step=2)
        def _(g0):
            for slot in range(2):
                g = g0 + slot
                load(g, slot).wait()

                @pl.when(g + 1 < n_ch)
                def _():
                    load(g + 1, 1 - slot).start()

                for k in range(TOP_K):
                    scatter(g, k, slot).start()
                for k in range(TOP_K):
                    scatter(g, k, slot).wait()

    pos = pos_km.reshape(TOP_K, SC_WORKERS, n_ch, SC_CH).transpose(1, 2, 0, 3)
    return scatter_rows(tok, pos)


def _gather_rows(table, idx):
    b, d = idx.shape[0], table.shape[1]
    n_ch = b // (SC_WORKERS * SC_CH)
    assert n_ch * SC_WORKERS * SC_CH == b and n_ch % 2 == 0

    @functools.partial(
        pl.kernel, mesh=_sc_mesh(), out_type=jax.ShapeDtypeStruct((b, d), table.dtype),
        scratch_types=[pltpu.VMEM((n_ch, SC_CH), I32), pltpu.VMEM((2, SC_CH, d), table.dtype),
                       pltpu.SemaphoreType.DMA((2,))])
    def gather_rows(table_hbm, idx_hbm, out_hbm, idx_v, rows_v, sems):
        wid = _sc_worker()
        base = wid * (n_ch * SC_CH)
        pltpu.sync_copy(idx_hbm.at[wid], idx_v)

        def gather(g, slot):
            return pltpu.make_async_copy(table_hbm.at[idx_v.at[g]], rows_v.at[slot], sems.at[slot])

        def write_out(g, slot):
            rows = pl.ds(pl.multiple_of(base + g * SC_CH, 8), SC_CH)
            pltpu.sync_copy(rows_v.at[slot], out_hbm.at[rows])

        gather(0, 0).start()

        @pl.loop(0, n_ch, step=2)
        def _(g):
            gather(g + 1, 1).start()
            gather(g, 0).wait()
            write_out(g, 0)

            @pl.when(g + 2 < n_ch)
            def _():
                gather(g + 2, 0).start()

            gather(g + 1, 1).wait()
            write_out(g + 1, 1)

    return gather_rows(table, idx.reshape(SC_WORKERS, n_ch, SC_CH))


def _moe_body(te_ref, tv_ref, slot_ref, nxt_ref, x_ref, w1_hbm, b1_ref, w2_hbm, b2_ref, perm_ref, o_ref,
              w1_buf, w2_buf, w1_scr, w2_scr, sem, *, layer):
    t = pl.program_id(0)
    e = te_ref[t]
    slot = slot_ref[t]
    nxt = nxt_ref[t]
    first = jnp.logical_or(t == 0, e != te_ref[jnp.maximum(t - 1, 0)])

    def fetch(expert, s):
        return (pltpu.make_async_copy(w1_hbm.at[layer, expert], w1_buf.at[s], sem.at[s, 0]),
                pltpu.make_async_copy(w2_hbm.at[layer, expert], w2_buf.at[s], sem.at[s, 1]))

    @pl.when(t == 0)
    def _():
        for cp in fetch(e, slot):
            cp.start()

    @pl.when(jnp.logical_and(tv_ref[t] == 1, first))
    def _():
        for cp in fetch(e, slot):
            cp.wait()

        @pl.when(nxt != e)
        def _():
            for cp in fetch(nxt, 1 - slot):
                cp.start()

        w2_scr[...] = w2_buf[slot].astype(BF16)
        half = PERM_W // 2
        for c in range(2 * D_EXPERT // PERM_W):
            blk = _dot(w1_buf[slot, :, c * PERM_W:(c + 1) * PERM_W].astype(BF16), perm_ref[...]).astype(BF16)
            w1_scr[:, c * half:(c + 1) * half] = blk[:, :half]
            w1_scr[:, D_EXPERT + c * half:D_EXPERT + (c + 1) * half] = blk[:, half:]

    @pl.when(tv_ref[t] == 0)
    def _():
        o_ref[...] = jnp.zeros_like(o_ref)

    @pl.when(tv_ref[t] == 1)
    def _():
        x_lo, x_hi = _unpack_rows(x_ref[...])
        x = jnp.concatenate([x_lo.astype(BF16), x_hi.astype(BF16)], axis=-1)
        hh = _dot(x, w1_scr[...]) + b1_ref[...]
        h_glu = jnp.minimum(hh[:, :D_EXPERT], SWIGLU_LIMIT)
        h_lin = jnp.clip(hh[:, D_EXPERT:], -SWIGLU_LIMIT, SWIGLU_LIMIT)
        a = h_glu * _sigmoid(SWIGLU_ALPHA * h_glu) * (h_lin + 1.0)
        o_ref[...] = _pack_rows(_dot(a.astype(BF16), w2_scr[...]) + b2_ref[...])


def _deinterleave_bias_body(b_ref, perm_ref, o_ref):
    half = PERM_W // 2
    for c in range(2 * D_EXPERT // PERM_W):
        r = b_ref[:, c * PERM_W:(c + 1) * PERM_W]
        acc = jnp.zeros(r.shape, F32)
        for _ in range(3):
            piece = r.astype(BF16)
            acc = acc + _dot(piece, perm_ref[...])
            r = r - piece.astype(F32)
        o_ref[:, c * half:(c + 1) * half] = acc[:, :half]
        o_ref[:, D_EXPERT + c * half:D_EXPERT + (c + 1) * half] = acc[:, half:]


def _deinterleave_bias(b1, perm):
    return pl.pallas_call(
        _deinterleave_bias_body,
        out_shape=jax.ShapeDtypeStruct(b1.shape, F32),
        compiler_params=_cparams(0),
        name="deinterleave_bias",
    )(b1, perm)


def _moe_experts(tile_expert, tile_valid, tile_slot, tile_next, xs, w1, b1_split, w2, b2, perm, layer):
    grid_spec = pltpu.PrefetchScalarGridSpec(
        num_scalar_prefetch=4,
        grid=(MOE_TILES,),
        in_specs=[
            pl.BlockSpec((MOE_TM, HALF), lambda t, te, *_: (t, 0)),
            pl.BlockSpec(memory_space=pl.ANY),
            pl.BlockSpec((None, None, 1, 2 * D_EXPERT), lambda t, te, *_: (layer, te[t], 0, 0)),
            pl.BlockSpec(memory_space=pl.ANY),
            pl.BlockSpec((None, None, 1, D_MODEL), lambda t, te, *_: (layer, te[t], 0, 0)),
            pl.BlockSpec((PERM_W, PERM_W), lambda t, te, *_: (0, 0)),
        ],
        out_specs=pl.BlockSpec((MOE_TM, HALF), lambda t, te, *_: (t, 0)),
        scratch_shapes=[
            pltpu.VMEM((2, D_MODEL, 2 * D_EXPERT), F32),
            pltpu.VMEM((2, D_EXPERT, D_MODEL), F32),
            pltpu.VMEM((D_MODEL, 2 * D_EXPERT), BF16),
            pltpu.VMEM((D_EXPERT, D_MODEL), BF16),
            pltpu.SemaphoreType.DMA((2, 2)),
        ],
    )
    return pl.pallas_call(
        functools.partial(_moe_body, layer=layer),
        grid_spec=grid_spec,
        out_shape=jax.ShapeDtypeStruct((MOE_ROWS, HALF), U32),
        compiler_params=_cparams(1),
        name="moe_experts",
    )(tile_expert, tile_valid, tile_slot, tile_next, xs, w1, b1_split, w2, b2, perm)


COMBINE_TM = 256


def _combine_body(*refs, final):
    y_refs = refs[:TOP_K]
    gate_ref, x_ref, mod_ref, g_ref = refs[TOP_K:TOP_K + 4]
    gates = gate_ref[...]
    y_lo, y_hi = (gates[:, 0:1] * h for h in _unpack_rows(y_refs[0][...]))
    for k in range(1, TOP_K):
        lo, hi = _unpack_rows(y_refs[k][...])
        y_lo = y_lo + gates[:, k:k + 1] * lo
        y_hi = y_hi + gates[:, k:k + 1] * hi
    x = x_ref[...] + mod_ref[5:6, :] * jnp.concatenate([y_lo, y_hi], axis=-1)
    if not final:
        refs[-1][...] = x
        return
    x = (x * lax.rsqrt(jnp.mean(x * x, axis=-1, keepdims=True) + EPS)) * g_ref[...]
    prompt_ref, sample_ref = refs[-2:]
    is_prompt = pl.program_id(0) < N_PROMPT // COMBINE_TM

    @pl.when(is_prompt)
    def _():
        prompt_ref[...] = x

    @pl.when(jnp.logical_not(is_prompt))
    def _():
        sample_ref[...] = x


def _combine(y_km, gates, x, mod, g_final, final):
    n_t = N_TOK // COMBINE_TM
    n_pt = N_PROMPT // COMBINE_TM
    y_specs = [pl.BlockSpec((COMBINE_TM, HALF), lambda i, k=k: (k * n_t + i, 0)) for k in range(TOP_K)]
    if final:
        out_specs = [pl.BlockSpec((COMBINE_TM, D_MODEL), lambda i: (jnp.minimum(i, n_pt - 1), 0)),
                     pl.BlockSpec((COMBINE_TM, D_MODEL), lambda i: (jnp.maximum(i - n_pt, 0), 0))]
        out_shape = [jax.ShapeDtypeStruct((N_PROMPT, D_MODEL), F32), jax.ShapeDtypeStruct((N_SAMPLE, D_MODEL), F32)]
    else:
        out_specs = pl.BlockSpec((COMBINE_TM, D_MODEL), lambda i: (i, 0))
        out_shape = jax.ShapeDtypeStruct((N_TOK, D_MODEL), F32)
    return pl.pallas_call(
        functools.partial(_combine_body, final=final),
        grid=(n_t,),
        in_specs=y_specs + [
            pl.BlockSpec((COMBINE_TM, LANES), lambda i: (i, 0)),
            pl.BlockSpec((COMBINE_TM, D_MODEL), lambda i: (i, 0)),
            pl.BlockSpec((None, MOD_ROWS, D_MODEL), lambda i: (_group_of_tile(i, COMBINE_TM), 0, 0)),
            pl.BlockSpec((1, D_MODEL), lambda i: (0, 0)),
        ],
        out_specs=out_specs,
        out_shape=out_shape,
        compiler_params=_cparams(1),
        name="combine",
    )(*([y_km] * TOP_K), gates, x, mod, g_final.reshape(1, D_MODEL))


def _routing_tables(idx, rank, counts):
    counts = counts[0, :N_EXPERTS]
    padded = ((counts + MOE_TM - 1) // MOE_TM) * MOE_TM
    ends = jnp.cumsum(padded)
    offsets = ends - padded
    idx = idx[:, :TOP_K]
    pos = rank[:, :TOP_K] + jnp.sum(
        jnp.where(idx[..., None] == jnp.arange(N_EXPERTS, dtype=I32), offsets, 0), axis=-1)
    pos_km = pos.T.astype(I32)
    tile_start = jnp.arange(MOE_TILES, dtype=I32) * MOE_TM
    tile_valid = (tile_start < ends[-1]).astype(I32)
    tile_expert = jnp.sum((tile_start[:, None] >= ends[None, :]).astype(I32), axis=-1)
    last_expert = jnp.sum((ends[-1] - 1 >= ends).astype(I32))
    tile_expert = jnp.where(tile_valid == 1, tile_expert, last_expert).astype(I32)
    e_ids = jnp.arange(N_EXPERTS, dtype=I32)
    owns = counts > 0
    ordinal = jnp.cumsum(owns.astype(I32)) - 1
    later = jnp.where(owns[None, :] & (e_ids[None, :] > e_ids[:, None]), e_ids[None, :], N_EXPERTS)
    nxt = jnp.min(later, axis=1)
    nxt = jnp.where(nxt == N_EXPERTS, e_ids, nxt)
    onehot = (tile_expert[:, None] == e_ids[None, :]).astype(I32)
    tile_slot = jnp.sum(onehot * (ordinal % 2)[None, :], axis=1).astype(I32)
    tile_next = jnp.sum(onehot * nxt[None, :], axis=1).astype(I32)
    return pos_km, tile_expert, tile_valid, tile_slot, tile_next


def _rope_tables():
    t = np.arange(DEC_SEQ)
    row = (t // GRID_W).astype(np.float32)
    col = (t % GRID_W).astype(np.float32)
    d_axis = RET_DK // 2
    inv = jnp.asarray(ROPE_BASE, F32) ** (-jnp.arange(0, d_axis, 2, dtype=F32) / d_axis)
    ang = jnp.concatenate([row[:, None] * inv, col[:, None] * inv], axis=-1)
    cos = jnp.repeat(jnp.cos(ang), 2, axis=-1)
    sin = jnp.repeat(jnp.sin(ang), 2, axis=-1)
    sign = jnp.asarray(np.tile(np.array([-1.0, 1.0], np.float32), RET_DK // 2))
    return cos, sin * sign


def _deinterleave_matrix():
    p = np.zeros((PERM_W, PERM_W), np.float32)
    half = PERM_W // 2
    for j in range(half):
        p[2 * j, j] = 1.0
        p[2 * j + 1, half + j] = 1.0
    return jnp.asarray(p, BF16)


def kernel(x_prompt, x_sample, state_ret_fwd, state_ret_bwd, cache_na_k, cache_na_v, c, c_ctx, ada_w, ada_b, norm_mix, norm_ffn, norm_final, ret_w_in, ret_w_out, ret_decay_fwd, ret_decay_bwd, na_w_in, na_w_out, na_rpb, moe_w_router, moe_b_router, moe_w1, moe_b1, moe_w2, moe_b2):
    x = jnp.concatenate([x_prompt.reshape(N_PROMPT, D_MODEL), x_sample.reshape(N_SAMPLE, D_MODEL)], axis=0)

    cond = jnp.zeros((MOD_ROWS, D_MODEL), F32).at[0].set(c_ctx).at[1:1 + DEC_BATCH].set(c)
    mod_all = _adaln(cond, ada_w, ada_b)
    mod_all = mod_all[:, :N_GROUPS].reshape(DEPTH, N_GROUPS, 6, D_MODEL)
    mod_all = jnp.pad(mod_all, ((0, 0), (0, 0), (0, MOD_ROWS - 6), (0, 0)))

    cos, sin_signed = _rope_tables()
    perm = _deinterleave_matrix()
    wr_all = jnp.pad(moe_w_router, ((0, 0), (0, 0), (0, LANES - N_EXPERTS)))
    br_all = jnp.pad(moe_b_router, ((0, 0), (0, LANES - N_EXPERTS))).reshape(DEPTH, 1, LANES)
    b1_split = _deinterleave_bias(moe_b1.reshape(DEPTH * N_EXPERTS, 2 * D_EXPERT), perm)
    b1_split = b1_split.reshape(DEPTH, N_EXPERTS, 1, 2 * D_EXPERT)
    b2_all = moe_b2.reshape(DEPTH, N_EXPERTS, 1, D_MODEL)
    ret_states = None
    na_caches = None
    for layer in range(DEPTH):
        mod = mod_all[layer]
        j = layer // 2
        if layer % 2 == 0:
            proj = _inproj(x, norm_mix[layer], mod, _cast_layer_bf16(ret_w_in, j))
            decay = jnp.stack([ret_decay_fwd[j], ret_decay_bwd[j]])
            a_p, s_f, s_b = _ret_prompt(proj, decay, j, ret_states)
            ret_states = (s_f, s_b)
            a_s = _ret_sample(proj, decay, cos, sin_signed, state_ret_fwd, state_ret_bwd, j)
            w_out = _cast_layer_bf16(ret_w_out, j)
        else:
            proj, ck, cv = _inproj(x, norm_mix[layer], mod, _cast_layer_bf16(na_w_in, j), j, na_caches)
            na_caches = (ck, cv)
            a_p = _ctx_attn(proj)
            a_s = _na_attn(proj, cache_na_k, cache_na_v, _na_bias_table(na_rpb[j]), j)
            w_out = _cast_layer_bf16(na_w_out, j)

        x, tok, logits = _outproj(a_p, a_s, w_out, x, norm_ffn[layer], mod, wr_all[layer], br_all[layer])

        idx, gates, rank, counts = _router(logits)
        pos_km, tile_expert, tile_valid, tile_slot, tile_next = _routing_tables(idx, rank, counts)
        xs = _dispatch(tok, pos_km)
        ys = _moe_experts(tile_expert, tile_valid, tile_slot, tile_next, xs, moe_w1, b1_split, moe_w2, b2_all,
                          perm, layer)
        y_km = _gather_rows(ys, pos_km.reshape(-1))
        x = _combine(y_km, gates, x, mod, norm_final, final=(layer == DEPTH - 1))

    y_prompt = x[0].reshape(BATCH, SEQ, D_MODEL)
    y_sample = x[1].reshape(DEC_BATCH, DEC_SEQ, D_MODEL)
    return (y_prompt, y_sample, ret_states[0], ret_states[1],
            na_caches[0].reshape(BATCH, N_NA_LAYERS, SEQ, NA_HEADS, NA_HD),
            na_caches[1].reshape(BATCH, N_NA_LAYERS, SEQ, NA_HEADS, NA_HD))
```

```python
import functools

import numpy as np
import jax
import jax.numpy as jnp
from jax import lax
from jax.experimental import pallas as pl
from jax.experimental.pallas import tpu as pltpu
from jax.experimental.pallas import tpu_sc as plsc

F32 = jnp.float32
BF16 = jnp.bfloat16
I32 = jnp.int32
U32 = jnp.uint32

D_MODEL = 1024
BATCH = 32
SEQ = 256
DEPTH = 4
DEC_BATCH = 2
DEC_SEQ = 1024
PAST_LEN = 512
GRID_W = 64
EPS = 1e-6
ROPE_BASE = 10000.0
RET_HEADS = 4
RET_DK = D_MODEL // RET_HEADS
RET_DV = 2 * RET_DK
RET_QK_W = RET_HEADS * RET_DK
RET_V_W = RET_HEADS * RET_DV
NA_HEADS = 16
NA_HD = D_MODEL // NA_HEADS
NA_KH = 8
NA_KW = 16
N_EXPERTS = 32
TOP_K = 4
D_EXPERT = D_MODEL
SWIGLU_LIMIT = 7.0
SWIGLU_ALPHA = 1.702
N_RET_LAYERS = (DEPTH + 1) // 2
N_NA_LAYERS = DEPTH // 2

N_PROMPT = BATCH * SEQ
N_SAMPLE = DEC_BATCH * DEC_SEQ
N_TOK = N_PROMPT + N_SAMPLE
N_GROUPS = 1 + DEC_BATCH
MOD_ROWS = 8
LANES = 128

TM = 256
RET_CHUNK = 256
MOE_TM = 256
MOE_ROWS = N_TOK * TOP_K + N_EXPERTS * MOE_TM
MOE_TILES = MOE_ROWS // MOE_TM
ROUTER_TM = 512
PERM_W = 256
VMEM_LIMIT = 56 * 1024 * 1024


def _cparams(n_axes, vmem=VMEM_LIMIT):
    return pltpu.CompilerParams(dimension_semantics=("arbitrary",) * n_axes, vmem_limit_bytes=vmem)


def _group_of_tile(i, tm):
    return jnp.maximum((i * tm) // DEC_SEQ - (N_PROMPT // DEC_SEQ - 1), 0)


def _sigmoid(x):
    return 1.0 / (1.0 + jnp.exp(-x))


def _silu(x):
    return x * _sigmoid(x)


def _log_sigmoid(x):
    return jnp.minimum(x, 0.0) - jnp.log(1.0 + jnp.exp(-jnp.abs(x)))


def _norm_mod(x, g, shift, scale):
    y = x * lax.rsqrt(jnp.mean(x * x, axis=-1, keepdims=True) + EPS)
    return (y * g) * (1.0 + scale) + shift


def _dot(a, b):
    return jnp.dot(a, b, preferred_element_type=F32)


HALF = D_MODEL // 2
HI_MASK = 0xFFFF0000


def _pack_rows(x):
    lo = pltpu.bitcast(x[:, :HALF].astype(BF16).astype(F32), U32)
    hi = pltpu.bitcast(x[:, HALF:].astype(BF16).astype(F32), U32)
    return (lo >> 16) | (hi & jnp.uint32(HI_MASK))


def _unpack_rows(p):
    return pltpu.bitcast(p << 16, F32), pltpu.bitcast(p & jnp.uint32(HI_MASK), F32)


def _dot_nt(a, b):
    return lax.dot_general(a, b, (((1,), (1,)), ((), ())), preferred_element_type=F32)


def _dot_tn(a, b):
    return lax.dot_general(a, b, (((0,), (0,)), ((), ())), preferred_element_type=F32)


ADA_TN = 1536


def _adaln_body(cond_ref, w_ref, b_ref, o_ref):
    s = _silu(cond_ref[...]).astype(BF16)
    o_ref[...] = _dot(s, w_ref[...].astype(BF16)) + b_ref[...]


def _adaln(cond, ada_w, ada_b):
    n_out = 6 * D_MODEL
    return pl.pallas_call(
        _adaln_body,
        grid=(DEPTH, n_out // ADA_TN),
        in_specs=[
            pl.BlockSpec((MOD_ROWS, D_MODEL), lambda l, j: (0, 0)),
            pl.BlockSpec((None, D_MODEL, ADA_TN), lambda l, j: (l, 0, j)),
            pl.BlockSpec((None, 1, ADA_TN), lambda l, j: (l, 0, j)),
        ],
        out_specs=pl.BlockSpec((None, MOD_ROWS, ADA_TN), lambda l, j: (l, 0, j)),
        out_shape=jax.ShapeDtypeStruct((DEPTH, MOD_ROWS, n_out), F32),
        compiler_params=_cparams(2),
        name="adaln",
    )(cond, ada_w, ada_b.reshape(DEPTH, 1, n_out))


CAST_TR = 256


def _cast_body(w_ref, o_ref):
    o_ref[...] = w_ref[...].astype(BF16)


def _cast_layer_bf16(w, j):
    _, r, c = w.shape
    return pl.pallas_call(
        _cast_body,
        grid=(r // CAST_TR,),
        in_specs=[pl.BlockSpec((None, CAST_TR, c), lambda i: (j, i, 0))],
        out_specs=pl.BlockSpec((CAST_TR, c), lambda i: (i, 0)),
        out_shape=jax.ShapeDtypeStruct((r, c), BF16),
        compiler_params=_cparams(1),
        name="cast_bf16",
    )(w)


def _x_inputs(x):
    n_pt = N_PROMPT // TM
    xa, xb, off = (x[0], x[1], 0) if isinstance(x, tuple) else (x, x, n_pt)
    specs = [pl.BlockSpec((TM, D_MODEL), lambda i: (jnp.minimum(i, n_pt - 1), 0)),
             pl.BlockSpec((TM, D_MODEL), lambda i: (off + jnp.maximum(i - n_pt, 0), 0))]
    return specs, [xa, xb]


def _x_tile(xa_ref, xb_ref):
    return jnp.where(pl.program_id(0) < N_PROMPT // TM, xa_ref[...], xb_ref[...])


def _weight_input(w, layer):
    if layer is None:
        return pl.BlockSpec(w.shape, lambda i: (0, 0)), []
    _, r, c = w.shape
    spec = pl.BlockSpec((None, r, c), lambda i: (layer, 0, 0), pipeline_mode=pl.Buffered(1))
    return spec, [pltpu.VMEM((r, c), BF16)]


def _weight_ref(w_ref, scratch):
    if not scratch:
        return w_ref

    @pl.when(pl.program_id(0) == 0)
    def _():
        scratch[0][...] = w_ref[...].astype(BF16)

    return scratch[0]


def _inproj_body(xa_ref, xb_ref, g_ref, mod_ref, w_ref, *rest, cache_layer, first_cache, n_scratch):
    scratch = rest[len(rest) - n_scratch:]
    rest = rest[:len(rest) - n_scratch]
    o_ref = rest[-3] if cache_layer is not None else rest[-1]
    i = pl.program_id(0)
    w = _weight_ref(w_ref, scratch)
    h = _norm_mod(_x_tile(xa_ref, xb_ref), g_ref[...], mod_ref[0:1, :], mod_ref[1:2, :]).astype(BF16)
    for c in range(o_ref.shape[1] // D_MODEL):
        sl = slice(c * D_MODEL, (c + 1) * D_MODEL)
        r = _dot(h, w[:, sl])
        o_ref[:, sl] = r
        if cache_layer is not None and c >= 1:
            cache_ref = rest[-3 + c]

            @pl.when(i < BATCH)
            def _(r=r, cache_ref=cache_ref):
                if first_cache:
                    for l in range(N_NA_LAYERS):
                        cache_ref[l] = r if l == cache_layer else jnp.zeros_like(r)
                else:
                    cache_ref[...] = r


def _inproj(x, g, mod, w, w_layer=None, cache_layer=None, prev_caches=None):
    n_out = w.shape[-1]
    x_specs, x_args = _x_inputs(x)
    w_spec, scratch = _weight_input(w, w_layer)
    in_specs = x_specs + [
        pl.BlockSpec((1, D_MODEL), lambda i: (0, 0)),
        pl.BlockSpec((None, MOD_ROWS, D_MODEL), lambda i: (_group_of_tile(i, TM), 0, 0)),
        w_spec,
    ]
    out_specs = [pl.BlockSpec((TM, n_out), lambda i: (i, 0))]
    out_shape = [jax.ShapeDtypeStruct((N_TOK, n_out), F32)]
    args = x_args + [g.reshape(1, D_MODEL), mod, w]
    aliases = {}
    first_cache = prev_caches is None
    if cache_layer is not None:
        assert TM == SEQ and n_out == 3 * D_MODEL
        cache = jax.ShapeDtypeStruct((BATCH, N_NA_LAYERS, SEQ, D_MODEL), F32)
        if first_cache:
            spec = pl.BlockSpec((None, N_NA_LAYERS, SEQ, D_MODEL), lambda i: (jnp.minimum(i, BATCH - 1), 0, 0, 0))
        else:
            spec = pl.BlockSpec((None, None, SEQ, D_MODEL),
                                lambda i: (jnp.minimum(i, BATCH - 1), cache_layer, 0, 0))
            in_specs += [pl.BlockSpec(memory_space=pl.ANY)] * 2
            aliases = {len(args): 1, len(args) + 1: 2}
            args += list(prev_caches)
        out_specs += [spec, spec]
        out_shape += [cache, cache]
    outs = pl.pallas_call(
        functools.partial(_inproj_body, cache_layer=cache_layer, first_cache=first_cache, n_scratch=len(scratch)),
        grid=(N_TOK // TM,),
        in_specs=in_specs,
        out_specs=out_specs,
        out_shape=out_shape,
        scratch_shapes=scratch,
        input_output_aliases=aliases,
        compiler_params=_cparams(1),
        name="inproj",
    )(*args)
    return outs[0] if cache_layer is None else outs


def _decay_terms(dec_ref, h, c):
    lgf = _log_sigmoid(jnp.zeros((c, c), F32) + dec_ref[0, h])
    lgb = _log_sigmoid(jnp.zeros((c, c), F32) + dec_ref[1, h])
    row = lax.broadcasted_iota(I32, (c, c), 0)
    col = lax.broadcasted_iota(I32, (c, c), 1)
    diff = (row - col).astype(F32)
    lower = diff >= 0
    upper = diff <= 0
    mask = (jnp.where(lower, jnp.exp(jnp.where(lower, diff, 0.0) * lgf), 0.0)
            + jnp.where(upper, jnp.exp(jnp.where(upper, -diff, 0.0) * lgb), 0.0))
    pos = lax.broadcasted_iota(I32, (c, 1), 0).astype(F32)
    lgf1 = _log_sigmoid(jnp.zeros((c, 1), F32) + dec_ref[0, h])
    lgb1 = _log_sigmoid(jnp.zeros((c, 1), F32) + dec_ref[1, h])
    lgf0 = _log_sigmoid(jnp.zeros((1, 1), F32) + dec_ref[0, h])
    lgb0 = _log_sigmoid(jnp.zeros((1, 1), F32) + dec_ref[1, h])
    terms = dict(
        mask=mask,
        q_f=jnp.exp((pos + 1.0) * lgf1), k_f=jnp.exp((c - 1.0 - pos) * lgf1), c_f=jnp.exp(c * lgf0),
        q_b=jnp.exp((c - pos) * lgb1), k_b=jnp.exp(pos * lgb1), c_b=jnp.exp(c * lgb0),
    )
    return terms


def _group_norm_gate(o, g):
    o = o * lax.rsqrt(jnp.mean(o * o, axis=-1, keepdims=True) + EPS)
    return (_silu(g) * o).astype(BF16)


def _ret_prompt_body(dec_ref, q_ref, k_ref, v_ref, g_ref, *rest, layer, first):
    o_ref, sf_ref, sb_ref = rest[-3:]
    h = pl.program_id(1)
    t = _decay_terms(dec_ref, h, SEQ)
    q = q_ref[...]
    k = k_ref[...] * (RET_DK ** -0.5)
    v = v_ref[...].astype(BF16)
    scores = _dot_nt(q.astype(BF16), k.astype(BF16)) * t["mask"]
    o = _dot(scores.astype(BF16), v)
    o_ref[...] = _group_norm_gate(o, g_ref[...])
    for ref, k_dec in ((sf_ref, t["k_f"]), (sb_ref, t["k_b"])):
        s = _dot_tn((k * k_dec).astype(BF16), v)
        if first:
            for l in range(N_RET_LAYERS):
                ref[l] = s if l == layer else jnp.zeros_like(s)
        else:
            ref[...] = s


def _ret_prompt(proj, decay, layer, prev_states=None):
    kq = RET_QK_W // RET_DK
    kv = 2 * RET_QK_W // RET_DV
    kg = kv + RET_HEADS
    state = jax.ShapeDtypeStruct((BATCH, N_RET_LAYERS, RET_HEADS, RET_DK, RET_DV), F32)
    in_specs = [
        pl.BlockSpec(memory_space=pltpu.SMEM),
        pl.BlockSpec((SEQ, RET_DK), lambda b, h: (b, h)),
        pl.BlockSpec((SEQ, RET_DK), lambda b, h: (b, kq + h)),
        pl.BlockSpec((SEQ, RET_DV), lambda b, h: (b, kv + h)),
        pl.BlockSpec((SEQ, RET_DV), lambda b, h: (b, kg + h)),
    ]
    args = [decay, proj, proj, proj, proj]
    first = prev_states is None
    aliases = {}
    if first:
        s_spec = pl.BlockSpec((None, N_RET_LAYERS, None, RET_DK, RET_DV), lambda b, h: (b, 0, h, 0, 0))
    else:
        s_spec = pl.BlockSpec((None, None, None, RET_DK, RET_DV), lambda b, h: (b, layer, h, 0, 0))
        in_specs += [pl.BlockSpec(memory_space=pl.ANY)] * 2
        aliases = {len(args): 1, len(args) + 1: 2}
        args += list(prev_states)
    return pl.pallas_call(
        functools.partial(_ret_prompt_body, layer=layer, first=first),
        grid=(BATCH, RET_HEADS),
        in_specs=in_specs,
        out_specs=[pl.BlockSpec((SEQ, RET_DV), lambda b, h: (b, h)), s_spec, s_spec],
        out_shape=[jax.ShapeDtypeStruct((N_PROMPT, RET_V_W), BF16), state, state],
        input_output_aliases=aliases,
        compiler_params=_cparams(2),
        name="ret_prompt",
    )(*args)


def _rope(x, cos, sin_signed):
    n = x.shape[-1]
    lane = lax.broadcasted_iota(I32, x.shape, 1)
    swapped = jnp.where(lane % 2 == 0, pltpu.roll(x, n - 1, 1), pltpu.roll(x, 1, 1))
    return x * cos + swapped * sin_signed


def _ret_sample_body(dec_ref, q_ref, k_ref, v_ref, g_ref, cos_ref, sin_ref, s0f_ref, s0b_ref, o_ref,
                     sf_scr, sb_scr):
    h = pl.program_id(1)
    c = RET_CHUNK
    n_chunks = DEC_SEQ // c
    t = _decay_terms(dec_ref, h, c)

    def chunk(ref, i):
        return ref[i * c:(i + 1) * c, :]

    def qk(i):
        q = _rope(chunk(q_ref, i), chunk(cos_ref, i), chunk(sin_ref, i))
        k = _rope(chunk(k_ref, i) * (RET_DK ** -0.5), chunk(cos_ref, i), chunk(sin_ref, i))
        return q, k

    s = s0f_ref[...]
    for i in range(n_chunks):
        sf_scr[i] = s
        if i + 1 < n_chunks:
            _, k = qk(i)
            s = s * t["c_f"] + _dot_tn((k * t["k_f"]).astype(BF16), chunk(v_ref, i).astype(BF16))
    s = s0b_ref[...]
    for i in reversed(range(n_chunks)):
        sb_scr[i] = s
        if i > 0:
            _, k = qk(i)
            s = s * t["c_b"] + _dot_tn((k * t["k_b"]).astype(BF16), chunk(v_ref, i).astype(BF16))

    for i in range(n_chunks):
        q, k = qk(i)
        scores = _dot_nt(q.astype(BF16), k.astype(BF16)) * t["mask"]
        o = _dot(scores.astype(BF16), chunk(v_ref, i).astype(BF16))
        o = o + _dot((q * t["q_f"]).astype(BF16), sf_scr[i].astype(BF16))
        o = o + _dot((q * t["q_b"]).astype(BF16), sb_scr[i].astype(BF16))
        o_ref[i * c:(i + 1) * c, :] = _group_norm_gate(o, chunk(g_ref, i))


def _ret_sample(proj, decay, cos, sin_signed, s0f, s0b, j):
    kq = RET_QK_W // RET_DK
    kv = 2 * RET_QK_W // RET_DV
    kg = kv + RET_HEADS
    rb = N_PROMPT // DEC_SEQ
    n_chunks = DEC_SEQ // RET_CHUNK
    return pl.pallas_call(
        _ret_sample_body,
        grid=(DEC_BATCH, RET_HEADS),
        in_specs=[
            pl.BlockSpec(memory_space=pltpu.SMEM),
            pl.BlockSpec((DEC_SEQ, RET_DK), lambda b, h: (rb + b, h)),
            pl.BlockSpec((DEC_SEQ, RET_DK), lambda b, h: (rb + b, kq + h)),
            pl.BlockSpec((DEC_SEQ, RET_DV), lambda b, h: (rb + b, kv + h)),
            pl.BlockSpec((DEC_SEQ, RET_DV), lambda b, h: (rb + b, kg + h)),
            pl.BlockSpec((DEC_SEQ, RET_DK), lambda b, h: (0, 0)),
            pl.BlockSpec((DEC_SEQ, RET_DK), lambda b, h: (0, 0)),
            pl.BlockSpec((None, None, None, RET_DK, RET_DV), lambda b, h: (b, j, h, 0, 0)),
            pl.BlockSpec((None, None, None, RET_DK, RET_DV), lambda b, h: (b, j, h, 0, 0)),
        ],
        out_specs=pl.BlockSpec((DEC_SEQ, RET_DV), lambda b, h: (b, h)),
        out_shape=jax.ShapeDtypeStruct((N_SAMPLE, RET_V_W), BF16),
        scratch_shapes=[pltpu.VMEM((n_chunks, RET_DK, RET_DV), F32),
                        pltpu.VMEM((n_chunks, RET_DK, RET_DV), F32)],
        compiler_params=_cparams(2),
        name="ret_sample",
    )(decay, proj, proj, proj, proj, cos, sin_signed, s0f, s0b)


NA_HPB = LANES // NA_HD


def _head_lanes(rows, hh):
    lane = lax.broadcasted_iota(I32, (rows, LANES), 1)
    return jnp.logical_and(lane >= hh * NA_HD, lane < (hh + 1) * NA_HD)


def _with_ones(v):
    return jnp.concatenate([v, jnp.ones(v.shape, v.dtype)], axis=-1)


def _ctx_attn_body(q_ref, k_ref, v_ref, o_ref):
    outs = []
    for p in range(NA_HEADS // NA_HPB):
        sl = slice(p * LANES, (p + 1) * LANES)
        q2 = q_ref[:, sl] * (NA_HD ** -0.5)
        k2 = k_ref[:, sl].astype(BF16)
        v2 = _with_ones(v_ref[:, sl].astype(BF16))
        o_pair = None
        for hh in range(NA_HPB):
            mine = _head_lanes(SEQ, hh)
            s = _dot_nt(jnp.where(mine, q2, 0.0).astype(BF16), k2)
            e = jnp.exp(s - s.max(axis=-1, keepdims=True)).astype(BF16)
            acc = _dot(e, v2)
            o_h = acc[:, :LANES] / acc[:, LANES:]
            o_pair = o_h if o_pair is None else jnp.where(mine, o_h, o_pair)
        outs.append(o_pair)
    o_ref[...] = jnp.concatenate(outs, axis=-1).astype(BF16)


def _ctx_attn(proj):
    return pl.pallas_call(
        _ctx_attn_body,
        grid=(BATCH,),
        in_specs=[
            pl.BlockSpec((SEQ, D_MODEL), lambda b: (b, 0)),
            pl.BlockSpec((SEQ, D_MODEL), lambda b: (b, 1)),
            pl.BlockSpec((SEQ, D_MODEL), lambda b: (b, 2)),
        ],
        out_specs=pl.BlockSpec((SEQ, D_MODEL), lambda b: (b, 0)),
        out_shape=jax.ShapeDtypeStruct((N_PROMPT, D_MODEL), BF16),
        compiler_params=_cparams(1),
        name="ctx_attn",
    )(proj, proj, proj)


NA_QT = 256
NA_WIN = NA_KH * GRID_W


def _na_attn_body(q_ref, k_ref, v_ref, ck_ref, cv_ref, tbl_ref, o_ref):
    t = pl.program_id(1)
    rows_per_tile = NA_QT // GRID_W
    n_rows = DEC_SEQ // GRID_W
    q2 = q_ref[...] * (NA_HD ** -0.5)
    q = [jnp.where(_head_lanes(NA_QT, hh), q2, 0.0).astype(BF16) for hh in range(NA_HPB)]
    ck = ck_ref[...].astype(BF16)
    s_ctx = [_dot_nt(q[hh], ck) for hh in range(NA_HPB)]
    acc_win = [[] for _ in range(NA_HPB)]
    e_ctx = [[] for _ in range(NA_HPB)]
    for u in range(rows_per_tile):
        qr = t * rows_per_tile + u
        rs = jnp.clip(qr - NA_KH // 2, 0, n_rows - NA_KH)
        i0 = rs - qr + (NA_KH - 1)
        start = pl.multiple_of(rs * GRID_W, GRID_W)
        kw = k_ref[pl.ds(start, NA_WIN), :].astype(BF16)
        vw = _with_ones(v_ref[pl.ds(start, NA_WIN), :].astype(BF16))
        rows = slice(u * GRID_W, (u + 1) * GRID_W)
        for hh in range(NA_HPB):
            bias = jnp.concatenate([tbl_ref[hh, i0 + 2 * m] for m in range(NA_KH // 2)], axis=-1)
            sw = _dot_nt(q[hh][rows], kw) + bias
            sc = s_ctx[hh][rows]
            m = jnp.maximum(sw.max(axis=-1, keepdims=True), sc.max(axis=-1, keepdims=True))
            acc_win[hh].append(_dot(jnp.exp(sw - m).astype(BF16), vw))
            e_ctx[hh].append(jnp.exp(sc - m).astype(BF16))
    cv = _with_ones(cv_ref[...].astype(BF16))
    out = None
    for hh in range(NA_HPB):
        acc = jnp.concatenate(acc_win[hh], axis=0) + _dot(jnp.concatenate(e_ctx[hh], axis=0), cv)
        o_h = acc[:, :LANES] / acc[:, LANES:]
        out = o_h if out is None else jnp.where(_head_lanes(NA_QT, hh), o_h, out)
    o_ref[...] = out.astype(BF16)


def _na_attn(proj, cache_k, cache_v, tbl, j):
    n_qt = DEC_SEQ // NA_QT
    rb_q = N_PROMPT // NA_QT
    rb_kv = N_PROMPT // DEC_SEQ
    cb = D_MODEL // LANES
    ck = cache_k.reshape(DEC_BATCH, -1, PAST_LEN, D_MODEL)
    cv = cache_v.reshape(DEC_BATCH, -1, PAST_LEN, D_MODEL)
    return pl.pallas_call(
        _na_attn_body,
        grid=(cb, n_qt, DEC_BATCH),
        in_specs=[
            pl.BlockSpec((NA_QT, LANES), lambda p, t, b: (rb_q + b * n_qt + t, p)),
            pl.BlockSpec((DEC_SEQ, LANES), lambda p, t, b: (rb_kv + b, cb + p)),
            pl.BlockSpec((DEC_SEQ, LANES), lambda p, t, b: (rb_kv + b, 2 * cb + p)),
            pl.BlockSpec((None, None, PAST_LEN, LANES), lambda p, t, b: (b, j, 0, p)),
            pl.BlockSpec((None, None, PAST_LEN, LANES), lambda p, t, b: (b, j, 0, p)),
            pl.BlockSpec((None, NA_HPB, 2 * NA_KH - 2, GRID_W, 2 * GRID_W), lambda p, t, b: (j, p, 0, 0, 0)),
        ],
        out_specs=pl.BlockSpec((NA_QT, LANES), lambda p, t, b: (b * n_qt + t, p)),
        out_shape=jax.ShapeDtypeStruct((N_SAMPLE, D_MODEL), BF16),
        compiler_params=_cparams(3),
        name="na_attn",
    )(proj, proj, proj, ck, cv, tbl)


def _na_bias_table(rpb):
    c = np.arange(GRID_W)
    cs = np.clip(c - NA_KW // 2, 0, GRID_W - NA_KW)
    col_valid = (c[None, :] >= cs[:, None]) & (c[None, :] < cs[:, None] + NA_KW)
    col_off = np.clip(c[None, :] - c[:, None] + NA_KW - 1, 0, 2 * NA_KW - 2)
    sel_c = (col_off[..., None] == np.arange(2 * NA_KW - 1)).astype(np.float32)
    t = jnp.einsum("lhij,qkj->lhiqk", rpb, sel_c, precision=lax.Precision.HIGHEST)
    t = jnp.where(col_valid, t, -jnp.inf)
    return jnp.concatenate([t[:, :, :-1], t[:, :, 1:]], axis=-1)


def _outproj_body(ap_ref, as_ref, w_ref, xa_ref, xb_ref, g_ref, mod_ref, wr_ref, br_ref, xo_ref, tok_ref, lg_ref,
                  *scratch):
    i = pl.program_id(0)
    a = jnp.where(i < N_PROMPT // TM, ap_ref[...], as_ref[...])
    x = _x_tile(xa_ref, xb_ref) + mod_ref[2:3, :] * _dot(a, _weight_ref(w_ref, scratch)[...])
    xo_ref[...] = x
    tok = _norm_mod(x, g_ref[...], mod_ref[3:4, :], mod_ref[4:5, :])
    tok_ref[...] = _pack_rows(tok)
    hi = tok.astype(BF16)
    lo = (tok - hi.astype(F32)).astype(BF16)
    wr = wr_ref[...]
    wr_hi = wr.astype(BF16)
    wr_lo = (wr - wr_hi.astype(F32)).astype(BF16)
    lg_ref[...] = _dot(hi, wr_hi) + _dot(lo, wr_hi) + _dot(hi, wr_lo) + br_ref[...]


def _outproj(a_prompt, a_sample, w, w_layer, x, g, mod, wr, br):
    k_in = a_prompt.shape[1]
    n_pt = N_PROMPT // TM
    x_specs, x_args = _x_inputs(x)
    w_spec, scratch = _weight_input(w, w_layer)
    return pl.pallas_call(
        _outproj_body,
        grid=(N_TOK // TM,),
        in_specs=[
            pl.BlockSpec((TM, k_in), lambda i: (jnp.minimum(i, n_pt - 1), 0)),
            pl.BlockSpec((TM, k_in), lambda i: (jnp.maximum(i - n_pt, 0), 0)),
            w_spec,
            *x_specs,
            pl.BlockSpec((1, D_MODEL), lambda i: (0, 0)),
            pl.BlockSpec((None, MOD_ROWS, D_MODEL), lambda i: (_group_of_tile(i, TM), 0, 0)),
            pl.BlockSpec((D_MODEL, LANES), lambda i: (0, 0)),
            pl.BlockSpec((1, LANES), lambda i: (0, 0)),
        ],
        out_specs=[
            pl.BlockSpec((TM, D_MODEL), lambda i: (i, 0)),
            pl.BlockSpec((TM, HALF), lambda i: (i, 0)),
            pl.BlockSpec((TM, LANES), lambda i: (i, 0)),
        ],
        out_shape=[
            jax.ShapeDtypeStruct((N_TOK, D_MODEL), F32),
            jax.ShapeDtypeStruct((N_TOK, HALF), U32),
            jax.ShapeDtypeStruct((N_TOK, LANES), F32),
        ],
        scratch_shapes=scratch,
        compiler_params=_cparams(1),
        name="outproj",
    )(a_prompt, a_sample, w, *x_args, g.reshape(1, D_MODEL), mod, wr, br)


def _router_body(lg_ref, idx_ref, gate_ref, rank_ref, cnt_ref, carry):
    i = pl.program_id(0)

    @pl.when(i == 0)
    def _():
        carry[...] = jnp.zeros_like(carry)

    lane = lax.broadcasted_iota(I32, (ROUTER_TM, LANES), 1)
    l = jnp.where(lane < N_EXPERTS, lg_ref[...], -jnp.inf)
    vals, idxs, hots = [], [], []
    for _ in range(TOP_K):
        m = l.max(axis=-1, keepdims=True)
        idx = jnp.where(l == m, lane, LANES).min(axis=-1, keepdims=True)
        hot = lane == idx
        l = jnp.where(hot, -jnp.inf, l)
        vals.append(m)
        idxs.append(idx)
        hots.append(hot)
    es = [jnp.exp(v - vals[0]) for v in vals]
    denom = es[0] + es[1] + es[2] + es[3]
    hot_all = (hots[0] | hots[1] | hots[2] | hots[3]).astype(F32)
    r = lax.broadcasted_iota(I32, (ROUTER_TM, ROUTER_TM), 0)
    c = lax.broadcasted_iota(I32, (ROUTER_TM, ROUTER_TM), 1)
    before = _dot((c < r).astype(BF16), hot_all.astype(BF16)) + carry[...]
    idx_out = jnp.zeros((ROUTER_TM, LANES), I32)
    gate_out = jnp.zeros((ROUTER_TM, LANES), F32)
    rank_out = jnp.zeros((ROUTER_TM, LANES), I32)
    for k in range(TOP_K):
        rank_k = jnp.where(hots[k], before, 0.0).sum(axis=-1, keepdims=True).astype(I32)
        idx_out = jnp.where(lane == k, idxs[k], idx_out)
        gate_out = jnp.where(lane == k, es[k] / denom, gate_out)
        rank_out = jnp.where(lane == k, rank_k, rank_out)
    idx_ref[...] = idx_out
    gate_ref[...] = gate_out
    rank_ref[...] = rank_out
    carry[...] = carry[...] + hot_all.sum(axis=0, keepdims=True)
    cnt_ref[...] = carry[...].astype(I32)


def _router(logits):
    spec = pl.BlockSpec((ROUTER_TM, LANES), lambda i: (i, 0))
    return pl.pallas_call(
        _router_body,
        grid=(N_TOK // ROUTER_TM,),
        in_specs=[spec],
        out_specs=[spec, spec, spec, pl.BlockSpec((1, LANES), lambda i: (0, 0))],
        out_shape=[
            jax.ShapeDtypeStruct((N_TOK, LANES), I32),
            jax.ShapeDtypeStruct((N_TOK, LANES), F32),
            jax.ShapeDtypeStruct((N_TOK, LANES), I32),
            jax.ShapeDtypeStruct((1, LANES), I32),
        ],
        scratch_shapes=[pltpu.VMEM((1, LANES), F32)],
        compiler_params=_cparams(1),
        name="router",
    )(logits)


SC_CORES = 2
SC_SUBCORES = 16
SC_WORKERS = SC_CORES * SC_SUBCORES
SC_CH = 32


def _sc_mesh():
    return plsc.VectorSubcoreMesh(core_axis_name="c", subcore_axis_name="s",
                                  num_cores=SC_CORES, num_subcores=SC_SUBCORES)


def _sc_worker():
    return lax.axis_index("s") * SC_CORES + lax.axis_index("c")


def _dispatch(tok, pos_km):
    n, d = tok.shape
    n_ch = n // (SC_WORKERS * SC_CH)
    assert n_ch * SC_WORKERS * SC_CH == n and n_ch % 2 == 0

    @functools.partial(
        pl.kernel, mesh=_sc_mesh(), out_type=jax.ShapeDtypeStruct((MOE_ROWS, d), tok.dtype),
        scratch_types=[pltpu.VMEM((n_ch, TOP_K, SC_CH), I32), pltpu.VMEM((2, SC_CH, d), tok.dtype),
                       pltpu.SemaphoreType.DMA((2,)), pltpu.SemaphoreType.DMA((2,))])
    def scatter_rows(tok_hbm, pos_hbm, out_hbm, idx_v, rows_v, load_sem, scat_sem):
        wid = _sc_worker()
        base = wid * (n_ch * SC_CH)
        pltpu.sync_copy(pos_hbm.at[wid], idx_v)

        def load(g, slot):
            rows = pl.ds(pl.multiple_of(base + g * SC_CH, 8), SC_CH)
            return pltpu.make_async_copy(tok_hbm.at[rows], rows_v.at[slot], load_sem.at[slot])

        def scatter(g, k, slot):
            return pltpu.make_async_copy(rows_v.at[slot], out_hbm.at[idx_v.at[g, k]], scat_sem.at[slot])

        load(0, 0).start()

        @pl.loop(0, n_ch, step=2)
        def _(g0):
            for slot in range(2):
                g = g0 + slot
                load(g, slot).wait()

                @pl.when(g + 1 < n_ch)
                def _():
                    load(g + 1, 1 - slot).start()

                for k in range(TOP_K):
                    scatter(g, k, slot).start()
                for k in range(TOP_K):
                    scatter(g, k, slot).wait()

    pos = pos_km.reshape(TOP_K, SC_WORKERS, n_ch, SC_CH).transpose(1, 2, 0, 3)
    return scatter_rows(tok, pos)


def _gather_rows(table, idx):
    b, d = idx.shape[0], table.shape[1]
    n_ch = b // (SC_WORKERS * SC_CH)
    assert n_ch * SC_WORKERS * SC_CH == b and n_ch % 2 == 0

    @functools.partial(
        pl.kernel, mesh=_sc_mesh(), out_type=jax.ShapeDtypeStruct((b, d), table.dtype),
        scratch_types=[pltpu.VMEM((n_ch, SC_CH), I32), pltpu.VMEM((2, SC_CH, d), table.dtype),
                       pltpu.SemaphoreType.DMA((2,))])
    def gather_rows(table_hbm, idx_hbm, out_hbm, idx_v, rows_v, sems):
        wid = _sc_worker()
        base = wid * (n_ch * SC_CH)
        pltpu.sync_copy(idx_hbm.at[wid], idx_v)

        def gather(g, slot):
            return pltpu.make_async_copy(table_hbm.at[idx_v.at[g]], rows_v.at[slot], sems.at[slot])

        def write_out(g, slot):
            rows = pl.ds(pl.multiple_of(base + g * SC_CH, 8), SC_CH)
            pltpu.sync_copy(rows_v.at[slot], out_hbm.at[rows])

        gather(0, 0).start()

        @pl.loop(0, n_ch, step=2)
        def _(g):
            gather(g + 1, 1).start()
            gather(g, 0).wait()
            write_out(g, 0)

            @pl.when(g + 2 < n_ch)
            def _():
                gather(g + 2, 0).start()

            gather(g + 1, 1).wait()
            write_out(g + 1, 1)

    return gather_rows(table, idx.reshape(SC_WORKERS, n_ch, SC_CH))


def _moe_body(te_ref, tv_ref, slot_ref, nxt_ref, x_ref, w1_hbm, b1_ref, w2_hbm, b2_ref, perm_ref, o_ref,
              w1_buf, w2_buf, w1_scr, w2_scr, sem, *, layer):
    t = pl.program_id(0)
    e = te_ref[t]
    slot = slot_ref[t]
    nxt = nxt_ref[t]
    first = jnp.logical_or(t == 0, e != te_ref[jnp.maximum(t - 1, 0)])

    def fetch(expert, s):
        return (pltpu.make_async_copy(w1_hbm.at[layer, expert], w1_buf.at[s], sem.at[s, 0]),
                pltpu.make_async_copy(w2_hbm.at[layer, expert], w2_buf.at[s], sem.at[s, 1]))

    @pl.when(t == 0)
    def _():
        for cp in fetch(e, slot):
            cp.start()

    @pl.when(jnp.logical_and(tv_ref[t] == 1, first))
    def _():
        for cp in fetch(e, slot):
            cp.wait()

        @pl.when(nxt != e)
        def _():
            for cp in fetch(nxt, 1 - slot):
                cp.start()

        w2_scr[...] = w2_buf[slot].astype(BF16)
        half = PERM_W // 2
        for c in range(2 * D_EXPERT // PERM_W):
            blk = _dot(w1_buf[slot, :, c * PERM_W:(c + 1) * PERM_W].astype(BF16), perm_ref[...]).astype(BF16)
            w1_scr[:, c * half:(c + 1) * half] = blk[:, :half]
            w1_scr[:, D_EXPERT + c * half:D_EXPERT + (c + 1) * half] = blk[:, half:]

    @pl.when(tv_ref[t] == 0)
    def _():
        o_ref[...] = jnp.zeros_like(o_ref)

    @pl.when(tv_ref[t] == 1)
    def _():
        x_lo, x_hi = _unpack_rows(x_ref[...])
        x = jnp.concatenate([x_lo.astype(BF16), x_hi.astype(BF16)], axis=-1)
        hh = _dot(x, w1_scr[...]) + b1_ref[...]
        h_glu = jnp.minimum(hh[:, :D_EXPERT], SWIGLU_LIMIT)
        h_lin = jnp.clip(hh[:, D_EXPERT:], -SWIGLU_LIMIT, SWIGLU_LIMIT)
        a = h_glu * _sigmoid(SWIGLU_ALPHA * h_glu) * (h_lin + 1.0)
        o_ref[...] = _pack_rows(_dot(a.astype(BF16), w2_scr[...]) + b2_ref[...])


def _deinterleave_bias_body(b_ref, perm_ref, o_ref):
    half = PERM_W // 2
    for c in range(2 * D_EXPERT // PERM_W):
        r = b_ref[:, c * PERM_W:(c + 1) * PERM_W]
        acc = jnp.zeros(r.shape, F32)
        for _ in range(3):
            piece = r.astype(BF16)
            acc = acc + _dot(piece, perm_ref[...])
            r = r - piece.astype(F32)
        o_ref[:, c * half:(c + 1) * half] = acc[:, :half]
        o_ref[:, D_EXPERT + c * half:D_EXPERT + (c + 1) * half] = acc[:, half:]


def _deinterleave_bias(b1, perm):
    return pl.pallas_call(
        _deinterleave_bias_body,
        out_shape=jax.ShapeDtypeStruct(b1.shape, F32),
        compiler_params=_cparams(0),
        name="deinterleave_bias",
    )(b1, perm)


def _moe_experts(tile_expert, tile_valid, tile_slot, tile_next, xs, w1, b1_split, w2, b2, perm, layer):
    grid_spec = pltpu.PrefetchScalarGridSpec(
        num_scalar_prefetch=4,
        grid=(MOE_TILES,),
        in_specs=[
            pl.BlockSpec((MOE_TM, HALF), lambda t, te, *_: (t, 0)),
            pl.BlockSpec(memory_space=pl.ANY),
            pl.BlockSpec((None, None, 1, 2 * D_EXPERT), lambda t, te, *_: (layer, te[t], 0, 0)),
            pl.BlockSpec(memory_space=pl.ANY),
            pl.BlockSpec((None, None, 1, D_MODEL), lambda t, te, *_: (layer, te[t], 0, 0)),
            pl.BlockSpec((PERM_W, PERM_W), lambda t, te, *_: (0, 0)),
        ],
        out_specs=pl.BlockSpec((MOE_TM, HALF), lambda t, te, *_: (t, 0)),
        scratch_shapes=[
            pltpu.VMEM((2, D_MODEL, 2 * D_EXPERT), F32),
            pltpu.VMEM((2, D_EXPERT, D_MODEL), F32),
            pltpu.VMEM((D_MODEL, 2 * D_EXPERT), BF16),
            pltpu.VMEM((D_EXPERT, D_MODEL), BF16),
            pltpu.SemaphoreType.DMA((2, 2)),
        ],
    )
    return pl.pallas_call(
        functools.partial(_moe_body, layer=layer),
        grid_spec=grid_spec,
        out_shape=jax.ShapeDtypeStruct((MOE_ROWS, HALF), U32),
        compiler_params=_cparams(1),
        name="moe_experts",
    )(tile_expert, tile_valid, tile_slot, tile_next, xs, w1, b1_split, w2, b2, perm)


COMBINE_TM = 256


def _combine_body(*refs, final):
    y_refs = refs[:TOP_K]
    gate_ref, x_ref, mod_ref, g_ref = refs[TOP_K:TOP_K + 4]
    gates = gate_ref[...]
    y_lo, y_hi = (gates[:, 0:1] * h for h in _unpack_rows(y_refs[0][...]))
    for k in range(1, TOP_K):
        lo, hi = _unpack_rows(y_refs[k][...])
        y_lo = y_lo + gates[:, k:k + 1] * lo
        y_hi = y_hi + gates[:, k:k + 1] * hi
    x = x_ref[...] + mod_ref[5:6, :] * jnp.concatenate([y_lo, y_hi], axis=-1)
    if not final:
        refs[-1][...] = x
        return
    x = (x * lax.rsqrt(jnp.mean(x * x, axis=-1, keepdims=True) + EPS)) * g_ref[...]
    prompt_ref, sample_ref = refs[-2:]
    is_prompt = pl.program_id(0) < N_PROMPT // COMBINE_TM

    @pl.when(is_prompt)
    def _():
        prompt_ref[...] = x

    @pl.when(jnp.logical_not(is_prompt))
    def _():
        sample_ref[...] = x


def _combine(y_km, gates, x, mod, g_final, final):
    n_t = N_TOK // COMBINE_TM
    n_pt = N_PROMPT // COMBINE_TM
    y_specs = [pl.BlockSpec((COMBINE_TM, HALF), lambda i, k=k: (k * n_t + i, 0)) for k in range(TOP_K)]
    if final:
        out_specs = [pl.BlockSpec((COMBINE_TM, D_MODEL), lambda i: (jnp.minimum(i, n_pt - 1), 0)),
                     pl.BlockSpec((COMBINE_TM, D_MODEL), lambda i: (jnp.maximum(i - n_pt, 0), 0))]
        out_shape = [jax.ShapeDtypeStruct((N_PROMPT, D_MODEL), F32), jax.ShapeDtypeStruct((N_SAMPLE, D_MODEL), F32)]
    else:
        out_specs = pl.BlockSpec((COMBINE_TM, D_MODEL), lambda i: (i, 0))
        out_shape = jax.ShapeDtypeStruct((N_TOK, D_MODEL), F32)
    return pl.pallas_call(
        functools.partial(_combine_body, final=final),
        grid=(n_t,),
        in_specs=y_specs + [
            pl.BlockSpec((COMBINE_TM, LANES), lambda i: (i, 0)),
            pl.BlockSpec((COMBINE_TM, D_MODEL), lambda i: (i, 0)),
            pl.BlockSpec((None, MOD_ROWS, D_MODEL), lambda i: (_group_of_tile(i, COMBINE_TM), 0, 0)),
            pl.BlockSpec((1, D_MODEL), lambda i: (0, 0)),
        ],
        out_specs=out_specs,
        out_shape=out_shape,
        compiler_params=_cparams(1),
        name="combine",
    )(*([y_km] * TOP_K), gates, x, mod, g_final.reshape(1, D_MODEL))


def _routing_tables(idx, rank, counts):
    counts = counts[0, :N_EXPERTS]
    padded = ((counts + MOE_TM - 1) // MOE_TM) * MOE_TM
    ends = jnp.cumsum(padded)
    offsets = ends - padded
    idx = idx[:, :TOP_K]
    pos = rank[:, :TOP_K] + jnp.sum(
        jnp.where(idx[..., None] == jnp.arange(N_EXPERTS, dtype=I32), offsets, 0), axis=-1)
    pos_km = pos.T.astype(I32)
    tile_start = jnp.arange(MOE_TILES, dtype=I32) * MOE_TM
    tile_valid = (tile_start < ends[-1]).astype(I32)
    tile_expert = jnp.sum((tile_start[:, None] >= ends[None, :]).astype(I32), axis=-1)
    last_expert = jnp.sum((ends[-1] - 1 >= ends).astype(I32))
    tile_expert = jnp.where(tile_valid == 1, tile_expert, last_expert).astype(I32)
    e_ids = jnp.arange(N_EXPERTS, dtype=I32)
    owns = counts > 0
    ordinal = jnp.cumsum(owns.astype(I32)) - 1
    later = jnp.where(owns[None, :] & (e_ids[None, :] > e_ids[:, None]), e_ids[None, :], N_EXPERTS)
    nxt = jnp.min(later, axis=1)
    nxt = jnp.where(nxt == N_EXPERTS, e_ids, nxt)
    onehot = (tile_expert[:, None] == e_ids[None, :]).astype(I32)
    tile_slot = jnp.sum(onehot * (ordinal % 2)[None, :], axis=1).astype(I32)
    tile_next = jnp.sum(onehot * nxt[None, :], axis=1).astype(I32)
    return pos_km, tile_expert, tile_valid, tile_slot, tile_next


def _rope_tables():
    t = np.arange(DEC_SEQ)
    row = (t // GRID_W).astype(np.float32)
    col = (t % GRID_W).astype(np.float32)
    d_axis = RET_DK // 2
    inv = jnp.asarray(ROPE_BASE, F32) ** (-jnp.arange(0, d_axis, 2, dtype=F32) / d_axis)
    ang = jnp.concatenate([row[:, None] * inv, col[:, None] * inv], axis=-1)
    cos = jnp.repeat(jnp.cos(ang), 2, axis=-1)
    sin = jnp.repeat(jnp.sin(ang), 2, axis=-1)
    sign = jnp.asarray(np.tile(np.array([-1.0, 1.0], np.float32), RET_DK // 2))
    return cos, sin * sign


def _deinterleave_matrix():
    p = np.zeros((PERM_W, PERM_W), np.float32)
    half = PERM_W // 2
    for j in range(half):
        p[2 * j, j] = 1.0
        p[2 * j + 1, half + j] = 1.0
    return jnp.asarray(p, BF16)


def kernel(x_prompt, x_sample, state_ret_fwd, state_ret_bwd, cache_na_k, cache_na_v, c, c_ctx, ada_w, ada_b, norm_mix, norm_ffn, norm_final, ret_w_in, ret_w_out, ret_decay_fwd, ret_decay_bwd, na_w_in, na_w_out, na_rpb, moe_w_router, moe_b_router, moe_w1, moe_b1, moe_w2, moe_b2):
    x = (x_prompt.reshape(N_PROMPT, D_MODEL), x_sample.reshape(N_SAMPLE, D_MODEL))

    cond = jnp.zeros((MOD_ROWS, D_MODEL), F32).at[0].set(c_ctx).at[1:1 + DEC_BATCH].set(c)
    mod_all = _adaln(cond, ada_w, ada_b)
    mod_all = mod_all[:, :N_GROUPS].reshape(DEPTH, N_GROUPS, 6, D_MODEL)
    mod_all = jnp.pad(mod_all, ((0, 0), (0, 0), (0, MOD_ROWS - 6), (0, 0)))

    cos, sin_signed = _rope_tables()
    perm = _deinterleave_matrix()
    wr_all = jnp.pad(moe_w_router, ((0, 0), (0, 0), (0, LANES - N_EXPERTS)))
    br_all = jnp.pad(moe_b_router, ((0, 0), (0, LANES - N_EXPERTS))).reshape(DEPTH, 1, LANES)
    b1_split = _deinterleave_bias(moe_b1.reshape(DEPTH * N_EXPERTS, 2 * D_EXPERT), perm)
    b1_split = b1_split.reshape(DEPTH, N_EXPERTS, 1, 2 * D_EXPERT)
    b2_all = moe_b2.reshape(DEPTH, N_EXPERTS, 1, D_MODEL)
    na_tbl = _na_bias_table(na_rpb)
    ret_states = None
    na_caches = None
    for layer in range(DEPTH):
        mod = mod_all[layer]
        j = layer // 2
        if layer % 2 == 0:
            proj = _inproj(x, norm_mix[layer], mod, _cast_layer_bf16(ret_w_in, j))
            decay = jnp.stack([ret_decay_fwd[j], ret_decay_bwd[j]])
            a_p, s_f, s_b = _ret_prompt(proj, decay, j, ret_states)
            ret_states = (s_f, s_b)
            a_s = _ret_sample(proj, decay, cos, sin_signed, state_ret_fwd, state_ret_bwd, j)
            w_out = ret_w_out
        else:
            proj, ck, cv = _inproj(x, norm_mix[layer], mod, na_w_in, j, j, na_caches)
            na_caches = (ck, cv)
            a_p = _ctx_attn(proj)
            a_s = _na_attn(proj, cache_na_k, cache_na_v, na_tbl, j)
            w_out = na_w_out

        x, tok, logits = _outproj(a_p, a_s, w_out, j, x, norm_ffn[layer], mod, wr_all[layer], br_all[layer])

        idx, gates, rank, counts = _router(logits)
        pos_km, tile_expert, tile_valid, tile_slot, tile_next = _routing_tables(idx, rank, counts)
        xs = _dispatch(tok, pos_km)
        ys = _moe_experts(tile_expert, tile_valid, tile_slot, tile_next, xs, moe_w1, b1_split, moe_w2, b2_all,
                          perm, layer)
        y_km = _gather_rows(ys, pos_km.reshape(-1))
        x = _combine(y_km, gates, x, mod, norm_final, final=(layer == DEPTH - 1))

    y_prompt = x[0].reshape(BATCH, SEQ, D_MODEL)
    y_sample = x[1].reshape(DEC_BATCH, DEC_SEQ, D_MODEL)
    return (y_prompt, y_sample, ret_states[0], ret_states[1],
            na_caches[0].reshape(BATCH, N_NA_LAYERS, SEQ, NA_HEADS, NA_HD),
            na_caches[1].reshape(BATCH, N_NA_LAYERS, SEQ, NA_HEADS, NA_HD))
```

```python
import functools

import numpy as np
import jax
import jax.numpy as jnp
from jax import lax
from jax.experimental import pallas as pl
from jax.experimental.pallas import tpu as pltpu
from jax.experimental.pallas import tpu_sc as plsc

F32 = jnp.float32
BF16 = jnp.bfloat16
I32 = jnp.int32
U32 = jnp.uint32

D_MODEL = 1024
BATCH = 32
SEQ = 256
DEPTH = 4
DEC_BATCH = 2
DEC_SEQ = 1024
PAST_LEN = 512
GRID_W = 64
EPS = 1e-6
ROPE_BASE = 10000.0
RET_HEADS = 4
RET_DK = D_MODEL // RET_HEADS
RET_DV = 2 * RET_DK
RET_QK_W = RET_HEADS * RET_DK
RET_V_W = RET_HEADS * RET_DV
NA_HEADS = 16
NA_HD = D_MODEL // NA_HEADS
NA_KH = 8
NA_KW = 16
N_EXPERTS = 32
TOP_K = 4
D_EXPERT = D_MODEL
SWIGLU_LIMIT = 7.0
SWIGLU_ALPHA = 1.702
N_RET_LAYERS = (DEPTH + 1) // 2
N_NA_LAYERS = DEPTH // 2

N_PROMPT = BATCH * SEQ
N_SAMPLE = DEC_BATCH * DEC_SEQ
N_TOK = N_PROMPT + N_SAMPLE
N_GROUPS = 1 + DEC_BATCH
MOD_ROWS = 8
LANES = 128

TM = 256
RET_CHUNK = 256
MOE_TM = 256
MOE_ROWS = N_TOK * TOP_K + N_EXPERTS * MOE_TM
MOE_TILES = MOE_ROWS // MOE_TM
ROUTER_TM = 512
PERM_W = 256
VMEM_LIMIT = 56 * 1024 * 1024


def _cparams(n_axes, vmem=VMEM_LIMIT):
    return pltpu.CompilerParams(dimension_semantics=("arbitrary",) * n_axes, vmem_limit_bytes=vmem)


def _group_of_tile(i, tm):
    return jnp.maximum((i * tm) // DEC_SEQ - (N_PROMPT // DEC_SEQ - 1), 0)


def _sigmoid(x):
    return 1.0 / (1.0 + jnp.exp(-x))


def _silu(x):
    return x * _sigmoid(x)


def _log_sigmoid(x):
    return jnp.minimum(x, 0.0) - jnp.log(1.0 + jnp.exp(-jnp.abs(x)))


def _norm_mod(x, g, shift, scale):
    y = x * lax.rsqrt(jnp.mean(x * x, axis=-1, keepdims=True) + EPS)
    return (y * g) * (1.0 + scale) + shift


def _dot(a, b):
    return jnp.dot(a, b, preferred_element_type=F32)


HALF = D_MODEL // 2
HI_MASK = 0xFFFF0000


def _pack_rows(x):
    lo = pltpu.bitcast(x[:, :HALF].astype(BF16).astype(F32), U32)
    hi = pltpu.bitcast(x[:, HALF:].astype(BF16).astype(F32), U32)
    return (lo >> 16) | (hi & jnp.uint32(HI_MASK))


def _unpack_rows(p):
    return pltpu.bitcast(p << 16, F32), pltpu.bitcast(p & jnp.uint32(HI_MASK), F32)


def _dot_nt(a, b):
    return lax.dot_general(a, b, (((1,), (1,)), ((), ())), preferred_element_type=F32)


def _dot_tn(a, b):
    return lax.dot_general(a, b, (((0,), (0,)), ((), ())), preferred_element_type=F32)


ADA_TN = 1536


def _adaln_body(cond_ref, w_ref, b_ref, o_ref):
    s = _silu(cond_ref[...]).astype(BF16)
    o_ref[...] = _dot(s, w_ref[...].astype(BF16)) + b_ref[...]


def _adaln(cond, ada_w, ada_b):
    n_out = 6 * D_MODEL
    return pl.pallas_call(
        _adaln_body,
        grid=(DEPTH, n_out // ADA_TN),
        in_specs=[
            pl.BlockSpec((MOD_ROWS, D_MODEL), lambda l, j: (0, 0)),
            pl.BlockSpec((None, D_MODEL, ADA_TN), lambda l, j: (l, 0, j)),
            pl.BlockSpec((None, 1, ADA_TN), lambda l, j: (l, 0, j)),
        ],
        out_specs=pl.BlockSpec((None, MOD_ROWS, ADA_TN), lambda l, j: (l, 0, j)),
        out_shape=jax.ShapeDtypeStruct((DEPTH, MOD_ROWS, n_out), F32),
        compiler_params=_cparams(2),
        name="adaln",
    )(cond, ada_w, ada_b.reshape(DEPTH, 1, n_out))


CAST_TR = 256


def _cast_body(w_ref, o_ref):
    o_ref[...] = w_ref[...].astype(BF16)


def _cast_layer_bf16(w, j):
    _, r, c = w.shape
    return pl.pallas_call(
        _cast_body,
        grid=(r // CAST_TR,),
        in_specs=[pl.BlockSpec((None, CAST_TR, c), lambda i: (j, i, 0))],
        out_specs=pl.BlockSpec((CAST_TR, c), lambda i: (i, 0)),
        out_shape=jax.ShapeDtypeStruct((r, c), BF16),
        compiler_params=_cparams(1),
        name="cast_bf16",
    )(w)


def _x_inputs(x):
    n_pt = N_PROMPT // TM
    xa, xb, off = (x[0], x[1], 0) if isinstance(x, tuple) else (x, x, n_pt)
    specs = [pl.BlockSpec((TM, D_MODEL), lambda i: (jnp.minimum(i, n_pt - 1), 0)),
             pl.BlockSpec((TM, D_MODEL), lambda i: (off + jnp.maximum(i - n_pt, 0), 0))]
    return specs, [xa, xb]


def _x_tile(xa_ref, xb_ref):
    return jnp.where(pl.program_id(0) < N_PROMPT // TM, xa_ref[...], xb_ref[...])


def _weight_input(w, layer):
    if layer is None:
        return pl.BlockSpec(w.shape, lambda i: (0, 0)), []
    _, r, c = w.shape
    spec = pl.BlockSpec((None, r, c), lambda i: (layer, 0, 0), pipeline_mode=pl.Buffered(1))
    return spec, [pltpu.VMEM((r, c), BF16)]


def _weight_ref(w_ref, scratch):
    if not scratch:
        return w_ref

    @pl.when(pl.program_id(0) == 0)
    def _():
        scratch[0][...] = w_ref[...].astype(BF16)

    return scratch[0]


def _inproj_body(xa_ref, xb_ref, g_ref, mod_ref, w_ref, *rest, cache_layer, first_cache, n_scratch):
    scratch = rest[len(rest) - n_scratch:]
    rest = rest[:len(rest) - n_scratch]
    o_ref = rest[-3] if cache_layer is not None else rest[-1]
    i = pl.program_id(0)
    w = _weight_ref(w_ref, scratch)
    h = _norm_mod(_x_tile(xa_ref, xb_ref), g_ref[...], mod_ref[0:1, :], mod_ref[1:2, :]).astype(BF16)
    for c in range(o_ref.shape[1] // D_MODEL):
        sl = slice(c * D_MODEL, (c + 1) * D_MODEL)
        r = _dot(h, w[:, sl])
        o_ref[:, sl] = r.astype(o_ref.dtype)
        if cache_layer is not None and c >= 1:
            cache_ref = rest[-3 + c]

            @pl.when(i < BATCH)
            def _(r=r, cache_ref=cache_ref):
                if first_cache:
                    for l in range(N_NA_LAYERS):
                        cache_ref[l] = r if l == cache_layer else jnp.zeros_like(r)
                else:
                    cache_ref[...] = r


def _inproj(x, g, mod, w, w_layer=None, cache_layer=None, prev_caches=None, out_dtype=F32):
    n_out = w.shape[-1]
    x_specs, x_args = _x_inputs(x)
    w_spec, scratch = _weight_input(w, w_layer)
    in_specs = x_specs + [
        pl.BlockSpec((1, D_MODEL), lambda i: (0, 0)),
        pl.BlockSpec((None, MOD_ROWS, D_MODEL), lambda i: (_group_of_tile(i, TM), 0, 0)),
        w_spec,
    ]
    out_specs = [pl.BlockSpec((TM, n_out), lambda i: (i, 0))]
    out_shape = [jax.ShapeDtypeStruct((N_TOK, n_out), out_dtype)]
    args = x_args + [g.reshape(1, D_MODEL), mod, w]
    aliases = {}
    first_cache = prev_caches is None
    if cache_layer is not None:
        assert TM == SEQ and n_out == 3 * D_MODEL
        cache = jax.ShapeDtypeStruct((BATCH, N_NA_LAYERS, SEQ, D_MODEL), F32)
        if first_cache:
            spec = pl.BlockSpec((None, N_NA_LAYERS, SEQ, D_MODEL), lambda i: (jnp.minimum(i, BATCH - 1), 0, 0, 0))
        else:
            spec = pl.BlockSpec((None, None, SEQ, D_MODEL),
                                lambda i: (jnp.minimum(i, BATCH - 1), cache_layer, 0, 0))
            in_specs += [pl.BlockSpec(memory_space=pl.ANY)] * 2
            aliases = {len(args): 1, len(args) + 1: 2}
            args += list(prev_caches)
        out_specs += [spec, spec]
        out_shape += [cache, cache]
    outs = pl.pallas_call(
        functools.partial(_inproj_body, cache_layer=cache_layer, first_cache=first_cache, n_scratch=len(scratch)),
        grid=(N_TOK // TM,),
        in_specs=in_specs,
        out_specs=out_specs,
        out_shape=out_shape,
        scratch_shapes=scratch,
        input_output_aliases=aliases,
        compiler_params=_cparams(1),
        name="inproj",
    )(*args)
    return outs[0] if cache_layer is None else outs


def _decay_terms(dec_ref, h, c):
    lgf = _log_sigmoid(jnp.zeros((c, c), F32) + dec_ref[0, h])
    lgb = _log_sigmoid(jnp.zeros((c, c), F32) + dec_ref[1, h])
    row = lax.broadcasted_iota(I32, (c, c), 0)
    col = lax.broadcasted_iota(I32, (c, c), 1)
    diff = (row - col).astype(F32)
    lower = diff >= 0
    upper = diff <= 0
    mask = (jnp.where(lower, jnp.exp(jnp.where(lower, diff, 0.0) * lgf), 0.0)
            + jnp.where(upper, jnp.exp(jnp.where(upper, -diff, 0.0) * lgb), 0.0))
    pos = lax.broadcasted_iota(I32, (c, 1), 0).astype(F32)
    lgf1 = _log_sigmoid(jnp.zeros((c, 1), F32) + dec_ref[0, h])
    lgb1 = _log_sigmoid(jnp.zeros((c, 1), F32) + dec_ref[1, h])
    lgf0 = _log_sigmoid(jnp.zeros((1, 1), F32) + dec_ref[0, h])
    lgb0 = _log_sigmoid(jnp.zeros((1, 1), F32) + dec_ref[1, h])
    terms = dict(
        mask=mask,
        q_f=jnp.exp((pos + 1.0) * lgf1), k_f=jnp.exp((c - 1.0 - pos) * lgf1), c_f=jnp.exp(c * lgf0),
        q_b=jnp.exp((c - pos) * lgb1), k_b=jnp.exp(pos * lgb1), c_b=jnp.exp(c * lgb0),
    )
    return terms


def _group_norm_gate(o, g):
    o = o * lax.rsqrt(jnp.mean(o * o, axis=-1, keepdims=True) + EPS)
    return (_silu(g) * o).astype(BF16)


def _ret_prompt_body(dec_ref, q_ref, k_ref, v_ref, g_ref, *rest, layer, first):
    o_ref, sf_ref, sb_ref = rest[-3:]
    h = pl.program_id(1)
    t = _decay_terms(dec_ref, h, SEQ)
    q = q_ref[...].astype(F32)
    k = k_ref[...].astype(F32) * (RET_DK ** -0.5)
    v = v_ref[...].astype(BF16)
    scores = _dot_nt(q.astype(BF16), k.astype(BF16)) * t["mask"]
    o = _dot(scores.astype(BF16), v)
    o_ref[...] = _group_norm_gate(o, g_ref[...].astype(F32))
    for ref, k_dec in ((sf_ref, t["k_f"]), (sb_ref, t["k_b"])):
        s = _dot_tn((k * k_dec).astype(BF16), v)
        if first:
            for l in range(N_RET_LAYERS):
                ref[l] = s if l == layer else jnp.zeros_like(s)
        else:
            ref[...] = s


def _ret_prompt(proj, decay, layer, prev_states=None):
    kq = RET_QK_W // RET_DK
    kv = 2 * RET_QK_W // RET_DV
    kg = kv + RET_HEADS
    state = jax.ShapeDtypeStruct((BATCH, N_RET_LAYERS, RET_HEADS, RET_DK, RET_DV), F32)
    in_specs = [
        pl.BlockSpec(memory_space=pltpu.SMEM),
        pl.BlockSpec((SEQ, RET_DK), lambda b, h: (b, h)),
        pl.BlockSpec((SEQ, RET_DK), lambda b, h: (b, kq + h)),
        pl.BlockSpec((SEQ, RET_DV), lambda b, h: (b, kv + h)),
        pl.BlockSpec((SEQ, RET_DV), lambda b, h: (b, kg + h)),
    ]
    args = [decay, proj, proj, proj, proj]
    first = prev_states is None
    aliases = {}
    if first:
        s_spec = pl.BlockSpec((None, N_RET_LAYERS, None, RET_DK, RET_DV), lambda b, h: (b, 0, h, 0, 0))
    else:
        s_spec = pl.BlockSpec((None, None, None, RET_DK, RET_DV), lambda b, h: (b, layer, h, 0, 0))
        in_specs += [pl.BlockSpec(memory_space=pl.ANY)] * 2
        aliases = {len(args): 1, len(args) + 1: 2}
        args += list(prev_states)
    return pl.pallas_call(
        functools.partial(_ret_prompt_body, layer=layer, first=first),
        grid=(BATCH, RET_HEADS),
        in_specs=in_specs,
        out_specs=[pl.BlockSpec((SEQ, RET_DV), lambda b, h: (b, h)), s_spec, s_spec],
        out_shape=[jax.ShapeDtypeStruct((N_PROMPT, RET_V_W), BF16), state, state],
        input_output_aliases=aliases,
        compiler_params=_cparams(2),
        name="ret_prompt",
    )(*args)


def _rope(x, cos, sin_signed):
    n = x.shape[-1]
    lane = lax.broadcasted_iota(I32, x.shape, 1)
    swapped = jnp.where(lane % 2 == 0, pltpu.roll(x, n - 1, 1), pltpu.roll(x, 1, 1))
    return x * cos + swapped * sin_signed


def _ret_sample_body(dec_ref, q_ref, k_ref, v_ref, g_ref, cos_ref, sin_ref, s0f_ref, s0b_ref, o_ref,
                     sf_scr, sb_scr):
    h = pl.program_id(1)
    c = RET_CHUNK
    n_chunks = DEC_SEQ // c
    t = _decay_terms(dec_ref, h, c)

    def chunk(ref, i):
        return ref[i * c:(i + 1) * c, :].astype(F32)

    def qk(i):
        q = _rope(chunk(q_ref, i), chunk(cos_ref, i), chunk(sin_ref, i))
        k = _rope(chunk(k_ref, i) * (RET_DK ** -0.5), chunk(cos_ref, i), chunk(sin_ref, i))
        return q, k

    s = s0f_ref[...]
    for i in range(n_chunks):
        sf_scr[i] = s
        if i + 1 < n_chunks:
            _, k = qk(i)
            s = s * t["c_f"] + _dot_tn((k * t["k_f"]).astype(BF16), chunk(v_ref, i).astype(BF16))
    s = s0b_ref[...]
    for i in reversed(range(n_chunks)):
        sb_scr[i] = s
        if i > 0:
            _, k = qk(i)
            s = s * t["c_b"] + _dot_tn((k * t["k_b"]).astype(BF16), chunk(v_ref, i).astype(BF16))

    for i in range(n_chunks):
        q, k = qk(i)
        scores = _dot_nt(q.astype(BF16), k.astype(BF16)) * t["mask"]
        o = _dot(scores.astype(BF16), chunk(v_ref, i).astype(BF16))
        o = o + _dot((q * t["q_f"]).astype(BF16), sf_scr[i].astype(BF16))
        o = o + _dot((q * t["q_b"]).astype(BF16), sb_scr[i].astype(BF16))
        o_ref[i * c:(i + 1) * c, :] = _group_norm_gate(o, chunk(g_ref, i))


def _ret_sample(proj, decay, cos, sin_signed, s0f, s0b, j):
    kq = RET_QK_W // RET_DK
    kv = 2 * RET_QK_W // RET_DV
    kg = kv + RET_HEADS
    rb = N_PROMPT // DEC_SEQ
    n_chunks = DEC_SEQ // RET_CHUNK
    return pl.pallas_call(
        _ret_sample_body,
        grid=(DEC_BATCH, RET_HEADS),
        in_specs=[
            pl.BlockSpec(memory_space=pltpu.SMEM),
            pl.BlockSpec((DEC_SEQ, RET_DK), lambda b, h: (rb + b, h)),
            pl.BlockSpec((DEC_SEQ, RET_DK), lambda b, h: (rb + b, kq + h)),
            pl.BlockSpec((DEC_SEQ, RET_DV), lambda b, h: (rb + b, kv + h)),
            pl.BlockSpec((DEC_SEQ, RET_DV), lambda b, h: (rb + b, kg + h)),
            pl.BlockSpec((DEC_SEQ, RET_DK), lambda b, h: (0, 0)),
            pl.BlockSpec((DEC_SEQ, RET_DK), lambda b, h: (0, 0)),
            pl.BlockSpec((None, None, None, RET_DK, RET_DV), lambda b, h: (b, j, h, 0, 0)),
            pl.BlockSpec((None, None, None, RET_DK, RET_DV), lambda b, h: (b, j, h, 0, 0)),
        ],
        out_specs=pl.BlockSpec((DEC_SEQ, RET_DV), lambda b, h: (b, h)),
        out_shape=jax.ShapeDtypeStruct((N_SAMPLE, RET_V_W), BF16),
        scratch_shapes=[pltpu.VMEM((n_chunks, RET_DK, RET_DV), F32),
                        pltpu.VMEM((n_chunks, RET_DK, RET_DV), F32)],
        compiler_params=_cparams(2),
        name="ret_sample",
    )(decay, proj, proj, proj, proj, cos, sin_signed, s0f, s0b)


NA_HPB = LANES // NA_HD


def _head_lanes(rows, hh):
    lane = lax.broadcasted_iota(I32, (rows, LANES), 1)
    return jnp.logical_and(lane >= hh * NA_HD, lane < (hh + 1) * NA_HD)


def _with_ones(v):
    return jnp.concatenate([v, jnp.ones(v.shape, v.dtype)], axis=-1)


def _ctx_attn_body(q_ref, k_ref, v_ref, o_ref):
    outs = []
    for p in range(NA_HEADS // NA_HPB):
        sl = slice(p * LANES, (p + 1) * LANES)
        q2 = q_ref[:, sl] * (NA_HD ** -0.5)
        k2 = k_ref[:, sl].astype(BF16)
        v2 = _with_ones(v_ref[:, sl].astype(BF16))
        o_pair = None
        for hh in range(NA_HPB):
            mine = _head_lanes(SEQ, hh)
            s = _dot_nt(jnp.where(mine, q2, 0.0).astype(BF16), k2)
            e = jnp.exp(s - s.max(axis=-1, keepdims=True)).astype(BF16)
            acc = _dot(e, v2)
            o_h = acc[:, :LANES] / acc[:, LANES:]
            o_pair = o_h if o_pair is None else jnp.where(mine, o_h, o_pair)
        outs.append(o_pair)
    o_ref[...] = jnp.concatenate(outs, axis=-1).astype(BF16)


def _ctx_attn(proj):
    return pl.pallas_call(
        _ctx_attn_body,
        grid=(BATCH,),
        in_specs=[
            pl.BlockSpec((SEQ, D_MODEL), lambda b: (b, 0)),
            pl.BlockSpec((SEQ, D_MODEL), lambda b: (b, 1)),
            pl.BlockSpec((SEQ, D_MODEL), lambda b: (b, 2)),
        ],
        out_specs=pl.BlockSpec((SEQ, D_MODEL), lambda b: (b, 0)),
        out_shape=jax.ShapeDtypeStruct((N_PROMPT, D_MODEL), BF16),
        compiler_params=_cparams(1),
        name="ctx_attn",
    )(proj, proj, proj)


NA_QT = 256
NA_WIN = NA_KH * GRID_W


def _na_attn_body(q_ref, k_ref, v_ref, ck_ref, cv_ref, tbl_ref, o_ref):
    t = pl.program_id(1)
    rows_per_tile = NA_QT // GRID_W
    n_rows = DEC_SEQ // GRID_W
    q2 = q_ref[...] * (NA_HD ** -0.5)
    q = [jnp.where(_head_lanes(NA_QT, hh), q2, 0.0).astype(BF16) for hh in range(NA_HPB)]
    ck = ck_ref[...].astype(BF16)
    s_ctx = [_dot_nt(q[hh], ck) for hh in range(NA_HPB)]
    acc_win = [[] for _ in range(NA_HPB)]
    e_ctx = [[] for _ in range(NA_HPB)]
    for u in range(rows_per_tile):
        qr = t * rows_per_tile + u
        rs = jnp.clip(qr - NA_KH // 2, 0, n_rows - NA_KH)
        i0 = rs - qr + (NA_KH - 1)
        start = pl.multiple_of(rs * GRID_W, GRID_W)
        kw = k_ref[pl.ds(start, NA_WIN), :].astype(BF16)
        vw = _with_ones(v_ref[pl.ds(start, NA_WIN), :].astype(BF16))
        rows = slice(u * GRID_W, (u + 1) * GRID_W)
        for hh in range(NA_HPB):
            bias = jnp.concatenate([tbl_ref[hh, i0 + 2 * m] for m in range(NA_KH // 2)], axis=-1)
            sw = _dot_nt(q[hh][rows], kw) + bias
            sc = s_ctx[hh][rows]
            m = jnp.maximum(sw.max(axis=-1, keepdims=True), sc.max(axis=-1, keepdims=True))
            acc_win[hh].append(_dot(jnp.exp(sw - m).astype(BF16), vw))
            e_ctx[hh].append(jnp.exp(sc - m).astype(BF16))
    cv = _with_ones(cv_ref[...].astype(BF16))
    out = None
    for hh in range(NA_HPB):
        acc = jnp.concatenate(acc_win[hh], axis=0) + _dot(jnp.concatenate(e_ctx[hh], axis=0), cv)
        o_h = acc[:, :LANES] / acc[:, LANES:]
        out = o_h if out is None else jnp.where(_head_lanes(NA_QT, hh), o_h, out)
    o_ref[...] = out.astype(BF16)


def _na_attn(proj, cache_k, cache_v, tbl, j):
    n_qt = DEC_SEQ // NA_QT
    rb_q = N_PROMPT // NA_QT
    rb_kv = N_PROMPT // DEC_SEQ
    cb = D_MODEL // LANES
    ck = cache_k.reshape(DEC_BATCH, -1, PAST_LEN, D_MODEL)
    cv = cache_v.reshape(DEC_BATCH, -1, PAST_LEN, D_MODEL)
    return pl.pallas_call(
        _na_attn_body,
        grid=(cb, n_qt, DEC_BATCH),
        in_specs=[
            pl.BlockSpec((NA_QT, LANES), lambda p, t, b: (rb_q + b * n_qt + t, p)),
            pl.BlockSpec((DEC_SEQ, LANES), lambda p, t, b: (rb_kv + b, cb + p)),
            pl.BlockSpec((DEC_SEQ, LANES), lambda p, t, b: (rb_kv + b, 2 * cb + p)),
            pl.BlockSpec((None, None, PAST_LEN, LANES), lambda p, t, b: (b, j, 0, p)),
            pl.BlockSpec((None, None, PAST_LEN, LANES), lambda p, t, b: (b, j, 0, p)),
            pl.BlockSpec((None, NA_HPB, 2 * NA_KH - 2, GRID_W, 2 * GRID_W), lambda p, t, b: (j, p, 0, 0, 0)),
        ],
        out_specs=pl.BlockSpec((NA_QT, LANES), lambda p, t, b: (b * n_qt + t, p)),
        out_shape=jax.ShapeDtypeStruct((N_SAMPLE, D_MODEL), BF16),
        compiler_params=_cparams(3),
        name="na_attn",
    )(proj, proj, proj, ck, cv, tbl)


def _na_bias_table(rpb):
    c = np.arange(GRID_W)
    cs = np.clip(c - NA_KW // 2, 0, GRID_W - NA_KW)
    col_valid = (c[None, :] >= cs[:, None]) & (c[None, :] < cs[:, None] + NA_KW)
    col_off = np.clip(c[None, :] - c[:, None] + NA_KW - 1, 0, 2 * NA_KW - 2)
    sel_c = (col_off[..., None] == np.arange(2 * NA_KW - 1)).astype(np.float32)
    t = jnp.einsum("lhij,qkj->lhiqk", rpb, sel_c, precision=lax.Precision.HIGHEST)
    t = jnp.where(col_valid, t, -jnp.inf)
    return jnp.concatenate([t[:, :, :-1], t[:, :, 1:]], axis=-1)


def _outproj_body(ap_ref, as_ref, w_ref, xa_ref, xb_ref, g_ref, mod_ref, wr_ref, br_ref, xo_ref, tok_ref, lg_ref,
                  *scratch):
    i = pl.program_id(0)
    a = jnp.where(i < N_PROMPT // TM, ap_ref[...], as_ref[...])
    x = _x_tile(xa_ref, xb_ref) + mod_ref[2:3, :] * _dot(a, _weight_ref(w_ref, scratch)[...])
    xo_ref[...] = x
    tok = _norm_mod(x, g_ref[...], mod_ref[3:4, :], mod_ref[4:5, :])
    tok_ref[...] = _pack_rows(tok)
    hi = tok.astype(BF16)
    lo = (tok - hi.astype(F32)).astype(BF16)
    wr = wr_ref[...]
    wr_hi = wr.astype(BF16)
    wr_lo = (wr - wr_hi.astype(F32)).astype(BF16)
    lg_ref[...] = _dot(hi, wr_hi) + _dot(lo, wr_hi) + _dot(hi, wr_lo) + br_ref[...]


def _outproj(a_prompt, a_sample, w, w_layer, x, g, mod, wr, br):
    k_in = a_prompt.shape[1]
    n_pt = N_PROMPT // TM
    x_specs, x_args = _x_inputs(x)
    w_spec, scratch = _weight_input(w, w_layer)
    return pl.pallas_call(
        _outproj_body,
        grid=(N_TOK // TM,),
        in_specs=[
            pl.BlockSpec((TM, k_in), lambda i: (jnp.minimum(i, n_pt - 1), 0)),
            pl.BlockSpec((TM, k_in), lambda i: (jnp.maximum(i - n_pt, 0), 0)),
            w_spec,
            *x_specs,
            pl.BlockSpec((1, D_MODEL), lambda i: (0, 0)),
            pl.BlockSpec((None, MOD_ROWS, D_MODEL), lambda i: (_group_of_tile(i, TM), 0, 0)),
            pl.BlockSpec((D_MODEL, LANES), lambda i: (0, 0)),
            pl.BlockSpec((1, LANES), lambda i: (0, 0)),
        ],
        out_specs=[
            pl.BlockSpec((TM, D_MODEL), lambda i: (i, 0)),
            pl.BlockSpec((TM, HALF), lambda i: (i, 0)),
            pl.BlockSpec((TM, LANES), lambda i: (i, 0)),
        ],
        out_shape=[
            jax.ShapeDtypeStruct((N_TOK, D_MODEL), F32),
            jax.ShapeDtypeStruct((N_TOK, HALF), U32),
            jax.ShapeDtypeStruct((N_TOK, LANES), F32),
        ],
        scratch_shapes=scratch,
        compiler_params=_cparams(1),
        name="outproj",
    )(a_prompt, a_sample, w, *x_args, g.reshape(1, D_MODEL), mod, wr, br)


def _router_body(lg_ref, idx_ref, gate_ref, rank_ref, cnt_ref, carry):
    i = pl.program_id(0)

    @pl.when(i == 0)
    def _():
        carry[...] = jnp.zeros_like(carry)

    lane = lax.broadcasted_iota(I32, (ROUTER_TM, LANES), 1)
    l = jnp.where(lane < N_EXPERTS, lg_ref[...], -jnp.inf)
    vals, idxs, hots = [], [], []
    for _ in range(TOP_K):
        m = l.max(axis=-1, keepdims=True)
        idx = jnp.where(l == m, lane, LANES).min(axis=-1, keepdims=True)
        hot = lane == idx
        l = jnp.where(hot, -jnp.inf, l)
        vals.append(m)
        idxs.append(idx)
        hots.append(hot)
    es = [jnp.exp(v - vals[0]) for v in vals]
    denom = es[0] + es[1] + es[2] + es[3]
    hot_all = (hots[0] | hots[1] | hots[2] | hots[3]).astype(F32)
    r = lax.broadcasted_iota(I32, (ROUTER_TM, ROUTER_TM), 0)
    c = lax.broadcasted_iota(I32, (ROUTER_TM, ROUTER_TM), 1)
    before = _dot((c < r).astype(BF16), hot_all.astype(BF16)) + carry[...]
    idx_out = jnp.zeros((ROUTER_TM, LANES), I32)
    gate_out = jnp.zeros((ROUTER_TM, LANES), F32)
    rank_out = jnp.zeros((ROUTER_TM, LANES), I32)
    for k in range(TOP_K):
        rank_k = jnp.where(hots[k], before, 0.0).sum(axis=-1, keepdims=True).astype(I32)
        idx_out = jnp.where(lane == k, idxs[k], idx_out)
        gate_out = jnp.where(lane == k, es[k] / denom, gate_out)
        rank_out = jnp.where(lane == k, rank_k, rank_out)
    idx_ref[...] = idx_out
    gate_ref[...] = gate_out
    rank_ref[...] = rank_out
    carry[...] = carry[...] + hot_all.sum(axis=0, keepdims=True)
    cnt_ref[...] = carry[...].astype(I32)


def _router(logits):
    spec = pl.BlockSpec((ROUTER_TM, LANES), lambda i: (i, 0))
    return pl.pallas_call(
        _router_body,
        grid=(N_TOK // ROUTER_TM,),
        in_specs=[spec],
        out_specs=[spec, spec, spec, pl.BlockSpec((1, LANES), lambda i: (0, 0))],
        out_shape=[
            jax.ShapeDtypeStruct((N_TOK, LANES), I32),
            jax.ShapeDtypeStruct((N_TOK, LANES), F32),
            jax.ShapeDtypeStruct((N_TOK, LANES), I32),
            jax.ShapeDtypeStruct((1, LANES), I32),
        ],
        scratch_shapes=[pltpu.VMEM((1, LANES), F32)],
        compiler_params=_cparams(1),
        name="router",
    )(logits)


SC_CORES = 2
SC_SUBCORES = 16
SC_WORKERS = SC_CORES * SC_SUBCORES
SC_CH = 32


def _sc_mesh():
    return plsc.VectorSubcoreMesh(core_axis_name="c", subcore_axis_name="s",
                                  num_cores=SC_CORES, num_subcores=SC_SUBCORES)


def _sc_worker():
    return lax.axis_index("s") * SC_CORES + lax.axis_index("c")


def _dispatch(tok, pos_km):
    n, d = tok.shape
    n_ch = n // (SC_WORKERS * SC_CH)
    assert n_ch * SC_WORKERS * SC_CH == n and n_ch % 2 == 0

    @functools.partial(
        pl.kernel, mesh=_sc_mesh(), out_type=jax.ShapeDtypeStruct((MOE_ROWS, d), tok.dtype),
        scratch_types=[pltpu.VMEM((n_ch, TOP_K, SC_CH), I32), pltpu.VMEM((2, SC_CH, d), tok.dtype),
                       pltpu.SemaphoreType.DMA((2,)), pltpu.SemaphoreType.DMA((2,))])
    def scatter_rows(tok_hbm, pos_hbm, out_hbm, idx_v, rows_v, load_sem, scat_sem):
        wid = _sc_worker()
        base = wid * (n_ch * SC_CH)
        pltpu.sync_copy(pos_hbm.at[wid], idx_v)

        def load(g, slot):
            rows = pl.ds(pl.multiple_of(base + g * SC_CH, 8), SC_CH)
            return pltpu.make_async_copy(tok_hbm.at[rows], rows_v.at[slot], load_sem.at[slot])

        def scatter(g, k, slot):
            return pltpu.make_async_copy(rows_v.at[slot], out_hbm.at[idx_v.at[g, k]], scat_sem.at[slot])

        load(0, 0).start()

        @pl.loop(0, n_ch, step=2)
        def _(g0):
            for slot in range(2):
                g = g0 + slot
                load(g, slot).wait()

                @pl.when(g + 1 < n_ch)
                def _():
                    load(g + 1, 1 - slot).start()

                for k in range(TOP_K):
                    scatter(g, k, slot).start()
                for k in range(TOP_K):
                    scatter(g, k, slot).wait()

    pos = pos_km.reshape(TOP_K, SC_WORKERS, n_ch, SC_CH).transpose(1, 2, 0, 3)
    return scatter_rows(tok, pos)


def _gather_rows(table, idx):
    b, d = idx.shape[0], table.shape[1]
    n_ch = b // (SC_WORKERS * SC_CH)
    assert n_ch * SC_WORKERS * SC_CH == b and n_ch % 2 == 0

    @functools.partial(
        pl.kernel, mesh=_sc_mesh(), out_type=jax.ShapeDtypeStruct((b, d), table.dtype),
        scratch_types=[pltpu.VMEM((n_ch, SC_CH), I32), pltpu.VMEM((2, SC_CH, d), table.dtype),
                       pltpu.SemaphoreType.DMA((2,))])
    def gather_rows(table_hbm, idx_hbm, out_hbm, idx_v, rows_v, sems):
        wid = _sc_worker()
        base = wid * (n_ch * SC_CH)
        pltpu.sync_copy(idx_hbm.at[wid], idx_v)

        def gather(g, slot):
            return pltpu.make_async_copy(table_hbm.at[idx_v.at[g]], rows_v.at[slot], sems.at[slot])

        def write_out(g, slot):
            rows = pl.ds(pl.multiple_of(base + g * SC_CH, 8), SC_CH)
            pltpu.sync_copy(rows_v.at[slot], out_hbm.at[rows])

        gather(0, 0).start()

        @pl.loop(0, n_ch, step=2)
        def _(g):
            gather(g + 1, 1).start()
            gather(g, 0).wait()
            write_out(g, 0)

            @pl.when(g + 2 < n_ch)
            def _():
                gather(g + 2, 0).start()

            gather(g + 1, 1).wait()
            write_out(g + 1, 1)

    return gather_rows(table, idx.reshape(SC_WORKERS, n_ch, SC_CH))


def _moe_body(te_ref, tv_ref, slot_ref, nxt_ref, x_ref, w1_hbm, b1_ref, w2_hbm, b2_ref, perm_ref, o_ref,
              w1_buf, w2_buf, w1_scr, w2_scr, sem, *, layer):
    t = pl.program_id(0)
    e = te_ref[t]
    slot = slot_ref[t]
    nxt = nxt_ref[t]
    first = jnp.logical_or(t == 0, e != te_ref[jnp.maximum(t - 1, 0)])

    def fetch(expert, s):
        return (pltpu.make_async_copy(w1_hbm.at[layer, expert], w1_buf.at[s], sem.at[s, 0]),
                pltpu.make_async_copy(w2_hbm.at[layer, expert], w2_buf.at[s], sem.at[s, 1]))

    @pl.when(t == 0)
    def _():
        for cp in fetch(e, slot):
            cp.start()

    @pl.when(jnp.logical_and(tv_ref[t] == 1, first))
    def _():
        for cp in fetch(e, slot):
            cp.wait()

        @pl.when(nxt != e)
        def _():
            for cp in fetch(nxt, 1 - slot):
                cp.start()

        w2_scr[...] = w2_buf[slot].astype(BF16)
        half = PERM_W // 2
        for c in range(2 * D_EXPERT // PERM_W):
            blk = _dot(w1_buf[slot, :, c * PERM_W:(c + 1) * PERM_W].astype(BF16), perm_ref[...]).astype(BF16)
            w1_scr[:, c * half:(c + 1) * half] = blk[:, :half]
            w1_scr[:, D_EXPERT + c * half:D_EXPERT + (c + 1) * half] = blk[:, half:]

    @pl.when(tv_ref[t] == 0)
    def _():
        o_ref[...] = jnp.zeros_like(o_ref)

    @pl.when(tv_ref[t] == 1)
    def _():
        x_lo, x_hi = _unpack_rows(x_ref[...])
        x = jnp.concatenate([x_lo.astype(BF16), x_hi.astype(BF16)], axis=-1)
        hh = _dot(x, w1_scr[...]) + b1_ref[...]
        h_glu = jnp.minimum(hh[:, :D_EXPERT], SWIGLU_LIMIT)
        h_lin = jnp.clip(hh[:, D_EXPERT:], -SWIGLU_LIMIT, SWIGLU_LIMIT)
        a = h_glu * _sigmoid(SWIGLU_ALPHA * h_glu) * (h_lin + 1.0)
        o_ref[...] = _pack_rows(_dot(a.astype(BF16), w2_scr[...]) + b2_ref[...])


def _deinterleave_bias_body(b_ref, perm_ref, o_ref):
    half = PERM_W // 2
    for c in range(2 * D_EXPERT // PERM_W):
        r = b_ref[:, c * PERM_W:(c + 1) * PERM_W]
        acc = jnp.zeros(r.shape, F32)
        for _ in range(3):
            piece = r.astype(BF16)
            acc = acc + _dot(piece, perm_ref[...])
            r = r - piece.astype(F32)
        o_ref[:, c * half:(c + 1) * half] = acc[:, :half]
        o_ref[:, D_EXPERT + c * half:D_EXPERT + (c + 1) * half] = acc[:, half:]


def _deinterleave_bias(b1, perm):
    return pl.pallas_call(
        _deinterleave_bias_body,
        out_shape=jax.ShapeDtypeStruct(b1.shape, F32),
        compiler_params=_cparams(0),
        name="deinterleave_bias",
    )(b1, perm)


def _moe_experts(tile_expert, tile_valid, tile_slot, tile_next, xs, w1, b1_split, w2, b2, perm, layer):
    grid_spec = pltpu.PrefetchScalarGridSpec(
        num_scalar_prefetch=4,
        grid=(MOE_TILES,),
        in_specs=[
            pl.BlockSpec((MOE_TM, HALF), lambda t, te, *_: (t, 0)),
            pl.BlockSpec(memory_space=pl.ANY),
            pl.BlockSpec((None, None, 1, 2 * D_EXPERT), lambda t, te, *_: (layer, te[t], 0, 0)),
            pl.BlockSpec(memory_space=pl.ANY),
            pl.BlockSpec((None, None, 1, D_MODEL), lambda t, te, *_: (layer, te[t], 0, 0)),
            pl.BlockSpec((PERM_W, PERM_W), lambda t, te, *_: (0, 0)),
        ],
        out_specs=pl.BlockSpec((MOE_TM, HALF), lambda t, te, *_: (t, 0)),
        scratch_shapes=[
            pltpu.VMEM((2, D_MODEL, 2 * D_EXPERT), F32),
            pltpu.VMEM((2, D_EXPERT, D_MODEL), F32),
            pltpu.VMEM((D_MODEL, 2 * D_EXPERT), BF16),
            pltpu.VMEM((D_EXPERT, D_MODEL), BF16),
            pltpu.SemaphoreType.DMA((2, 2)),
        ],
    )
    return pl.pallas_call(
        functools.partial(_moe_body, layer=layer),
        grid_spec=grid_spec,
        out_shape=jax.ShapeDtypeStruct((MOE_ROWS, HALF), U32),
        compiler_params=_cparams(1),
        name="moe_experts",
    )(tile_expert, tile_valid, tile_slot, tile_next, xs, w1, b1_split, w2, b2, perm)


COMBINE_TM = 256


def _combine_body(*refs, final):
    y_refs = refs[:TOP_K]
    gate_ref, x_ref, mod_ref, g_ref = refs[TOP_K:TOP_K + 4]
    gates = gate_ref[...]
    y_lo, y_hi = (gates[:, 0:1] * h for h in _unpack_rows(y_refs[0][...]))
    for k in range(1, TOP_K):
        lo, hi = _unpack_rows(y_refs[k][...])
        y_lo = y_lo + gates[:, k:k + 1] * lo
        y_hi = y_hi + gates[:, k:k + 1] * hi
    x = x_ref[...] + mod_ref[5:6, :] * jnp.concatenate([y_lo, y_hi], axis=-1)
    if not final:
        refs[-1][...] = x
        return
    x = (x * lax.rsqrt(jnp.mean(x * x, axis=-1, keepdims=True) + EPS)) * g_ref[...]
    prompt_ref, sample_ref = refs[-2:]
    is_prompt = pl.program_id(0) < N_PROMPT // COMBINE_TM

    @pl.when(is_prompt)
    def _():
        prompt_ref[...] = x

    @pl.when(jnp.logical_not(is_prompt))
    def _():
        sample_ref[...] = x


def _combine(y_km, gates, x, mod, g_final, final):
    n_t = N_TOK // COMBINE_TM
    n_pt = N_PROMPT // COMBINE_TM
    y_specs = [pl.BlockSpec((COMBINE_TM, HALF), lambda i, k=k: (k * n_t + i, 0)) for k in range(TOP_K)]
    if final:
        out_specs = [pl.BlockSpec((COMBINE_TM, D_MODEL), lambda i: (jnp.minimum(i, n_pt - 1), 0)),
                     pl.BlockSpec((COMBINE_TM, D_MODEL), lambda i: (jnp.maximum(i - n_pt, 0), 0))]
        out_shape = [jax.ShapeDtypeStruct((N_PROMPT, D_MODEL), F32), jax.ShapeDtypeStruct((N_SAMPLE, D_MODEL), F32)]
    else:
        out_specs = pl.BlockSpec((COMBINE_TM, D_MODEL), lambda i: (i, 0))
        out_shape = jax.ShapeDtypeStruct((N_TOK, D_MODEL), F32)
    return pl.pallas_call(
        functools.partial(_combine_body, final=final),
        grid=(n_t,),
        in_specs=y_specs + [
            pl.BlockSpec((COMBINE_TM, LANES), lambda i: (i, 0)),
            pl.BlockSpec((COMBINE_TM, D_MODEL), lambda i: (i, 0)),
            pl.BlockSpec((None, MOD_ROWS, D_MODEL), lambda i: (_group_of_tile(i, COMBINE_TM), 0, 0)),
            pl.BlockSpec((1, D_MODEL), lambda i: (0, 0)),
        ],
        out_specs=out_specs,
        out_shape=out_shape,
        compiler_params=_cparams(1),
        name="combine",
    )(*([y_km] * TOP_K), gates, x, mod, g_final.reshape(1, D_MODEL))


def _routing_tables(idx, rank, counts):
    counts = counts[0, :N_EXPERTS]
    padded = ((counts + MOE_TM - 1) // MOE_TM) * MOE_TM
    ends = jnp.cumsum(padded)
    offsets = ends - padded
    idx = idx[:, :TOP_K]
    pos = rank[:, :TOP_K] + jnp.sum(
        jnp.where(idx[..., None] == jnp.arange(N_EXPERTS, dtype=I32), offsets, 0), axis=-1)
    pos_km = pos.T.astype(I32)
    tile_start = jnp.arange(MOE_TILES, dtype=I32) * MOE_TM
    tile_valid = (tile_start < ends[-1]).astype(I32)
    tile_expert = jnp.sum((tile_start[:, None] >= ends[None, :]).astype(I32), axis=-1)
    last_expert = jnp.sum((ends[-1] - 1 >= ends).astype(I32))
    tile_expert = jnp.where(tile_valid == 1, tile_expert, last_expert).astype(I32)
    e_ids = jnp.arange(N_EXPERTS, dtype=I32)
    owns = counts > 0
    ordinal = jnp.cumsum(owns.astype(I32)) - 1
    later = jnp.where(owns[None, :] & (e_ids[None, :] > e_ids[:, None]), e_ids[None, :], N_EXPERTS)
    nxt = jnp.min(later, axis=1)
    nxt = jnp.where(nxt == N_EXPERTS, e_ids, nxt)
    onehot = (tile_expert[:, None] == e_ids[None, :]).astype(I32)
    tile_slot = jnp.sum(onehot * (ordinal % 2)[None, :], axis=1).astype(I32)
    tile_next = jnp.sum(onehot * nxt[None, :], axis=1).astype(I32)
    return pos_km, tile_expert, tile_valid, tile_slot, tile_next


def _rope_tables():
    t = np.arange(DEC_SEQ)
    row = (t // GRID_W).astype(np.float32)
    col = (t % GRID_W).astype(np.float32)
    d_axis = RET_DK // 2
    inv = jnp.asarray(ROPE_BASE, F32) ** (-jnp.arange(0, d_axis, 2, dtype=F32) / d_axis)
    ang = jnp.concatenate([row[:, None] * inv, col[:, None] * inv], axis=-1)
    cos = jnp.repeat(jnp.cos(ang), 2, axis=-1)
    sin = jnp.repeat(jnp.sin(ang), 2, axis=-1)
    sign = jnp.asarray(np.tile(np.array([-1.0, 1.0], np.float32), RET_DK // 2))
    return cos, sin * sign


def _deinterleave_matrix():
    p = np.zeros((PERM_W, PERM_W), np.float32)
    half = PERM_W // 2
    for j in range(half):
        p[2 * j, j] = 1.0
        p[2 * j + 1, half + j] = 1.0
    return jnp.asarray(p, BF16)


def kernel(x_prompt, x_sample, state_ret_fwd, state_ret_bwd, cache_na_k, cache_na_v, c, c_ctx, ada_w, ada_b, norm_mix, norm_ffn, norm_final, ret_w_in, ret_w_out, ret_decay_fwd, ret_decay_bwd, na_w_in, na_w_out, na_rpb, moe_w_router, moe_b_router, moe_w1, moe_b1, moe_w2, moe_b2):
    x = (x_prompt.reshape(N_PROMPT, D_MODEL), x_sample.reshape(N_SAMPLE, D_MODEL))

    cond = jnp.zeros((MOD_ROWS, D_MODEL), F32).at[0].set(c_ctx).at[1:1 + DEC_BATCH].set(c)
    mod_all = _adaln(cond, ada_w, ada_b)
    mod_all = mod_all[:, :N_GROUPS].reshape(DEPTH, N_GROUPS, 6, D_MODEL)
    mod_all = jnp.pad(mod_all, ((0, 0), (0, 0), (0, MOD_ROWS - 6), (0, 0)))

    cos, sin_signed = _rope_tables()
    perm = _deinterleave_matrix()
    wr_all = jnp.pad(moe_w_router, ((0, 0), (0, 0), (0, LANES - N_EXPERTS)))
    br_all = jnp.pad(moe_b_router, ((0, 0), (0, LANES - N_EXPERTS))).reshape(DEPTH, 1, LANES)
    b1_split = _deinterleave_bias(moe_b1.reshape(DEPTH * N_EXPERTS, 2 * D_EXPERT), perm)
    b1_split = b1_split.reshape(DEPTH, N_EXPERTS, 1, 2 * D_EXPERT)
    b2_all = moe_b2.reshape(DEPTH, N_EXPERTS, 1, D_MODEL)
    na_tbl = _na_bias_table(na_rpb)
    ret_states = None
    na_caches = None
    for layer in range(DEPTH):
        mod = mod_all[layer]
        j = layer // 2
        if layer % 2 == 0:
            proj = _inproj(x, norm_mix[layer], mod, _cast_layer_bf16(ret_w_in, j), out_dtype=BF16)
            decay = jnp.stack([ret_decay_fwd[j], ret_decay_bwd[j]])
            a_p, s_f, s_b = _ret_prompt(proj, decay, j, ret_states)
            ret_states = (s_f, s_b)
            a_s = _ret_sample(proj, decay, cos, sin_signed, state_ret_fwd, state_ret_bwd, j)
            w_out = ret_w_out
        else:
            proj, ck, cv = _inproj(x, norm_mix[layer], mod, na_w_in, j, j, na_caches)
            na_caches = (ck, cv)
            a_p = _ctx_attn(proj)
            a_s = _na_attn(proj, cache_na_k, cache_na_v, na_tbl, j)
            w_out = na_w_out

        x, tok, logits = _outproj(a_p, a_s, w_out, j, x, norm_ffn[layer], mod, wr_all[layer], br_all[layer])

        idx, gates, rank, counts = _router(logits)
        pos_km, tile_expert, tile_valid, tile_slot, tile_next = _routing_tables(idx, rank, counts)
        xs = _dispatch(tok, pos_km)
        ys = _moe_experts(tile_expert, tile_valid, tile_slot, tile_next, xs, moe_w1, b1_split, moe_w2, b2_all,
                          perm, layer)
        y_km = _gather_rows(ys, pos_km.reshape(-1))
        x = _combine(y_km, gates, x, mod, norm_final, final=(layer == DEPTH - 1))

    y_prompt = x[0].reshape(BATCH, SEQ, D_MODEL)
    y_sample = x[1].reshape(DEC_BATCH, DEC_SEQ, D_MODEL)
    return (y_prompt, y_sample, ret_states[0], ret_states[1],
            na_caches[0].reshape(BATCH, N_NA_LAYERS, SEQ, NA_HEADS, NA_HD),
            na_caches[1].reshape(BATCH, N_NA_LAYERS, SEQ, NA_HEADS, NA_HD))
```

```python
import functools

import numpy as np
import jax
import jax.numpy as jnp
from jax import lax
from jax.experimental import pallas as pl
from jax.experimental.pallas import tpu as pltpu
from jax.experimental.pallas import tpu_sc as plsc

F32 = jnp.float32
BF16 = jnp.bfloat16
I32 = jnp.int32
U32 = jnp.uint32

D_MODEL = 1024
BATCH = 32
SEQ = 256
DEPTH = 4
DEC_BATCH = 2
DEC_SEQ = 1024
PAST_LEN = 512
GRID_W = 64
EPS = 1e-6
ROPE_BASE = 10000.0
RET_HEADS = 4
RET_DK = D_MODEL // RET_HEADS
RET_DV = 2 * RET_DK
RET_QK_W = RET_HEADS * RET_DK
RET_V_W = RET_HEADS * RET_DV
NA_HEADS = 16
NA_HD = D_MODEL // NA_HEADS
NA_KH = 8
NA_KW = 16
N_EXPERTS = 32
TOP_K = 4
D_EXPERT = D_MODEL
SWIGLU_LIMIT = 7.0
SWIGLU_ALPHA = 1.702
N_RET_LAYERS = (DEPTH + 1) // 2
N_NA_LAYERS = DEPTH // 2

N_PROMPT = BATCH * SEQ
N_SAMPLE = DEC_BATCH * DEC_SEQ
N_TOK = N_PROMPT + N_SAMPLE
N_GROUPS = 1 + DEC_BATCH
MOD_ROWS = 8
LANES = 128

TM = 256
RET_CHUNK = 256
MOE_TM = 256
MOE_ROWS = N_TOK * TOP_K + N_EXPERTS * MOE_TM
MOE_TILES = MOE_ROWS // MOE_TM
ROUTER_TM = 512
PERM_W = 256
VMEM_LIMIT = 56 * 1024 * 1024


def _cparams(n_axes, vmem=VMEM_LIMIT):
    return pltpu.CompilerParams(dimension_semantics=("arbitrary",) * n_axes, vmem_limit_bytes=vmem)


def _group_of_tile(i, tm):
    return jnp.maximum((i * tm) // DEC_SEQ - (N_PROMPT // DEC_SEQ - 1), 0)


def _sigmoid(x):
    return 1.0 / (1.0 + jnp.exp(-x))


def _silu(x):
    return x * _sigmoid(x)


def _log_sigmoid(x):
    return jnp.minimum(x, 0.0) - jnp.log(1.0 + jnp.exp(-jnp.abs(x)))


def _norm_mod(x, g, shift, scale):
    y = x * lax.rsqrt(jnp.mean(x * x, axis=-1, keepdims=True) + EPS)
    return (y * g) * (1.0 + scale) + shift


def _dot(a, b):
    return jnp.dot(a, b, preferred_element_type=F32)


HALF = D_MODEL // 2
HI_MASK = 0xFFFF0000


def _pack_rows(x):
    lo = pltpu.bitcast(x[:, :HALF].astype(BF16).astype(F32), U32)
    hi = pltpu.bitcast(x[:, HALF:].astype(BF16).astype(F32), U32)
    return (lo >> 16) | (hi & jnp.uint32(HI_MASK))


def _unpack_rows(p):
    return pltpu.bitcast(p << 16, F32), pltpu.bitcast(p & jnp.uint32(HI_MASK), F32)


def _dot_nt(a, b):
    return lax.dot_general(a, b, (((1,), (1,)), ((), ())), preferred_element_type=F32)


def _dot_tn(a, b):
    return lax.dot_general(a, b, (((0,), (0,)), ((), ())), preferred_element_type=F32)


ADA_TN = 1536


def _adaln_body(cond_ref, w_ref, b_ref, o_ref):
    s = _silu(cond_ref[...]).astype(BF16)
    o_ref[...] = _dot(s, w_ref[...].astype(BF16)) + b_ref[...]


def _adaln(cond, ada_w, ada_b):
    n_out = 6 * D_MODEL
    return pl.pallas_call(
        _adaln_body,
        grid=(DEPTH, n_out // ADA_TN),
        in_specs=[
            pl.BlockSpec((MOD_ROWS, D_MODEL), lambda l, j: (0, 0)),
            pl.BlockSpec((None, D_MODEL, ADA_TN), lambda l, j: (l, 0, j)),
            pl.BlockSpec((None, 1, ADA_TN), lambda l, j: (l, 0, j)),
        ],
        out_specs=pl.BlockSpec((None, MOD_ROWS, ADA_TN), lambda l, j: (l, 0, j)),
        out_shape=jax.ShapeDtypeStruct((DEPTH, MOD_ROWS, n_out), F32),
        compiler_params=_cparams(2),
        name="adaln",
    )(cond, ada_w, ada_b.reshape(DEPTH, 1, n_out))


CAST_TR = 256


def _cast_body(w_ref, o_ref):
    o_ref[...] = w_ref[...].astype(BF16)


def _cast_layer_bf16(w, j):
    _, r, c = w.shape
    return pl.pallas_call(
        _cast_body,
        grid=(r // CAST_TR,),
        in_specs=[pl.BlockSpec((None, CAST_TR, c), lambda i: (j, i, 0))],
        out_specs=pl.BlockSpec((CAST_TR, c), lambda i: (i, 0)),
        out_shape=jax.ShapeDtypeStruct((r, c), BF16),
        compiler_params=_cparams(1),
        name="cast_bf16",
    )(w)


def _x_inputs(x):
    n_pt = N_PROMPT // TM
    xa, xb, off = (x[0], x[1], 0) if isinstance(x, tuple) else (x, x, n_pt)
    specs = [pl.BlockSpec((TM, D_MODEL), lambda i: (jnp.minimum(i, n_pt - 1), 0)),
             pl.BlockSpec((TM, D_MODEL), lambda i: (off + jnp.maximum(i - n_pt, 0), 0))]
    return specs, [xa, xb]


def _x_tile(xa_ref, xb_ref):
    return jnp.where(pl.program_id(0) < N_PROMPT // TM, xa_ref[...], xb_ref[...])


def _weight_input(w, layer):
    if layer is None:
        return pl.BlockSpec(w.shape, lambda i: (0, 0)), []
    _, r, c = w.shape
    spec = pl.BlockSpec((None, r, c), lambda i: (layer, 0, 0), pipeline_mode=pl.Buffered(1))
    return spec, [pltpu.VMEM((r, c), BF16)]


def _weight_ref(w_ref, scratch):
    if not scratch:
        return w_ref

    @pl.when(pl.program_id(0) == 0)
    def _():
        scratch[0][...] = w_ref[...].astype(BF16)

    return scratch[0]


def _inproj_body(xa_ref, xb_ref, g_ref, mod_ref, w_ref, *rest, cache_layer, n_scratch):
    scratch = rest[len(rest) - n_scratch:]
    rest = rest[:len(rest) - n_scratch]
    o_ref = rest[-3] if cache_layer is not None else rest[-1]
    i = pl.program_id(0)
    w = _weight_ref(w_ref, scratch)
    h = _norm_mod(_x_tile(xa_ref, xb_ref), g_ref[...], mod_ref[0:1, :], mod_ref[1:2, :]).astype(BF16)
    for c in range(o_ref.shape[1] // D_MODEL):
        sl = slice(c * D_MODEL, (c + 1) * D_MODEL)
        r = _dot(h, w[:, sl])
        o_ref[:, sl] = r.astype(o_ref.dtype)
        if cache_layer is not None and c >= 1:
            cache_ref = rest[-3 + c]

            @pl.when(i < BATCH)
            def _(r=r, cache_ref=cache_ref):
                cache_ref[...] = r.T.reshape(cache_ref.shape)


def _inproj(x, g, mod, w, w_layer=None, cache_layer=None, prev_caches=None, out_dtype=F32):
    n_out = w.shape[-1]
    x_specs, x_args = _x_inputs(x)
    w_spec, scratch = _weight_input(w, w_layer)
    in_specs = x_specs + [
        pl.BlockSpec((1, D_MODEL), lambda i: (0, 0)),
        pl.BlockSpec((None, MOD_ROWS, D_MODEL), lambda i: (_group_of_tile(i, TM), 0, 0)),
        w_spec,
    ]
    out_specs = [pl.BlockSpec((TM, n_out), lambda i: (i, 0))]
    out_shape = [jax.ShapeDtypeStruct((N_TOK, n_out), out_dtype)]
    args = x_args + [g.reshape(1, D_MODEL), mod, w]
    aliases = {}
    first_cache = prev_caches is None
    if cache_layer is not None:
        assert TM == SEQ and n_out == 3 * D_MODEL
        cache = jax.ShapeDtypeStruct((BATCH, N_NA_LAYERS, NA_HEADS, NA_HD, SEQ), F32)
        spec = pl.BlockSpec((None, None, NA_HEADS, NA_HD, SEQ),
                            lambda i: (jnp.minimum(i, BATCH - 1), cache_layer, 0, 0, 0))
        if not first_cache:
            in_specs += [pl.BlockSpec(memory_space=pl.ANY)] * 2
            aliases = {len(args): 1, len(args) + 1: 2}
            args += list(prev_caches)
        out_specs += [spec, spec]
        out_shape += [cache, cache]
    outs = pl.pallas_call(
        functools.partial(_inproj_body, cache_layer=cache_layer, n_scratch=len(scratch)),
        grid=(N_TOK // TM,),
        in_specs=in_specs,
        out_specs=out_specs,
        out_shape=out_shape,
        scratch_shapes=scratch,
        input_output_aliases=aliases,
        compiler_params=_cparams(1),
        name="inproj",
    )(*args)
    return outs[0] if cache_layer is None else outs


def _decay_terms(dec_ref, h, c):
    lgf = _log_sigmoid(jnp.zeros((c, c), F32) + dec_ref[0, h])
    lgb = _log_sigmoid(jnp.zeros((c, c), F32) + dec_ref[1, h])
    row = lax.broadcasted_iota(I32, (c, c), 0)
    col = lax.broadcasted_iota(I32, (c, c), 1)
    diff = (row - col).astype(F32)
    lower = diff >= 0
    upper = diff <= 0
    mask = (jnp.where(lower, jnp.exp(jnp.where(lower, diff, 0.0) * lgf), 0.0)
            + jnp.where(upper, jnp.exp(jnp.where(upper, -diff, 0.0) * lgb), 0.0))
    pos = lax.broadcasted_iota(I32, (c, 1), 0).astype(F32)
    lgf1 = _log_sigmoid(jnp.zeros((c, 1), F32) + dec_ref[0, h])
    lgb1 = _log_sigmoid(jnp.zeros((c, 1), F32) + dec_ref[1, h])
    lgf0 = _log_sigmoid(jnp.zeros((1, 1), F32) + dec_ref[0, h])
    lgb0 = _log_sigmoid(jnp.zeros((1, 1), F32) + dec_ref[1, h])
    terms = dict(
        mask=mask,
        q_f=jnp.exp((pos + 1.0) * lgf1), k_f=jnp.exp((c - 1.0 - pos) * lgf1), c_f=jnp.exp(c * lgf0),
        q_b=jnp.exp((c - pos) * lgb1), k_b=jnp.exp(pos * lgb1), c_b=jnp.exp(c * lgb0),
    )
    return terms


def _group_norm_gate(o, g):
    o = o * lax.rsqrt(jnp.mean(o * o, axis=-1, keepdims=True) + EPS)
    return (_silu(g) * o).astype(BF16)


def _ret_prompt_body(dec_ref, q_ref, k_ref, v_ref, g_ref, *rest):
    o_ref, sf_ref, sb_ref = rest[-3:]
    h = pl.program_id(1)
    t = _decay_terms(dec_ref, h, SEQ)
    q = q_ref[...].astype(F32)
    k = k_ref[...].astype(F32) * (RET_DK ** -0.5)
    v = v_ref[...].astype(BF16)
    scores = _dot_nt(q.astype(BF16), k.astype(BF16)) * t["mask"]
    o = _dot(scores.astype(BF16), v)
    o_ref[...] = _group_norm_gate(o, g_ref[...].astype(F32))
    for ref, k_dec in ((sf_ref, t["k_f"]), (sb_ref, t["k_b"])):
        ref[...] = _dot_tn((k * k_dec).astype(BF16), v)


def _ret_prompt(proj, decay, layer, prev_states=None):
    kq = RET_QK_W // RET_DK
    kv = 2 * RET_QK_W // RET_DV
    kg = kv + RET_HEADS
    state = jax.ShapeDtypeStruct((BATCH, N_RET_LAYERS, RET_HEADS, RET_DK, RET_DV), F32)
    in_specs = [
        pl.BlockSpec(memory_space=pltpu.SMEM),
        pl.BlockSpec((SEQ, RET_DK), lambda b, h: (b, h)),
        pl.BlockSpec((SEQ, RET_DK), lambda b, h: (b, kq + h)),
        pl.BlockSpec((SEQ, RET_DV), lambda b, h: (b, kv + h)),
        pl.BlockSpec((SEQ, RET_DV), lambda b, h: (b, kg + h)),
    ]
    args = [decay, proj, proj, proj, proj]
    first = prev_states is None
    aliases = {}
    s_spec = pl.BlockSpec((None, None, None, RET_DK, RET_DV), lambda b, h: (b, layer, h, 0, 0))
    if not first:
        in_specs += [pl.BlockSpec(memory_space=pl.ANY)] * 2
        aliases = {len(args): 1, len(args) + 1: 2}
        args += list(prev_states)
    return pl.pallas_call(
        _ret_prompt_body,
        grid=(BATCH, RET_HEADS),
        in_specs=in_specs,
        out_specs=[pl.BlockSpec((SEQ, RET_DV), lambda b, h: (b, h)), s_spec, s_spec],
        out_shape=[jax.ShapeDtypeStruct((N_PROMPT, RET_V_W), BF16), state, state],
        input_output_aliases=aliases,
        compiler_params=_cparams(2),
        name="ret_prompt",
    )(*args)


def _rope(x, cos, sin_signed):
    n = x.shape[-1]
    lane = lax.broadcasted_iota(I32, x.shape, 1)
    swapped = jnp.where(lane % 2 == 0, pltpu.roll(x, n - 1, 1), pltpu.roll(x, 1, 1))
    return x * cos + swapped * sin_signed


def _ret_sample_body(dec_ref, q_ref, k_ref, v_ref, g_ref, cos_ref, sin_ref, s0f_ref, s0b_ref, o_ref,
                     sf_scr, sb_scr):
    h = pl.program_id(1)
    c = RET_CHUNK
    n_chunks = DEC_SEQ // c
    t = _decay_terms(dec_ref, h, c)

    def chunk(ref, i):
        return ref[i * c:(i + 1) * c, :].astype(F32)

    def qk(i):
        q = _rope(chunk(q_ref, i), chunk(cos_ref, i), chunk(sin_ref, i))
        k = _rope(chunk(k_ref, i) * (RET_DK ** -0.5), chunk(cos_ref, i), chunk(sin_ref, i))
        return q, k

    s = s0f_ref[...]
    for i in range(n_chunks):
        sf_scr[i] = s
        if i + 1 < n_chunks:
            _, k = qk(i)
            s = s * t["c_f"] + _dot_tn((k * t["k_f"]).astype(BF16), chunk(v_ref, i).astype(BF16))
    s = s0b_ref[...]
    for i in reversed(range(n_chunks)):
        sb_scr[i] = s
        if i > 0:
            _, k = qk(i)
            s = s * t["c_b"] + _dot_tn((k * t["k_b"]).astype(BF16), chunk(v_ref, i).astype(BF16))

    for i in range(n_chunks):
        q, k = qk(i)
        scores = _dot_nt(q.astype(BF16), k.astype(BF16)) * t["mask"]
        o = _dot(scores.astype(BF16), chunk(v_ref, i).astype(BF16))
        o = o + _dot((q * t["q_f"]).astype(BF16), sf_scr[i].astype(BF16))
        o = o + _dot((q * t["q_b"]).astype(BF16), sb_scr[i].astype(BF16))
        o_ref[i * c:(i + 1) * c, :] = _group_norm_gate(o, chunk(g_ref, i))


def _ret_sample(proj, decay, cos, sin_signed, s0f, s0b, j):
    kq = RET_QK_W // RET_DK
    kv = 2 * RET_QK_W // RET_DV
    kg = kv + RET_HEADS
    rb = N_PROMPT // DEC_SEQ
    n_chunks = DEC_SEQ // RET_CHUNK
    return pl.pallas_call(
        _ret_sample_body,
        grid=(DEC_BATCH, RET_HEADS),
        in_specs=[
            pl.BlockSpec(memory_space=pltpu.SMEM),
            pl.BlockSpec((DEC_SEQ, RET_DK), lambda b, h: (rb + b, h)),
            pl.BlockSpec((DEC_SEQ, RET_DK), lambda b, h: (rb + b, kq + h)),
            pl.BlockSpec((DEC_SEQ, RET_DV), lambda b, h: (rb + b, kv + h)),
            pl.BlockSpec((DEC_SEQ, RET_DV), lambda b, h: (rb + b, kg + h)),
            pl.BlockSpec((DEC_SEQ, RET_DK), lambda b, h: (0, 0)),
            pl.BlockSpec((DEC_SEQ, RET_DK), lambda b, h: (0, 0)),
            pl.BlockSpec((None, None, None, RET_DK, RET_DV), lambda b, h: (b, j, h, 0, 0)),
            pl.BlockSpec((None, None, None, RET_DK, RET_DV), lambda b, h: (b, j, h, 0, 0)),
        ],
        out_specs=pl.BlockSpec((DEC_SEQ, RET_DV), lambda b, h: (b, h)),
        out_shape=jax.ShapeDtypeStruct((N_SAMPLE, RET_V_W), BF16),
        scratch_shapes=[pltpu.VMEM((n_chunks, RET_DK, RET_DV), F32),
                        pltpu.VMEM((n_chunks, RET_DK, RET_DV), F32)],
        compiler_params=_cparams(2),
        name="ret_sample",
    )(decay, proj, proj, proj, proj, cos, sin_signed, s0f, s0b)


NA_HPB = LANES // NA_HD


def _head_lanes(rows, hh):
    lane = lax.broadcasted_iota(I32, (rows, LANES), 1)
    return jnp.logical_and(lane >= hh * NA_HD, lane < (hh + 1) * NA_HD)


def _with_ones(v):
    return jnp.concatenate([v, jnp.ones(v.shape, v.dtype)], axis=-1)


def _ctx_attn_body(q_ref, k_ref, v_ref, o_ref):
    outs = []
    for p in range(NA_HEADS // NA_HPB):
        sl = slice(p * LANES, (p + 1) * LANES)
        q2 = q_ref[:, sl] * (NA_HD ** -0.5)
        k2 = k_ref[:, sl].astype(BF16)
        v2 = _with_ones(v_ref[:, sl].astype(BF16))
        o_pair = None
        for hh in range(NA_HPB):
            mine = _head_lanes(SEQ, hh)
            s = _dot_nt(jnp.where(mine, q2, 0.0).astype(BF16), k2)
            e = jnp.exp(s - s.max(axis=-1, keepdims=True)).astype(BF16)
            acc = _dot(e, v2)
            o_h = acc[:, :LANES] / acc[:, LANES:]
            o_pair = o_h if o_pair is None else jnp.where(mine, o_h, o_pair)
        outs.append(o_pair)
    o_ref[...] = jnp.concatenate(outs, axis=-1).astype(BF16)


def _ctx_attn(proj):
    return pl.pallas_call(
        _ctx_attn_body,
        grid=(BATCH,),
        in_specs=[
            pl.BlockSpec((SEQ, D_MODEL), lambda b: (b, 0)),
            pl.BlockSpec((SEQ, D_MODEL), lambda b: (b, 1)),
            pl.BlockSpec((SEQ, D_MODEL), lambda b: (b, 2)),
        ],
        out_specs=pl.BlockSpec((SEQ, D_MODEL), lambda b: (b, 0)),
        out_shape=jax.ShapeDtypeStruct((N_PROMPT, D_MODEL), BF16),
        compiler_params=_cparams(1),
        name="ctx_attn",
    )(proj, proj, proj)


NA_QT = 256
NA_WIN = NA_KH * GRID_W


def _na_attn_body(q_ref, k_ref, v_ref, ck_ref, cv_ref, tbl_ref, o_ref):
    t = pl.program_id(1)
    rows_per_tile = NA_QT // GRID_W
    n_rows = DEC_SEQ // GRID_W
    q2 = q_ref[...] * (NA_HD ** -0.5)
    q = [jnp.where(_head_lanes(NA_QT, hh), q2, 0.0).astype(BF16) for hh in range(NA_HPB)]
    ck = ck_ref[...].astype(BF16)
    s_ctx = [_dot_nt(q[hh], ck) for hh in range(NA_HPB)]
    acc_win = [[] for _ in range(NA_HPB)]
    e_ctx = [[] for _ in range(NA_HPB)]
    for u in range(rows_per_tile):
        qr = t * rows_per_tile + u
        rs = jnp.clip(qr - NA_KH // 2, 0, n_rows - NA_KH)
        i0 = rs - qr + (NA_KH - 1)
        start = pl.multiple_of(rs * GRID_W, GRID_W)
        kw = k_ref[pl.ds(start, NA_WIN), :].astype(BF16)
        vw = _with_ones(v_ref[pl.ds(start, NA_WIN), :].astype(BF16))
        rows = slice(u * GRID_W, (u + 1) * GRID_W)
        for hh in range(NA_HPB):
            bias = jnp.concatenate([tbl_ref[hh, i0 + 2 * m] for m in range(NA_KH // 2)], axis=-1)
            sw = _dot_nt(q[hh][rows], kw) + bias
            sc = s_ctx[hh][rows]
            m = jnp.maximum(sw.max(axis=-1, keepdims=True), sc.max(axis=-1, keepdims=True))
            acc_win[hh].append(_dot(jnp.exp(sw - m).astype(BF16), vw))
            e_ctx[hh].append(jnp.exp(sc - m).astype(BF16))
    cv = _with_ones(cv_ref[...].astype(BF16))
    out = None
    for hh in range(NA_HPB):
        acc = jnp.concatenate(acc_win[hh], axis=0) + _dot(jnp.concatenate(e_ctx[hh], axis=0), cv)
        o_h = acc[:, :LANES] / acc[:, LANES:]
        out = o_h if out is None else jnp.where(_head_lanes(NA_QT, hh), o_h, out)
    o_ref[...] = out.astype(BF16)


def _na_attn(proj, cache_k, cache_v, tbl, j):
    n_qt = DEC_SEQ // NA_QT
    rb_q = N_PROMPT // NA_QT
    rb_kv = N_PROMPT // DEC_SEQ
    cb = D_MODEL // LANES
    ck = cache_k.reshape(DEC_BATCH, -1, PAST_LEN, D_MODEL)
    cv = cache_v.reshape(DEC_BATCH, -1, PAST_LEN, D_MODEL)
    return pl.pallas_call(
        _na_attn_body,
        grid=(cb, n_qt, DEC_BATCH),
        in_specs=[
            pl.BlockSpec((NA_QT, LANES), lambda p, t, b: (rb_q + b * n_qt + t, p)),
            pl.BlockSpec((DEC_SEQ, LANES), lambda p, t, b: (rb_kv + b, cb + p)),
            pl.BlockSpec((DEC_SEQ, LANES), lambda p, t, b: (rb_kv + b, 2 * cb + p)),
            pl.BlockSpec((None, None, PAST_LEN, LANES), lambda p, t, b: (b, j, 0, p)),
            pl.BlockSpec((None, None, PAST_LEN, LANES), lambda p, t, b: (b, j, 0, p)),
            pl.BlockSpec((None, NA_HPB, 2 * NA_KH - 2, GRID_W, 2 * GRID_W), lambda p, t, b: (j, p, 0, 0, 0)),
        ],
        out_specs=pl.BlockSpec((NA_QT, LANES), lambda p, t, b: (b * n_qt + t, p)),
        out_shape=jax.ShapeDtypeStruct((N_SAMPLE, D_MODEL), BF16),
        compiler_params=_cparams(3),
        name="na_attn",
    )(proj, proj, proj, ck, cv, tbl)


def _na_bias_table(rpb):
    c = np.arange(GRID_W)
    cs = np.clip(c - NA_KW // 2, 0, GRID_W - NA_KW)
    col_valid = (c[None, :] >= cs[:, None]) & (c[None, :] < cs[:, None] + NA_KW)
    col_off = np.clip(c[None, :] - c[:, None] + NA_KW - 1, 0, 2 * NA_KW - 2)
    sel_c = (col_off[..., None] == np.arange(2 * NA_KW - 1)).astype(np.float32)
    t = jnp.einsum("lhij,qkj->lhiqk", rpb, sel_c, precision=lax.Precision.HIGHEST)
    t = jnp.where(col_valid, t, -jnp.inf)
    return jnp.concatenate([t[:, :, :-1], t[:, :, 1:]], axis=-1)


def _outproj_body(ap_ref, as_ref, w_ref, xa_ref, xb_ref, g_ref, mod_ref, wr_ref, br_ref, xo_ref, tok_ref, lg_ref,
                  *scratch):
    i = pl.program_id(0)
    a = jnp.where(i < N_PROMPT // TM, ap_ref[...], as_ref[...])
    x = _x_tile(xa_ref, xb_ref) + mod_ref[2:3, :] * _dot(a, _weight_ref(w_ref, scratch)[...])
    xo_ref[...] = x
    tok = _norm_mod(x, g_ref[...], mod_ref[3:4, :], mod_ref[4:5, :])
    tok_ref[...] = _pack_rows(tok)
    hi = tok.astype(BF16)
    lo = (tok - hi.astype(F32)).astype(BF16)
    wr = wr_ref[...]
    wr_hi = wr.astype(BF16)
    wr_lo = (wr - wr_hi.astype(F32)).astype(BF16)
    lg_ref[...] = _dot(hi, wr_hi) + _dot(lo, wr_hi) + _dot(hi, wr_lo) + br_ref[...]


def _outproj(a_prompt, a_sample, w, w_layer, x, g, mod, wr, br):
    k_in = a_prompt.shape[1]
    n_pt = N_PROMPT // TM
    x_specs, x_args = _x_inputs(x)
    w_spec, scratch = _weight_input(w, w_layer)
    return pl.pallas_call(
        _outproj_body,
        grid=(N_TOK // TM,),
        in_specs=[
            pl.BlockSpec((TM, k_in), lambda i: (jnp.minimum(i, n_pt - 1), 0)),
            pl.BlockSpec((TM, k_in), lambda i: (jnp.maximum(i - n_pt, 0), 0)),
            w_spec,
            *x_specs,
            pl.BlockSpec((1, D_MODEL), lambda i: (0, 0)),
            pl.BlockSpec((None, MOD_ROWS, D_MODEL), lambda i: (_group_of_tile(i, TM), 0, 0)),
            pl.BlockSpec((D_MODEL, LANES), lambda i: (0, 0)),
            pl.BlockSpec((1, LANES), lambda i: (0, 0)),
        ],
        out_specs=[
            pl.BlockSpec((TM, D_MODEL), lambda i: (i, 0)),
            pl.BlockSpec((TM, HALF), lambda i: (i, 0)),
            pl.BlockSpec((TM, LANES), lambda i: (i, 0)),
        ],
        out_shape=[
            jax.ShapeDtypeStruct((N_TOK, D_MODEL), F32),
            jax.ShapeDtypeStruct((N_TOK, HALF), U32),
            jax.ShapeDtypeStruct((N_TOK, LANES), F32),
        ],
        scratch_shapes=scratch,
        compiler_params=_cparams(1),
        name="outproj",
    )(a_prompt, a_sample, w, *x_args, g.reshape(1, D_MODEL), mod, wr, br)


def _router_body(lg_ref, idx_ref, gate_ref, rank_ref, cnt_ref, carry):
    i = pl.program_id(0)

    @pl.when(i == 0)
    def _():
        carry[...] = jnp.zeros_like(carry)

    lane = lax.broadcasted_iota(I32, (ROUTER_TM, LANES), 1)
    l = jnp.where(lane < N_EXPERTS, lg_ref[...], -jnp.inf)
    vals, idxs, hots = [], [], []
    for _ in range(TOP_K):
        m = l.max(axis=-1, keepdims=True)
        idx = jnp.where(l == m, lane, LANES).min(axis=-1, keepdims=True)
        hot = lane == idx
        l = jnp.where(hot, -jnp.inf, l)
        vals.append(m)
        idxs.append(idx)
        hots.append(hot)
    es = [jnp.exp(v - vals[0]) for v in vals]
    denom = es[0] + es[1] + es[2] + es[3]
    hot_all = (hots[0] | hots[1] | hots[2] | hots[3]).astype(F32)
    r = lax.broadcasted_iota(I32, (ROUTER_TM, ROUTER_TM), 0)
    c = lax.broadcasted_iota(I32, (ROUTER_TM, ROUTER_TM), 1)
    before = _dot((c < r).astype(BF16), hot_all.astype(BF16)) + carry[...]
    idx_out = jnp.zeros((ROUTER_TM, LANES), I32)
    gate_out = jnp.zeros((ROUTER_TM, LANES), F32)
    rank_out = jnp.zeros((ROUTER_TM, LANES), I32)
    for k in range(TOP_K):
        rank_k = jnp.where(hots[k], before, 0.0).sum(axis=-1, keepdims=True).astype(I32)
        idx_out = jnp.where(lane == k, idxs[k], idx_out)
        gate_out = jnp.where(lane == k, es[k] / denom, gate_out)
        rank_out = jnp.where(lane == k, rank_k, rank_out)
    idx_ref[...] = idx_out
    gate_ref[...] = gate_out
    rank_ref[...] = rank_out
    carry[...] = carry[...] + hot_all.sum(axis=0, keepdims=True)
    cnt_ref[...] = carry[...].astype(I32)


def _router(logits):
    spec = pl.BlockSpec((ROUTER_TM, LANES), lambda i: (i, 0))
    return pl.pallas_call(
        _router_body,
        grid=(N_TOK // ROUTER_TM,),
        in_specs=[spec],
        out_specs=[spec, spec, spec, pl.BlockSpec((1, LANES), lambda i: (0, 0))],
        out_shape=[
            jax.ShapeDtypeStruct((N_TOK, LANES), I32),
            jax.ShapeDtypeStruct((N_TOK, LANES), F32),
            jax.ShapeDtypeStruct((N_TOK, LANES), I32),
            jax.ShapeDtypeStruct((1, LANES), I32),
        ],
        scratch_shapes=[pltpu.VMEM((1, LANES), F32)],
        compiler_params=_cparams(1),
        name="router",
    )(logits)


SC_CORES = 2
SC_SUBCORES = 16
SC_WORKERS = SC_CORES * SC_SUBCORES
SC_CH = 32


def _sc_mesh():
    return plsc.VectorSubcoreMesh(core_axis_name="c", subcore_axis_name="s",
                                  num_cores=SC_CORES, num_subcores=SC_SUBCORES)


def _sc_worker():
    return lax.axis_index("s") * SC_CORES + lax.axis_index("c")


def _dispatch(tok, pos_km):
    n, d = tok.shape
    n_ch = n // (SC_WORKERS * SC_CH)
    assert n_ch * SC_WORKERS * SC_CH == n and n_ch % 2 == 0

    @functools.partial(
        pl.kernel, mesh=_sc_mesh(), out_type=jax.ShapeDtypeStruct((MOE_ROWS, d), tok.dtype),
        scratch_types=[pltpu.VMEM((n_ch, TOP_K, SC_CH), I32), pltpu.VMEM((2, SC_CH, d), tok.dtype),
                       pltpu.SemaphoreType.DMA((2,)), pltpu.SemaphoreType.DMA((2,))])
    def scatter_rows(tok_hbm, pos_hbm, out_hbm, idx_v, rows_v, load_sem, scat_sem):
        wid = _sc_worker()
        base = wid * (n_ch * SC_CH)
        pltpu.sync_copy(pos_hbm.at[wid], idx_v)

        def load(g, slot):
            rows = pl.ds(pl.multiple_of(base + g * SC_CH, 8), SC_CH)
            return pltpu.make_async_copy(tok_hbm.at[rows], rows_v.at[slot], load_sem.at[slot])

        def scatter(g, k, slot):
            return pltpu.make_async_copy(rows_v.at[slot], out_hbm.at[idx_v.at[g, k]], scat_sem.at[slot])

        load(0, 0).start()

        @pl.loop(0, n_ch, step=2)
        def _(g0):
            for slot in range(2):
                g = g0 + slot
                load(g, slot).wait()

                @pl.when(g + 1 < n_ch)
                def _():
                    load(g + 1, 1 - slot).start()

                for k in range(TOP_K):
                    scatter(g, k, slot).start()
                for k in range(TOP_K):
                    scatter(g, k, slot).wait()

    pos = pos_km.reshape(TOP_K, SC_WORKERS, n_ch, SC_CH).transpose(1, 2, 0, 3)
    return scatter_rows(tok, pos)


def _gather_rows(table, idx):
    b, d = idx.shape[0], table.shape[1]
    n_ch = b // (SC_WORKERS * SC_CH)
    assert n_ch * SC_WORKERS * SC_CH == b and n_ch % 2 == 0

    @functools.partial(
        pl.kernel, mesh=_sc_mesh(), out_type=jax.ShapeDtypeStruct((b, d), table.dtype),
        scratch_types=[pltpu.VMEM((n_ch, SC_CH), I32), pltpu.VMEM((2, SC_CH, d), table.dtype),
                       pltpu.SemaphoreType.DMA((2,))])
    def gather_rows(table_hbm, idx_hbm, out_hbm, idx_v, rows_v, sems):
        wid = _sc_worker()
        base = wid * (n_ch * SC_CH)
        pltpu.sync_copy(idx_hbm.at[wid], idx_v)

        def gather(g, slot):
            return pltpu.make_async_copy(table_hbm.at[idx_v.at[g]], rows_v.at[slot], sems.at[slot])

        def write_out(g, slot):
            rows = pl.ds(pl.multiple_of(base + g * SC_CH, 8), SC_CH)
            pltpu.sync_copy(rows_v.at[slot], out_hbm.at[rows])

        gather(0, 0).start()

        @pl.loop(0, n_ch, step=2)
        def _(g):
            gather(g + 1, 1).start()
            gather(g, 0).wait()
            write_out(g, 0)

            @pl.when(g + 2 < n_ch)
            def _():
                gather(g + 2, 0).start()

            gather(g + 1, 1).wait()
            write_out(g + 1, 1)

    return gather_rows(table, idx.reshape(SC_WORKERS, n_ch, SC_CH))


def _moe_body(te_ref, tv_ref, slot_ref, nxt_ref, x_ref, w1_hbm, b1_ref, w2_hbm, b2_ref, perm_ref, o_ref,
              w1_buf, w2_buf, w1_scr, w2_scr, sem, *, layer):
    t = pl.program_id(0)
    e = te_ref[t]
    slot = slot_ref[t]
    nxt = nxt_ref[t]
    first = jnp.logical_or(t == 0, e != te_ref[jnp.maximum(t - 1, 0)])

    def fetch(expert, s):
        return (pltpu.make_async_copy(w1_hbm.at[layer, expert], w1_buf.at[s], sem.at[s, 0]),
                pltpu.make_async_copy(w2_hbm.at[layer, expert], w2_buf.at[s], sem.at[s, 1]))

    @pl.when(t == 0)
    def _():
        for cp in fetch(e, slot):
            cp.start()

    @pl.when(jnp.logical_and(tv_ref[t] == 1, first))
    def _():
        for cp in fetch(e, slot):
            cp.wait()

        @pl.when(nxt != e)
        def _():
            for cp in fetch(nxt, 1 - slot):
                cp.start()

        w2_scr[...] = w2_buf[slot].astype(BF16)
        half = PERM_W // 2
        for c in range(2 * D_EXPERT // PERM_W):
            blk = _dot(w1_buf[slot, :, c * PERM_W:(c + 1) * PERM_W].astype(BF16), perm_ref[...]).astype(BF16)
            w1_scr[:, c * half:(c + 1) * half] = blk[:, :half]
            w1_scr[:, D_EXPERT + c * half:D_EXPERT + (c + 1) * half] = blk[:, half:]

    @pl.when(tv_ref[t] == 0)
    def _():
        o_ref[...] = jnp.zeros_like(o_ref)

    @pl.when(tv_ref[t] == 1)
    def _():
        x_lo, x_hi = _unpack_rows(x_ref[...])
        x = jnp.concatenate([x_lo.astype(BF16), x_hi.astype(BF16)], axis=-1)
        hh = _dot(x, w1_scr[...]) + b1_ref[...]
        h_glu = jnp.minimum(hh[:, :D_EXPERT], SWIGLU_LIMIT)
        h_lin = jnp.clip(hh[:, D_EXPERT:], -SWIGLU_LIMIT, SWIGLU_LIMIT)
        a = h_glu * _sigmoid(SWIGLU_ALPHA * h_glu) * (h_lin + 1.0)
        o_ref[...] = _pack_rows(_dot(a.astype(BF16), w2_scr[...]) + b2_ref[...])


def _deinterleave_bias_body(b_ref, perm_ref, o_ref):
    half = PERM_W // 2
    for c in range(2 * D_EXPERT // PERM_W):
        r = b_ref[:, c * PERM_W:(c + 1) * PERM_W]
        acc = jnp.zeros(r.shape, F32)
        for _ in range(3):
            piece = r.astype(BF16)
            acc = acc + _dot(piece, perm_ref[...])
            r = r - piece.astype(F32)
        o_ref[:, c * half:(c + 1) * half] = acc[:, :half]
        o_ref[:, D_EXPERT + c * half:D_EXPERT + (c + 1) * half] = acc[:, half:]


def _deinterleave_bias(b1, perm):
    return pl.pallas_call(
        _deinterleave_bias_body,
        out_shape=jax.ShapeDtypeStruct(b1.shape, F32),
        compiler_params=_cparams(0),
        name="deinterleave_bias",
    )(b1, perm)


def _moe_experts(tile_expert, tile_valid, tile_slot, tile_next, xs, w1, b1_split, w2, b2, perm, layer):
    grid_spec = pltpu.PrefetchScalarGridSpec(
        num_scalar_prefetch=4,
        grid=(MOE_TILES,),
        in_specs=[
            pl.BlockSpec((MOE_TM, HALF), lambda t, te, *_: (t, 0)),
            pl.BlockSpec(memory_space=pl.ANY),
            pl.BlockSpec((None, None, 1, 2 * D_EXPERT), lambda t, te, *_: (layer, te[t], 0, 0)),
            pl.BlockSpec(memory_space=pl.ANY),
            pl.BlockSpec((None, None, 1, D_MODEL), lambda t, te, *_: (layer, te[t], 0, 0)),
            pl.BlockSpec((PERM_W, PERM_W), lambda t, te, *_: (0, 0)),
        ],
        out_specs=pl.BlockSpec((MOE_TM, HALF), lambda t, te, *_: (t, 0)),
        scratch_shapes=[
            pltpu.VMEM((2, D_MODEL, 2 * D_EXPERT), F32),
            pltpu.VMEM((2, D_EXPERT, D_MODEL), F32),
            pltpu.VMEM((D_MODEL, 2 * D_EXPERT), BF16),
            pltpu.VMEM((D_EXPERT, D_MODEL), BF16),
            pltpu.SemaphoreType.DMA((2, 2)),
        ],
    )
    return pl.pallas_call(
        functools.partial(_moe_body, layer=layer),
        grid_spec=grid_spec,
        out_shape=jax.ShapeDtypeStruct((MOE_ROWS, HALF), U32),
        compiler_params=_cparams(1),
        name="moe_experts",
    )(tile_expert, tile_valid, tile_slot, tile_next, xs, w1, b1_split, w2, b2, perm)


COMBINE_TM = 256


def _combine_body(*refs, final):
    y_refs = refs[:TOP_K]
    gate_ref, x_ref, mod_ref, g_ref = refs[TOP_K:TOP_K + 4]
    gates = gate_ref[...]
    y_lo, y_hi = (gates[:, 0:1] * h for h in _unpack_rows(y_refs[0][...]))
    for k in range(1, TOP_K):
        lo, hi = _unpack_rows(y_refs[k][...])
        y_lo = y_lo + gates[:, k:k + 1] * lo
        y_hi = y_hi + gates[:, k:k + 1] * hi
    x = x_ref[...] + mod_ref[5:6, :] * jnp.concatenate([y_lo, y_hi], axis=-1)
    if not final:
        refs[-1][...] = x
        return
    x = (x * lax.rsqrt(jnp.mean(x * x, axis=-1, keepdims=True) + EPS)) * g_ref[...]
    prompt_ref, sample_ref = refs[-2:]
    is_prompt = pl.program_id(0) < N_PROMPT // COMBINE_TM

    @pl.when(is_prompt)
    def _():
        prompt_ref[...] = x

    @pl.when(jnp.logical_not(is_prompt))
    def _():
        sample_ref[...] = x


def _combine(y_km, gates, x, mod, g_final, final):
    n_t = N_TOK // COMBINE_TM
    n_pt = N_PROMPT // COMBINE_TM
    y_specs = [pl.BlockSpec((COMBINE_TM, HALF), lambda i, k=k: (k * n_t + i, 0)) for k in range(TOP_K)]
    if final:
        out_specs = [pl.BlockSpec((COMBINE_TM, D_MODEL), lambda i: (jnp.minimum(i, n_pt - 1), 0)),
                     pl.BlockSpec((COMBINE_TM, D_MODEL), lambda i: (jnp.maximum(i - n_pt, 0), 0))]
        out_shape = [jax.ShapeDtypeStruct((N_PROMPT, D_MODEL), F32), jax.ShapeDtypeStruct((N_SAMPLE, D_MODEL), F32)]
    else:
        out_specs = pl.BlockSpec((COMBINE_TM, D_MODEL), lambda i: (i, 0))
        out_shape = jax.ShapeDtypeStruct((N_TOK, D_MODEL), F32)
    return pl.pallas_call(
        functools.partial(_combine_body, final=final),
        grid=(n_t,),
        in_specs=y_specs + [
            pl.BlockSpec((COMBINE_TM, LANES), lambda i: (i, 0)),
            pl.BlockSpec((COMBINE_TM, D_MODEL), lambda i: (i, 0)),
            pl.BlockSpec((None, MOD_ROWS, D_MODEL), lambda i: (_group_of_tile(i, COMBINE_TM), 0, 0)),
            pl.BlockSpec((1, D_MODEL), lambda i: (0, 0)),
        ],
        out_specs=out_specs,
        out_shape=out_shape,
        compiler_params=_cparams(1),
        name="combine",
    )(*([y_km] * TOP_K), gates, x, mod, g_final.reshape(1, D_MODEL))


def _routing_tables(idx, rank, counts):
    counts = counts[0, :N_EXPERTS]
    padded = ((counts + MOE_TM - 1) // MOE_TM) * MOE_TM
    ends = jnp.cumsum(padded)
    offsets = ends - padded
    idx = idx[:, :TOP_K]
    pos = rank[:, :TOP_K] + jnp.sum(
        jnp.where(idx[..., None] == jnp.arange(N_EXPERTS, dtype=I32), offsets, 0), axis=-1)
    pos_km = pos.T.astype(I32)
    tile_start = jnp.arange(MOE_TILES, dtype=I32) * MOE_TM
    tile_valid = (tile_start < ends[-1]).astype(I32)
    tile_expert = jnp.sum((tile_start[:, None] >= ends[None, :]).astype(I32), axis=-1)
    last_expert = jnp.sum((ends[-1] - 1 >= ends).astype(I32))
    tile_expert = jnp.where(tile_valid == 1, tile_expert, last_expert).astype(I32)
    e_ids = jnp.arange(N_EXPERTS, dtype=I32)
    owns = counts > 0
    ordinal = jnp.cumsum(owns.astype(I32)) - 1
    later = jnp.where(owns[None, :] & (e_ids[None, :] > e_ids[:, None]), e_ids[None, :], N_EXPERTS)
    nxt = jnp.min(later, axis=1)
    nxt = jnp.where(nxt == N_EXPERTS, e_ids, nxt)
    onehot = (tile_expert[:, None] == e_ids[None, :]).astype(I32)
    tile_slot = jnp.sum(onehot * (ordinal % 2)[None, :], axis=1).astype(I32)
    tile_next = jnp.sum(onehot * nxt[None, :], axis=1).astype(I32)
    return pos_km, tile_expert, tile_valid, tile_slot, tile_next


def _rope_tables():
    t = np.arange(DEC_SEQ)
    row = (t // GRID_W).astype(np.float32)
    col = (t % GRID_W).astype(np.float32)
    d_axis = RET_DK // 2
    inv = jnp.asarray(ROPE_BASE, F32) ** (-jnp.arange(0, d_axis, 2, dtype=F32) / d_axis)
    ang = jnp.concatenate([row[:, None] * inv, col[:, None] * inv], axis=-1)
    cos = jnp.repeat(jnp.cos(ang), 2, axis=-1)
    sin = jnp.repeat(jnp.sin(ang), 2, axis=-1)
    sign = jnp.asarray(np.tile(np.array([-1.0, 1.0], np.float32), RET_DK // 2))
    return cos, sin * sign


def _deinterleave_matrix():
    p = np.zeros((PERM_W, PERM_W), np.float32)
    half = PERM_W // 2
    for j in range(half):
        p[2 * j, j] = 1.0
        p[2 * j + 1, half + j] = 1.0
    return jnp.asarray(p, BF16)


def kernel(x_prompt, x_sample, state_ret_fwd, state_ret_bwd, cache_na_k, cache_na_v, c, c_ctx, ada_w, ada_b, norm_mix, norm_ffn, norm_final, ret_w_in, ret_w_out, ret_decay_fwd, ret_decay_bwd, na_w_in, na_w_out, na_rpb, moe_w_router, moe_b_router, moe_w1, moe_b1, moe_w2, moe_b2):
    x = (x_prompt.reshape(N_PROMPT, D_MODEL), x_sample.reshape(N_SAMPLE, D_MODEL))

    cond = jnp.zeros((MOD_ROWS, D_MODEL), F32).at[0].set(c_ctx).at[1:1 + DEC_BATCH].set(c)
    mod_all = _adaln(cond, ada_w, ada_b)
    mod_all = mod_all[:, :N_GROUPS].reshape(DEPTH, N_GROUPS, 6, D_MODEL)
    mod_all = jnp.pad(mod_all, ((0, 0), (0, 0), (0, MOD_ROWS - 6), (0, 0)))

    cos, sin_signed = _rope_tables()
    perm = _deinterleave_matrix()
    wr_all = jnp.pad(moe_w_router, ((0, 0), (0, 0), (0, LANES - N_EXPERTS)))
    br_all = jnp.pad(moe_b_router, ((0, 0), (0, LANES - N_EXPERTS))).reshape(DEPTH, 1, LANES)
    b1_split = _deinterleave_bias(moe_b1.reshape(DEPTH * N_EXPERTS, 2 * D_EXPERT), perm)
    b1_split = b1_split.reshape(DEPTH, N_EXPERTS, 1, 2 * D_EXPERT)
    b2_all = moe_b2.reshape(DEPTH, N_EXPERTS, 1, D_MODEL)
    na_tbl = _na_bias_table(na_rpb)
    ret_states = None
    na_caches = None
    for layer in range(DEPTH):
        mod = mod_all[layer]
        j = layer // 2
        if layer % 2 == 0:
            proj = _inproj(x, norm_mix[layer], mod, _cast_layer_bf16(ret_w_in, j), out_dtype=BF16)
            decay = jnp.stack([ret_decay_fwd[j], ret_decay_bwd[j]])
            a_p, s_f, s_b = _ret_prompt(proj, decay, j, ret_states)
            ret_states = (s_f, s_b)
            a_s = _ret_sample(proj, decay, cos, sin_signed, state_ret_fwd, state_ret_bwd, j)
            w_out = ret_w_out
        else:
            proj, ck, cv = _inproj(x, norm_mix[layer], mod, na_w_in, j, j, na_caches)
            na_caches = (ck, cv)
            a_p = _ctx_attn(proj)
            a_s = _na_attn(proj, cache_na_k, cache_na_v, na_tbl, j)
            w_out = na_w_out

        x, tok, logits = _outproj(a_p, a_s, w_out, j, x, norm_ffn[layer], mod, wr_all[layer], br_all[layer])

        idx, gates, rank, counts = _router(logits)
        pos_km, tile_expert, tile_valid, tile_slot, tile_next = _routing_tables(idx, rank, counts)
        xs = _dispatch(tok, pos_km)
        ys = _moe_experts(tile_expert, tile_valid, tile_slot, tile_next, xs, moe_w1, b1_split, moe_w2, b2_all,
                          perm, layer)
        y_km = _gather_rows(ys, pos_km.reshape(-1))
        x = _combine(y_km, gates, x, mod, norm_final, final=(layer == DEPTH - 1))

    y_prompt = x[0].reshape(BATCH, SEQ, D_MODEL)
    y_sample = x[1].reshape(DEC_BATCH, DEC_SEQ, D_MODEL)
    return (y_prompt, y_sample, ret_states[0], ret_states[1],
            jnp.transpose(na_caches[0], (0, 1, 4, 2, 3)), jnp.transpose(na_caches[1], (0, 1, 4, 2, 3)))
```

```python
import functools

import numpy as np
import jax
import jax.numpy as jnp
from jax import lax
from jax.experimental import pallas as pl
from jax.experimental.pallas import tpu as pltpu
from jax.experimental.pallas import tpu_sc as plsc

F32 = jnp.float32
BF16 = jnp.bfloat16
I32 = jnp.int32
U32 = jnp.uint32

D_MODEL = 1024
BATCH = 32
SEQ = 256
DEPTH = 4
DEC_BATCH = 2
DEC_SEQ = 1024
PAST_LEN = 512
GRID_W = 64
EPS = 1e-6
ROPE_BASE = 10000.0
RET_HEADS = 4
RET_DK = D_MODEL // RET_HEADS
RET_DV = 2 * RET_DK
RET_QK_W = RET_HEADS * RET_DK
RET_V_W = RET_HEADS * RET_DV
NA_HEADS = 16
NA_HD = D_MODEL // NA_HEADS
NA_KH = 8
NA_KW = 16
N_EXPERTS = 32
TOP_K = 4
D_EXPERT = D_MODEL
SWIGLU_LIMIT = 7.0
SWIGLU_ALPHA = 1.702
N_RET_LAYERS = (DEPTH + 1) // 2
N_NA_LAYERS = DEPTH // 2

N_PROMPT = BATCH * SEQ
N_SAMPLE = DEC_BATCH * DEC_SEQ
N_TOK = N_PROMPT + N_SAMPLE
N_GROUPS = 1 + DEC_BATCH
MOD_ROWS = 8
LANES = 128

TM = 256
RET_CHUNK = 256
MOE_TM = 256
MOE_ROWS = N_TOK * TOP_K + N_EXPERTS * MOE_TM
MOE_TILES = MOE_ROWS // MOE_TM
ROUTER_TM = 512
PERM_W = 256
VMEM_LIMIT = 56 * 1024 * 1024


def _cparams(n_axes, vmem=VMEM_LIMIT):
    return pltpu.CompilerParams(dimension_semantics=("arbitrary",) * n_axes, vmem_limit_bytes=vmem)


def _group_of_tile(i, tm):
    return jnp.maximum((i * tm) // DEC_SEQ - (N_PROMPT // DEC_SEQ - 1), 0)


def _sigmoid(x):
    return 1.0 / (1.0 + jnp.exp(-x))


def _silu(x):
    return x * _sigmoid(x)


def _log_sigmoid(x):
    return jnp.minimum(x, 0.0) - jnp.log(1.0 + jnp.exp(-jnp.abs(x)))


def _norm_mod(x, g, shift, scale):
    y = x * lax.rsqrt(jnp.mean(x * x, axis=-1, keepdims=True) + EPS)
    return (y * g) * (1.0 + scale) + shift


def _dot(a, b):
    return jnp.dot(a, b, preferred_element_type=F32)


HALF = D_MODEL // 2
HI_MASK = 0xFFFF0000


def _pack_rows(x):
    lo = pltpu.bitcast(x[:, :HALF].astype(BF16).astype(F32), U32)
    hi = pltpu.bitcast(x[:, HALF:].astype(BF16).astype(F32), U32)
    return (lo >> 16) | (hi & jnp.uint32(HI_MASK))


def _unpack_rows(p):
    return pltpu.bitcast(p << 16, F32), pltpu.bitcast(p & jnp.uint32(HI_MASK), F32)


def _dot_nt(a, b):
    return lax.dot_general(a, b, (((1,), (1,)), ((), ())), preferred_element_type=F32)


def _dot_tn(a, b):
    return lax.dot_general(a, b, (((0,), (0,)), ((), ())), preferred_element_type=F32)


ADA_TN = 1536


def _adaln_body(cond_ref, w_ref, b_ref, o_ref):
    s = _silu(cond_ref[...]).astype(BF16)
    o_ref[...] = _dot(s, w_ref[...].astype(BF16)) + b_ref[...]


def _adaln(cond, ada_w, ada_b):
    n_out = 6 * D_MODEL
    return pl.pallas_call(
        _adaln_body,
        grid=(DEPTH, n_out // ADA_TN),
        in_specs=[
            pl.BlockSpec((MOD_ROWS, D_MODEL), lambda l, j: (0, 0)),
            pl.BlockSpec((None, D_MODEL, ADA_TN), lambda l, j: (l, 0, j)),
            pl.BlockSpec((None, 1, ADA_TN), lambda l, j: (l, 0, j)),
        ],
        out_specs=pl.BlockSpec((None, MOD_ROWS, ADA_TN), lambda l, j: (l, 0, j)),
        out_shape=jax.ShapeDtypeStruct((DEPTH, MOD_ROWS, n_out), F32),
        compiler_params=_cparams(2),
        name="adaln",
    )(cond, ada_w, ada_b.reshape(DEPTH, 1, n_out))


CAST_TR = 256


def _cast_body(w_ref, o_ref):
    o_ref[...] = w_ref[...].astype(BF16)


def _cast_layer_bf16(w, j):
    _, r, c = w.shape
    return pl.pallas_call(
        _cast_body,
        grid=(r // CAST_TR,),
        in_specs=[pl.BlockSpec((None, CAST_TR, c), lambda i: (j, i, 0))],
        out_specs=pl.BlockSpec((CAST_TR, c), lambda i: (i, 0)),
        out_shape=jax.ShapeDtypeStruct((r, c), BF16),
        compiler_params=_cparams(1),
        name="cast_bf16",
    )(w)


def _x_inputs(x):
    n_pt = N_PROMPT // TM
    xa, xb, off = (x[0], x[1], 0) if isinstance(x, tuple) else (x, x, n_pt)
    specs = [pl.BlockSpec((TM, D_MODEL), lambda i: (jnp.minimum(i, n_pt - 1), 0)),
             pl.BlockSpec((TM, D_MODEL), lambda i: (off + jnp.maximum(i - n_pt, 0), 0))]
    return specs, [xa, xb]


def _x_tile(xa_ref, xb_ref):
    return jnp.where(pl.program_id(0) < N_PROMPT // TM, xa_ref[...], xb_ref[...])


def _weight_input(w, layer):
    if layer is None:
        return pl.BlockSpec(w.shape, lambda i: (0, 0)), []
    _, r, c = w.shape
    spec = pl.BlockSpec((None, r, c), lambda i: (layer, 0, 0), pipeline_mode=pl.Buffered(1))
    return spec, [pltpu.VMEM((r, c), BF16)]


def _weight_ref(w_ref, scratch):
    if not scratch:
        return w_ref

    @pl.when(pl.program_id(0) == 0)
    def _():
        scratch[0][...] = w_ref[...].astype(BF16)

    return scratch[0]


def _inproj_body(xa_ref, xb_ref, g_ref, mod_ref, w_ref, *rest, cache_layer, first_cache, n_scratch):
    scratch = rest[len(rest) - n_scratch:]
    rest = rest[:len(rest) - n_scratch]
    o_ref = rest[-3] if cache_layer is not None else rest[-1]
    i = pl.program_id(0)
    w = _weight_ref(w_ref, scratch)
    h = _norm_mod(_x_tile(xa_ref, xb_ref), g_ref[...], mod_ref[0:1, :], mod_ref[1:2, :]).astype(BF16)
    for c in range(o_ref.shape[1] // D_MODEL):
        sl = slice(c * D_MODEL, (c + 1) * D_MODEL)
        r = _dot(h, w[:, sl])
        o_ref[:, sl] = r.astype(o_ref.dtype)
        if cache_layer is not None and c >= 1:
            cache_ref = rest[-3 + c]

            @pl.when(i < BATCH)
            def _(r=r, cache_ref=cache_ref):
                r_t = r.T.reshape(NA_HEADS, NA_HD, SEQ)
                if first_cache:
                    for l in range(N_NA_LAYERS):
                        cache_ref[l] = r_t if l == cache_layer else jnp.zeros_like(r_t)
                else:
                    cache_ref[...] = r_t


def _inproj(x, g, mod, w, w_layer=None, cache_layer=None, prev_caches=None, out_dtype=F32):
    n_out = w.shape[-1]
    x_specs, x_args = _x_inputs(x)
    w_spec, scratch = _weight_input(w, w_layer)
    in_specs = x_specs + [
        pl.BlockSpec((1, D_MODEL), lambda i: (0, 0)),
        pl.BlockSpec((None, MOD_ROWS, D_MODEL), lambda i: (_group_of_tile(i, TM), 0, 0)),
        w_spec,
    ]
    out_specs = [pl.BlockSpec((TM, n_out), lambda i: (i, 0))]
    out_shape = [jax.ShapeDtypeStruct((N_TOK, n_out), out_dtype)]
    args = x_args + [g.reshape(1, D_MODEL), mod, w]
    aliases = {}
    first_cache = prev_caches is None
    if cache_layer is not None:
        assert TM == SEQ and n_out == 3 * D_MODEL
        cache = jax.ShapeDtypeStruct((BATCH, N_NA_LAYERS, NA_HEADS, NA_HD, SEQ), F32)
        if first_cache:
            spec = pl.BlockSpec((None, N_NA_LAYERS, NA_HEADS, NA_HD, SEQ),
                                lambda i: (jnp.minimum(i, BATCH - 1), 0, 0, 0, 0))
        else:
            spec = pl.BlockSpec((None, None, NA_HEADS, NA_HD, SEQ),
                                lambda i: (jnp.minimum(i, BATCH - 1), cache_layer, 0, 0, 0))
            in_specs += [pl.BlockSpec(memory_space=pl.ANY)] * 2
            aliases = {len(args): 1, len(args) + 1: 2}
            args += list(prev_caches)
        out_specs += [spec, spec]
        out_shape += [cache, cache]
    outs = pl.pallas_call(
        functools.partial(_inproj_body, cache_layer=cache_layer, first_cache=first_cache, n_scratch=len(scratch)),
        grid=(N_TOK // TM,),
        in_specs=in_specs,
        out_specs=out_specs,
        out_shape=out_shape,
        scratch_shapes=scratch,
        input_output_aliases=aliases,
        compiler_params=_cparams(1),
        name="inproj",
    )(*args)
    return outs[0] if cache_layer is None else outs


def _decay_terms(dec_ref, h, c):
    lgf = _log_sigmoid(jnp.zeros((c, c), F32) + dec_ref[0, h])
    lgb = _log_sigmoid(jnp.zeros((c, c), F32) + dec_ref[1, h])
    row = lax.broadcasted_iota(I32, (c, c), 0)
    col = lax.broadcasted_iota(I32, (c, c), 1)
    diff = (row - col).astype(F32)
    lower = diff >= 0
    upper = diff <= 0
    mask = (jnp.where(lower, jnp.exp(jnp.where(lower, diff, 0.0) * lgf), 0.0)
            + jnp.where(upper, jnp.exp(jnp.where(upper, -diff, 0.0) * lgb), 0.0))
    pos = lax.broadcasted_iota(I32, (c, 1), 0).astype(F32)
    lgf1 = _log_sigmoid(jnp.zeros((c, 1), F32) + dec_ref[0, h])
    lgb1 = _log_sigmoid(jnp.zeros((c, 1), F32) + dec_ref[1, h])
    lgf0 = _log_sigmoid(jnp.zeros((1, 1), F32) + dec_ref[0, h])
    lgb0 = _log_sigmoid(jnp.zeros((1, 1), F32) + dec_ref[1, h])
    terms = dict(
        mask=mask,
        q_f=jnp.exp((pos + 1.0) * lgf1), k_f=jnp.exp((c - 1.0 - pos) * lgf1), c_f=jnp.exp(c * lgf0),
        q_b=jnp.exp((c - pos) * lgb1), k_b=jnp.exp(pos * lgb1), c_b=jnp.exp(c * lgb0),
    )
    return terms


def _group_norm_gate(o, g):
    o = o * lax.rsqrt(jnp.mean(o * o, axis=-1, keepdims=True) + EPS)
    return (_silu(g) * o).astype(BF16)


def _ret_prompt_body(dec_ref, q_ref, k_ref, v_ref, g_ref, *rest, layer, first):
    o_ref, sf_ref, sb_ref = rest[-3:]
    for h in range(RET_HEADS):
        t = _decay_terms(dec_ref, h, SEQ)
        qk_cols = slice(h * RET_DK, (h + 1) * RET_DK)
        v_cols = slice(h * RET_DV, (h + 1) * RET_DV)
        q = q_ref[:, qk_cols].astype(F32)
        k = k_ref[:, qk_cols].astype(F32) * (RET_DK ** -0.5)
        v = v_ref[:, v_cols].astype(BF16)
        scores = _dot_nt(q.astype(BF16), k.astype(BF16)) * t["mask"]
        o = _dot(scores.astype(BF16), v)
        o_ref[:, v_cols] = _group_norm_gate(o, g_ref[:, v_cols].astype(F32))
        for ref, k_dec in ((sf_ref, t["k_f"]), (sb_ref, t["k_b"])):
            s = _dot_tn((k * k_dec).astype(BF16), v)
            if first:
                for l in range(N_RET_LAYERS):
                    ref[l, h] = s if l == layer else jnp.zeros_like(s)
            else:
                ref[h] = s


def _ret_prompt(proj, decay, layer, prev_states=None):
    assert 2 * RET_QK_W == RET_V_W
    state = jax.ShapeDtypeStruct((BATCH, N_RET_LAYERS, RET_HEADS, RET_DK, RET_DV), F32)
    in_specs = [
        pl.BlockSpec(memory_space=pltpu.SMEM),
        pl.BlockSpec((SEQ, RET_QK_W), lambda b: (b, 0)),
        pl.BlockSpec((SEQ, RET_QK_W), lambda b: (b, 1)),
        pl.BlockSpec((SEQ, RET_V_W), lambda b: (b, 1)),
        pl.BlockSpec((SEQ, RET_V_W), lambda b: (b, 2)),
    ]
    args = [decay, proj, proj, proj, proj]
    first = prev_states is None
    aliases = {}
    if first:
        s_spec = pl.BlockSpec((None, N_RET_LAYERS, RET_HEADS, RET_DK, RET_DV), lambda b: (b, 0, 0, 0, 0))
    else:
        s_spec = pl.BlockSpec((None, None, RET_HEADS, RET_DK, RET_DV), lambda b: (b, layer, 0, 0, 0))
        in_specs += [pl.BlockSpec(memory_space=pl.ANY)] * 2
        aliases = {len(args): 1, len(args) + 1: 2}
        args += list(prev_states)
    return pl.pallas_call(
        functools.partial(_ret_prompt_body, layer=layer, first=first),
        grid=(BATCH,),
        in_specs=in_specs,
        out_specs=[pl.BlockSpec((SEQ, RET_V_W), lambda b: (b, 0)), s_spec, s_spec],
        out_shape=[jax.ShapeDtypeStruct((N_PROMPT, RET_V_W), BF16), state, state],
        input_output_aliases=aliases,
        compiler_params=_cparams(1),
        name="ret_prompt",
    )(*args)


def _rope(x, cos, sin_signed):
    n = x.shape[-1]
    lane = lax.broadcasted_iota(I32, x.shape, 1)
    swapped = jnp.where(lane % 2 == 0, pltpu.roll(x, n - 1, 1), pltpu.roll(x, 1, 1))
    return x * cos + swapped * sin_signed


def _ret_sample_body(dec_ref, q_ref, k_ref, v_ref, g_ref, cos_ref, sin_ref, s0f_ref, s0b_ref, o_ref,
                     sf_scr, sb_scr):
    h = pl.program_id(1)
    c = RET_CHUNK
    n_chunks = DEC_SEQ // c
    t = _decay_terms(dec_ref, h, c)

    def chunk(ref, i):
        return ref[i * c:(i + 1) * c, :].astype(F32)

    def qk(i):
        q = _rope(chunk(q_ref, i), chunk(cos_ref, i), chunk(sin_ref, i))
        k = _rope(chunk(k_ref, i) * (RET_DK ** -0.5), chunk(cos_ref, i), chunk(sin_ref, i))
        return q, k

    s = s0f_ref[...]
    for i in range(n_chunks):
        sf_scr[i] = s
        if i + 1 < n_chunks:
            _, k = qk(i)
            s = s * t["c_f"] + _dot_tn((k * t["k_f"]).astype(BF16), chunk(v_ref, i).astype(BF16))
    s = s0b_ref[...]
    for i in reversed(range(n_chunks)):
        sb_scr[i] = s
        if i > 0:
            _, k = qk(i)
            s = s * t["c_b"] + _dot_tn((k * t["k_b"]).astype(BF16), chunk(v_ref, i).astype(BF16))

    for i in range(n_chunks):
        q, k = qk(i)
        scores = _dot_nt(q.astype(BF16), k.astype(BF16)) * t["mask"]
        o = _dot(scores.astype(BF16), chunk(v_ref, i).astype(BF16))
        o = o + _dot((q * t["q_f"]).astype(BF16), sf_scr[i].astype(BF16))
        o = o + _dot((q * t["q_b"]).astype(BF16), sb_scr[i].astype(BF16))
        o_ref[i * c:(i + 1) * c, :] = _group_norm_gate(o, chunk(g_ref, i))


def _ret_sample(proj, decay, cos, sin_signed, s0f, s0b, j):
    kq = RET_QK_W // RET_DK
    kv = 2 * RET_QK_W // RET_DV
    kg = kv + RET_HEADS
    rb = N_PROMPT // DEC_SEQ
    n_chunks = DEC_SEQ // RET_CHUNK
    return pl.pallas_call(
        _ret_sample_body,
        grid=(DEC_BATCH, RET_HEADS),
        in_specs=[
            pl.BlockSpec(memory_space=pltpu.SMEM),
            pl.BlockSpec((DEC_SEQ, RET_DK), lambda b, h: (rb + b, h)),
            pl.BlockSpec((DEC_SEQ, RET_DK), lambda b, h: (rb + b, kq + h)),
            pl.BlockSpec((DEC_SEQ, RET_DV), lambda b, h: (rb + b, kv + h)),
            pl.BlockSpec((DEC_SEQ, RET_DV), lambda b, h: (rb + b, kg + h)),
            pl.BlockSpec((DEC_SEQ, RET_DK), lambda b, h: (0, 0)),
            pl.BlockSpec((DEC_SEQ, RET_DK), lambda b, h: (0, 0)),
            pl.BlockSpec((None, None, None, RET_DK, RET_DV), lambda b, h: (b, j, h, 0, 0)),
            pl.BlockSpec((None, None, None, RET_DK, RET_DV), lambda b, h: (b, j, h, 0, 0)),
        ],
        out_specs=pl.BlockSpec((DEC_SEQ, RET_DV), lambda b, h: (b, h)),
        out_shape=jax.ShapeDtypeStruct((N_SAMPLE, RET_V_W), BF16),
        scratch_shapes=[pltpu.VMEM((n_chunks, RET_DK, RET_DV), F32),
                        pltpu.VMEM((n_chunks, RET_DK, RET_DV), F32)],
        compiler_params=_cparams(2),
        name="ret_sample",
    )(decay, proj, proj, proj, proj, cos, sin_signed, s0f, s0b)


NA_HPB = LANES // NA_HD


def _head_lanes(rows, hh):
    lane = lax.broadcasted_iota(I32, (rows, LANES), 1)
    return jnp.logical_and(lane >= hh * NA_HD, lane < (hh + 1) * NA_HD)


def _with_ones(v):
    return jnp.concatenate([v, jnp.ones(v.shape, v.dtype)], axis=-1)


def _ctx_attn_body(q_ref, k_ref, v_ref, o_ref):
    outs = []
    for p in range(NA_HEADS // NA_HPB):
        sl = slice(p * LANES, (p + 1) * LANES)
        q2 = q_ref[:, sl] * (NA_HD ** -0.5)
        k2 = k_ref[:, sl].astype(BF16)
        v2 = _with_ones(v_ref[:, sl].astype(BF16))
        o_pair = None
        for hh in range(NA_HPB):
            mine = _head_lanes(SEQ, hh)
            s = _dot_nt(jnp.where(mine, q2, 0.0).astype(BF16), k2)
            e = jnp.exp(s - s.max(axis=-1, keepdims=True)).astype(BF16)
            acc = _dot(e, v2)
            o_h = acc[:, :LANES] / acc[:, LANES:]
            o_pair = o_h if o_pair is None else jnp.where(mine, o_h, o_pair)
        outs.append(o_pair)
    o_ref[...] = jnp.concatenate(outs, axis=-1).astype(BF16)


def _ctx_attn(proj):
    return pl.pallas_call(
        _ctx_attn_body,
        grid=(BATCH,),
        in_specs=[
            pl.BlockSpec((SEQ, D_MODEL), lambda b: (b, 0)),
            pl.BlockSpec((SEQ, D_MODEL), lambda b: (b, 1)),
            pl.BlockSpec((SEQ, D_MODEL), lambda b: (b, 2)),
        ],
        out_specs=pl.BlockSpec((SEQ, D_MODEL), lambda b: (b, 0)),
        out_shape=jax.ShapeDtypeStruct((N_PROMPT, D_MODEL), BF16),
        compiler_params=_cparams(1),
        name="ctx_attn",
    )(proj, proj, proj)


NA_QT = 256
NA_WIN = NA_KH * GRID_W


def _na_attn_body(q_ref, k_ref, v_ref, ck_ref, cv_ref, tbl_ref, o_ref):
    t = pl.program_id(1)
    rows_per_tile = NA_QT // GRID_W
    n_rows = DEC_SEQ // GRID_W
    q2 = q_ref[...] * (NA_HD ** -0.5)
    q = [jnp.where(_head_lanes(NA_QT, hh), q2, 0.0).astype(BF16) for hh in range(NA_HPB)]
    ck = ck_ref[...].astype(BF16)
    s_ctx = [_dot_nt(q[hh], ck) for hh in range(NA_HPB)]
    acc_win = [[] for _ in range(NA_HPB)]
    e_ctx = [[] for _ in range(NA_HPB)]
    for u in range(rows_per_tile):
        qr = t * rows_per_tile + u
        rs = jnp.clip(qr - NA_KH // 2, 0, n_rows - NA_KH)
        i0 = rs - qr + (NA_KH - 1)
        start = pl.multiple_of(rs * GRID_W, GRID_W)
        kw = k_ref[pl.ds(start, NA_WIN), :].astype(BF16)
        vw = _with_ones(v_ref[pl.ds(start, NA_WIN), :].astype(BF16))
        rows = slice(u * GRID_W, (u + 1) * GRID_W)
        for hh in range(NA_HPB):
            bias = jnp.concatenate([tbl_ref[hh, i0 + 2 * m] for m in range(NA_KH // 2)], axis=-1)
            sw = _dot_nt(q[hh][rows], kw) + bias
            sc = s_ctx[hh][rows]
            m = jnp.maximum(sw.max(axis=-1, keepdims=True), sc.max(axis=-1, keepdims=True))
            acc_win[hh].append(_dot(jnp.exp(sw - m).astype(BF16), vw))
            e_ctx[hh].append(jnp.exp(sc - m).astype(BF16))
    cv = _with_ones(cv_ref[...].astype(BF16))
    out = None
    for hh in range(NA_HPB):
        acc = jnp.concatenate(acc_win[hh], axis=0) + _dot(jnp.concatenate(e_ctx[hh], axis=0), cv)
        o_h = acc[:, :LANES] / acc[:, LANES:]
        out = o_h if out is None else jnp.where(_head_lanes(NA_QT, hh), o_h, out)
    o_ref[...] = out.astype(BF16)


def _na_attn(proj, cache_k, cache_v, tbl, j):
    n_qt = DEC_SEQ // NA_QT
    rb_q = N_PROMPT // NA_QT
    rb_kv = N_PROMPT // DEC_SEQ
    cb = D_MODEL // LANES
    ck = cache_k.reshape(DEC_BATCH, -1, PAST_LEN, D_MODEL)
    cv = cache_v.reshape(DEC_BATCH, -1, PAST_LEN, D_MODEL)
    return pl.pallas_call(
        _na_attn_body,
        grid=(cb, n_qt, DEC_BATCH),
        in_specs=[
            pl.BlockSpec((NA_QT, LANES), lambda p, t, b: (rb_q + b * n_qt + t, p)),
            pl.BlockSpec((DEC_SEQ, LANES), lambda p, t, b: (rb_kv + b, cb + p)),
            pl.BlockSpec((DEC_SEQ, LANES), lambda p, t, b: (rb_kv + b, 2 * cb + p)),
            pl.BlockSpec((None, None, PAST_LEN, LANES), lambda p, t, b: (b, j, 0, p)),
            pl.BlockSpec((None, None, PAST_LEN, LANES), lambda p, t, b: (b, j, 0, p)),
            pl.BlockSpec((None, NA_HPB, 2 * NA_KH - 2, GRID_W, 2 * GRID_W), lambda p, t, b: (j, p, 0, 0, 0)),
        ],
        out_specs=pl.BlockSpec((NA_QT, LANES), lambda p, t, b: (b * n_qt + t, p)),
        out_shape=jax.ShapeDtypeStruct((N_SAMPLE, D_MODEL), BF16),
        compiler_params=_cparams(3),
        name="na_attn",
    )(proj, proj, proj, ck, cv, tbl)


def _na_bias_table(rpb):
    c = np.arange(GRID_W)
    cs = np.clip(c - NA_KW // 2, 0, GRID_W - NA_KW)
    col_valid = (c[None, :] >= cs[:, None]) & (c[None, :] < cs[:, None] + NA_KW)
    col_off = np.clip(c[None, :] - c[:, None] + NA_KW - 1, 0, 2 * NA_KW - 2)
    sel_c = (col_off[..., None] == np.arange(2 * NA_KW - 1)).astype(np.float32)
    t = jnp.einsum("lhij,qkj->lhiqk", rpb, sel_c, precision=lax.Precision.HIGHEST)
    t = jnp.where(col_valid, t, -jnp.inf)
    return jnp.concatenate([t[:, :, :-1], t[:, :, 1:]], axis=-1)


def _outproj_body(ap_ref, as_ref, w_ref, xa_ref, xb_ref, g_ref, mod_ref, wr_ref, br_ref, xo_ref, tok_ref, lg_ref,
                  *scratch):
    i = pl.program_id(0)
    a = jnp.where(i < N_PROMPT // TM, ap_ref[...], as_ref[...])
    x = _x_tile(xa_ref, xb_ref) + mod_ref[2:3, :] * _dot(a, _weight_ref(w_ref, scratch)[...])
    xo_ref[...] = x
    tok = _norm_mod(x, g_ref[...], mod_ref[3:4, :], mod_ref[4:5, :])
    tok_ref[...] = _pack_rows(tok)
    hi = tok.astype(BF16)
    lo = (tok - hi.astype(F32)).astype(BF16)
    wr = wr_ref[...]
    wr_hi = wr.astype(BF16)
    wr_lo = (wr - wr_hi.astype(F32)).astype(BF16)
    lg_ref[...] = _dot(hi, wr_hi) + _dot(lo, wr_hi) + _dot(hi, wr_lo) + br_ref[...]


def _outproj(a_prompt, a_sample, w, w_layer, x, g, mod, wr, br):
    k_in = a_prompt.shape[1]
    n_pt = N_PROMPT // TM
    x_specs, x_args = _x_inputs(x)
    w_spec, scratch = _weight_input(w, w_layer)
    return pl.pallas_call(
        _outproj_body,
        grid=(N_TOK // TM,),
        in_specs=[
            pl.BlockSpec((TM, k_in), lambda i: (jnp.minimum(i, n_pt - 1), 0)),
            pl.BlockSpec((TM, k_in), lambda i: (jnp.maximum(i - n_pt, 0), 0)),
            w_spec,
            *x_specs,
            pl.BlockSpec((1, D_MODEL), lambda i: (0, 0)),
            pl.BlockSpec((None, MOD_ROWS, D_MODEL), lambda i: (_group_of_tile(i, TM), 0, 0)),
            pl.BlockSpec((D_MODEL, LANES), lambda i: (0, 0)),
            pl.BlockSpec((1, LANES), lambda i: (0, 0)),
        ],
        out_specs=[
            pl.BlockSpec((TM, D_MODEL), lambda i: (i, 0)),
            pl.BlockSpec((TM, HALF), lambda i: (i, 0)),
            pl.BlockSpec((TM, LANES), lambda i: (i, 0)),
        ],
        out_shape=[
            jax.ShapeDtypeStruct((N_TOK, D_MODEL), F32),
            jax.ShapeDtypeStruct((N_TOK, HALF), U32),
            jax.ShapeDtypeStruct((N_TOK, LANES), F32),
        ],
        scratch_shapes=scratch,
        compiler_params=_cparams(1),
        name="outproj",
    )(a_prompt, a_sample, w, *x_args, g.reshape(1, D_MODEL), mod, wr, br)


def _router_body(lg_ref, idx_ref, gate_ref, rank_ref, cnt_ref, carry):
    i = pl.program_id(0)

    @pl.when(i == 0)
    def _():
        carry[...] = jnp.zeros_like(carry)

    lane = lax.broadcasted_iota(I32, (ROUTER_TM, LANES), 1)
    l = jnp.where(lane < N_EXPERTS, lg_ref[...], -jnp.inf)
    vals, idxs, hots = [], [], []
    for _ in range(TOP_K):
        m = l.max(axis=-1, keepdims=True)
        idx = jnp.where(l == m, lane, LANES).min(axis=-1, keepdims=True)
        hot = lane == idx
        l = jnp.where(hot, -jnp.inf, l)
        vals.append(m)
        idxs.append(idx)
        hots.append(hot)
    es = [jnp.exp(v - vals[0]) for v in vals]
    denom = es[0] + es[1] + es[2] + es[3]
    hot_all = (hots[0] | hots[1] | hots[2] | hots[3]).astype(F32)
    r = lax.broadcasted_iota(I32, (ROUTER_TM, ROUTER_TM), 0)
    c = lax.broadcasted_iota(I32, (ROUTER_TM, ROUTER_TM), 1)
    before = _dot((c < r).astype(BF16), hot_all.astype(BF16)) + carry[...]
    idx_out = jnp.zeros((ROUTER_TM, LANES), I32)
    gate_out = jnp.zeros((ROUTER_TM, LANES), F32)
    rank_out = jnp.zeros((ROUTER_TM, LANES), I32)
    for k in range(TOP_K):
        rank_k = jnp.where(hots[k], before, 0.0).sum(axis=-1, keepdims=True).astype(I32)
        idx_out = jnp.where(lane == k, idxs[k], idx_out)
        gate_out = jnp.where(lane == k, es[k] / denom, gate_out)
        rank_out = jnp.where(lane == k, rank_k, rank_out)
    idx_ref[...] = idx_out
    gate_ref[...] = gate_out
    rank_ref[...] = rank_out
    carry[...] = carry[...] + hot_all.sum(axis=0, keepdims=True)
    cnt_ref[...] = carry[...].astype(I32)


def _router(logits):
    spec = pl.BlockSpec((ROUTER_TM, LANES), lambda i: (i, 0))
    return pl.pallas_call(
        _router_body,
        grid=(N_TOK // ROUTER_TM,),
        in_specs=[spec],
        out_specs=[spec, spec, spec, pl.BlockSpec((1, LANES), lambda i: (0, 0))],
        out_shape=[
            jax.ShapeDtypeStruct((N_TOK, LANES), I32),
            jax.ShapeDtypeStruct((N_TOK, LANES), F32),
            jax.ShapeDtypeStruct((N_TOK, LANES), I32),
            jax.ShapeDtypeStruct((1, LANES), I32),
        ],
        scratch_shapes=[pltpu.VMEM((1, LANES), F32)],
        compiler_params=_cparams(1),
        name="router",
    )(logits)


SC_CORES = 2
SC_SUBCORES = 16
SC_WORKERS = SC_CORES * SC_SUBCORES
SC_CH = 32


def _sc_mesh():
    return plsc.VectorSubcoreMesh(core_axis_name="c", subcore_axis_name="s",
                                  num_cores=SC_CORES, num_subcores=SC_SUBCORES)


def _sc_worker():
    return lax.axis_index("s") * SC_CORES + lax.axis_index("c")


def _dispatch(tok, pos_km):
    n, d = tok.shape
    n_ch = n // (SC_WORKERS * SC_CH)
    assert n_ch * SC_WORKERS * SC_CH == n and n_ch % 2 == 0

    @functools.partial(
        pl.kernel, mesh=_sc_mesh(), out_type=jax.ShapeDtypeStruct((MOE_ROWS, d), tok.dtype),
        scratch_types=[pltpu.VMEM((n_ch, TOP_K, SC_CH), I32), pltpu.VMEM((2, SC_CH, d), tok.dtype),
                       pltpu.SemaphoreType.DMA((2,)), pltpu.SemaphoreType.DMA((2,))])
    def scatter_rows(tok_hbm, pos_hbm, out_hbm, idx_v, rows_v, load_sem, scat_sem):
        wid = _sc_worker()
        base = wid * (n_ch * SC_CH)
        pltpu.sync_copy(pos_hbm.at[wid], idx_v)

        def load(g, slot):
            rows = pl.ds(pl.multiple_of(base + g * SC_CH, 8), SC_CH)
            return pltpu.make_async_copy(tok_hbm.at[rows], rows_v.at[slot], load_sem.at[slot])

        def scatter(g, k, slot):
            return pltpu.make_async_copy(rows_v.at[slot], out_hbm.at[idx_v.at[g, k]], scat_sem.at[slot])

        load(0, 0).start()

        @pl.loop(0, n_ch, step=2)
        def _(g0):
            for slot in range(2):
                g = g0 + slot
                load(g, slot).wait()

                @pl.when(g + 1 < n_ch)
                def _():
                    load(g + 1, 1 - slot).start()

                for k in range(TOP_K):
                    scatter(g, k, slot).start()
                for k in range(TOP_K):
                    scatter(g, k, slot).wait()

    pos = pos_km.reshape(TOP_K, SC_WORKERS, n_ch, SC_CH).transpose(1, 2, 0, 3)
    return scatter_rows(tok, pos)


def _gather_rows(table, idx):
    b, d = idx.shape[0], table.shape[1]
    n_ch = b // (SC_WORKERS * SC_CH)
    assert n_ch * SC_WORKERS * SC_CH == b and n_ch % 2 == 0

    @functools.partial(
        pl.kernel, mesh=_sc_mesh(), out_type=jax.ShapeDtypeStruct((b, d), table.dtype),
        scratch_types=[pltpu.VMEM((n_ch, SC_CH), I32), pltpu.VMEM((2, SC_CH, d), table.dtype),
                       pltpu.SemaphoreType.DMA((2,))])
    def gather_rows(table_hbm, idx_hbm, out_hbm, idx_v, rows_v, sems):
        wid = _sc_worker()
        base = wid * (n_ch * SC_CH)
        pltpu.sync_copy(idx_hbm.at[wid], idx_v)

        def gather(g, slot):
            return pltpu.make_async_copy(table_hbm.at[idx_v.at[g]], rows_v.at[slot], sems.at[slot])

        def write_out(g, slot):
            rows = pl.ds(pl.multiple_of(base + g * SC_CH, 8), SC_CH)
            pltpu.sync_copy(rows_v.at[slot], out_hbm.at[rows])

        gather(0, 0).start()

        @pl.loop(0, n_ch, step=2)
        def _(g):
            gather(g + 1, 1).start()
            gather(g, 0).wait()
            write_out(g, 0)

            @pl.when(g + 2 < n_ch)
            def _():
                gather(g + 2, 0).start()

            gather(g + 1, 1).wait()
            write_out(g + 1, 1)

    return gather_rows(table, idx.reshape(SC_WORKERS, n_ch, SC_CH))


def _moe_body(te_ref, tv_ref, slot_ref, nxt_ref, x_ref, w1_hbm, b1_ref, w2_hbm, b2_ref, perm_ref, o_ref,
              w1_buf, w2_buf, w1_scr, w2_scr, sem, *, layer):
    t = pl.program_id(0)
    e = te_ref[t]
    slot = slot_ref[t]
    nxt = nxt_ref[t]
    first = jnp.logical_or(t == 0, e != te_ref[jnp.maximum(t - 1, 0)])

    def fetch(expert, s):
        return (pltpu.make_async_copy(w1_hbm.at[layer, expert], w1_buf.at[s], sem.at[s, 0]),
                pltpu.make_async_copy(w2_hbm.at[layer, expert], w2_buf.at[s], sem.at[s, 1]))

    @pl.when(t == 0)
    def _():
        for cp in fetch(e, slot):
            cp.start()

    @pl.when(jnp.logical_and(tv_ref[t] == 1, first))
    def _():
        for cp in fetch(e, slot):
            cp.wait()

        @pl.when(nxt != e)
        def _():
            for cp in fetch(nxt, 1 - slot):
                cp.start()

        w2_scr[...] = w2_buf[slot].astype(BF16)
        half = PERM_W // 2
        for c in range(2 * D_EXPERT // PERM_W):
            blk = _dot(w1_buf[slot, :, c * PERM_W:(c + 1) * PERM_W].astype(BF16), perm_ref[...]).astype(BF16)
            w1_scr[:, c * half:(c + 1) * half] = blk[:, :half]
            w1_scr[:, D_EXPERT + c * half:D_EXPERT + (c + 1) * half] = blk[:, half:]

    @pl.when(tv_ref[t] == 0)
    def _():
        o_ref[...] = jnp.zeros_like(o_ref)

    @pl.when(tv_ref[t] == 1)
    def _():
        x_lo, x_hi = _unpack_rows(x_ref[...])
        x = jnp.concatenate([x_lo.astype(BF16), x_hi.astype(BF16)], axis=-1)
        hh = _dot(x, w1_scr[...]) + b1_ref[...]
        h_glu = jnp.minimum(hh[:, :D_EXPERT], SWIGLU_LIMIT)
        h_lin = jnp.clip(hh[:, D_EXPERT:], -SWIGLU_LIMIT, SWIGLU_LIMIT)
        a = h_glu * _sigmoid(SWIGLU_ALPHA * h_glu) * (h_lin + 1.0)
        o_ref[...] = _pack_rows(_dot(a.astype(BF16), w2_scr[...]) + b2_ref[...])


def _deinterleave_bias_body(b_ref, perm_ref, o_ref):
    half = PERM_W // 2
    for c in range(2 * D_EXPERT // PERM_W):
        r = b_ref[:, c * PERM_W:(c + 1) * PERM_W]
        acc = jnp.zeros(r.shape, F32)
        for _ in range(3):
            piece = r.astype(BF16)
            acc = acc + _dot(piece, perm_ref[...])
            r = r - piece.astype(F32)
        o_ref[:, c * half:(c + 1) * half] = acc[:, :half]
        o_ref[:, D_EXPERT + c * half:D_EXPERT + (c + 1) * half] = acc[:, half:]


def _deinterleave_bias(b1, perm):
    return pl.pallas_call(
        _deinterleave_bias_body,
        out_shape=jax.ShapeDtypeStruct(b1.shape, F32),
        compiler_params=_cparams(0),
        name="deinterleave_bias",
    )(b1, perm)


def _moe_experts(tile_expert, tile_valid, tile_slot, tile_next, xs, w1, b1_split, w2, b2, perm, layer):
    grid_spec = pltpu.PrefetchScalarGridSpec(
        num_scalar_prefetch=4,
        grid=(MOE_TILES,),
        in_specs=[
            pl.BlockSpec((MOE_TM, HALF), lambda t, te, *_: (t, 0)),
            pl.BlockSpec(memory_space=pl.ANY),
            pl.BlockSpec((None, None, 1, 2 * D_EXPERT), lambda t, te, *_: (layer, te[t], 0, 0)),
            pl.BlockSpec(memory_space=pl.ANY),
            pl.BlockSpec((None, None, 1, D_MODEL), lambda t, te, *_: (layer, te[t], 0, 0)),
            pl.BlockSpec((PERM_W, PERM_W), lambda t, te, *_: (0, 0)),
        ],
        out_specs=pl.BlockSpec((MOE_TM, HALF), lambda t, te, *_: (t, 0)),
        scratch_shapes=[
            pltpu.VMEM((2, D_MODEL, 2 * D_EXPERT), F32),
            pltpu.VMEM((2, D_EXPERT, D_MODEL), F32),
            pltpu.VMEM((D_MODEL, 2 * D_EXPERT), BF16),
            pltpu.VMEM((D_EXPERT, D_MODEL), BF16),
            pltpu.SemaphoreType.DMA((2, 2)),
        ],
    )
    return pl.pallas_call(
        functools.partial(_moe_body, layer=layer),
        grid_spec=grid_spec,
        out_shape=jax.ShapeDtypeStruct((MOE_ROWS, HALF), U32),
        compiler_params=_cparams(1),
        name="moe_experts",
    )(tile_expert, tile_valid, tile_slot, tile_next, xs, w1, b1_split, w2, b2, perm)


COMBINE_TM = 256


def _combine_body(*refs, final):
    y_refs = refs[:TOP_K]
    gate_ref, x_ref, mod_ref, g_ref = refs[TOP_K:TOP_K + 4]
    gates = gate_ref[...]
    y_lo, y_hi = (gates[:, 0:1] * h for h in _unpack_rows(y_refs[0][...]))
    for k in range(1, TOP_K):
        lo, hi = _unpack_rows(y_refs[k][...])
        y_lo = y_lo + gates[:, k:k + 1] * lo
        y_hi = y_hi + gates[:, k:k + 1] * hi
    x = x_ref[...] + mod_ref[5:6, :] * jnp.concatenate([y_lo, y_hi], axis=-1)
    if not final:
        refs[-1][...] = x
        return
    x = (x * lax.rsqrt(jnp.mean(x * x, axis=-1, keepdims=True) + EPS)) * g_ref[...]
    prompt_ref, sample_ref = refs[-2:]
    is_prompt = pl.program_id(0) < N_PROMPT // COMBINE_TM

    @pl.when(is_prompt)
    def _():
        prompt_ref[...] = x

    @pl.when(jnp.logical_not(is_prompt))
    def _():
        sample_ref[...] = x


def _combine(y_km, gates, x, mod, g_final, final):
    n_t = N_TOK // COMBINE_TM
    n_pt = N_PROMPT // COMBINE_TM
    y_specs = [pl.BlockSpec((COMBINE_TM, HALF), lambda i, k=k: (k * n_t + i, 0)) for k in range(TOP_K)]
    if final:
        out_specs = [pl.BlockSpec((COMBINE_TM, D_MODEL), lambda i: (jnp.minimum(i, n_pt - 1), 0)),
                     pl.BlockSpec((COMBINE_TM, D_MODEL), lambda i: (jnp.maximum(i - n_pt, 0), 0))]
        out_shape = [jax.ShapeDtypeStruct((N_PROMPT, D_MODEL), F32), jax.ShapeDtypeStruct((N_SAMPLE, D_MODEL), F32)]
    else:
        out_specs = pl.BlockSpec((COMBINE_TM, D_MODEL), lambda i: (i, 0))
        out_shape = jax.ShapeDtypeStruct((N_TOK, D_MODEL), F32)
    return pl.pallas_call(
        functools.partial(_combine_body, final=final),
        grid=(n_t,),
        in_specs=y_specs + [
            pl.BlockSpec((COMBINE_TM, LANES), lambda i: (i, 0)),
            pl.BlockSpec((COMBINE_TM, D_MODEL), lambda i: (i, 0)),
            pl.BlockSpec((None, MOD_ROWS, D_MODEL), lambda i: (_group_of_tile(i, COMBINE_TM), 0, 0)),
            pl.BlockSpec((1, D_MODEL), lambda i: (0, 0)),
        ],
        out_specs=out_specs,
        out_shape=out_shape,
        compiler_params=_cparams(1),
        name="combine",
    )(*([y_km] * TOP_K), gates, x, mod, g_final.reshape(1, D_MODEL))


def _routing_tables(idx, rank, counts):
    counts = counts[0, :N_EXPERTS]
    padded = ((counts + MOE_TM - 1) // MOE_TM) * MOE_TM
    ends = jnp.cumsum(padded)
    offsets = ends - padded
    idx = idx[:, :TOP_K]
    pos = rank[:, :TOP_K] + jnp.sum(
        jnp.where(idx[..., None] == jnp.arange(N_EXPERTS, dtype=I32), offsets, 0), axis=-1)
    pos_km = pos.T.astype(I32)
    tile_start = jnp.arange(MOE_TILES, dtype=I32) * MOE_TM
    tile_valid = (tile_start < ends[-1]).astype(I32)
    tile_expert = jnp.sum((tile_start[:, None] >= ends[None, :]).astype(I32), axis=-1)
    last_expert = jnp.sum((ends[-1] - 1 >= ends).astype(I32))
    tile_expert = jnp.where(tile_valid == 1, tile_expert, last_expert).astype(I32)
    e_ids = jnp.arange(N_EXPERTS, dtype=I32)
    owns = counts > 0
    ordinal = jnp.cumsum(owns.astype(I32)) - 1
    later = jnp.where(owns[None, :] & (e_ids[None, :] > e_ids[:, None]), e_ids[None, :], N_EXPERTS)
    nxt = jnp.min(later, axis=1)
    nxt = jnp.where(nxt == N_EXPERTS, e_ids, nxt)
    onehot = (tile_expert[:, None] == e_ids[None, :]).astype(I32)
    tile_slot = jnp.sum(onehot * (ordinal % 2)[None, :], axis=1).astype(I32)
    tile_next = jnp.sum(onehot * nxt[None, :], axis=1).astype(I32)
    return pos_km, tile_expert, tile_valid, tile_slot, tile_next


def _rope_tables():
    t = np.arange(DEC_SEQ)
    row = (t // GRID_W).astype(np.float32)
    col = (t % GRID_W).astype(np.float32)
    d_axis = RET_DK // 2
    inv = jnp.asarray(ROPE_BASE, F32) ** (-jnp.arange(0, d_axis, 2, dtype=F32) / d_axis)
    ang = jnp.concatenate([row[:, None] * inv, col[:, None] * inv], axis=-1)
    cos = jnp.repeat(jnp.cos(ang), 2, axis=-1)
    sin = jnp.repeat(jnp.sin(ang), 2, axis=-1)
    sign = jnp.asarray(np.tile(np.array([-1.0, 1.0], np.float32), RET_DK // 2))
    return cos, sin * sign


def _deinterleave_matrix():
    p = np.zeros((PERM_W, PERM_W), np.float32)
    half = PERM_W // 2
    for j in range(half):
        p[2 * j, j] = 1.0
        p[2 * j + 1, half + j] = 1.0
    return jnp.asarray(p, BF16)


def kernel(x_prompt, x_sample, state_ret_fwd, state_ret_bwd, cache_na_k, cache_na_v, c, c_ctx, ada_w, ada_b, norm_mix, norm_ffn, norm_final, ret_w_in, ret_w_out, ret_decay_fwd, ret_decay_bwd, na_w_in, na_w_out, na_rpb, moe_w_router, moe_b_router, moe_w1, moe_b1, moe_w2, moe_b2):
    x = (x_prompt.reshape(N_PROMPT, D_MODEL), x_sample.reshape(N_SAMPLE, D_MODEL))

    cond = jnp.zeros((MOD_ROWS, D_MODEL), F32).at[0].set(c_ctx).at[1:1 + DEC_BATCH].set(c)
    mod_all = _adaln(cond, ada_w, ada_b)
    mod_all = mod_all[:, :N_GROUPS].reshape(DEPTH, N_GROUPS, 6, D_MODEL)
    mod_all = jnp.pad(mod_all, ((0, 0), (0, 0), (0, MOD_ROWS - 6), (0, 0)))

    cos, sin_signed = _rope_tables()
    perm = _deinterleave_matrix()
    wr_all = jnp.pad(moe_w_router, ((0, 0), (0, 0), (0, LANES - N_EXPERTS)))
    br_all = jnp.pad(moe_b_router, ((0, 0), (0, LANES - N_EXPERTS))).reshape(DEPTH, 1, LANES)
    b1_split = _deinterleave_bias(moe_b1.reshape(DEPTH * N_EXPERTS, 2 * D_EXPERT), perm)
    b1_split = b1_split.reshape(DEPTH, N_EXPERTS, 1, 2 * D_EXPERT)
    b2_all = moe_b2.reshape(DEPTH, N_EXPERTS, 1, D_MODEL)
    na_tbl = _na_bias_table(na_rpb)
    ret_states = None
    na_caches = None
    for layer in range(DEPTH):
        mod = mod_all[layer]
        j = layer // 2
        if layer % 2 == 0:
            proj = _inproj(x, norm_mix[layer], mod, _cast_layer_bf16(ret_w_in, j), out_dtype=BF16)
            decay = jnp.stack([ret_decay_fwd[j], ret_decay_bwd[j]])
            a_p, s_f, s_b = _ret_prompt(proj, decay, j, ret_states)
            ret_states = (s_f, s_b)
            a_s = _ret_sample(proj, decay, cos, sin_signed, state_ret_fwd, state_ret_bwd, j)
            w_out = ret_w_out
        else:
            proj, ck, cv = _inproj(x, norm_mix[layer], mod, na_w_in, j, j, na_caches)
            na_caches = (ck, cv)
            a_p = _ctx_attn(proj)
            a_s = _na_attn(proj, cache_na_k, cache_na_v, na_tbl, j)
            w_out = na_w_out

        x, tok, logits = _outproj(a_p, a_s, w_out, j, x, norm_ffn[layer], mod, wr_all[layer], br_all[layer])

        idx, gates, rank, counts = _router(logits)
        pos_km, tile_expert, tile_valid, tile_slot, tile_next = _routing_tables(idx, rank, counts)
        xs = _dispatch(tok, pos_km)
        ys = _moe_experts(tile_expert, tile_valid, tile_slot, tile_next, xs, moe_w1, b1_split, moe_w2, b2_all,
                          perm, layer)
        y_km = _gather_rows(ys, pos_km.reshape(-1))
        x = _combine(y_km, gates, x, mod, norm_final, final=(layer == DEPTH - 1))

    y_prompt = x[0].reshape(BATCH, SEQ, D_MODEL)
    y_sample = x[1].reshape(DEC_BATCH, DEC_SEQ, D_MODEL)
    return (y_prompt, y_sample, ret_states[0], ret_states[1],
            jnp.transpose(na_caches[0], (0, 1, 4, 2, 3)), jnp.transpose(na_caches[1], (0, 1, 4, 2, 3)))
```

```python
import functools

import numpy as np
import jax
import jax.numpy as jnp
from jax import lax
from jax.experimental import pallas as pl
from jax.experimental.pallas import tpu as pltpu
from jax.experimental.pallas import tpu_sc as plsc

F32 = jnp.float32
BF16 = jnp.bfloat16
I32 = jnp.int32
U32 = jnp.uint32

D_MODEL = 1024
BATCH = 32
SEQ = 256
DEPTH = 4
DEC_BATCH = 2
DEC_SEQ = 1024
PAST_LEN = 512
GRID_W = 64
EPS = 1e-6
ROPE_BASE = 10000.0
RET_HEADS = 4
RET_DK = D_MODEL // RET_HEADS
RET_DV = 2 * RET_DK
RET_QK_W = RET_HEADS * RET_DK
RET_V_W = RET_HEADS * RET_DV
NA_HEADS = 16
NA_HD = D_MODEL // NA_HEADS
NA_KH = 8
NA_KW = 16
N_EXPERTS = 32
TOP_K = 4
D_EXPERT = D_MODEL
SWIGLU_LIMIT = 7.0
SWIGLU_ALPHA = 1.702
N_RET_LAYERS = (DEPTH + 1) // 2
N_NA_LAYERS = DEPTH // 2

N_PROMPT = BATCH * SEQ
N_SAMPLE = DEC_BATCH * DEC_SEQ
N_TOK = N_PROMPT + N_SAMPLE
N_GROUPS = 1 + DEC_BATCH
MOD_ROWS = 8
LANES = 128

TM = 256
RET_CHUNK = 256
MOE_TM = 256
MOE_ROWS = N_TOK * TOP_K + N_EXPERTS * MOE_TM
MOE_TILES = MOE_ROWS // MOE_TM
ROUTER_TM = 512
PERM_W = 256
VMEM_LIMIT = 56 * 1024 * 1024


def _cparams(n_axes, vmem=VMEM_LIMIT):
    return pltpu.CompilerParams(dimension_semantics=("arbitrary",) * n_axes, vmem_limit_bytes=vmem)


def _group_of_tile(i, tm):
    return jnp.maximum((i * tm) // DEC_SEQ - (N_PROMPT // DEC_SEQ - 1), 0)


def _sigmoid(x):
    return 1.0 / (1.0 + jnp.exp(-x))


def _silu(x):
    return x * _sigmoid(x)


def _log_sigmoid(x):
    return jnp.minimum(x, 0.0) - jnp.log(1.0 + jnp.exp(-jnp.abs(x)))


def _norm_mod(x, g, shift, scale):
    y = x * lax.rsqrt(jnp.mean(x * x, axis=-1, keepdims=True) + EPS)
    return (y * g) * (1.0 + scale) + shift


def _dot(a, b):
    return jnp.dot(a, b, preferred_element_type=F32)


HALF = D_MODEL // 2
HI_MASK = 0xFFFF0000


def _pack_rows(x):
    lo = pltpu.bitcast(x[:, :HALF].astype(BF16).astype(F32), U32)
    hi = pltpu.bitcast(x[:, HALF:].astype(BF16).astype(F32), U32)
    return (lo >> 16) | (hi & jnp.uint32(HI_MASK))


def _unpack_rows(p):
    return pltpu.bitcast(p << 16, F32), pltpu.bitcast(p & jnp.uint32(HI_MASK), F32)


def _dot_nt(a, b):
    return lax.dot_general(a, b, (((1,), (1,)), ((), ())), preferred_element_type=F32)


def _dot_tn(a, b):
    return lax.dot_general(a, b, (((0,), (0,)), ((), ())), preferred_element_type=F32)


ADA_TN = 1536


def _adaln_body(cond_ref, w_ref, b_ref, o_ref):
    s = _silu(cond_ref[...]).astype(BF16)
    o_ref[...] = _dot(s, w_ref[...].astype(BF16)) + b_ref[...]


def _adaln(cond, ada_w, ada_b):
    n_out = 6 * D_MODEL
    return pl.pallas_call(
        _adaln_body,
        grid=(DEPTH, n_out // ADA_TN),
        in_specs=[
            pl.BlockSpec((MOD_ROWS, D_MODEL), lambda l, j: (0, 0)),
            pl.BlockSpec((None, D_MODEL, ADA_TN), lambda l, j: (l, 0, j)),
            pl.BlockSpec((None, 1, ADA_TN), lambda l, j: (l, 0, j)),
        ],
        out_specs=pl.BlockSpec((None, MOD_ROWS, ADA_TN), lambda l, j: (l, 0, j)),
        out_shape=jax.ShapeDtypeStruct((DEPTH, MOD_ROWS, n_out), F32),
        compiler_params=_cparams(2),
        name="adaln",
    )(cond, ada_w, ada_b.reshape(DEPTH, 1, n_out))


CAST_TR = 256


def _cast_body(w_ref, o_ref):
    o_ref[...] = w_ref[...].astype(BF16)


def _cast_layer_bf16(w, j):
    _, r, c = w.shape
    return pl.pallas_call(
        _cast_body,
        grid=(r // CAST_TR,),
        in_specs=[pl.BlockSpec((None, CAST_TR, c), lambda i: (j, i, 0))],
        out_specs=pl.BlockSpec((CAST_TR, c), lambda i: (i, 0)),
        out_shape=jax.ShapeDtypeStruct((r, c), BF16),
        compiler_params=_cparams(1),
        name="cast_bf16",
    )(w)


def _x_inputs(x):
    n_pt = N_PROMPT // TM
    xa, xb, off = (x[0], x[1], 0) if isinstance(x, tuple) else (x, x, n_pt)
    specs = [pl.BlockSpec((TM, D_MODEL), lambda i: (jnp.minimum(i, n_pt - 1), 0)),
             pl.BlockSpec((TM, D_MODEL), lambda i: (off + jnp.maximum(i - n_pt, 0), 0))]
    return specs, [xa, xb]


def _x_tile(xa_ref, xb_ref):
    return jnp.where(pl.program_id(0) < N_PROMPT // TM, xa_ref[...], xb_ref[...])


def _weight_input(w, layer):
    if layer is None:
        return pl.BlockSpec(w.shape, lambda i: (0, 0)), []
    _, r, c = w.shape
    spec = pl.BlockSpec((None, r, c), lambda i: (layer, 0, 0), pipeline_mode=pl.Buffered(1))
    return spec, [pltpu.VMEM((r, c), BF16)]


def _weight_ref(w_ref, scratch):
    if not scratch:
        return w_ref

    @pl.when(pl.program_id(0) == 0)
    def _():
        scratch[0][...] = w_ref[...].astype(BF16)

    return scratch[0]


def _inproj_body(xa_ref, xb_ref, g_ref, mod_ref, w_ref, *rest, cache_layer, first_cache, n_scratch):
    scratch = rest[len(rest) - n_scratch:]
    rest = rest[:len(rest) - n_scratch]
    o_ref = rest[-3] if cache_layer is not None else rest[-1]
    i = pl.program_id(0)
    w = _weight_ref(w_ref, scratch)
    h = _norm_mod(_x_tile(xa_ref, xb_ref), g_ref[...], mod_ref[0:1, :], mod_ref[1:2, :]).astype(BF16)
    for c in range(o_ref.shape[1] // D_MODEL):
        sl = slice(c * D_MODEL, (c + 1) * D_MODEL)
        r = _dot(h, w[:, sl])
        o_ref[:, sl] = r.astype(o_ref.dtype)
        if cache_layer is not None and c >= 1:
            cache_ref = rest[-3 + c]

            @pl.when(i < BATCH)
            def _(r=r, cache_ref=cache_ref):
                r_t = r.T.reshape(NA_HEADS, NA_HD, SEQ)
                if first_cache:
                    for l in range(N_NA_LAYERS):
                        cache_ref[l] = r_t if l == cache_layer else jnp.zeros_like(r_t)
                else:
                    cache_ref[...] = r_t


def _inproj(x, g, mod, w, w_layer=None, cache_layer=None, prev_caches=None, out_dtype=F32):
    n_out = w.shape[-1]
    x_specs, x_args = _x_inputs(x)
    w_spec, scratch = _weight_input(w, w_layer)
    in_specs = x_specs + [
        pl.BlockSpec((1, D_MODEL), lambda i: (0, 0)),
        pl.BlockSpec((None, MOD_ROWS, D_MODEL), lambda i: (_group_of_tile(i, TM), 0, 0)),
        w_spec,
    ]
    out_specs = [pl.BlockSpec((TM, n_out), lambda i: (i, 0))]
    out_shape = [jax.ShapeDtypeStruct((N_TOK, n_out), out_dtype)]
    args = x_args + [g.reshape(1, D_MODEL), mod, w]
    aliases = {}
    first_cache = prev_caches is None
    if cache_layer is not None:
        assert TM == SEQ and n_out == 3 * D_MODEL
        cache = jax.ShapeDtypeStruct((BATCH, N_NA_LAYERS, NA_HEADS, NA_HD, SEQ), F32)
        if first_cache:
            spec = pl.BlockSpec((None, N_NA_LAYERS, NA_HEADS, NA_HD, SEQ),
                                lambda i: (jnp.minimum(i, BATCH - 1), 0, 0, 0, 0))
        else:
            spec = pl.BlockSpec((None, None, NA_HEADS, NA_HD, SEQ),
                                lambda i: (jnp.minimum(i, BATCH - 1), cache_layer, 0, 0, 0))
            in_specs += [pl.BlockSpec(memory_space=pl.ANY)] * 2
            aliases = {len(args): 1, len(args) + 1: 2}
            args += list(prev_caches)
        out_specs += [spec, spec]
        out_shape += [cache, cache]
    outs = pl.pallas_call(
        functools.partial(_inproj_body, cache_layer=cache_layer, first_cache=first_cache, n_scratch=len(scratch)),
        grid=(N_TOK // TM,),
        in_specs=in_specs,
        out_specs=out_specs,
        out_shape=out_shape,
        scratch_shapes=scratch,
        input_output_aliases=aliases,
        compiler_params=_cparams(1),
        name="inproj",
    )(*args)
    return outs[0] if cache_layer is None else outs


def _decay_terms(dec_ref, h, c):
    lgf = _log_sigmoid(jnp.zeros((c, c), F32) + dec_ref[0, h])
    lgb = _log_sigmoid(jnp.zeros((c, c), F32) + dec_ref[1, h])
    row = lax.broadcasted_iota(I32, (c, c), 0)
    col = lax.broadcasted_iota(I32, (c, c), 1)
    diff = (row - col).astype(F32)
    lower = diff >= 0
    upper = diff <= 0
    mask = (jnp.where(lower, jnp.exp(jnp.where(lower, diff, 0.0) * lgf), 0.0)
            + jnp.where(upper, jnp.exp(jnp.where(upper, -diff, 0.0) * lgb), 0.0))
    pos = lax.broadcasted_iota(I32, (c, 1), 0).astype(F32)
    lgf1 = _log_sigmoid(jnp.zeros((c, 1), F32) + dec_ref[0, h])
    lgb1 = _log_sigmoid(jnp.zeros((c, 1), F32) + dec_ref[1, h])
    lgf0 = _log_sigmoid(jnp.zeros((1, 1), F32) + dec_ref[0, h])
    lgb0 = _log_sigmoid(jnp.zeros((1, 1), F32) + dec_ref[1, h])
    terms = dict(
        mask=mask,
        q_f=jnp.exp((pos + 1.0) * lgf1), k_f=jnp.exp((c - 1.0 - pos) * lgf1), c_f=jnp.exp(c * lgf0),
        q_b=jnp.exp((c - pos) * lgb1), k_b=jnp.exp(pos * lgb1), c_b=jnp.exp(c * lgb0),
    )
    return terms


def _group_norm_gate(o, g):
    o = o * lax.rsqrt(jnp.mean(o * o, axis=-1, keepdims=True) + EPS)
    return (_silu(g) * o).astype(BF16)


def _ret_prompt_body(dec_ref, q_ref, k_ref, v_ref, g_ref, *rest, layer, first):
    o_ref, sf_ref, sb_ref = rest[-3:]
    for h in range(RET_HEADS):
        t = _decay_terms(dec_ref, h, SEQ)
        qk_cols = slice(h * RET_DK, (h + 1) * RET_DK)
        v_cols = slice(h * RET_DV, (h + 1) * RET_DV)
        q = q_ref[:, qk_cols].astype(F32)
        k = k_ref[:, qk_cols].astype(F32) * (RET_DK ** -0.5)
        v = v_ref[:, v_cols].astype(BF16)
        scores = _dot_nt(q.astype(BF16), k.astype(BF16)) * t["mask"]
        o = _dot(scores.astype(BF16), v)
        o_ref[:, v_cols] = _group_norm_gate(o, g_ref[:, v_cols].astype(F32))
        for ref, k_dec in ((sf_ref, t["k_f"]), (sb_ref, t["k_b"])):
            s = _dot_tn((k * k_dec).astype(BF16), v)
            if first:
                for l in range(N_RET_LAYERS):
                    ref[l, h] = s if l == layer else jnp.zeros_like(s)
            else:
                ref[h] = s


def _ret_prompt(proj, decay, layer, prev_states=None):
    assert 2 * RET_QK_W == RET_V_W
    state = jax.ShapeDtypeStruct((BATCH, N_RET_LAYERS, RET_HEADS, RET_DK, RET_DV), F32)
    in_specs = [
        pl.BlockSpec(memory_space=pltpu.SMEM),
        pl.BlockSpec((SEQ, RET_QK_W), lambda b: (b, 0)),
        pl.BlockSpec((SEQ, RET_QK_W), lambda b: (b, 1)),
        pl.BlockSpec((SEQ, RET_V_W), lambda b: (b, 1)),
        pl.BlockSpec((SEQ, RET_V_W), lambda b: (b, 2)),
    ]
    args = [decay, proj, proj, proj, proj]
    first = prev_states is None
    aliases = {}
    if first:
        s_spec = pl.BlockSpec((None, N_RET_LAYERS, RET_HEADS, RET_DK, RET_DV), lambda b: (b, 0, 0, 0, 0))
    else:
        s_spec = pl.BlockSpec((None, None, RET_HEADS, RET_DK, RET_DV), lambda b: (b, layer, 0, 0, 0))
        in_specs += [pl.BlockSpec(memory_space=pl.ANY)] * 2
        aliases = {len(args): 1, len(args) + 1: 2}
        args += list(prev_states)
    return pl.pallas_call(
        functools.partial(_ret_prompt_body, layer=layer, first=first),
        grid=(BATCH,),
        in_specs=in_specs,
        out_specs=[pl.BlockSpec((SEQ, RET_V_W), lambda b: (b, 0)), s_spec, s_spec],
        out_shape=[jax.ShapeDtypeStruct((N_PROMPT, RET_V_W), BF16), state, state],
        input_output_aliases=aliases,
        compiler_params=_cparams(1),
        name="ret_prompt",
    )(*args)


def _rope(x, cos, sin_signed):
    n = x.shape[-1]
    lane = lax.broadcasted_iota(I32, x.shape, 1)
    swapped = jnp.where(lane % 2 == 0, pltpu.roll(x, n - 1, 1), pltpu.roll(x, 1, 1))
    return x * cos + swapped * sin_signed


def _ret_sample_body(dec_ref, q_ref, k_ref, v_ref, g_ref, cos_ref, sin_ref, s0f_ref, s0b_ref, o_ref,
                     sf_scr, sb_scr):
    h = pl.program_id(1)
    c = RET_CHUNK
    n_chunks = DEC_SEQ // c
    t = _decay_terms(dec_ref, h, c)

    def chunk(ref, i):
        return ref[i * c:(i + 1) * c, :].astype(F32)

    def qk(i):
        q = _rope(chunk(q_ref, i), chunk(cos_ref, i), chunk(sin_ref, i))
        k = _rope(chunk(k_ref, i) * (RET_DK ** -0.5), chunk(cos_ref, i), chunk(sin_ref, i))
        return q, k

    s = s0f_ref[...]
    for i in range(n_chunks):
        sf_scr[i] = s
        if i + 1 < n_chunks:
            _, k = qk(i)
            s = s * t["c_f"] + _dot_tn((k * t["k_f"]).astype(BF16), chunk(v_ref, i).astype(BF16))
    s = s0b_ref[...]
    for i in reversed(range(n_chunks)):
        sb_scr[i] = s
        if i > 0:
            _, k = qk(i)
            s = s * t["c_b"] + _dot_tn((k * t["k_b"]).astype(BF16), chunk(v_ref, i).astype(BF16))

    for i in range(n_chunks):
        q, k = qk(i)
        scores = _dot_nt(q.astype(BF16), k.astype(BF16)) * t["mask"]
        o = _dot(scores.astype(BF16), chunk(v_ref, i).astype(BF16))
        o = o + _dot((q * t["q_f"]).astype(BF16), sf_scr[i].astype(BF16))
        o = o + _dot((q * t["q_b"]).astype(BF16), sb_scr[i].astype(BF16))
        o_ref[i * c:(i + 1) * c, :] = _group_norm_gate(o, chunk(g_ref, i))


def _ret_sample(proj, decay, cos, sin_signed, s0f, s0b, j):
    kq = RET_QK_W // RET_DK
    kv = 2 * RET_QK_W // RET_DV
    kg = kv + RET_HEADS
    rb = N_PROMPT // DEC_SEQ
    n_chunks = DEC_SEQ // RET_CHUNK
    return pl.pallas_call(
        _ret_sample_body,
        grid=(DEC_BATCH, RET_HEADS),
        in_specs=[
            pl.BlockSpec(memory_space=pltpu.SMEM),
            pl.BlockSpec((DEC_SEQ, RET_DK), lambda b, h: (rb + b, h)),
            pl.BlockSpec((DEC_SEQ, RET_DK), lambda b, h: (rb + b, kq + h)),
            pl.BlockSpec((DEC_SEQ, RET_DV), lambda b, h: (rb + b, kv + h)),
            pl.BlockSpec((DEC_SEQ, RET_DV), lambda b, h: (rb + b, kg + h)),
            pl.BlockSpec((DEC_SEQ, RET_DK), lambda b, h: (0, 0)),
            pl.BlockSpec((DEC_SEQ, RET_DK), lambda b, h: (0, 0)),
            pl.BlockSpec((None, None, None, RET_DK, RET_DV), lambda b, h: (b, j, h, 0, 0)),
            pl.BlockSpec((None, None, None, RET_DK, RET_DV), lambda b, h: (b, j, h, 0, 0)),
        ],
        out_specs=pl.BlockSpec((DEC_SEQ, RET_DV), lambda b, h: (b, h)),
        out_shape=jax.ShapeDtypeStruct((N_SAMPLE, RET_V_W), BF16),
        scratch_shapes=[pltpu.VMEM((n_chunks, RET_DK, RET_DV), F32),
                        pltpu.VMEM((n_chunks, RET_DK, RET_DV), F32)],
        compiler_params=_cparams(2),
        name="ret_sample",
    )(decay, proj, proj, proj, proj, cos, sin_signed, s0f, s0b)


NA_HPB = LANES // NA_HD


def _head_lanes(rows, hh):
    lane = lax.broadcasted_iota(I32, (rows, LANES), 1)
    return jnp.logical_and(lane >= hh * NA_HD, lane < (hh + 1) * NA_HD)


def _with_ones(v):
    return jnp.concatenate([v, jnp.ones(v.shape, v.dtype)], axis=-1)


def _ctx_attn_body(q_ref, k_ref, v_ref, o_ref):
    outs = []
    for p in range(NA_HEADS // NA_HPB):
        sl = slice(p * LANES, (p + 1) * LANES)
        q2 = q_ref[:, sl] * (NA_HD ** -0.5)
        k2 = k_ref[:, sl].astype(BF16)
        v2 = _with_ones(v_ref[:, sl].astype(BF16))
        o_pair = None
        for hh in range(NA_HPB):
            mine = _head_lanes(SEQ, hh)
            s = _dot_nt(jnp.where(mine, q2, 0.0).astype(BF16), k2)
            e = jnp.exp(s - s.max(axis=-1, keepdims=True)).astype(BF16)
            acc = _dot(e, v2)
            o_h = acc[:, :LANES] / acc[:, LANES:]
            o_pair = o_h if o_pair is None else jnp.where(mine, o_h, o_pair)
        outs.append(o_pair)
    o_ref[...] = jnp.concatenate(outs, axis=-1).astype(BF16)


def _ctx_attn(proj):
    return pl.pallas_call(
        _ctx_attn_body,
        grid=(BATCH,),
        in_specs=[
            pl.BlockSpec((SEQ, D_MODEL), lambda b: (b, 0)),
            pl.BlockSpec((SEQ, D_MODEL), lambda b: (b, 1)),
            pl.BlockSpec((SEQ, D_MODEL), lambda b: (b, 2)),
        ],
        out_specs=pl.BlockSpec((SEQ, D_MODEL), lambda b: (b, 0)),
        out_shape=jax.ShapeDtypeStruct((N_PROMPT, D_MODEL), BF16),
        compiler_params=_cparams(1),
        name="ctx_attn",
    )(proj, proj, proj)


NA_QT = 256
NA_WIN = NA_KH * GRID_W


def _na_attn_body(q_ref, k_ref, v_ref, ck_ref, cv_ref, tbl_ref, o_ref):
    t = pl.program_id(1)
    rows_per_tile = NA_QT // GRID_W
    n_rows = DEC_SEQ // GRID_W
    q2 = q_ref[...] * (NA_HD ** -0.5)
    q = [jnp.where(_head_lanes(NA_QT, hh), q2, 0.0).astype(BF16) for hh in range(NA_HPB)]
    s_ctx = _dot_nt(jnp.concatenate(q, axis=0), ck_ref[...].astype(BF16))
    acc_win = []
    e_ctx = []
    for u in range(rows_per_tile):
        qr = t * rows_per_tile + u
        rs = jnp.clip(qr - NA_KH // 2, 0, n_rows - NA_KH)
        i0 = rs - qr + (NA_KH - 1)
        start = pl.multiple_of(rs * GRID_W, GRID_W)
        kw = k_ref[pl.ds(start, NA_WIN), :].astype(BF16)
        vw = _with_ones(v_ref[pl.ds(start, NA_WIN), :].astype(BF16))
        rows = slice(u * GRID_W, (u + 1) * GRID_W)
        bias = jnp.concatenate(
            [jnp.concatenate([tbl_ref[hh, i0 + 2 * m] for m in range(NA_KH // 2)], axis=-1) for hh in range(NA_HPB)],
            axis=0)
        sw = _dot_nt(jnp.concatenate([q[hh][rows] for hh in range(NA_HPB)], axis=0), kw) + bias
        sc = jnp.concatenate([s_ctx[hh * NA_QT + u * GRID_W:hh * NA_QT + (u + 1) * GRID_W] for hh in range(NA_HPB)],
                             axis=0)
        m = jnp.maximum(sw.max(axis=-1, keepdims=True), sc.max(axis=-1, keepdims=True))
        acc_win.append(_dot(jnp.exp(sw - m).astype(BF16), vw))
        e_ctx.append(jnp.exp(sc - m).astype(BF16))
    acc = jnp.concatenate(acc_win, axis=0) + _dot(jnp.concatenate(e_ctx, axis=0), _with_ones(cv_ref[...].astype(BF16)))
    o = acc[:, :LANES] / acc[:, LANES:]
    outs = []
    for u in range(rows_per_tile):
        base = u * NA_HPB * GRID_W
        o_u = o[base:base + GRID_W]
        for hh in range(1, NA_HPB):
            o_u = jnp.where(_head_lanes(GRID_W, hh), o[base + hh * GRID_W:base + (hh + 1) * GRID_W], o_u)
        outs.append(o_u)
    o_ref[...] = jnp.concatenate(outs, axis=0).astype(BF16)


def _na_attn(proj, cache_k, cache_v, tbl, j):
    n_qt = DEC_SEQ // NA_QT
    rb_q = N_PROMPT // NA_QT
    rb_kv = N_PROMPT // DEC_SEQ
    cb = D_MODEL // LANES
    ck = cache_k.reshape(DEC_BATCH, -1, PAST_LEN, D_MODEL)
    cv = cache_v.reshape(DEC_BATCH, -1, PAST_LEN, D_MODEL)
    return pl.pallas_call(
        _na_attn_body,
        grid=(cb, n_qt, DEC_BATCH),
        in_specs=[
            pl.BlockSpec((NA_QT, LANES), lambda p, t, b: (rb_q + b * n_qt + t, p)),
            pl.BlockSpec((DEC_SEQ, LANES), lambda p, t, b: (rb_kv + b, cb + p)),
            pl.BlockSpec((DEC_SEQ, LANES), lambda p, t, b: (rb_kv + b, 2 * cb + p)),
            pl.BlockSpec((None, None, PAST_LEN, LANES), lambda p, t, b: (b, j, 0, p)),
            pl.BlockSpec((None, None, PAST_LEN, LANES), lambda p, t, b: (b, j, 0, p)),
            pl.BlockSpec((None, NA_HPB, 2 * NA_KH - 2, GRID_W, 2 * GRID_W), lambda p, t, b: (j, p, 0, 0, 0)),
        ],
        out_specs=pl.BlockSpec((NA_QT, LANES), lambda p, t, b: (b * n_qt + t, p)),
        out_shape=jax.ShapeDtypeStruct((N_SAMPLE, D_MODEL), BF16),
        compiler_params=_cparams(3),
        name="na_attn",
    )(proj, proj, proj, ck, cv, tbl)


def _na_bias_table(rpb):
    c = np.arange(GRID_W)
    cs = np.clip(c - NA_KW // 2, 0, GRID_W - NA_KW)
    col_valid = (c[None, :] >= cs[:, None]) & (c[None, :] < cs[:, None] + NA_KW)
    col_off = np.clip(c[None, :] - c[:, None] + NA_KW - 1, 0, 2 * NA_KW - 2)
    sel_c = (col_off[..., None] == np.arange(2 * NA_KW - 1)).astype(np.float32)
    t = jnp.einsum("lhij,qkj->lhiqk", rpb, sel_c, precision=lax.Precision.HIGHEST)
    t = jnp.where(col_valid, t, -jnp.inf)
    return jnp.concatenate([t[:, :, :-1], t[:, :, 1:]], axis=-1)


def _outproj_body(ap_ref, as_ref, w_ref, xa_ref, xb_ref, g_ref, mod_ref, wr_ref, br_ref, xo_ref, tok_ref, lg_ref,
                  *scratch):
    i = pl.program_id(0)
    a = jnp.where(i < N_PROMPT // TM, ap_ref[...], as_ref[...])
    x = _x_tile(xa_ref, xb_ref) + mod_ref[2:3, :] * _dot(a, _weight_ref(w_ref, scratch)[...])
    xo_ref[...] = x
    tok = _norm_mod(x, g_ref[...], mod_ref[3:4, :], mod_ref[4:5, :])
    tok_ref[...] = _pack_rows(tok)
    hi = tok.astype(BF16)
    lo = (tok - hi.astype(F32)).astype(BF16)
    wr = wr_ref[...]
    wr_hi = wr.astype(BF16)
    wr_lo = (wr - wr_hi.astype(F32)).astype(BF16)
    hi_terms = _dot(hi, jnp.concatenate([wr_hi, wr_lo], axis=-1))
    lg_ref[...] = hi_terms[:, :LANES] + _dot(lo, wr_hi) + hi_terms[:, LANES:] + br_ref[...]


def _outproj(a_prompt, a_sample, w, w_layer, x, g, mod, wr, br):
    k_in = a_prompt.shape[1]
    n_pt = N_PROMPT // TM
    x_specs, x_args = _x_inputs(x)
    w_spec, scratch = _weight_input(w, w_layer)
    return pl.pallas_call(
        _outproj_body,
        grid=(N_TOK // TM,),
        in_specs=[
            pl.BlockSpec((TM, k_in), lambda i: (jnp.minimum(i, n_pt - 1), 0)),
            pl.BlockSpec((TM, k_in), lambda i: (jnp.maximum(i - n_pt, 0), 0)),
            w_spec,
            *x_specs,
            pl.BlockSpec((1, D_MODEL), lambda i: (0, 0)),
            pl.BlockSpec((None, MOD_ROWS, D_MODEL), lambda i: (_group_of_tile(i, TM), 0, 0)),
            pl.BlockSpec((D_MODEL, LANES), lambda i: (0, 0)),
            pl.BlockSpec((1, LANES), lambda i: (0, 0)),
        ],
        out_specs=[
            pl.BlockSpec((TM, D_MODEL), lambda i: (i, 0)),
            pl.BlockSpec((TM, HALF), lambda i: (i, 0)),
            pl.BlockSpec((TM, LANES), lambda i: (i, 0)),
        ],
        out_shape=[
            jax.ShapeDtypeStruct((N_TOK, D_MODEL), F32),
            jax.ShapeDtypeStruct((N_TOK, HALF), U32),
            jax.ShapeDtypeStruct((N_TOK, LANES), F32),
        ],
        scratch_shapes=scratch,
        compiler_params=_cparams(1),
        name="outproj",
    )(a_prompt, a_sample, w, *x_args, g.reshape(1, D_MODEL), mod, wr, br)


def _router_body(lg_ref, idx_ref, gate_ref, rank_ref, cnt_ref, carry):
    i = pl.program_id(0)

    @pl.when(i == 0)
    def _():
        carry[...] = jnp.zeros_like(carry)

    lane = lax.broadcasted_iota(I32, (ROUTER_TM, LANES), 1)
    l = jnp.where(lane < N_EXPERTS, lg_ref[...], -jnp.inf)
    vals, idxs, hots = [], [], []
    for _ in range(TOP_K):
        m = l.max(axis=-1, keepdims=True)
        idx = jnp.where(l == m, lane, LANES).min(axis=-1, keepdims=True)
        hot = lane == idx
        l = jnp.where(hot, -jnp.inf, l)
        vals.append(m)
        idxs.append(idx)
        hots.append(hot)
    es = [jnp.exp(v - vals[0]) for v in vals]
    denom = es[0] + es[1] + es[2] + es[3]
    hot_all = (hots[0] | hots[1] | hots[2] | hots[3]).astype(F32)
    r = lax.broadcasted_iota(I32, (ROUTER_TM, ROUTER_TM), 0)
    c = lax.broadcasted_iota(I32, (ROUTER_TM, ROUTER_TM), 1)
    before = _dot((c < r).astype(BF16), hot_all.astype(BF16)) + carry[...]
    idx_out = jnp.zeros((ROUTER_TM, LANES), I32)
    gate_out = jnp.zeros((ROUTER_TM, LANES), F32)
    rank_out = jnp.zeros((ROUTER_TM, LANES), I32)
    for k in range(TOP_K):
        rank_k = jnp.where(hots[k], before, 0.0).sum(axis=-1, keepdims=True).astype(I32)
        idx_out = jnp.where(lane == k, idxs[k], idx_out)
        gate_out = jnp.where(lane == k, es[k] / denom, gate_out)
        rank_out = jnp.where(lane == k, rank_k, rank_out)
    idx_ref[...] = idx_out
    gate_ref[...] = gate_out
    rank_ref[...] = rank_out
    carry[...] = carry[...] + hot_all.sum(axis=0, keepdims=True)
    cnt_ref[...] = carry[...].astype(I32)


def _router(logits):
    spec = pl.BlockSpec((ROUTER_TM, LANES), lambda i: (i, 0))
    return pl.pallas_call(
        _router_body,
        grid=(N_TOK // ROUTER_TM,),
        in_specs=[spec],
        out_specs=[spec, spec, spec, pl.BlockSpec((1, LANES), lambda i: (0, 0))],
        out_shape=[
            jax.ShapeDtypeStruct((N_TOK, LANES), I32),
            jax.ShapeDtypeStruct((N_TOK, LANES), F32),
            jax.ShapeDtypeStruct((N_TOK, LANES), I32),
            jax.ShapeDtypeStruct((1, LANES), I32),
        ],
        scratch_shapes=[pltpu.VMEM((1, LANES), F32)],
        compiler_params=_cparams(1),
        name="router",
    )(logits)


SC_CORES = 2
SC_SUBCORES = 16
SC_WORKERS = SC_CORES * SC_SUBCORES
SC_CH = 32


def _sc_mesh():
    return plsc.VectorSubcoreMesh(core_axis_name="c", subcore_axis_name="s",
                                  num_cores=SC_CORES, num_subcores=SC_SUBCORES)


def _sc_worker():
    return lax.axis_index("s") * SC_CORES + lax.axis_index("c")


def _dispatch(tok, pos_km):
    n, d = tok.shape
    n_ch = n // (SC_WORKERS * SC_CH)
    assert n_ch * SC_WORKERS * SC_CH == n and n_ch % 2 == 0

    @functools.partial(
        pl.kernel, mesh=_sc_mesh(), out_type=jax.ShapeDtypeStruct((MOE_ROWS, d), tok.dtype),
        scratch_types=[pltpu.VMEM((n_ch, TOP_K, SC_CH), I32), pltpu.VMEM((2, SC_CH, d), tok.dtype),
                       pltpu.SemaphoreType.DMA((2,)), pltpu.SemaphoreType.DMA((2,))])
    def scatter_rows(tok_hbm, pos_hbm, out_hbm, idx_v, rows_v, load_sem, scat_sem):
        wid = _sc_worker()
        base = wid * (n_ch * SC_CH)
        pltpu.sync_copy(pos_hbm.at[wid], idx_v)

        def load(g, slot):
            rows = pl.ds(pl.multiple_of(base + g * SC_CH, 8), SC_CH)
            return pltpu.make_async_copy(tok_hbm.at[rows], rows_v.at[slot], load_sem.at[slot])

        def scatter(g, k, slot):
            return pltpu.make_async_copy(rows_v.at[slot], out_hbm.at[idx_v.at[g, k]], scat_sem.at[slot])

        load(0, 0).start()

        @pl.loop(0, n_ch, step=2)
        def _(g0):
            for slot in range(2):
                g = g0 + slot
                load(g, slot).wait()

                @pl.when(g + 1 < n_ch)
                def _():
                    load(g + 1, 1 - slot).start()

                for k in range(TOP_K):
                    scatter(g, k, slot).start()
                for k in range(TOP_K):
                    scatter(g, k, slot).wait()

    pos = pos_km.reshape(TOP_K, SC_WORKERS, n_ch, SC_CH).transpose(1, 2, 0, 3)
    return scatter_rows(tok, pos)


def _gather_rows(table, idx):
    b, d = idx.shape[0], table.shape[1]
    n_ch = b // (SC_WORKERS * SC_CH)
    assert n_ch * SC_WORKERS * SC_CH == b and n_ch % 2 == 0

    @functools.partial(
        pl.kernel, mesh=_sc_mesh(), out_type=jax.ShapeDtypeStruct((b, d), table.dtype),
        scratch_types=[pltpu.VMEM((n_ch, SC_CH), I32), pltpu.VMEM((2, SC_CH, d), table.dtype),
                       pltpu.SemaphoreType.DMA((2,))])
    def gather_rows(table_hbm, idx_hbm, out_hbm, idx_v, rows_v, sems):
        wid = _sc_worker()
        base = wid * (n_ch * SC_CH)
        pltpu.sync_copy(idx_hbm.at[wid], idx_v)

        def gather(g, slot):
            return pltpu.make_async_copy(table_hbm.at[idx_v.at[g]], rows_v.at[slot], sems.at[slot])

        def write_out(g, slot):
            rows = pl.ds(pl.multiple_of(base + g * SC_CH, 8), SC_CH)
            pltpu.sync_copy(rows_v.at[slot], out_hbm.at[rows])

        gather(0, 0).start()

        @pl.loop(0, n_ch, step=2)
        def _(g):
            gather(g + 1, 1).start()
            gather(g, 0).wait()
            write_out(g, 0)

            @pl.when(g + 2 < n_ch)
            def _():
                gather(g + 2, 0).start()

            gather(g + 1, 1).wait()
            write_out(g + 1, 1)

    return gather_rows(table, idx.reshape(SC_WORKERS, n_ch, SC_CH))


def _moe_body(te_ref, tv_ref, slot_ref, nxt_ref, x_ref, w1_hbm, b1_ref, w2_hbm, b2_ref, perm_ref, o_ref,
              w1_buf, w2_buf, w1_scr, w2_scr, sem, *, layer):
    t = pl.program_id(0)
    e = te_ref[t]
    slot = slot_ref[t]
    nxt = nxt_ref[t]
    first = jnp.logical_or(t == 0, e != te_ref[jnp.maximum(t - 1, 0)])

    def fetch(expert, s):
        return (pltpu.make_async_copy(w1_hbm.at[layer, expert], w1_buf.at[s], sem.at[s, 0]),
                pltpu.make_async_copy(w2_hbm.at[layer, expert], w2_buf.at[s], sem.at[s, 1]))

    @pl.when(t == 0)
    def _():
        for cp in fetch(e, slot):
            cp.start()

    @pl.when(jnp.logical_and(tv_ref[t] == 1, first))
    def _():
        for cp in fetch(e, slot):
            cp.wait()

        @pl.when(nxt != e)
        def _():
            for cp in fetch(nxt, 1 - slot):
                cp.start()

        w2_scr[...] = w2_buf[slot].astype(BF16)
        half = PERM_W // 2
        for c in range(2 * D_EXPERT // PERM_W):
            blk = _dot(w1_buf[slot, :, c * PERM_W:(c + 1) * PERM_W].astype(BF16), perm_ref[...]).astype(BF16)
            w1_scr[:, c * half:(c + 1) * half] = blk[:, :half]
            w1_scr[:, D_EXPERT + c * half:D_EXPERT + (c + 1) * half] = blk[:, half:]

    @pl.when(tv_ref[t] == 0)
    def _():
        o_ref[...] = jnp.zeros_like(o_ref)

    @pl.when(tv_ref[t] == 1)
    def _():
        x_lo, x_hi = _unpack_rows(x_ref[...])
        x = jnp.concatenate([x_lo.astype(BF16), x_hi.astype(BF16)], axis=-1)
        hh = _dot(x, w1_scr[...]) + b1_ref[...]
        h_glu = jnp.minimum(hh[:, :D_EXPERT], SWIGLU_LIMIT)
        h_lin = jnp.clip(hh[:, D_EXPERT:], -SWIGLU_LIMIT, SWIGLU_LIMIT)
        a = h_glu * _sigmoid(SWIGLU_ALPHA * h_glu) * (h_lin + 1.0)
        o_ref[...] = _pack_rows(_dot(a.astype(BF16), w2_scr[...]) + b2_ref[...])


def _deinterleave_bias_body(b_ref, perm_ref, o_ref):
    half = PERM_W // 2
    for c in range(2 * D_EXPERT // PERM_W):
        r = b_ref[:, c * PERM_W:(c + 1) * PERM_W]
        acc = jnp.zeros(r.shape, F32)
        for _ in range(3):
            piece = r.astype(BF16)
            acc = acc + _dot(piece, perm_ref[...])
            r = r - piece.astype(F32)
        o_ref[:, c * half:(c + 1) * half] = acc[:, :half]
        o_ref[:, D_EXPERT + c * half:D_EXPERT + (c + 1) * half] = acc[:, half:]


def _deinterleave_bias(b1, perm):
    return pl.pallas_call(
        _deinterleave_bias_body,
        out_shape=jax.ShapeDtypeStruct(b1.shape, F32),
        compiler_params=_cparams(0),
        name="deinterleave_bias",
    )(b1, perm)


def _moe_experts(tile_expert, tile_valid, tile_slot, tile_next, xs, w1, b1_split, w2, b2, perm, layer):
    grid_spec = pltpu.PrefetchScalarGridSpec(
        num_scalar_prefetch=4,
        grid=(MOE_TILES,),
        in_specs=[
            pl.BlockSpec((MOE_TM, HALF), lambda t, te, *_: (t, 0)),
            pl.BlockSpec(memory_space=pl.ANY),
            pl.BlockSpec((None, None, 1, 2 * D_EXPERT), lambda t, te, *_: (layer, te[t], 0, 0)),
            pl.BlockSpec(memory_space=pl.ANY),
            pl.BlockSpec((None, None, 1, D_MODEL), lambda t, te, *_: (layer, te[t], 0, 0)),
            pl.BlockSpec((PERM_W, PERM_W), lambda t, te, *_: (0, 0)),
        ],
        out_specs=pl.BlockSpec((MOE_TM, HALF), lambda t, te, *_: (t, 0)),
        scratch_shapes=[
            pltpu.VMEM((2, D_MODEL, 2 * D_EXPERT), F32),
            pltpu.VMEM((2, D_EXPERT, D_MODEL), F32),
            pltpu.VMEM((D_MODEL, 2 * D_EXPERT), BF16),
            pltpu.VMEM((D_EXPERT, D_MODEL), BF16),
            pltpu.SemaphoreType.DMA((2, 2)),
        ],
    )
    return pl.pallas_call(
        functools.partial(_moe_body, layer=layer),
        grid_spec=grid_spec,
        out_shape=jax.ShapeDtypeStruct((MOE_ROWS, HALF), U32),
        compiler_params=_cparams(1),
        name="moe_experts",
    )(tile_expert, tile_valid, tile_slot, tile_next, xs, w1, b1_split, w2, b2, perm)


COMBINE_TM = 256


def _combine_body(*refs, final):
    y_refs = refs[:TOP_K]
    gate_ref, x_ref, mod_ref, g_ref = refs[TOP_K:TOP_K + 4]
    gates = gate_ref[...]
    y_lo, y_hi = (gates[:, 0:1] * h for h in _unpack_rows(y_refs[0][...]))
    for k in range(1, TOP_K):
        lo, hi = _unpack_rows(y_refs[k][...])
        y_lo = y_lo + gates[:, k:k + 1] * lo
        y_hi = y_hi + gates[:, k:k + 1] * hi
    x = x_ref[...] + mod_ref[5:6, :] * jnp.concatenate([y_lo, y_hi], axis=-1)
    if not final:
        refs[-1][...] = x
        return
    x = (x * lax.rsqrt(jnp.mean(x * x, axis=-1, keepdims=True) + EPS)) * g_ref[...]
    prompt_ref, sample_ref = refs[-2:]
    is_prompt = pl.program_id(0) < N_PROMPT // COMBINE_TM

    @pl.when(is_prompt)
    def _():
        prompt_ref[...] = x

    @pl.when(jnp.logical_not(is_prompt))
    def _():
        sample_ref[...] = x


def _combine(y_km, gates, x, mod, g_final, final):
    n_t = N_TOK // COMBINE_TM
    n_pt = N_PROMPT // COMBINE_TM
    y_specs = [pl.BlockSpec((COMBINE_TM, HALF), lambda i, k=k: (k * n_t + i, 0)) for k in range(TOP_K)]
    if final:
        out_specs = [pl.BlockSpec((COMBINE_TM, D_MODEL), lambda i: (jnp.minimum(i, n_pt - 1), 0)),
                     pl.BlockSpec((COMBINE_TM, D_MODEL), lambda i: (jnp.maximum(i - n_pt, 0), 0))]
        out_shape = [jax.ShapeDtypeStruct((N_PROMPT, D_MODEL), F32), jax.ShapeDtypeStruct((N_SAMPLE, D_MODEL), F32)]
    else:
        out_specs = pl.BlockSpec((COMBINE_TM, D_MODEL), lambda i: (i, 0))
        out_shape = jax.ShapeDtypeStruct((N_TOK, D_MODEL), F32)
    return pl.pallas_call(
        functools.partial(_combine_body, final=final),
        grid=(n_t,),
        in_specs=y_specs + [
            pl.BlockSpec((COMBINE_TM, LANES), lambda i: (i, 0)),
            pl.BlockSpec((COMBINE_TM, D_MODEL), lambda i: (i, 0)),
            pl.BlockSpec((None, MOD_ROWS, D_MODEL), lambda i: (_group_of_tile(i, COMBINE_TM), 0, 0)),
            pl.BlockSpec((1, D_MODEL), lambda i: (0, 0)),
        ],
        out_specs=out_specs,
        out_shape=out_shape,
        compiler_params=_cparams(1),
        name="combine",
    )(*([y_km] * TOP_K), gates, x, mod, g_final.reshape(1, D_MODEL))


def _routing_tables(idx, rank, counts):
    counts = counts[0, :N_EXPERTS]
    padded = ((counts + MOE_TM - 1) // MOE_TM) * MOE_TM
    ends = jnp.cumsum(padded)
    offsets = ends - padded
    idx = idx[:, :TOP_K]
    pos = rank[:, :TOP_K] + jnp.sum(
        jnp.where(idx[..., None] == jnp.arange(N_EXPERTS, dtype=I32), offsets, 0), axis=-1)
    pos_km = pos.T.astype(I32)
    tile_start = jnp.arange(MOE_TILES, dtype=I32) * MOE_TM
    tile_valid = (tile_start < ends[-1]).astype(I32)
    tile_expert = jnp.sum((tile_start[:, None] >= ends[None, :]).astype(I32), axis=-1)
    last_expert = jnp.sum((ends[-1] - 1 >= ends).astype(I32))
    tile_expert = jnp.where(tile_valid == 1, tile_expert, last_expert).astype(I32)
    e_ids = jnp.arange(N_EXPERTS, dtype=I32)
    owns = counts > 0
    ordinal = jnp.cumsum(owns.astype(I32)) - 1
    later = jnp.where(owns[None, :] & (e_ids[None, :] > e_ids[:, None]), e_ids[None, :], N_EXPERTS)
    nxt = jnp.min(later, axis=1)
    nxt = jnp.where(nxt == N_EXPERTS, e_ids, nxt)
    onehot = (tile_expert[:, None] == e_ids[None, :]).astype(I32)
    tile_slot = jnp.sum(onehot * (ordinal % 2)[None, :], axis=1).astype(I32)
    tile_next = jnp.sum(onehot * nxt[None, :], axis=1).astype(I32)
    return pos_km, tile_expert, tile_valid, tile_slot, tile_next


def _rope_tables():
    t = np.arange(DEC_SEQ)
    row = (t // GRID_W).astype(np.float32)
    col = (t % GRID_W).astype(np.float32)
    d_axis = RET_DK // 2
    inv = jnp.asarray(ROPE_BASE, F32) ** (-jnp.arange(0, d_axis, 2, dtype=F32) / d_axis)
    ang = jnp.concatenate([row[:, None] * inv, col[:, None] * inv], axis=-1)
    cos = jnp.repeat(jnp.cos(ang), 2, axis=-1)
    sin = jnp.repeat(jnp.sin(ang), 2, axis=-1)
    sign = jnp.asarray(np.tile(np.array([-1.0, 1.0], np.float32), RET_DK // 2))
    return cos, sin * sign


def _deinterleave_matrix():
    p = np.zeros((PERM_W, PERM_W), np.float32)
    half = PERM_W // 2
    for j in range(half):
        p[2 * j, j] = 1.0
        p[2 * j + 1, half + j] = 1.0
    return jnp.asarray(p, BF16)


def kernel(x_prompt, x_sample, state_ret_fwd, state_ret_bwd, cache_na_k, cache_na_v, c, c_ctx, ada_w, ada_b, norm_mix, norm_ffn, norm_final, ret_w_in, ret_w_out, ret_decay_fwd, ret_decay_bwd, na_w_in, na_w_out, na_rpb, moe_w_router, moe_b_router, moe_w1, moe_b1, moe_w2, moe_b2):
    x = (x_prompt.reshape(N_PROMPT, D_MODEL), x_sample.reshape(N_SAMPLE, D_MODEL))

    cond = jnp.zeros((MOD_ROWS, D_MODEL), F32).at[0].set(c_ctx).at[1:1 + DEC_BATCH].set(c)
    mod_all = _adaln(cond, ada_w, ada_b)
    mod_all = mod_all[:, :N_GROUPS].reshape(DEPTH, N_GROUPS, 6, D_MODEL)
    mod_all = jnp.pad(mod_all, ((0, 0), (0, 0), (0, MOD_ROWS - 6), (0, 0)))

    cos, sin_signed = _rope_tables()
    perm = _deinterleave_matrix()
    wr_all = jnp.pad(moe_w_router, ((0, 0), (0, 0), (0, LANES - N_EXPERTS)))
    br_all = jnp.pad(moe_b_router, ((0, 0), (0, LANES - N_EXPERTS))).reshape(DEPTH, 1, LANES)
    b1_split = _deinterleave_bias(moe_b1.reshape(DEPTH * N_EXPERTS, 2 * D_EXPERT), perm)
    b1_split = b1_split.reshape(DEPTH, N_EXPERTS, 1, 2 * D_EXPERT)
    b2_all = moe_b2.reshape(DEPTH, N_EXPERTS, 1, D_MODEL)
    na_tbl = _na_bias_table(na_rpb)
    ret_states = None
    na_caches = None
    for layer in range(DEPTH):
        mod = mod_all[layer]
        j = layer // 2
        if layer % 2 == 0:
            proj = _inproj(x, norm_mix[layer], mod, _cast_layer_bf16(ret_w_in, j), out_dtype=BF16)
            decay = jnp.stack([ret_decay_fwd[j], ret_decay_bwd[j]])
            a_p, s_f, s_b = _ret_prompt(proj, decay, j, ret_states)
            ret_states = (s_f, s_b)
            a_s = _ret_sample(proj, decay, cos, sin_signed, state_ret_fwd, state_ret_bwd, j)
            w_out = ret_w_out
        else:
            proj, ck, cv = _inproj(x, norm_mix[layer], mod, na_w_in, j, j, na_caches)
            na_caches = (ck, cv)
            a_p = _ctx_attn(proj)
            a_s = _na_attn(proj, cache_na_k, cache_na_v, na_tbl, j)
            w_out = na_w_out

        x, tok, logits = _outproj(a_p, a_s, w_out, j, x, norm_ffn[layer], mod, wr_all[layer], br_all[layer])

        idx, gates, rank, counts = _router(logits)
        pos_km, tile_expert, tile_valid, tile_slot, tile_next = _routing_tables(idx, rank, counts)
        xs = _dispatch(tok, pos_km)
        ys = _moe_experts(tile_expert, tile_valid, tile_slot, tile_next, xs, moe_w1, b1_split, moe_w2, b2_all,
                          perm, layer)
        y_km = _gather_rows(ys, pos_km.reshape(-1))
        x = _combine(y_km, gates, x, mod, norm_final, final=(layer == DEPTH - 1))

    y_prompt = x[0].reshape(BATCH, SEQ, D_MODEL)
    y_sample = x[1].reshape(DEC_BATCH, DEC_SEQ, D_MODEL)
    return (y_prompt, y_sample, ret_states[0], ret_states[1],
            jnp.transpose(na_caches[0], (0, 1, 4, 2, 3)), jnp.transpose(na_caches[1], (0, 1, 4, 2, 3)))
```

```python
import functools

import numpy as np
import jax
import jax.numpy as jnp
from jax import lax
from jax.experimental import pallas as pl
from jax.experimental.pallas import tpu as pltpu
from jax.experimental.pallas import tpu_sc as plsc

F32 = jnp.float32
BF16 = jnp.bfloat16
I32 = jnp.int32
U32 = jnp.uint32

D_MODEL = 1024
BATCH = 32
SEQ = 256
DEPTH = 4
DEC_BATCH = 2
DEC_SEQ = 1024
PAST_LEN = 512
GRID_W = 64
EPS = 1e-6
ROPE_BASE = 10000.0
RET_HEADS = 4
RET_DK = D_MODEL // RET_HEADS
RET_DV = 2 * RET_DK
RET_QK_W = RET_HEADS * RET_DK
RET_V_W = RET_HEADS * RET_DV
NA_HEADS = 16
NA_HD = D_MODEL // NA_HEADS
NA_KH = 8
NA_KW = 16
N_EXPERTS = 32
TOP_K = 4
D_EXPERT = D_MODEL
SWIGLU_LIMIT = 7.0
SWIGLU_ALPHA = 1.702
N_RET_LAYERS = (DEPTH + 1) // 2
N_NA_LAYERS = DEPTH // 2

N_PROMPT = BATCH * SEQ
N_SAMPLE = DEC_BATCH * DEC_SEQ
N_TOK = N_PROMPT + N_SAMPLE
N_GROUPS = 1 + DEC_BATCH
MOD_ROWS = 8
LANES = 128

TM = 256
RET_CHUNK = 256
MOE_TM = 256
MOE_ROWS = N_TOK * TOP_K + N_EXPERTS * MOE_TM
MOE_TILES = MOE_ROWS // MOE_TM
ROUTER_TM = 512
PERM_W = 256
VMEM_LIMIT = 56 * 1024 * 1024


def _cparams(n_axes, vmem=VMEM_LIMIT):
    return pltpu.CompilerParams(dimension_semantics=("arbitrary",) * n_axes, vmem_limit_bytes=vmem)


def _group_of_tile(i, tm):
    return jnp.maximum((i * tm) // DEC_SEQ - (N_PROMPT // DEC_SEQ - 1), 0)


def _sigmoid(x):
    return 1.0 / (1.0 + jnp.exp(-x))


def _silu(x):
    return x * _sigmoid(x)


def _log_sigmoid(x):
    return jnp.minimum(x, 0.0) - jnp.log(1.0 + jnp.exp(-jnp.abs(x)))


def _norm_mod(x, g, shift, scale):
    y = x * lax.rsqrt(jnp.mean(x * x, axis=-1, keepdims=True) + EPS)
    return (y * g) * (1.0 + scale) + shift


def _dot(a, b):
    return jnp.dot(a, b, preferred_element_type=F32)


HALF = D_MODEL // 2
HI_MASK = 0xFFFF0000


def _pack_rows(x):
    lo = pltpu.bitcast(x[:, :HALF].astype(BF16).astype(F32), U32)
    hi = pltpu.bitcast(x[:, HALF:].astype(BF16).astype(F32), U32)
    return (lo >> 16) | (hi & jnp.uint32(HI_MASK))


def _unpack_rows(p):
    return pltpu.bitcast(p << 16, F32), pltpu.bitcast(p & jnp.uint32(HI_MASK), F32)


def _dot_nt(a, b):
    return lax.dot_general(a, b, (((1,), (1,)), ((), ())), preferred_element_type=F32)


def _dot_tn(a, b):
    return lax.dot_general(a, b, (((0,), (0,)), ((), ())), preferred_element_type=F32)


ADA_TN = 1536


def _adaln_body(cond_ref, w_ref, b_ref, o_ref):
    s = _silu(cond_ref[...]).astype(BF16)
    o_ref[...] = _dot(s, w_ref[...].astype(BF16)) + b_ref[...]


def _adaln(cond, ada_w, ada_b):
    n_out = 6 * D_MODEL
    return pl.pallas_call(
        _adaln_body,
        grid=(DEPTH, n_out // ADA_TN),
        in_specs=[
            pl.BlockSpec((MOD_ROWS, D_MODEL), lambda l, j: (0, 0)),
            pl.BlockSpec((None, D_MODEL, ADA_TN), lambda l, j: (l, 0, j)),
            pl.BlockSpec((None, 1, ADA_TN), lambda l, j: (l, 0, j)),
        ],
        out_specs=pl.BlockSpec((None, MOD_ROWS, ADA_TN), lambda l, j: (l, 0, j)),
        out_shape=jax.ShapeDtypeStruct((DEPTH, MOD_ROWS, n_out), F32),
        compiler_params=_cparams(2),
        name="adaln",
    )(cond, ada_w, ada_b.reshape(DEPTH, 1, n_out))


CAST_TR = 256


def _cast_body(w_ref, o_ref):
    o_ref[...] = w_ref[...].astype(BF16)


def _cast_layer_bf16(w, j):
    _, r, c = w.shape
    return pl.pallas_call(
        _cast_body,
        grid=(r // CAST_TR,),
        in_specs=[pl.BlockSpec((None, CAST_TR, c), lambda i: (j, i, 0))],
        out_specs=pl.BlockSpec((CAST_TR, c), lambda i: (i, 0)),
        out_shape=jax.ShapeDtypeStruct((r, c), BF16),
        compiler_params=_cparams(1),
        name="cast_bf16",
    )(w)


def _x_inputs(x):
    n_pt = N_PROMPT // TM
    xa, xb, off = (x[0], x[1], 0) if isinstance(x, tuple) else (x, x, n_pt)
    specs = [pl.BlockSpec((TM, D_MODEL), lambda i: (jnp.minimum(i, n_pt - 1), 0)),
             pl.BlockSpec((TM, D_MODEL), lambda i: (off + jnp.maximum(i - n_pt, 0), 0))]
    return specs, [xa, xb]


def _x_tile(xa_ref, xb_ref):
    return jnp.where(pl.program_id(0) < N_PROMPT // TM, xa_ref[...], xb_ref[...])


def _weight_input(w, layer):
    if layer is None:
        return pl.BlockSpec(w.shape, lambda i: (0, 0)), []
    _, r, c = w.shape
    spec = pl.BlockSpec((None, r, c), lambda i: (layer, 0, 0), pipeline_mode=pl.Buffered(1))
    return spec, [pltpu.VMEM((r, c), BF16)]


def _weight_ref(w_ref, scratch):
    if not scratch:
        return w_ref

    @pl.when(pl.program_id(0) == 0)
    def _():
        scratch[0][...] = w_ref[...].astype(BF16)

    return scratch[0]


def _inproj_body(xa_ref, xb_ref, g_ref, mod_ref, w_ref, *rest, cache_layer, first_cache, n_scratch):
    scratch = rest[len(rest) - n_scratch:]
    rest = rest[:len(rest) - n_scratch]
    o_ref = rest[-3] if cache_layer is not None else rest[-1]
    i = pl.program_id(0)
    w = _weight_ref(w_ref, scratch)
    h = _norm_mod(_x_tile(xa_ref, xb_ref), g_ref[...], mod_ref[0:1, :], mod_ref[1:2, :]).astype(BF16)
    for c in range(o_ref.shape[1] // D_MODEL):
        sl = slice(c * D_MODEL, (c + 1) * D_MODEL)
        r = _dot(h, w[:, sl])
        o_ref[:, sl] = r.astype(o_ref.dtype)
        if cache_layer is not None and c >= 1:
            cache_ref = rest[-3 + c]

            @pl.when(i < BATCH)
            def _(r=r, cache_ref=cache_ref):
                r_t = r.T.reshape(NA_HEADS, NA_HD, SEQ)
                if first_cache:
                    for l in range(N_NA_LAYERS):
                        cache_ref[l] = r_t if l == cache_layer else jnp.zeros_like(r_t)
                else:
                    cache_ref[...] = r_t


def _inproj(x, g, mod, w, w_layer=None, cache_layer=None, prev_caches=None, out_dtype=F32):
    n_out = w.shape[-1]
    x_specs, x_args = _x_inputs(x)
    w_spec, scratch = _weight_input(w, w_layer)
    in_specs = x_specs + [
        pl.BlockSpec((1, D_MODEL), lambda i: (0, 0)),
        pl.BlockSpec((None, MOD_ROWS, D_MODEL), lambda i: (_group_of_tile(i, TM), 0, 0)),
        w_spec,
    ]
    out_specs = [pl.BlockSpec((TM, n_out), lambda i: (i, 0))]
    out_shape = [jax.ShapeDtypeStruct((N_TOK, n_out), out_dtype)]
    args = x_args + [g.reshape(1, D_MODEL), mod, w]
    aliases = {}
    first_cache = prev_caches is None
    if cache_layer is not None:
        assert TM == SEQ and n_out == 3 * D_MODEL
        cache = jax.ShapeDtypeStruct((BATCH, N_NA_LAYERS, NA_HEADS, NA_HD, SEQ), F32)
        if first_cache:
            spec = pl.BlockSpec((None, N_NA_LAYERS, NA_HEADS, NA_HD, SEQ),
                                lambda i: (jnp.minimum(i, BATCH - 1), 0, 0, 0, 0))
        else:
            spec = pl.BlockSpec((None, None, NA_HEADS, NA_HD, SEQ),
                                lambda i: (jnp.minimum(i, BATCH - 1), cache_layer, 0, 0, 0))
            in_specs += [pl.BlockSpec(memory_space=pl.ANY)] * 2
            aliases = {len(args): 1, len(args) + 1: 2}
            args += list(prev_caches)
        out_specs += [spec, spec]
        out_shape += [cache, cache]
    outs = pl.pallas_call(
        functools.partial(_inproj_body, cache_layer=cache_layer, first_cache=first_cache, n_scratch=len(scratch)),
        grid=(N_TOK // TM,),
        in_specs=in_specs,
        out_specs=out_specs,
        out_shape=out_shape,
        scratch_shapes=scratch,
        input_output_aliases=aliases,
        compiler_params=_cparams(1),
        name="inproj",
    )(*args)
    return outs[0] if cache_layer is None else outs


def _decay_terms(dec_ref, h, c):
    lgf = _log_sigmoid(jnp.zeros((c, c), F32) + dec_ref[0, h])
    lgb = _log_sigmoid(jnp.zeros((c, c), F32) + dec_ref[1, h])
    row = lax.broadcasted_iota(I32, (c, c), 0)
    col = lax.broadcasted_iota(I32, (c, c), 1)
    diff = (row - col).astype(F32)
    lower = diff >= 0
    upper = diff <= 0
    mask = (jnp.where(lower, jnp.exp(jnp.where(lower, diff, 0.0) * lgf), 0.0)
            + jnp.where(upper, jnp.exp(jnp.where(upper, -diff, 0.0) * lgb), 0.0))
    pos = lax.broadcasted_iota(I32, (c, 1), 0).astype(F32)
    lgf1 = _log_sigmoid(jnp.zeros((c, 1), F32) + dec_ref[0, h])
    lgb1 = _log_sigmoid(jnp.zeros((c, 1), F32) + dec_ref[1, h])
    lgf0 = _log_sigmoid(jnp.zeros((1, 1), F32) + dec_ref[0, h])
    lgb0 = _log_sigmoid(jnp.zeros((1, 1), F32) + dec_ref[1, h])
    terms = dict(
        mask=mask,
        q_f=jnp.exp((pos + 1.0) * lgf1), k_f=jnp.exp((c - 1.0 - pos) * lgf1), c_f=jnp.exp(c * lgf0),
        q_b=jnp.exp((c - pos) * lgb1), k_b=jnp.exp(pos * lgb1), c_b=jnp.exp(c * lgb0),
    )
    return terms


def _group_norm_gate(o, g):
    o = o * lax.rsqrt(jnp.mean(o * o, axis=-1, keepdims=True) + EPS)
    return (_silu(g) * o).astype(BF16)


def _ret_prompt_body(dec_ref, q_ref, k_ref, v_ref, g_ref, *rest, layer, first):
    o_ref, sf_ref, sb_ref = rest[-3:]
    for h in range(RET_HEADS):
        t = _decay_terms(dec_ref, h, SEQ)
        qk_cols = slice(h * RET_DK, (h + 1) * RET_DK)
        v_cols = slice(h * RET_DV, (h + 1) * RET_DV)
        q = q_ref[:, qk_cols].astype(F32)
        k = k_ref[:, qk_cols].astype(F32) * (RET_DK ** -0.5)
        v = v_ref[:, v_cols].astype(BF16)
        scores = _dot_nt(q.astype(BF16), k.astype(BF16)) * t["mask"]
        o = _dot(scores.astype(BF16), v)
        o_ref[:, v_cols] = _group_norm_gate(o, g_ref[:, v_cols].astype(F32))
        for ref, k_dec in ((sf_ref, t["k_f"]), (sb_ref, t["k_b"])):
            s = _dot_tn((k * k_dec).astype(BF16), v)
            if first:
                for l in range(N_RET_LAYERS):
                    ref[l, h] = s if l == layer else jnp.zeros_like(s)
            else:
                ref[h] = s


def _ret_prompt(proj, decay, layer, prev_states=None):
    assert 2 * RET_QK_W == RET_V_W
    state = jax.ShapeDtypeStruct((BATCH, N_RET_LAYERS, RET_HEADS, RET_DK, RET_DV), F32)
    in_specs = [
        pl.BlockSpec(memory_space=pltpu.SMEM),
        pl.BlockSpec((SEQ, RET_QK_W), lambda b: (b, 0)),
        pl.BlockSpec((SEQ, RET_QK_W), lambda b: (b, 1)),
        pl.BlockSpec((SEQ, RET_V_W), lambda b: (b, 1)),
        pl.BlockSpec((SEQ, RET_V_W), lambda b: (b, 2)),
    ]
    args = [decay, proj, proj, proj, proj]
    first = prev_states is None
    aliases = {}
    if first:
        s_spec = pl.BlockSpec((None, N_RET_LAYERS, RET_HEADS, RET_DK, RET_DV), lambda b: (b, 0, 0, 0, 0))
    else:
        s_spec = pl.BlockSpec((None, None, RET_HEADS, RET_DK, RET_DV), lambda b: (b, layer, 0, 0, 0))
        in_specs += [pl.BlockSpec(memory_space=pl.ANY)] * 2
        aliases = {len(args): 1, len(args) + 1: 2}
        args += list(prev_states)
    return pl.pallas_call(
        functools.partial(_ret_prompt_body, layer=layer, first=first),
        grid=(BATCH,),
        in_specs=in_specs,
        out_specs=[pl.BlockSpec((SEQ, RET_V_W), lambda b: (b, 0)), s_spec, s_spec],
        out_shape=[jax.ShapeDtypeStruct((N_PROMPT, RET_V_W), BF16), state, state],
        input_output_aliases=aliases,
        compiler_params=_cparams(1),
        name="ret_prompt",
    )(*args)


def _rope(x, cos, sin_signed):
    n = x.shape[-1]
    lane = lax.broadcasted_iota(I32, x.shape, 1)
    swapped = jnp.where(lane % 2 == 0, pltpu.roll(x, n - 1, 1), pltpu.roll(x, 1, 1))
    return x * cos + swapped * sin_signed


def _ret_sample_body(dec_ref, q_ref, k_ref, v_ref, g_ref, cos_ref, sin_ref, s0f_ref, s0b_ref, o_ref,
                     sf_scr, sb_scr):
    h = pl.program_id(1)
    c = RET_CHUNK
    n_chunks = DEC_SEQ // c
    t = _decay_terms(dec_ref, h, c)

    def chunk(ref, i):
        return ref[i * c:(i + 1) * c, :].astype(F32)

    def qk(i):
        q = _rope(chunk(q_ref, i), chunk(cos_ref, i), chunk(sin_ref, i))
        k = _rope(chunk(k_ref, i) * (RET_DK ** -0.5), chunk(cos_ref, i), chunk(sin_ref, i))
        return q, k

    s = s0f_ref[...]
    for i in range(n_chunks):
        sf_scr[i] = s
        if i + 1 < n_chunks:
            _, k = qk(i)
            s = s * t["c_f"] + _dot_tn((k * t["k_f"]).astype(BF16), chunk(v_ref, i).astype(BF16))
    s = s0b_ref[...]
    for i in reversed(range(n_chunks)):
        sb_scr[i] = s
        if i > 0:
            _, k = qk(i)
            s = s * t["c_b"] + _dot_tn((k * t["k_b"]).astype(BF16), chunk(v_ref, i).astype(BF16))

    for i in range(n_chunks):
        q, k = qk(i)
        scores = _dot_nt(q.astype(BF16), k.astype(BF16)) * t["mask"]
        o = _dot(scores.astype(BF16), chunk(v_ref, i).astype(BF16))
        o = o + _dot((q * t["q_f"]).astype(BF16), sf_scr[i].astype(BF16))
        o = o + _dot((q * t["q_b"]).astype(BF16), sb_scr[i].astype(BF16))
        o_ref[i * c:(i + 1) * c, :] = _group_norm_gate(o, chunk(g_ref, i))


def _ret_sample(proj, decay, cos, sin_signed, s0f, s0b, j):
    kq = RET_QK_W // RET_DK
    kv = 2 * RET_QK_W // RET_DV
    kg = kv + RET_HEADS
    rb = N_PROMPT // DEC_SEQ
    n_chunks = DEC_SEQ // RET_CHUNK
    return pl.pallas_call(
        _ret_sample_body,
        grid=(DEC_BATCH, RET_HEADS),
        in_specs=[
            pl.BlockSpec(memory_space=pltpu.SMEM),
            pl.BlockSpec((DEC_SEQ, RET_DK), lambda b, h: (rb + b, h)),
            pl.BlockSpec((DEC_SEQ, RET_DK), lambda b, h: (rb + b, kq + h)),
            pl.BlockSpec((DEC_SEQ, RET_DV), lambda b, h: (rb + b, kv + h)),
            pl.BlockSpec((DEC_SEQ, RET_DV), lambda b, h: (rb + b, kg + h)),
            pl.BlockSpec((DEC_SEQ, RET_DK), lambda b, h: (0, 0)),
            pl.BlockSpec((DEC_SEQ, RET_DK), lambda b, h: (0, 0)),
            pl.BlockSpec((None, None, None, RET_DK, RET_DV), lambda b, h: (b, j, h, 0, 0)),
            pl.BlockSpec((None, None, None, RET_DK, RET_DV), lambda b, h: (b, j, h, 0, 0)),
        ],
        out_specs=pl.BlockSpec((DEC_SEQ, RET_DV), lambda b, h: (b, h)),
        out_shape=jax.ShapeDtypeStruct((N_SAMPLE, RET_V_W), BF16),
        scratch_shapes=[pltpu.VMEM((n_chunks, RET_DK, RET_DV), F32),
                        pltpu.VMEM((n_chunks, RET_DK, RET_DV), F32)],
        compiler_params=_cparams(2),
        name="ret_sample",
    )(decay, proj, proj, proj, proj, cos, sin_signed, s0f, s0b)


NA_HPB = LANES // NA_HD


def _head_lanes(rows, hh):
    lane = lax.broadcasted_iota(I32, (rows, LANES), 1)
    return jnp.logical_and(lane >= hh * NA_HD, lane < (hh + 1) * NA_HD)


def _with_ones(v):
    return jnp.concatenate([v, jnp.ones(v.shape, v.dtype)], axis=-1)


def _ctx_attn_body(q_ref, k_ref, v_ref, o_ref):
    outs = []
    for p in range(NA_HEADS // NA_HPB):
        sl = slice(p * LANES, (p + 1) * LANES)
        q2 = q_ref[:, sl] * (NA_HD ** -0.5)
        k2 = k_ref[:, sl].astype(BF16)
        v2 = _with_ones(v_ref[:, sl].astype(BF16))
        o_pair = None
        for hh in range(NA_HPB):
            mine = _head_lanes(SEQ, hh)
            s = _dot_nt(jnp.where(mine, q2, 0.0).astype(BF16), k2)
            e = jnp.exp(s - s.max(axis=-1, keepdims=True)).astype(BF16)
            acc = _dot(e, v2)
            o_h = acc[:, :LANES] / acc[:, LANES:]
            o_pair = o_h if o_pair is None else jnp.where(mine, o_h, o_pair)
        outs.append(o_pair)
    o_ref[...] = jnp.concatenate(outs, axis=-1).astype(BF16)


def _ctx_attn(proj):
    return pl.pallas_call(
        _ctx_attn_body,
        grid=(BATCH,),
        in_specs=[
            pl.BlockSpec((SEQ, D_MODEL), lambda b: (b, 0)),
            pl.BlockSpec((SEQ, D_MODEL), lambda b: (b, 1)),
            pl.BlockSpec((SEQ, D_MODEL), lambda b: (b, 2)),
        ],
        out_specs=pl.BlockSpec((SEQ, D_MODEL), lambda b: (b, 0)),
        out_shape=jax.ShapeDtypeStruct((N_PROMPT, D_MODEL), BF16),
        compiler_params=_cparams(1),
        name="ctx_attn",
    )(proj, proj, proj)


NA_QT = 256
NA_WIN = NA_KH * GRID_W


def _na_attn_body(q_ref, k_ref, v_ref, ck_ref, cv_ref, tbl_ref, o_ref):
    t = pl.program_id(1)
    rows_per_tile = NA_QT // GRID_W
    n_rows = DEC_SEQ // GRID_W
    q2 = q_ref[...] * (NA_HD ** -0.5)
    q = [jnp.where(_head_lanes(NA_QT, hh), q2, 0.0).astype(BF16) for hh in range(NA_HPB)]
    s_ctx = _dot_nt(jnp.concatenate(q, axis=0), ck_ref[...].astype(BF16))
    acc_win = []
    e_ctx = []
    for u in range(rows_per_tile):
        qr = t * rows_per_tile + u
        rs = jnp.clip(qr - NA_KH // 2, 0, n_rows - NA_KH)
        i0 = rs - qr + (NA_KH - 1)
        start = pl.multiple_of(rs * GRID_W, GRID_W)
        kw = k_ref[pl.ds(start, NA_WIN), :].astype(BF16)
        vw = _with_ones(v_ref[pl.ds(start, NA_WIN), :].astype(BF16))
        rows = slice(u * GRID_W, (u + 1) * GRID_W)
        bias = jnp.concatenate(
            [jnp.concatenate([tbl_ref[hh, i0 + 2 * m] for m in range(NA_KH // 2)], axis=-1) for hh in range(NA_HPB)],
            axis=0)
        sw = _dot_nt(jnp.concatenate([q[hh][rows] for hh in range(NA_HPB)], axis=0), kw) + bias
        sc = jnp.concatenate([s_ctx[hh * NA_QT + u * GRID_W:hh * NA_QT + (u + 1) * GRID_W] for hh in range(NA_HPB)],
                             axis=0)
        m = jnp.maximum(sw.max(axis=-1, keepdims=True), sc.max(axis=-1, keepdims=True))
        acc_win.append(_dot(jnp.exp(sw - m).astype(BF16), vw))
        e_ctx.append(jnp.exp(sc - m).astype(BF16))
    acc = jnp.concatenate(acc_win, axis=0) + _dot(jnp.concatenate(e_ctx, axis=0), _with_ones(cv_ref[...].astype(BF16)))
    o = acc[:, :LANES] / acc[:, LANES:]
    outs = []
    for u in range(rows_per_tile):
        base = u * NA_HPB * GRID_W
        o_u = o[base:base + GRID_W]
        for hh in range(1, NA_HPB):
            o_u = jnp.where(_head_lanes(GRID_W, hh), o[base + hh * GRID_W:base + (hh + 1) * GRID_W], o_u)
        outs.append(o_u)
    o_ref[...] = jnp.concatenate(outs, axis=0).astype(BF16)


def _na_attn(proj, cache_k, cache_v, tbl, j):
    n_qt = DEC_SEQ // NA_QT
    rb_q = N_PROMPT // NA_QT
    rb_kv = N_PROMPT // DEC_SEQ
    cb = D_MODEL // LANES
    ck = cache_k.reshape(DEC_BATCH, -1, PAST_LEN, D_MODEL)
    cv = cache_v.reshape(DEC_BATCH, -1, PAST_LEN, D_MODEL)
    return pl.pallas_call(
        _na_attn_body,
        grid=(cb, n_qt, DEC_BATCH),
        in_specs=[
            pl.BlockSpec((NA_QT, LANES), lambda p, t, b: (rb_q + b * n_qt + t, p)),
            pl.BlockSpec((DEC_SEQ, LANES), lambda p, t, b: (rb_kv + b, cb + p)),
            pl.BlockSpec((DEC_SEQ, LANES), lambda p, t, b: (rb_kv + b, 2 * cb + p)),
            pl.BlockSpec((None, None, PAST_LEN, LANES), lambda p, t, b: (b, j, 0, p)),
            pl.BlockSpec((None, None, PAST_LEN, LANES), lambda p, t, b: (b, j, 0, p)),
            pl.BlockSpec((None, NA_HPB, 2 * NA_KH - 2, GRID_W, 2 * GRID_W), lambda p, t, b: (j, p, 0, 0, 0)),
        ],
        out_specs=pl.BlockSpec((NA_QT, LANES), lambda p, t, b: (b * n_qt + t, p)),
        out_shape=jax.ShapeDtypeStruct((N_SAMPLE, D_MODEL), BF16),
        compiler_params=_cparams(3),
        name="na_attn",
    )(proj, proj, proj, ck, cv, tbl)


def _na_bias_table(rpb):
    c = np.arange(GRID_W)
    cs = np.clip(c - NA_KW // 2, 0, GRID_W - NA_KW)
    col_valid = (c[None, :] >= cs[:, None]) & (c[None, :] < cs[:, None] + NA_KW)
    col_off = np.clip(c[None, :] - c[:, None] + NA_KW - 1, 0, 2 * NA_KW - 2)
    sel_c = (col_off[..., None] == np.arange(2 * NA_KW - 1)).astype(np.float32)
    t = jnp.einsum("lhij,qkj->lhiqk", rpb, sel_c, precision=lax.Precision.HIGHEST)
    t = jnp.where(col_valid, t, -jnp.inf)
    return jnp.concatenate([t[:, :, :-1], t[:, :, 1:]], axis=-1)


def _outproj_body(ap_ref, as_ref, w_ref, xa_ref, xb_ref, g_ref, mod_ref, wr_ref, br_ref, xo_ref, tok_ref, lg_ref,
                  *scratch):
    i = pl.program_id(0)
    a = jnp.where(i < N_PROMPT // TM, ap_ref[...], as_ref[...])
    x = _x_tile(xa_ref, xb_ref) + mod_ref[2:3, :] * _dot(a, _weight_ref(w_ref, scratch)[...])
    xo_ref[...] = x
    tok = _norm_mod(x, g_ref[...], mod_ref[3:4, :], mod_ref[4:5, :])
    tok_ref[...] = _pack_rows(tok)
    hi = tok.astype(BF16)
    lo = (tok - hi.astype(F32)).astype(BF16)
    wr = wr_ref[...]
    wr_hi = wr.astype(BF16)
    wr_lo = (wr - wr_hi.astype(F32)).astype(BF16)
    hi_terms = _dot(hi, jnp.concatenate([wr_hi, wr_lo], axis=-1))
    lg_ref[...] = hi_terms[:, :LANES] + _dot(lo, wr_hi) + hi_terms[:, LANES:] + br_ref[...]


def _outproj(a_prompt, a_sample, w, w_layer, x, g, mod, wr, br):
    k_in = a_prompt.shape[1]
    n_pt = N_PROMPT // TM
    x_specs, x_args = _x_inputs(x)
    w_spec, scratch = _weight_input(w, w_layer)
    return pl.pallas_call(
        _outproj_body,
        grid=(N_TOK // TM,),
        in_specs=[
            pl.BlockSpec((TM, k_in), lambda i: (jnp.minimum(i, n_pt - 1), 0)),
            pl.BlockSpec((TM, k_in), lambda i: (jnp.maximum(i - n_pt, 0), 0)),
            w_spec,
            *x_specs,
            pl.BlockSpec((1, D_MODEL), lambda i: (0, 0)),
            pl.BlockSpec((None, MOD_ROWS, D_MODEL), lambda i: (_group_of_tile(i, TM), 0, 0)),
            pl.BlockSpec((D_MODEL, LANES), lambda i: (0, 0)),
            pl.BlockSpec((1, LANES), lambda i: (0, 0)),
        ],
        out_specs=[
            pl.BlockSpec((TM, D_MODEL), lambda i: (i, 0)),
            pl.BlockSpec((TM, HALF), lambda i: (i, 0)),
            pl.BlockSpec((TM, LANES), lambda i: (i, 0)),
        ],
        out_shape=[
            jax.ShapeDtypeStruct((N_TOK, D_MODEL), F32),
            jax.ShapeDtypeStruct((N_TOK, HALF), U32),
            jax.ShapeDtypeStruct((N_TOK, LANES), F32),
        ],
        scratch_shapes=scratch,
        compiler_params=_cparams(1),
        name="outproj",
    )(a_prompt, a_sample, w, *x_args, g.reshape(1, D_MODEL), mod, wr, br)


def _router_body(lg_ref, idx_ref, gate_ref, rank_ref, cnt_ref, carry):
    i = pl.program_id(0)

    @pl.when(i == 0)
    def _():
        carry[...] = jnp.zeros_like(carry)

    lane = lax.broadcasted_iota(I32, (ROUTER_TM, LANES), 1)
    l = jnp.where(lane < N_EXPERTS, lg_ref[...], -jnp.inf)
    vals, idxs, hots = [], [], []
    for _ in range(TOP_K):
        m = l.max(axis=-1, keepdims=True)
        idx = jnp.where(l == m, lane, LANES).min(axis=-1, keepdims=True)
        hot = lane == idx
        l = jnp.where(hot, -jnp.inf, l)
        vals.append(m)
        idxs.append(idx)
        hots.append(hot)
    es = [jnp.exp(v - vals[0]) for v in vals]
    denom = es[0] + es[1] + es[2] + es[3]
    hot_all = (hots[0] | hots[1] | hots[2] | hots[3]).astype(F32)
    r = lax.broadcasted_iota(I32, (ROUTER_TM, ROUTER_TM), 0)
    c = lax.broadcasted_iota(I32, (ROUTER_TM, ROUTER_TM), 1)
    before = _dot((c < r).astype(BF16), hot_all.astype(BF16)) + carry[...]
    idx_out = jnp.zeros((ROUTER_TM, LANES), I32)
    gate_out = jnp.zeros((ROUTER_TM, LANES), F32)
    rank_out = jnp.zeros((ROUTER_TM, LANES), I32)
    for k in range(TOP_K):
        rank_k = jnp.where(hots[k], before, 0.0).sum(axis=-1, keepdims=True).astype(I32)
        idx_out = jnp.where(lane == k, idxs[k], idx_out)
        gate_out = jnp.where(lane == k, es[k] / denom, gate_out)
        rank_out = jnp.where(lane == k, rank_k, rank_out)
    idx_ref[...] = idx_out
    gate_ref[...] = gate_out
    rank_ref[...] = rank_out
    carry[...] = carry[...] + hot_all.sum(axis=0, keepdims=True)
    cnt_ref[...] = carry[...].astype(I32)


def _router(logits):
    spec = pl.BlockSpec((ROUTER_TM, LANES), lambda i: (i, 0))
    return pl.pallas_call(
        _router_body,
        grid=(N_TOK // ROUTER_TM,),
        in_specs=[spec],
        out_specs=[spec, spec, spec, pl.BlockSpec((1, LANES), lambda i: (0, 0))],
        out_shape=[
            jax.ShapeDtypeStruct((N_TOK, LANES), I32),
            jax.ShapeDtypeStruct((N_TOK, LANES), F32),
            jax.ShapeDtypeStruct((N_TOK, LANES), I32),
            jax.ShapeDtypeStruct((1, LANES), I32),
        ],
        scratch_shapes=[pltpu.VMEM((1, LANES), F32)],
        compiler_params=_cparams(1),
        name="router",
    )(logits)


SC_CORES = 2
SC_SUBCORES = 16
SC_WORKERS = SC_CORES * SC_SUBCORES
SC_CH = 32
SC_SCATTER_BUFS = 5
SC_GATHER_BUFS = 4


def _sc_mesh():
    return plsc.VectorSubcoreMesh(core_axis_name="c", subcore_axis_name="s",
                                  num_cores=SC_CORES, num_subcores=SC_SUBCORES)


def _sc_worker():
    return lax.axis_index("s") * SC_CORES + lax.axis_index("c")


def _dispatch(tok, pos_km):
    n, d = tok.shape
    n_ch = n // (SC_WORKERS * SC_CH)
    nb = SC_SCATTER_BUFS
    assert n_ch * SC_WORKERS * SC_CH == n and n_ch % nb == 0

    @functools.partial(
        pl.kernel, mesh=_sc_mesh(), out_type=jax.ShapeDtypeStruct((MOE_ROWS, d), tok.dtype),
        scratch_types=[pltpu.VMEM((n_ch, TOP_K, SC_CH), I32), pltpu.VMEM((nb, SC_CH, d), tok.dtype),
                       pltpu.SemaphoreType.DMA((nb,)), pltpu.SemaphoreType.DMA((nb,))])
    def scatter_rows(tok_hbm, pos_hbm, out_hbm, idx_v, rows_v, load_sem, scat_sem):
        wid = _sc_worker()
        base = wid * (n_ch * SC_CH)
        pltpu.sync_copy(pos_hbm.at[wid], idx_v)

        def load(g, slot):
            rows = pl.ds(pl.multiple_of(base + g * SC_CH, 8), SC_CH)
            return pltpu.make_async_copy(tok_hbm.at[rows], rows_v.at[slot], load_sem.at[slot])

        def scatter(g, k, slot):
            return pltpu.make_async_copy(rows_v.at[slot], out_hbm.at[idx_v.at[g, k]], scat_sem.at[slot])

        for s in range(nb - 1):
            load(s, s).start()

        @pl.loop(0, n_ch, step=nb)
        def _(g0):
            for s in range(nb):
                g = g0 + s
                reuse = (s + nb - 1) % nb

                @pl.when(g >= 1)
                def _():
                    for k in range(TOP_K):
                        scatter(g - 1, k, reuse).wait()

                @pl.when(g + nb - 1 < n_ch)
                def _():
                    load(g + nb - 1, reuse).start()

                load(g, s).wait()
                for k in range(TOP_K):
                    scatter(g, k, s).start()

        for k in range(TOP_K):
            scatter(n_ch - 1, k, (n_ch - 1) % nb).wait()

    pos = pos_km.reshape(TOP_K, SC_WORKERS, n_ch, SC_CH).transpose(1, 2, 0, 3)
    return scatter_rows(tok, pos)


def _gather_rows(table, idx):
    b, d = idx.shape[0], table.shape[1]
    n_ch = b // (SC_WORKERS * SC_CH)
    nb = SC_GATHER_BUFS
    assert n_ch * SC_WORKERS * SC_CH == b and n_ch % nb == 0

    @functools.partial(
        pl.kernel, mesh=_sc_mesh(), out_type=jax.ShapeDtypeStruct((b, d), table.dtype),
        scratch_types=[pltpu.VMEM((n_ch, SC_CH), I32), pltpu.VMEM((nb, SC_CH, d), table.dtype),
                       pltpu.SemaphoreType.DMA((nb,)), pltpu.SemaphoreType.DMA((nb,))])
    def gather_rows(table_hbm, idx_hbm, out_hbm, idx_v, rows_v, gather_sem, write_sem):
        wid = _sc_worker()
        base = wid * (n_ch * SC_CH)
        pltpu.sync_copy(idx_hbm.at[wid], idx_v)

        def gather(g, slot):
            return pltpu.make_async_copy(table_hbm.at[idx_v.at[g]], rows_v.at[slot], gather_sem.at[slot])

        def write_out(g, slot):
            rows = pl.ds(pl.multiple_of(base + g * SC_CH, 8), SC_CH)
            return pltpu.make_async_copy(rows_v.at[slot], out_hbm.at[rows], write_sem.at[slot])

        for s in range(nb - 1):
            gather(s, s).start()

        @pl.loop(0, n_ch, step=nb)
        def _(g0):
            for s in range(nb):
                g = g0 + s
                reuse = (s + nb - 1) % nb

                @pl.when(g >= 1)
                def _():
                    write_out(g - 1, reuse).wait()

                @pl.when(g + nb - 1 < n_ch)
                def _():
                    gather(g + nb - 1, reuse).start()

                gather(g, s).wait()
                write_out(g, s).start()

        write_out(n_ch - 1, (n_ch - 1) % nb).wait()

    return gather_rows(table, idx.reshape(SC_WORKERS, n_ch, SC_CH))


def _moe_body(te_ref, tv_ref, slot_ref, nxt_ref, x_ref, w1_hbm, b1_ref, w2_hbm, b2_ref, perm_ref, o_ref,
              w1_buf, w2_buf, w1_scr, w2_scr, sem, *, layer):
    t = pl.program_id(0)
    e = te_ref[t]
    slot = slot_ref[t]
    nxt = nxt_ref[t]
    first = jnp.logical_or(t == 0, e != te_ref[jnp.maximum(t - 1, 0)])

    def fetch(expert, s):
        return (pltpu.make_async_copy(w1_hbm.at[layer, expert], w1_buf.at[s], sem.at[s, 0]),
                pltpu.make_async_copy(w2_hbm.at[layer, expert], w2_buf.at[s], sem.at[s, 1]))

    @pl.when(t == 0)
    def _():
        for cp in fetch(e, slot):
            cp.start()

    @pl.when(jnp.logical_and(tv_ref[t] == 1, first))
    def _():
        for cp in fetch(e, slot):
            cp.wait()

        @pl.when(nxt != e)
        def _():
            for cp in fetch(nxt, 1 - slot):
                cp.start()

        w2_scr[...] = w2_buf[slot].astype(BF16)
        half = PERM_W // 2
        for c in range(2 * D_EXPERT // PERM_W):
            blk = _dot(w1_buf[slot, :, c * PERM_W:(c + 1) * PERM_W].astype(BF16), perm_ref[...]).astype(BF16)
            w1_scr[:, c * half:(c + 1) * half] = blk[:, :half]
            w1_scr[:, D_EXPERT + c * half:D_EXPERT + (c + 1) * half] = blk[:, half:]

    @pl.when(tv_ref[t] == 0)
    def _():
        o_ref[...] = jnp.zeros_like(o_ref)

    @pl.when(tv_ref[t] == 1)
    def _():
        x_lo, x_hi = _unpack_rows(x_ref[...])
        x = jnp.concatenate([x_lo.astype(BF16), x_hi.astype(BF16)], axis=-1)
        hh = _dot(x, w1_scr[...]) + b1_ref[...]
        h_glu = jnp.minimum(hh[:, :D_EXPERT], SWIGLU_LIMIT)
        h_lin = jnp.clip(hh[:, D_EXPERT:], -SWIGLU_LIMIT, SWIGLU_LIMIT)
        a = h_glu * _sigmoid(SWIGLU_ALPHA * h_glu) * (h_lin + 1.0)
        o_ref[...] = _pack_rows(_dot(a.astype(BF16), w2_scr[...]) + b2_ref[...])


def _deinterleave_bias_body(b_ref, perm_ref, o_ref):
    half = PERM_W // 2
    for c in range(2 * D_EXPERT // PERM_W):
        r = b_ref[:, c * PERM_W:(c + 1) * PERM_W]
        acc = jnp.zeros(r.shape, F32)
        for _ in range(3):
            piece = r.astype(BF16)
            acc = acc + _dot(piece, perm_ref[...])
            r = r - piece.astype(F32)
        o_ref[:, c * half:(c + 1) * half] = acc[:, :half]
        o_ref[:, D_EXPERT + c * half:D_EXPERT + (c + 1) * half] = acc[:, half:]


def _deinterleave_bias(b1, perm):
    return pl.pallas_call(
        _deinterleave_bias_body,
        out_shape=jax.ShapeDtypeStruct(b1.shape, F32),
        compiler_params=_cparams(0),
        name="deinterleave_bias",
    )(b1, perm)


def _moe_experts(tile_expert, tile_valid, tile_slot, tile_next, xs, w1, b1_split, w2, b2, perm, layer):
    grid_spec = pltpu.PrefetchScalarGridSpec(
        num_scalar_prefetch=4,
        grid=(MOE_TILES,),
        in_specs=[
            pl.BlockSpec((MOE_TM, HALF), lambda t, te, *_: (t, 0)),
            pl.BlockSpec(memory_space=pl.ANY),
            pl.BlockSpec((None, None, 1, 2 * D_EXPERT), lambda t, te, *_: (layer, te[t], 0, 0)),
            pl.BlockSpec(memory_space=pl.ANY),
            pl.BlockSpec((None, None, 1, D_MODEL), lambda t, te, *_: (layer, te[t], 0, 0)),
            pl.BlockSpec((PERM_W, PERM_W), lambda t, te, *_: (0, 0)),
        ],
        out_specs=pl.BlockSpec((MOE_TM, HALF), lambda t, te, *_: (t, 0)),
        scratch_shapes=[
            pltpu.VMEM((2, D_MODEL, 2 * D_EXPERT), F32),
            pltpu.VMEM((2, D_EXPERT, D_MODEL), F32),
            pltpu.VMEM((D_MODEL, 2 * D_EXPERT), BF16),
            pltpu.VMEM((D_EXPERT, D_MODEL), BF16),
            pltpu.SemaphoreType.DMA((2, 2)),
        ],
    )
    return pl.pallas_call(
        functools.partial(_moe_body, layer=layer),
        grid_spec=grid_spec,
        out_shape=jax.ShapeDtypeStruct((MOE_ROWS, HALF), U32),
        compiler_params=_cparams(1),
        name="moe_experts",
    )(tile_expert, tile_valid, tile_slot, tile_next, xs, w1, b1_split, w2, b2, perm)


COMBINE_TM = 256


def _combine_body(*refs, final):
    y_refs = refs[:TOP_K]
    gate_ref, x_ref, mod_ref, g_ref = refs[TOP_K:TOP_K + 4]
    gates = gate_ref[...]
    y_lo, y_hi = (gates[:, 0:1] * h for h in _unpack_rows(y_refs[0][...]))
    for k in range(1, TOP_K):
        lo, hi = _unpack_rows(y_refs[k][...])
        y_lo = y_lo + gates[:, k:k + 1] * lo
        y_hi = y_hi + gates[:, k:k + 1] * hi
    x = x_ref[...] + mod_ref[5:6, :] * jnp.concatenate([y_lo, y_hi], axis=-1)
    if not final:
        refs[-1][...] = x
        return
    x = (x * lax.rsqrt(jnp.mean(x * x, axis=-1, keepdims=True) + EPS)) * g_ref[...]
    prompt_ref, sample_ref = refs[-2:]
    is_prompt = pl.program_id(0) < N_PROMPT // COMBINE_TM

    @pl.when(is_prompt)
    def _():
        prompt_ref[...] = x

    @pl.when(jnp.logical_not(is_prompt))
    def _():
        sample_ref[...] = x


def _combine(y_km, gates, x, mod, g_final, final):
    n_t = N_TOK // COMBINE_TM
    n_pt = N_PROMPT // COMBINE_TM
    y_specs = [pl.BlockSpec((COMBINE_TM, HALF), lambda i, k=k: (k * n_t + i, 0)) for k in range(TOP_K)]
    if final:
        out_specs = [pl.BlockSpec((COMBINE_TM, D_MODEL), lambda i: (jnp.minimum(i, n_pt - 1), 0)),
                     pl.BlockSpec((COMBINE_TM, D_MODEL), lambda i: (jnp.maximum(i - n_pt, 0), 0))]
        out_shape = [jax.ShapeDtypeStruct((N_PROMPT, D_MODEL), F32), jax.ShapeDtypeStruct((N_SAMPLE, D_MODEL), F32)]
    else:
        out_specs = pl.BlockSpec((COMBINE_TM, D_MODEL), lambda i: (i, 0))
        out_shape = jax.ShapeDtypeStruct((N_TOK, D_MODEL), F32)
    return pl.pallas_call(
        functools.partial(_combine_body, final=final),
        grid=(n_t,),
        in_specs=y_specs + [
            pl.BlockSpec((COMBINE_TM, LANES), lambda i: (i, 0)),
            pl.BlockSpec((COMBINE_TM, D_MODEL), lambda i: (i, 0)),
            pl.BlockSpec((None, MOD_ROWS, D_MODEL), lambda i: (_group_of_tile(i, COMBINE_TM), 0, 0)),
            pl.BlockSpec((1, D_MODEL), lambda i: (0, 0)),
        ],
        out_specs=out_specs,
        out_shape=out_shape,
        compiler_params=_cparams(1),
        name="combine",
    )(*([y_km] * TOP_K), gates, x, mod, g_final.reshape(1, D_MODEL))


def _routing_tables(idx, rank, counts):
    counts = counts[0, :N_EXPERTS]
    padded = ((counts + MOE_TM - 1) // MOE_TM) * MOE_TM
    ends = jnp.cumsum(padded)
    offsets = ends - padded
    idx = idx[:, :TOP_K]
    pos = rank[:, :TOP_K] + jnp.sum(
        jnp.where(idx[..., None] == jnp.arange(N_EXPERTS, dtype=I32), offsets, 0), axis=-1)
    pos_km = pos.T.astype(I32)
    tile_start = jnp.arange(MOE_TILES, dtype=I32) * MOE_TM
    tile_valid = (tile_start < ends[-1]).astype(I32)
    tile_expert = jnp.sum((tile_start[:, None] >= ends[None, :]).astype(I32), axis=-1)
    last_expert = jnp.sum((ends[-1] - 1 >= ends).astype(I32))
    tile_expert = jnp.where(tile_valid == 1, tile_expert, last_expert).astype(I32)
    e_ids = jnp.arange(N_EXPERTS, dtype=I32)
    owns = counts > 0
    ordinal = jnp.cumsum(owns.astype(I32)) - 1
    later = jnp.where(owns[None, :] & (e_ids[None, :] > e_ids[:, None]), e_ids[None, :], N_EXPERTS)
    nxt = jnp.min(later, axis=1)
    nxt = jnp.where(nxt == N_EXPERTS, e_ids, nxt)
    onehot = (tile_expert[:, None] == e_ids[None, :]).astype(I32)
    tile_slot = jnp.sum(onehot * (ordinal % 2)[None, :], axis=1).astype(I32)
    tile_next = jnp.sum(onehot * nxt[None, :], axis=1).astype(I32)
    return pos_km, tile_expert, tile_valid, tile_slot, tile_next


def _rope_tables():
    t = np.arange(DEC_SEQ)
    row = (t // GRID_W).astype(np.float32)
    col = (t % GRID_W).astype(np.float32)
    d_axis = RET_DK // 2
    inv = jnp.asarray(ROPE_BASE, F32) ** (-jnp.arange(0, d_axis, 2, dtype=F32) / d_axis)
    ang = jnp.concatenate([row[:, None] * inv, col[:, None] * inv], axis=-1)
    cos = jnp.repeat(jnp.cos(ang), 2, axis=-1)
    sin = jnp.repeat(jnp.sin(ang), 2, axis=-1)
    sign = jnp.asarray(np.tile(np.array([-1.0, 1.0], np.float32), RET_DK // 2))
    return cos, sin * sign


def _deinterleave_matrix():
    p = np.zeros((PERM_W, PERM_W), np.float32)
    half = PERM_W // 2
    for j in range(half):
        p[2 * j, j] = 1.0
        p[2 * j + 1, half + j] = 1.0
    return jnp.asarray(p, BF16)


def kernel(x_prompt, x_sample, state_ret_fwd, state_ret_bwd, cache_na_k, cache_na_v, c, c_ctx, ada_w, ada_b, norm_mix, norm_ffn, norm_final, ret_w_in, ret_w_out, ret_decay_fwd, ret_decay_bwd, na_w_in, na_w_out, na_rpb, moe_w_router, moe_b_router, moe_w1, moe_b1, moe_w2, moe_b2):
    x = (x_prompt.reshape(N_PROMPT, D_MODEL), x_sample.reshape(N_SAMPLE, D_MODEL))

    cond = jnp.zeros((MOD_ROWS, D_MODEL), F32).at[0].set(c_ctx).at[1:1 + DEC_BATCH].set(c)
    mod_all = _adaln(cond, ada_w, ada_b)
    mod_all = mod_all[:, :N_GROUPS].reshape(DEPTH, N_GROUPS, 6, D_MODEL)
    mod_all = jnp.pad(mod_all, ((0, 0), (0, 0), (0, MOD_ROWS - 6), (0, 0)))

    cos, sin_signed = _rope_tables()
    perm = _deinterleave_matrix()
    wr_all = jnp.pad(moe_w_router, ((0, 0), (0, 0), (0, LANES - N_EXPERTS)))
    br_all = jnp.pad(moe_b_router, ((0, 0), (0, LANES - N_EXPERTS))).reshape(DEPTH, 1, LANES)
    b1_split = _deinterleave_bias(moe_b1.reshape(DEPTH * N_EXPERTS, 2 * D_EXPERT), perm)
    b1_split = b1_split.reshape(DEPTH, N_EXPERTS, 1, 2 * D_EXPERT)
    b2_all = moe_b2.reshape(DEPTH, N_EXPERTS, 1, D_MODEL)
    na_tbl = _na_bias_table(na_rpb)
    ret_states = None
    na_caches = None
    for layer in range(DEPTH):
        mod = mod_all[layer]
        j = layer // 2
        if layer % 2 == 0:
            proj = _inproj(x, norm_mix[layer], mod, _cast_layer_bf16(ret_w_in, j), out_dtype=BF16)
            decay = jnp.stack([ret_decay_fwd[j], ret_decay_bwd[j]])
            a_p, s_f, s_b = _ret_prompt(proj, decay, j, ret_states)
            ret_states = (s_f, s_b)
            a_s = _ret_sample(proj, decay, cos, sin_signed, state_ret_fwd, state_ret_bwd, j)
            w_out = ret_w_out
        else:
            proj, ck, cv = _inproj(x, norm_mix[layer], mod, na_w_in, j, j, na_caches)
            na_caches = (ck, cv)
            a_p = _ctx_attn(proj)
            a_s = _na_attn(proj, cache_na_k, cache_na_v, na_tbl, j)
            w_out = na_w_out

        x, tok, logits = _outproj(a_p, a_s, w_out, j, x, norm_ffn[layer], mod, wr_all[layer], br_all[layer])

        idx, gates, rank, counts = _router(logits)
        pos_km, tile_expert, tile_valid, tile_slot, tile_next = _routing_tables(idx, rank, counts)
        xs = _dispatch(tok, pos_km)
        ys = _moe_experts(tile_expert, tile_valid, tile_slot, tile_next, xs, moe_w1, b1_split, moe_w2, b2_all,
                          perm, layer)
        y_km = _gather_rows(ys, pos_km.reshape(-1))
        x = _combine(y_km, gates, x, mod, norm_final, final=(layer == DEPTH - 1))

    y_prompt = x[0].reshape(BATCH, SEQ, D_MODEL)
    y_sample = x[1].reshape(DEC_BATCH, DEC_SEQ, D_MODEL)
    return (y_prompt, y_sample, ret_states[0], ret_states[1],
            jnp.transpose(na_caches[0], (0, 1, 4, 2, 3)), jnp.transpose(na_caches[1], (0, 1, 4, 2, 3)))
```

```python
import functools

import numpy as np
import jax
import jax.numpy as jnp
from jax import lax
from jax.experimental import pallas as pl
from jax.experimental.pallas import tpu as pltpu
from jax.experimental.pallas import tpu_sc as plsc

F32 = jnp.float32
BF16 = jnp.bfloat16
I32 = jnp.int32
U32 = jnp.uint32

D_MODEL = 1024
BATCH = 32
SEQ = 256
DEPTH = 4
DEC_BATCH = 2
DEC_SEQ = 1024
PAST_LEN = 512
GRID_W = 64
EPS = 1e-6
ROPE_BASE = 10000.0
RET_HEADS = 4
RET_DK = D_MODEL // RET_HEADS
RET_DV = 2 * RET_DK
RET_QK_W = RET_HEADS * RET_DK
RET_V_W = RET_HEADS * RET_DV
NA_HEADS = 16
NA_HD = D_MODEL // NA_HEADS
NA_KH = 8
NA_KW = 16
N_EXPERTS = 32
TOP_K = 4
D_EXPERT = D_MODEL
SWIGLU_LIMIT = 7.0
SWIGLU_ALPHA = 1.702
N_RET_LAYERS = (DEPTH + 1) // 2
N_NA_LAYERS = DEPTH // 2

N_PROMPT = BATCH * SEQ
N_SAMPLE = DEC_BATCH * DEC_SEQ
N_TOK = N_PROMPT + N_SAMPLE
N_GROUPS = 1 + DEC_BATCH
MOD_ROWS = 8
LANES = 128

TM = 256
RET_CHUNK = 256
MOE_TM = 256
MOE_ROWS = N_TOK * TOP_K + N_EXPERTS * MOE_TM
MOE_TILES = MOE_ROWS // MOE_TM
MOE_TPS = 2
ROUTER_TM = 512
PERM_W = 256
VMEM_LIMIT = 56 * 1024 * 1024


def _cparams(n_axes, vmem=VMEM_LIMIT):
    return pltpu.CompilerParams(dimension_semantics=("arbitrary",) * n_axes, vmem_limit_bytes=vmem)


def _group_of_tile(i, tm):
    return jnp.maximum((i * tm) // DEC_SEQ - (N_PROMPT // DEC_SEQ - 1), 0)


def _sigmoid(x):
    return 1.0 / (1.0 + jnp.exp(-x))


def _silu(x):
    return x * _sigmoid(x)


def _log_sigmoid(x):
    return jnp.minimum(x, 0.0) - jnp.log(1.0 + jnp.exp(-jnp.abs(x)))


def _norm_mod(x, g, shift, scale):
    y = x * lax.rsqrt(jnp.mean(x * x, axis=-1, keepdims=True) + EPS)
    return (y * g) * (1.0 + scale) + shift


def _dot(a, b):
    return jnp.dot(a, b, preferred_element_type=F32)


HALF = D_MODEL // 2
HI_MASK = 0xFFFF0000


def _pack_rows(x):
    lo = pltpu.bitcast(x[:, :HALF].astype(BF16).astype(F32), U32)
    hi = pltpu.bitcast(x[:, HALF:].astype(BF16).astype(F32), U32)
    return (lo >> 16) | (hi & jnp.uint32(HI_MASK))


def _unpack_rows(p):
    return pltpu.bitcast(p << 16, F32), pltpu.bitcast(p & jnp.uint32(HI_MASK), F32)


def _dot_nt(a, b):
    return lax.dot_general(a, b, (((1,), (1,)), ((), ())), preferred_element_type=F32)


def _dot_tn(a, b):
    return lax.dot_general(a, b, (((0,), (0,)), ((), ())), preferred_element_type=F32)


ADA_TN = 1536


def _adaln_body(cond_ref, w_ref, b_ref, o_ref):
    s = _silu(cond_ref[...]).astype(BF16)
    o_ref[...] = _dot(s, w_ref[...].astype(BF16)) + b_ref[...]


def _adaln(cond, ada_w, ada_b):
    n_out = 6 * D_MODEL
    return pl.pallas_call(
        _adaln_body,
        grid=(DEPTH, n_out // ADA_TN),
        in_specs=[
            pl.BlockSpec((MOD_ROWS, D_MODEL), lambda l, j: (0, 0)),
            pl.BlockSpec((None, D_MODEL, ADA_TN), lambda l, j: (l, 0, j)),
            pl.BlockSpec((None, 1, ADA_TN), lambda l, j: (l, 0, j)),
        ],
        out_specs=pl.BlockSpec((None, MOD_ROWS, ADA_TN), lambda l, j: (l, 0, j)),
        out_shape=jax.ShapeDtypeStruct((DEPTH, MOD_ROWS, n_out), F32),
        compiler_params=_cparams(2),
        name="adaln",
    )(cond, ada_w, ada_b.reshape(DEPTH, 1, n_out))


CAST_TR = 256


def _cast_body(w_ref, o_ref):
    o_ref[...] = w_ref[...].astype(BF16)


def _cast_layer_bf16(w, j):
    _, r, c = w.shape
    return pl.pallas_call(
        _cast_body,
        grid=(r // CAST_TR,),
        in_specs=[pl.BlockSpec((None, CAST_TR, c), lambda i: (j, i, 0))],
        out_specs=pl.BlockSpec((CAST_TR, c), lambda i: (i, 0)),
        out_shape=jax.ShapeDtypeStruct((r, c), BF16),
        compiler_params=_cparams(1),
        name="cast_bf16",
    )(w)


def _x_inputs(x):
    n_pt = N_PROMPT // TM
    xa, xb, off = (x[0], x[1], 0) if isinstance(x, tuple) else (x, x, n_pt)
    specs = [pl.BlockSpec((TM, D_MODEL), lambda i: (jnp.minimum(i, n_pt - 1), 0)),
             pl.BlockSpec((TM, D_MODEL), lambda i: (off + jnp.maximum(i - n_pt, 0), 0))]
    return specs, [xa, xb]


def _x_tile(xa_ref, xb_ref):
    return jnp.where(pl.program_id(0) < N_PROMPT // TM, xa_ref[...], xb_ref[...])


def _weight_input(w, layer):
    if layer is None:
        return pl.BlockSpec(w.shape, lambda i: (0, 0)), []
    _, r, c = w.shape
    spec = pl.BlockSpec((None, r, c), lambda i: (layer, 0, 0), pipeline_mode=pl.Buffered(1))
    return spec, [pltpu.VMEM((r, c), BF16)]


def _weight_ref(w_ref, scratch):
    if not scratch:
        return w_ref

    @pl.when(pl.program_id(0) == 0)
    def _():
        scratch[0][...] = w_ref[...].astype(BF16)

    return scratch[0]


def _inproj_body(xa_ref, xb_ref, g_ref, mod_ref, w_ref, *rest, cache_layer, first_cache, n_scratch):
    scratch = rest[len(rest) - n_scratch:]
    rest = rest[:len(rest) - n_scratch]
    o_ref = rest[-3] if cache_layer is not None else rest[-1]
    i = pl.program_id(0)
    w = _weight_ref(w_ref, scratch)
    h = _norm_mod(_x_tile(xa_ref, xb_ref), g_ref[...], mod_ref[0:1, :], mod_ref[1:2, :]).astype(BF16)
    for c in range(o_ref.shape[1] // D_MODEL):
        sl = slice(c * D_MODEL, (c + 1) * D_MODEL)
        r = _dot(h, w[:, sl])
        o_ref[:, sl] = r.astype(o_ref.dtype)
        if cache_layer is not None and c >= 1:
            cache_ref = rest[-3 + c]

            @pl.when(i < BATCH)
            def _(r=r, cache_ref=cache_ref):
                r_t = r.T.reshape(NA_HEADS, NA_HD, SEQ)
                if first_cache:
                    for l in range(N_NA_LAYERS):
                        cache_ref[l] = r_t if l == cache_layer else jnp.zeros_like(r_t)
                else:
                    cache_ref[...] = r_t


def _inproj(x, g, mod, w, w_layer=None, cache_layer=None, prev_caches=None, out_dtype=F32):
    n_out = w.shape[-1]
    x_specs, x_args = _x_inputs(x)
    w_spec, scratch = _weight_input(w, w_layer)
    in_specs = x_specs + [
        pl.BlockSpec((1, D_MODEL), lambda i: (0, 0)),
        pl.BlockSpec((None, MOD_ROWS, D_MODEL), lambda i: (_group_of_tile(i, TM), 0, 0)),
        w_spec,
    ]
    out_specs = [pl.BlockSpec((TM, n_out), lambda i: (i, 0))]
    out_shape = [jax.ShapeDtypeStruct((N_TOK, n_out), out_dtype)]
    args = x_args + [g.reshape(1, D_MODEL), mod, w]
    aliases = {}
    first_cache = prev_caches is None
    if cache_layer is not None:
        assert TM == SEQ and n_out == 3 * D_MODEL
        cache = jax.ShapeDtypeStruct((BATCH, N_NA_LAYERS, NA_HEADS, NA_HD, SEQ), F32)
        if first_cache:
            spec = pl.BlockSpec((None, N_NA_LAYERS, NA_HEADS, NA_HD, SEQ),
                                lambda i: (jnp.minimum(i, BATCH - 1), 0, 0, 0, 0))
        else:
            spec = pl.BlockSpec((None, None, NA_HEADS, NA_HD, SEQ),
                                lambda i: (jnp.minimum(i, BATCH - 1), cache_layer, 0, 0, 0))
            in_specs += [pl.BlockSpec(memory_space=pl.ANY)] * 2
            aliases = {len(args): 1, len(args) + 1: 2}
            args += list(prev_caches)
        out_specs += [spec, spec]
        out_shape += [cache, cache]
    outs = pl.pallas_call(
        functools.partial(_inproj_body, cache_layer=cache_layer, first_cache=first_cache, n_scratch=len(scratch)),
        grid=(N_TOK // TM,),
        in_specs=in_specs,
        out_specs=out_specs,
        out_shape=out_shape,
        scratch_shapes=scratch,
        input_output_aliases=aliases,
        compiler_params=_cparams(1),
        name="inproj",
    )(*args)
    return outs[0] if cache_layer is None else outs


def _decay_terms(dec_ref, h, c):
    lgf = _log_sigmoid(jnp.zeros((c, c), F32) + dec_ref[0, h])
    lgb = _log_sigmoid(jnp.zeros((c, c), F32) + dec_ref[1, h])
    row = lax.broadcasted_iota(I32, (c, c), 0)
    col = lax.broadcasted_iota(I32, (c, c), 1)
    diff = (row - col).astype(F32)
    lower = diff >= 0
    upper = diff <= 0
    mask = (jnp.where(lower, jnp.exp(jnp.where(lower, diff, 0.0) * lgf), 0.0)
            + jnp.where(upper, jnp.exp(jnp.where(upper, -diff, 0.0) * lgb), 0.0))
    pos = lax.broadcasted_iota(I32, (c, 1), 0).astype(F32)
    lgf1 = _log_sigmoid(jnp.zeros((c, 1), F32) + dec_ref[0, h])
    lgb1 = _log_sigmoid(jnp.zeros((c, 1), F32) + dec_ref[1, h])
    lgf0 = _log_sigmoid(jnp.zeros((1, 1), F32) + dec_ref[0, h])
    lgb0 = _log_sigmoid(jnp.zeros((1, 1), F32) + dec_ref[1, h])
    terms = dict(
        mask=mask,
        q_f=jnp.exp((pos + 1.0) * lgf1), k_f=jnp.exp((c - 1.0 - pos) * lgf1), c_f=jnp.exp(c * lgf0),
        q_b=jnp.exp((c - pos) * lgb1), k_b=jnp.exp(pos * lgb1), c_b=jnp.exp(c * lgb0),
    )
    return terms


def _group_norm_gate(o, g):
    o = o * lax.rsqrt(jnp.mean(o * o, axis=-1, keepdims=True) + EPS)
    return (_silu(g) * o).astype(BF16)


def _ret_prompt_body(dec_ref, q_ref, k_ref, v_ref, g_ref, *rest, layer, first):
    o_ref, sf_ref, sb_ref = rest[-3:]
    for h in range(RET_HEADS):
        t = _decay_terms(dec_ref, h, SEQ)
        qk_cols = slice(h * RET_DK, (h + 1) * RET_DK)
        v_cols = slice(h * RET_DV, (h + 1) * RET_DV)
        q = q_ref[:, qk_cols].astype(F32)
        k = k_ref[:, qk_cols].astype(F32) * (RET_DK ** -0.5)
        v = v_ref[:, v_cols].astype(BF16)
        scores = _dot_nt(q.astype(BF16), k.astype(BF16)) * t["mask"]
        o = _dot(scores.astype(BF16), v)
        o_ref[:, v_cols] = _group_norm_gate(o, g_ref[:, v_cols].astype(F32))
        for ref, k_dec in ((sf_ref, t["k_f"]), (sb_ref, t["k_b"])):
            s = _dot_tn((k * k_dec).astype(BF16), v)
            if first:
                for l in range(N_RET_LAYERS):
                    ref[l, h] = s if l == layer else jnp.zeros_like(s)
            else:
                ref[h] = s


def _ret_prompt(proj, decay, layer, prev_states=None):
    assert 2 * RET_QK_W == RET_V_W
    state = jax.ShapeDtypeStruct((BATCH, N_RET_LAYERS, RET_HEADS, RET_DK, RET_DV), F32)
    in_specs = [
        pl.BlockSpec(memory_space=pltpu.SMEM),
        pl.BlockSpec((SEQ, RET_QK_W), lambda b: (b, 0)),
        pl.BlockSpec((SEQ, RET_QK_W), lambda b: (b, 1)),
        pl.BlockSpec((SEQ, RET_V_W), lambda b: (b, 1)),
        pl.BlockSpec((SEQ, RET_V_W), lambda b: (b, 2)),
    ]
    args = [decay, proj, proj, proj, proj]
    first = prev_states is None
    aliases = {}
    if first:
        s_spec = pl.BlockSpec((None, N_RET_LAYERS, RET_HEADS, RET_DK, RET_DV), lambda b: (b, 0, 0, 0, 0))
    else:
        s_spec = pl.BlockSpec((None, None, RET_HEADS, RET_DK, RET_DV), lambda b: (b, layer, 0, 0, 0))
        in_specs += [pl.BlockSpec(memory_space=pl.ANY)] * 2
        aliases = {len(args): 1, len(args) + 1: 2}
        args += list(prev_states)
    return pl.pallas_call(
        functools.partial(_ret_prompt_body, layer=layer, first=first),
        grid=(BATCH,),
        in_specs=in_specs,
        out_specs=[pl.BlockSpec((SEQ, RET_V_W), lambda b: (b, 0)), s_spec, s_spec],
        out_shape=[jax.ShapeDtypeStruct((N_PROMPT, RET_V_W), BF16), state, state],
        input_output_aliases=aliases,
        compiler_params=_cparams(1),
        name="ret_prompt",
    )(*args)


def _rope(x, cos, sin_signed):
    n = x.shape[-1]
    lane = lax.broadcasted_iota(I32, x.shape, 1)
    swapped = jnp.where(lane % 2 == 0, pltpu.roll(x, n - 1, 1), pltpu.roll(x, 1, 1))
    return x * cos + swapped * sin_signed


def _ret_sample_body(dec_ref, q_ref, k_ref, v_ref, g_ref, cos_ref, sin_ref, s0f_ref, s0b_ref, o_ref,
                     sf_scr, sb_scr):
    h = pl.program_id(1)
    c = RET_CHUNK
    n_chunks = DEC_SEQ // c
    t = _decay_terms(dec_ref, h, c)

    def chunk(ref, i):
        return ref[i * c:(i + 1) * c, :].astype(F32)

    def qk(i):
        q = _rope(chunk(q_ref, i), chunk(cos_ref, i), chunk(sin_ref, i))
        k = _rope(chunk(k_ref, i) * (RET_DK ** -0.5), chunk(cos_ref, i), chunk(sin_ref, i))
        return q, k

    s = s0f_ref[...]
    for i in range(n_chunks):
        sf_scr[i] = s
        if i + 1 < n_chunks:
            _, k = qk(i)
            s = s * t["c_f"] + _dot_tn((k * t["k_f"]).astype(BF16), chunk(v_ref, i).astype(BF16))
    s = s0b_ref[...]
    for i in reversed(range(n_chunks)):
        sb_scr[i] = s
        if i > 0:
            _, k = qk(i)
            s = s * t["c_b"] + _dot_tn((k * t["k_b"]).astype(BF16), chunk(v_ref, i).astype(BF16))

    for i in range(n_chunks):
        q, k = qk(i)
        scores = _dot_nt(q.astype(BF16), k.astype(BF16)) * t["mask"]
        o = _dot(scores.astype(BF16), chunk(v_ref, i).astype(BF16))
        o = o + _dot((q * t["q_f"]).astype(BF16), sf_scr[i].astype(BF16))
        o = o + _dot((q * t["q_b"]).astype(BF16), sb_scr[i].astype(BF16))
        o_ref[i * c:(i + 1) * c, :] = _group_norm_gate(o, chunk(g_ref, i))


def _ret_sample(proj, decay, cos, sin_signed, s0f, s0b, j):
    kq = RET_QK_W // RET_DK
    kv = 2 * RET_QK_W // RET_DV
    kg = kv + RET_HEADS
    rb = N_PROMPT // DEC_SEQ
    n_chunks = DEC_SEQ // RET_CHUNK
    return pl.pallas_call(
        _ret_sample_body,
        grid=(DEC_BATCH, RET_HEADS),
        in_specs=[
            pl.BlockSpec(memory_space=pltpu.SMEM),
            pl.BlockSpec((DEC_SEQ, RET_DK), lambda b, h: (rb + b, h)),
            pl.BlockSpec((DEC_SEQ, RET_DK), lambda b, h: (rb + b, kq + h)),
            pl.BlockSpec((DEC_SEQ, RET_DV), lambda b, h: (rb + b, kv + h)),
            pl.BlockSpec((DEC_SEQ, RET_DV), lambda b, h: (rb + b, kg + h)),
            pl.BlockSpec((DEC_SEQ, RET_DK), lambda b, h: (0, 0)),
            pl.BlockSpec((DEC_SEQ, RET_DK), lambda b, h: (0, 0)),
            pl.BlockSpec((None, None, None, RET_DK, RET_DV), lambda b, h: (b, j, h, 0, 0)),
            pl.BlockSpec((None, None, None, RET_DK, RET_DV), lambda b, h: (b, j, h, 0, 0)),
        ],
        out_specs=pl.BlockSpec((DEC_SEQ, RET_DV), lambda b, h: (b, h)),
        out_shape=jax.ShapeDtypeStruct((N_SAMPLE, RET_V_W), BF16),
        scratch_shapes=[pltpu.VMEM((n_chunks, RET_DK, RET_DV), F32),
                        pltpu.VMEM((n_chunks, RET_DK, RET_DV), F32)],
        compiler_params=_cparams(2),
        name="ret_sample",
    )(decay, proj, proj, proj, proj, cos, sin_signed, s0f, s0b)


NA_HPB = LANES // NA_HD


def _head_lanes(rows, hh):
    lane = lax.broadcasted_iota(I32, (rows, LANES), 1)
    return jnp.logical_and(lane >= hh * NA_HD, lane < (hh + 1) * NA_HD)


def _with_ones(v):
    return jnp.concatenate([v, jnp.ones(v.shape, v.dtype)], axis=-1)


def _ctx_attn_body(q_ref, k_ref, v_ref, o_ref):
    outs = []
    for p in range(NA_HEADS // NA_HPB):
        sl = slice(p * LANES, (p + 1) * LANES)
        q2 = q_ref[:, sl] * (NA_HD ** -0.5)
        k2 = k_ref[:, sl].astype(BF16)
        v2 = _with_ones(v_ref[:, sl].astype(BF16))
        o_pair = None
        for hh in range(NA_HPB):
            mine = _head_lanes(SEQ, hh)
            s = _dot_nt(jnp.where(mine, q2, 0.0).astype(BF16), k2)
            e = jnp.exp(s - s.max(axis=-1, keepdims=True)).astype(BF16)
            acc = _dot(e, v2)
            o_h = acc[:, :LANES] / acc[:, LANES:]
            o_pair = o_h if o_pair is None else jnp.where(mine, o_h, o_pair)
        outs.append(o_pair)
    o_ref[...] = jnp.concatenate(outs, axis=-1).astype(BF16)


def _ctx_attn(proj):
    return pl.pallas_call(
        _ctx_attn_body,
        grid=(BATCH,),
        in_specs=[
            pl.BlockSpec((SEQ, D_MODEL), lambda b: (b, 0)),
            pl.BlockSpec((SEQ, D_MODEL), lambda b: (b, 1)),
            pl.BlockSpec((SEQ, D_MODEL), lambda b: (b, 2)),
        ],
        out_specs=pl.BlockSpec((SEQ, D_MODEL), lambda b: (b, 0)),
        out_shape=jax.ShapeDtypeStruct((N_PROMPT, D_MODEL), BF16),
        compiler_params=_cparams(1),
        name="ctx_attn",
    )(proj, proj, proj)


NA_QT = 256
NA_WIN = NA_KH * GRID_W


def _na_attn_body(q_ref, k_ref, v_ref, ck_ref, cv_ref, tbl_ref, o_ref):
    t = pl.program_id(1)
    rows_per_tile = NA_QT // GRID_W
    n_rows = DEC_SEQ // GRID_W
    q2 = q_ref[...] * (NA_HD ** -0.5)
    q = [jnp.where(_head_lanes(NA_QT, hh), q2, 0.0).astype(BF16) for hh in range(NA_HPB)]
    s_ctx = _dot_nt(jnp.concatenate(q, axis=0), ck_ref[...].astype(BF16))
    acc_win = []
    e_ctx = []
    for u in range(rows_per_tile):
        qr = t * rows_per_tile + u
        rs = jnp.clip(qr - NA_KH // 2, 0, n_rows - NA_KH)
        i0 = rs - qr + (NA_KH - 1)
        start = pl.multiple_of(rs * GRID_W, GRID_W)
        kw = k_ref[pl.ds(start, NA_WIN), :].astype(BF16)
        vw = _with_ones(v_ref[pl.ds(start, NA_WIN), :].astype(BF16))
        rows = slice(u * GRID_W, (u + 1) * GRID_W)
        bias = jnp.concatenate(
            [jnp.concatenate([tbl_ref[hh, i0 + 2 * m] for m in range(NA_KH // 2)], axis=-1) for hh in range(NA_HPB)],
            axis=0)
        sw = _dot_nt(jnp.concatenate([q[hh][rows] for hh in range(NA_HPB)], axis=0), kw) + bias
        sc = jnp.concatenate([s_ctx[hh * NA_QT + u * GRID_W:hh * NA_QT + (u + 1) * GRID_W] for hh in range(NA_HPB)],
                             axis=0)
        m = jnp.maximum(sw.max(axis=-1, keepdims=True), sc.max(axis=-1, keepdims=True))
        acc_win.append(_dot(jnp.exp(sw - m).astype(BF16), vw))
        e_ctx.append(jnp.exp(sc - m).astype(BF16))
    acc = jnp.concatenate(acc_win, axis=0) + _dot(jnp.concatenate(e_ctx, axis=0), _with_ones(cv_ref[...].astype(BF16)))
    o = acc[:, :LANES] / acc[:, LANES:]
    outs = []
    for u in range(rows_per_tile):
        base = u * NA_HPB * GRID_W
        o_u = o[base:base + GRID_W]
        for hh in range(1, NA_HPB):
            o_u = jnp.where(_head_lanes(GRID_W, hh), o[base + hh * GRID_W:base + (hh + 1) * GRID_W], o_u)
        outs.append(o_u)
    o_ref[...] = jnp.concatenate(outs, axis=0).astype(BF16)


def _na_attn(proj, cache_k, cache_v, tbl, j):
    n_qt = DEC_SEQ // NA_QT
    rb_q = N_PROMPT // NA_QT
    rb_kv = N_PROMPT // DEC_SEQ
    cb = D_MODEL // LANES
    ck = cache_k.reshape(DEC_BATCH, -1, PAST_LEN, D_MODEL)
    cv = cache_v.reshape(DEC_BATCH, -1, PAST_LEN, D_MODEL)
    return pl.pallas_call(
        _na_attn_body,
        grid=(cb, n_qt, DEC_BATCH),
        in_specs=[
            pl.BlockSpec((NA_QT, LANES), lambda p, t, b: (rb_q + b * n_qt + t, p)),
            pl.BlockSpec((DEC_SEQ, LANES), lambda p, t, b: (rb_kv + b, cb + p)),
            pl.BlockSpec((DEC_SEQ, LANES), lambda p, t, b: (rb_kv + b, 2 * cb + p)),
            pl.BlockSpec((None, None, PAST_LEN, LANES), lambda p, t, b: (b, j, 0, p)),
            pl.BlockSpec((None, None, PAST_LEN, LANES), lambda p, t, b: (b, j, 0, p)),
            pl.BlockSpec((None, NA_HPB, 2 * NA_KH - 2, GRID_W, 2 * GRID_W), lambda p, t, b: (j, p, 0, 0, 0)),
        ],
        out_specs=pl.BlockSpec((NA_QT, LANES), lambda p, t, b: (b * n_qt + t, p)),
        out_shape=jax.ShapeDtypeStruct((N_SAMPLE, D_MODEL), BF16),
        compiler_params=_cparams(3),
        name="na_attn",
    )(proj, proj, proj, ck, cv, tbl)


def _na_bias_table(rpb):
    c = np.arange(GRID_W)
    cs = np.clip(c - NA_KW // 2, 0, GRID_W - NA_KW)
    col_valid = (c[None, :] >= cs[:, None]) & (c[None, :] < cs[:, None] + NA_KW)
    col_off = np.clip(c[None, :] - c[:, None] + NA_KW - 1, 0, 2 * NA_KW - 2)
    sel_c = (col_off[..., None] == np.arange(2 * NA_KW - 1)).astype(np.float32)
    t = jnp.einsum("lhij,qkj->lhiqk", rpb, sel_c, precision=lax.Precision.HIGHEST)
    t = jnp.where(col_valid, t, -jnp.inf)
    return jnp.concatenate([t[:, :, :-1], t[:, :, 1:]], axis=-1)


def _outproj_body(ap_ref, as_ref, w_ref, xa_ref, xb_ref, g_ref, mod_ref, wr_ref, br_ref, xo_ref, tok_ref, lg_ref,
                  *scratch):
    i = pl.program_id(0)
    a = jnp.where(i < N_PROMPT // TM, ap_ref[...], as_ref[...])
    x = _x_tile(xa_ref, xb_ref) + mod_ref[2:3, :] * _dot(a, _weight_ref(w_ref, scratch)[...])
    xo_ref[...] = x
    tok = _norm_mod(x, g_ref[...], mod_ref[3:4, :], mod_ref[4:5, :])
    tok_ref[...] = _pack_rows(tok)
    hi = tok.astype(BF16)
    lo = (tok - hi.astype(F32)).astype(BF16)
    wr = wr_ref[...]
    wr_hi = wr.astype(BF16)
    wr_lo = (wr - wr_hi.astype(F32)).astype(BF16)
    hi_terms = _dot(hi, jnp.concatenate([wr_hi, wr_lo], axis=-1))
    lg_ref[...] = hi_terms[:, :LANES] + _dot(lo, wr_hi) + hi_terms[:, LANES:] + br_ref[...]


def _outproj(a_prompt, a_sample, w, w_layer, x, g, mod, wr, br):
    k_in = a_prompt.shape[1]
    n_pt = N_PROMPT // TM
    x_specs, x_args = _x_inputs(x)
    w_spec, scratch = _weight_input(w, w_layer)
    return pl.pallas_call(
        _outproj_body,
        grid=(N_TOK // TM,),
        in_specs=[
            pl.BlockSpec((TM, k_in), lambda i: (jnp.minimum(i, n_pt - 1), 0)),
            pl.BlockSpec((TM, k_in), lambda i: (jnp.maximum(i - n_pt, 0), 0)),
            w_spec,
            *x_specs,
            pl.BlockSpec((1, D_MODEL), lambda i: (0, 0)),
            pl.BlockSpec((None, MOD_ROWS, D_MODEL), lambda i: (_group_of_tile(i, TM), 0, 0)),
            pl.BlockSpec((D_MODEL, LANES), lambda i: (0, 0)),
            pl.BlockSpec((1, LANES), lambda i: (0, 0)),
        ],
        out_specs=[
            pl.BlockSpec((TM, D_MODEL), lambda i: (i, 0)),
            pl.BlockSpec((TM, HALF), lambda i: (i, 0)),
            pl.BlockSpec((TM, LANES), lambda i: (i, 0)),
        ],
        out_shape=[
            jax.ShapeDtypeStruct((N_TOK, D_MODEL), F32),
            jax.ShapeDtypeStruct((N_TOK, HALF), U32),
            jax.ShapeDtypeStruct((N_TOK, LANES), F32),
        ],
        scratch_shapes=scratch,
        compiler_params=_cparams(1),
        name="outproj",
    )(a_prompt, a_sample, w, *x_args, g.reshape(1, D_MODEL), mod, wr, br)


def _router_body(lg_ref, idx_ref, gate_ref, rank_ref, cnt_ref, carry):
    i = pl.program_id(0)

    @pl.when(i == 0)
    def _():
        carry[...] = jnp.zeros_like(carry)

    lane = lax.broadcasted_iota(I32, (ROUTER_TM, LANES), 1)
    l = jnp.where(lane < N_EXPERTS, lg_ref[...], -jnp.inf)
    vals, idxs, hots = [], [], []
    for _ in range(TOP_K):
        m = l.max(axis=-1, keepdims=True)
        idx = jnp.where(l == m, lane, LANES).min(axis=-1, keepdims=True)
        hot = lane == idx
        l = jnp.where(hot, -jnp.inf, l)
        vals.append(m)
        idxs.append(idx)
        hots.append(hot)
    es = [jnp.exp(v - vals[0]) for v in vals]
    denom = es[0] + es[1] + es[2] + es[3]
    hot_all = (hots[0] | hots[1] | hots[2] | hots[3]).astype(F32)
    r = lax.broadcasted_iota(I32, (ROUTER_TM, ROUTER_TM), 0)
    c = lax.broadcasted_iota(I32, (ROUTER_TM, ROUTER_TM), 1)
    before = _dot((c < r).astype(BF16), hot_all.astype(BF16)) + carry[...]
    idx_out = jnp.zeros((ROUTER_TM, LANES), I32)
    gate_out = jnp.zeros((ROUTER_TM, LANES), F32)
    rank_out = jnp.zeros((ROUTER_TM, LANES), I32)
    for k in range(TOP_K):
        rank_k = jnp.where(hots[k], before, 0.0).sum(axis=-1, keepdims=True).astype(I32)
        idx_out = jnp.where(lane == k, idxs[k], idx_out)
        gate_out = jnp.where(lane == k, es[k] / denom, gate_out)
        rank_out = jnp.where(lane == k, rank_k, rank_out)
    idx_ref[...] = idx_out
    gate_ref[...] = gate_out
    rank_ref[...] = rank_out
    carry[...] = carry[...] + hot_all.sum(axis=0, keepdims=True)
    cnt_ref[...] = carry[...].astype(I32)


def _router(logits):
    spec = pl.BlockSpec((ROUTER_TM, LANES), lambda i: (i, 0))
    return pl.pallas_call(
        _router_body,
        grid=(N_TOK // ROUTER_TM,),
        in_specs=[spec],
        out_specs=[spec, spec, spec, pl.BlockSpec((1, LANES), lambda i: (0, 0))],
        out_shape=[
            jax.ShapeDtypeStruct((N_TOK, LANES), I32),
            jax.ShapeDtypeStruct((N_TOK, LANES), F32),
            jax.ShapeDtypeStruct((N_TOK, LANES), I32),
            jax.ShapeDtypeStruct((1, LANES), I32),
        ],
        scratch_shapes=[pltpu.VMEM((1, LANES), F32)],
        compiler_params=_cparams(1),
        name="router",
    )(logits)


SC_CORES = 2
SC_SUBCORES = 16
SC_WORKERS = SC_CORES * SC_SUBCORES
SC_CH = 32
SC_SCATTER_BUFS = 5
SC_GATHER_BUFS = 4


def _sc_mesh():
    return plsc.VectorSubcoreMesh(core_axis_name="c", subcore_axis_name="s",
                                  num_cores=SC_CORES, num_subcores=SC_SUBCORES)


def _sc_worker():
    return lax.axis_index("s") * SC_CORES + lax.axis_index("c")


def _dispatch(tok, pos_km):
    n, d = tok.shape
    n_ch = n // (SC_WORKERS * SC_CH)
    nb = SC_SCATTER_BUFS
    assert n_ch * SC_WORKERS * SC_CH == n and n_ch % nb == 0

    @functools.partial(
        pl.kernel, mesh=_sc_mesh(), out_type=jax.ShapeDtypeStruct((MOE_ROWS, d), tok.dtype),
        scratch_types=[pltpu.VMEM((n_ch, TOP_K, SC_CH), I32), pltpu.VMEM((nb, SC_CH, d), tok.dtype),
                       pltpu.SemaphoreType.DMA((nb,)), pltpu.SemaphoreType.DMA((nb,))])
    def scatter_rows(tok_hbm, pos_hbm, out_hbm, idx_v, rows_v, load_sem, scat_sem):
        wid = _sc_worker()
        base = wid * (n_ch * SC_CH)
        pltpu.sync_copy(pos_hbm.at[wid], idx_v)

        def load(g, slot):
            rows = pl.ds(pl.multiple_of(base + g * SC_CH, 8), SC_CH)
            return pltpu.make_async_copy(tok_hbm.at[rows], rows_v.at[slot], load_sem.at[slot])

        def scatter(g, k, slot):
            return pltpu.make_async_copy(rows_v.at[slot], out_hbm.at[idx_v.at[g, k]], scat_sem.at[slot])

        for s in range(nb - 1):
            load(s, s).start()

        @pl.loop(0, n_ch, step=nb)
        def _(g0):
            for s in range(nb):
                g = g0 + s
                reuse = (s + nb - 1) % nb

                @pl.when(g >= 1)
                def _():
                    for k in range(TOP_K):
                        scatter(g - 1, k, reuse).wait()

                @pl.when(g + nb - 1 < n_ch)
                def _():
                    load(g + nb - 1, reuse).start()

                load(g, s).wait()
                for k in range(TOP_K):
                    scatter(g, k, s).start()

        for k in range(TOP_K):
            scatter(n_ch - 1, k, (n_ch - 1) % nb).wait()

    pos = pos_km.reshape(TOP_K, SC_WORKERS, n_ch, SC_CH).transpose(1, 2, 0, 3)
    return scatter_rows(tok, pos)


def _gather_rows(table, idx):
    b, d = idx.shape[0], table.shape[1]
    n_ch = b // (SC_WORKERS * SC_CH)
    nb = SC_GATHER_BUFS
    assert n_ch * SC_WORKERS * SC_CH == b and n_ch % nb == 0

    @functools.partial(
        pl.kernel, mesh=_sc_mesh(), out_type=jax.ShapeDtypeStruct((b, d), table.dtype),
        scratch_types=[pltpu.VMEM((n_ch, SC_CH), I32), pltpu.VMEM((nb, SC_CH, d), table.dtype),
                       pltpu.SemaphoreType.DMA((nb,)), pltpu.SemaphoreType.DMA((nb,))])
    def gather_rows(table_hbm, idx_hbm, out_hbm, idx_v, rows_v, gather_sem, write_sem):
        wid = _sc_worker()
        base = wid * (n_ch * SC_CH)
        pltpu.sync_copy(idx_hbm.at[wid], idx_v)

        def gather(g, slot):
            return pltpu.make_async_copy(table_hbm.at[idx_v.at[g]], rows_v.at[slot], gather_sem.at[slot])

        def write_out(g, slot):
            rows = pl.ds(pl.multiple_of(base + g * SC_CH, 8), SC_CH)
            return pltpu.make_async_copy(rows_v.at[slot], out_hbm.at[rows], write_sem.at[slot])

        for s in range(nb - 1):
            gather(s, s).start()

        @pl.loop(0, n_ch, step=nb)
        def _(g0):
            for s in range(nb):
                g = g0 + s
                reuse = (s + nb - 1) % nb

                @pl.when(g >= 1)
                def _():
                    write_out(g - 1, reuse).wait()

                @pl.when(g + nb - 1 < n_ch)
                def _():
                    gather(g + nb - 1, reuse).start()

                gather(g, s).wait()
                write_out(g, s).start()

        write_out(n_ch - 1, (n_ch - 1) % nb).wait()

    return gather_rows(table, idx.reshape(SC_WORKERS, n_ch, SC_CH))


def _moe_body(te_ref, tv_ref, slot_ref, nxt_ref, x_ref, w1_hbm, w2_hbm, perm_ref, *rest, layer):
    b_refs = rest[:2 * MOE_TPS]
    o_ref, w1_buf, w2_buf, w1_scr, w2_scr, sem = rest[2 * MOE_TPS:]
    for sub in range(MOE_TPS):
        rows = slice(sub * MOE_TM, (sub + 1) * MOE_TM)
        _moe_tile(te_ref, tv_ref, slot_ref, nxt_ref, pl.program_id(0) * MOE_TPS + sub, x_ref.at[rows], w1_hbm,
                  b_refs[2 * sub], w2_hbm, b_refs[2 * sub + 1], perm_ref, o_ref.at[rows],
                  w1_buf, w2_buf, w1_scr, w2_scr, sem, layer)


def _moe_tile(te_ref, tv_ref, slot_ref, nxt_ref, t, x_ref, w1_hbm, b1_ref, w2_hbm, b2_ref, perm_ref, o_ref,
              w1_buf, w2_buf, w1_scr, w2_scr, sem, layer):
    e = te_ref[t]
    slot = slot_ref[t]
    nxt = nxt_ref[t]
    first = jnp.logical_or(t == 0, e != te_ref[jnp.maximum(t - 1, 0)])

    def fetch(expert, s):
        return (pltpu.make_async_copy(w1_hbm.at[layer, expert], w1_buf.at[s], sem.at[s, 0]),
                pltpu.make_async_copy(w2_hbm.at[layer, expert], w2_buf.at[s], sem.at[s, 1]))

    @pl.when(t == 0)
    def _():
        for cp in fetch(e, slot):
            cp.start()

    @pl.when(jnp.logical_and(tv_ref[t] == 1, first))
    def _():
        for cp in fetch(e, slot):
            cp.wait()

        @pl.when(nxt != e)
        def _():
            for cp in fetch(nxt, 1 - slot):
                cp.start()

        w2_scr[...] = w2_buf[slot].astype(BF16)
        half = PERM_W // 2
        for c in range(2 * D_EXPERT // PERM_W):
            blk = _dot(w1_buf[slot, :, c * PERM_W:(c + 1) * PERM_W].astype(BF16), perm_ref[...]).astype(BF16)
            w1_scr[:, c * half:(c + 1) * half] = blk[:, :half]
            w1_scr[:, D_EXPERT + c * half:D_EXPERT + (c + 1) * half] = blk[:, half:]

    @pl.when(tv_ref[t] == 0)
    def _():
        o_ref[...] = jnp.zeros_like(o_ref)

    @pl.when(tv_ref[t] == 1)
    def _():
        x_lo, x_hi = _unpack_rows(x_ref[...])
        x = jnp.concatenate([x_lo.astype(BF16), x_hi.astype(BF16)], axis=-1)
        hh = _dot(x, w1_scr[...]) + b1_ref[...]
        h_glu = jnp.minimum(hh[:, :D_EXPERT], SWIGLU_LIMIT)
        h_lin = jnp.clip(hh[:, D_EXPERT:], -SWIGLU_LIMIT, SWIGLU_LIMIT)
        a = h_glu * _sigmoid(SWIGLU_ALPHA * h_glu) * (h_lin + 1.0)
        o_ref[...] = _pack_rows(_dot(a.astype(BF16), w2_scr[...]) + b2_ref[...])


def _deinterleave_bias_body(b_ref, perm_ref, o_ref):
    half = PERM_W // 2
    for c in range(2 * D_EXPERT // PERM_W):
        r = b_ref[:, c * PERM_W:(c + 1) * PERM_W]
        acc = jnp.zeros(r.shape, F32)
        for _ in range(3):
            piece = r.astype(BF16)
            acc = acc + _dot(piece, perm_ref[...])
            r = r - piece.astype(F32)
        o_ref[:, c * half:(c + 1) * half] = acc[:, :half]
        o_ref[:, D_EXPERT + c * half:D_EXPERT + (c + 1) * half] = acc[:, half:]


def _deinterleave_bias(b1, perm):
    return pl.pallas_call(
        _deinterleave_bias_body,
        out_shape=jax.ShapeDtypeStruct(b1.shape, F32),
        compiler_params=_cparams(0),
        name="deinterleave_bias",
    )(b1, perm)


def _moe_experts(tile_expert, tile_valid, tile_slot, tile_next, xs, w1, b1_split, w2, b2, perm, layer):
    bias_specs = []
    for sub in range(MOE_TPS):
        bias_specs += [
            pl.BlockSpec((None, None, 1, 2 * D_EXPERT), lambda s, te, *_, sub=sub: (layer, te[s * MOE_TPS + sub], 0, 0)),
            pl.BlockSpec((None, None, 1, D_MODEL), lambda s, te, *_, sub=sub: (layer, te[s * MOE_TPS + sub], 0, 0)),
        ]
    grid_spec = pltpu.PrefetchScalarGridSpec(
        num_scalar_prefetch=4,
        grid=(MOE_TILES // MOE_TPS,),
        in_specs=[
            pl.BlockSpec((MOE_TPS * MOE_TM, HALF), lambda s, *_: (s, 0)),
            pl.BlockSpec(memory_space=pl.ANY),
            pl.BlockSpec(memory_space=pl.ANY),
            pl.BlockSpec((PERM_W, PERM_W), lambda s, *_: (0, 0)),
            *bias_specs,
        ],
        out_specs=pl.BlockSpec((MOE_TPS * MOE_TM, HALF), lambda s, *_: (s, 0)),
        scratch_shapes=[
            pltpu.VMEM((2, D_MODEL, 2 * D_EXPERT), F32),
            pltpu.VMEM((2, D_EXPERT, D_MODEL), F32),
            pltpu.VMEM((D_MODEL, 2 * D_EXPERT), BF16),
            pltpu.VMEM((D_EXPERT, D_MODEL), BF16),
            pltpu.SemaphoreType.DMA((2, 2)),
        ],
    )
    return pl.pallas_call(
        functools.partial(_moe_body, layer=layer),
        grid_spec=grid_spec,
        out_shape=jax.ShapeDtypeStruct((MOE_ROWS, HALF), U32),
        compiler_params=_cparams(1),
        name="moe_experts",
    )(tile_expert, tile_valid, tile_slot, tile_next, xs, w1, w2, perm, *([b1_split, b2] * MOE_TPS))


COMBINE_TM = 256


def _combine_body(*refs, final):
    y_refs = refs[:TOP_K]
    gate_ref, x_ref, mod_ref, g_ref = refs[TOP_K:TOP_K + 4]
    gates = gate_ref[...]
    y_lo, y_hi = (gates[:, 0:1] * h for h in _unpack_rows(y_refs[0][...]))
    for k in range(1, TOP_K):
        lo, hi = _unpack_rows(y_refs[k][...])
        y_lo = y_lo + gates[:, k:k + 1] * lo
        y_hi = y_hi + gates[:, k:k + 1] * hi
    x = x_ref[...] + mod_ref[5:6, :] * jnp.concatenate([y_lo, y_hi], axis=-1)
    if not final:
        refs[-1][...] = x
        return
    x = (x * lax.rsqrt(jnp.mean(x * x, axis=-1, keepdims=True) + EPS)) * g_ref[...]
    prompt_ref, sample_ref = refs[-2:]
    is_prompt = pl.program_id(0) < N_PROMPT // COMBINE_TM

    @pl.when(is_prompt)
    def _():
        prompt_ref[...] = x

    @pl.when(jnp.logical_not(is_prompt))
    def _():
        sample_ref[...] = x


def _combine(y_km, gates, x, mod, g_final, final):
    n_t = N_TOK // COMBINE_TM
    n_pt = N_PROMPT // COMBINE_TM
    y_specs = [pl.BlockSpec((COMBINE_TM, HALF), lambda i, k=k: (k * n_t + i, 0)) for k in range(TOP_K)]
    if final:
        out_specs = [pl.BlockSpec((COMBINE_TM, D_MODEL), lambda i: (jnp.minimum(i, n_pt - 1), 0)),
                     pl.BlockSpec((COMBINE_TM, D_MODEL), lambda i: (jnp.maximum(i - n_pt, 0), 0))]
        out_shape = [jax.ShapeDtypeStruct((N_PROMPT, D_MODEL), F32), jax.ShapeDtypeStruct((N_SAMPLE, D_MODEL), F32)]
    else:
        out_specs = pl.BlockSpec((COMBINE_TM, D_MODEL), lambda i: (i, 0))
        out_shape = jax.ShapeDtypeStruct((N_TOK, D_MODEL), F32)
    return pl.pallas_call(
        functools.partial(_combine_body, final=final),
        grid=(n_t,),
        in_specs=y_specs + [
            pl.BlockSpec((COMBINE_TM, LANES), lambda i: (i, 0)),
            pl.BlockSpec((COMBINE_TM, D_MODEL), lambda i: (i, 0)),
            pl.BlockSpec((None, MOD_ROWS, D_MODEL), lambda i: (_group_of_tile(i, COMBINE_TM), 0, 0)),
            pl.BlockSpec((1, D_MODEL), lambda i: (0, 0)),
        ],
        out_specs=out_specs,
        out_shape=out_shape,
        compiler_params=_cparams(1),
        name="combine",
    )(*([y_km] * TOP_K), gates, x, mod, g_final.reshape(1, D_MODEL))


def _routing_tables(idx, rank, counts):
    counts = counts[0, :N_EXPERTS]
    padded = ((counts + MOE_TM - 1) // MOE_TM) * MOE_TM
    ends = jnp.cumsum(padded)
    offsets = ends - padded
    idx = idx[:, :TOP_K]
    pos = rank[:, :TOP_K] + jnp.sum(
        jnp.where(idx[..., None] == jnp.arange(N_EXPERTS, dtype=I32), offsets, 0), axis=-1)
    pos_km = pos.T.astype(I32)
    tile_start = jnp.arange(MOE_TILES, dtype=I32) * MOE_TM
    tile_valid = (tile_start < ends[-1]).astype(I32)
    tile_expert = jnp.sum((tile_start[:, None] >= ends[None, :]).astype(I32), axis=-1)
    last_expert = jnp.sum((ends[-1] - 1 >= ends).astype(I32))
    tile_expert = jnp.where(tile_valid == 1, tile_expert, last_expert).astype(I32)
    e_ids = jnp.arange(N_EXPERTS, dtype=I32)
    owns = counts > 0
    ordinal = jnp.cumsum(owns.astype(I32)) - 1
    later = jnp.where(owns[None, :] & (e_ids[None, :] > e_ids[:, None]), e_ids[None, :], N_EXPERTS)
    nxt = jnp.min(later, axis=1)
    nxt = jnp.where(nxt == N_EXPERTS, e_ids, nxt)
    onehot = (tile_expert[:, None] == e_ids[None, :]).astype(I32)
    tile_slot = jnp.sum(onehot * (ordinal % 2)[None, :], axis=1).astype(I32)
    tile_next = jnp.sum(onehot * nxt[None, :], axis=1).astype(I32)
    return pos_km, tile_expert, tile_valid, tile_slot, tile_next


def _rope_tables():
    t = np.arange(DEC_SEQ)
    row = (t // GRID_W).astype(np.float32)
    col = (t % GRID_W).astype(np.float32)
    d_axis = RET_DK // 2
    inv = jnp.asarray(ROPE_BASE, F32) ** (-jnp.arange(0, d_axis, 2, dtype=F32) / d_axis)
    ang = jnp.concatenate([row[:, None] * inv, col[:, None] * inv], axis=-1)
    cos = jnp.repeat(jnp.cos(ang), 2, axis=-1)
    sin = jnp.repeat(jnp.sin(ang), 2, axis=-1)
    sign = jnp.asarray(np.tile(np.array([-1.0, 1.0], np.float32), RET_DK // 2))
    return cos, sin * sign


def _deinterleave_matrix():
    p = np.zeros((PERM_W, PERM_W), np.float32)
    half = PERM_W // 2
    for j in range(half):
        p[2 * j, j] = 1.0
        p[2 * j + 1, half + j] = 1.0
    return jnp.asarray(p, BF16)


def kernel(x_prompt, x_sample, state_ret_fwd, state_ret_bwd, cache_na_k, cache_na_v, c, c_ctx, ada_w, ada_b, norm_mix, norm_ffn, norm_final, ret_w_in, ret_w_out, ret_decay_fwd, ret_decay_bwd, na_w_in, na_w_out, na_rpb, moe_w_router, moe_b_router, moe_w1, moe_b1, moe_w2, moe_b2):
    x = (x_prompt.reshape(N_PROMPT, D_MODEL), x_sample.reshape(N_SAMPLE, D_MODEL))

    cond = jnp.zeros((MOD_ROWS, D_MODEL), F32).at[0].set(c_ctx).at[1:1 + DEC_BATCH].set(c)
    mod_all = _adaln(cond, ada_w, ada_b)
    mod_all = mod_all[:, :N_GROUPS].reshape(DEPTH, N_GROUPS, 6, D_MODEL)
    mod_all = jnp.pad(mod_all, ((0, 0), (0, 0), (0, MOD_ROWS - 6), (0, 0)))

    cos, sin_signed = _rope_tables()
    perm = _deinterleave_matrix()
    wr_all = jnp.pad(moe_w_router, ((0, 0), (0, 0), (0, LANES - N_EXPERTS)))
    br_all = jnp.pad(moe_b_router, ((0, 0), (0, LANES - N_EXPERTS))).reshape(DEPTH, 1, LANES)
    b1_split = _deinterleave_bias(moe_b1.reshape(DEPTH * N_EXPERTS, 2 * D_EXPERT), perm)
    b1_split = b1_split.reshape(DEPTH, N_EXPERTS, 1, 2 * D_EXPERT)
    b2_all = moe_b2.reshape(DEPTH, N_EXPERTS, 1, D_MODEL)
    na_tbl = _na_bias_table(na_rpb)
    ret_states = None
    na_caches = None
    for layer in range(DEPTH):
        mod = mod_all[layer]
        j = layer // 2
        if layer % 2 == 0:
            proj = _inproj(x, norm_mix[layer], mod, _cast_layer_bf16(ret_w_in, j), out_dtype=BF16)
            decay = jnp.stack([ret_decay_fwd[j], ret_decay_bwd[j]])
            a_p, s_f, s_b = _ret_prompt(proj, decay, j, ret_states)
            ret_states = (s_f, s_b)
            a_s = _ret_sample(proj, decay, cos, sin_signed, state_ret_fwd, state_ret_bwd, j)
            w_out = ret_w_out
        else:
            proj, ck, cv = _inproj(x, norm_mix[layer], mod, na_w_in, j, j, na_caches)
            na_caches = (ck, cv)
            a_p = _ctx_attn(proj)
            a_s = _na_attn(proj, cache_na_k, cache_na_v, na_tbl, j)
            w_out = na_w_out

        x, tok, logits = _outproj(a_p, a_s, w_out, j, x, norm_ffn[layer], mod, wr_all[layer], br_all[layer])

        idx, gates, rank, counts = _router(logits)
        pos_km, tile_expert, tile_valid, tile_slot, tile_next = _routing_tables(idx, rank, counts)
        xs = _dispatch(tok, pos_km)
        ys = _moe_experts(tile_expert, tile_valid, tile_slot, tile_next, xs, moe_w1, b1_split, moe_w2, b2_all,
                          perm, layer)
        y_km = _gather_rows(ys, pos_km.reshape(-1))
        x = _combine(y_km, gates, x, mod, norm_final, final=(layer == DEPTH - 1))

    y_prompt = x[0].reshape(BATCH, SEQ, D_MODEL)
    y_sample = x[1].reshape(DEC_BATCH, DEC_SEQ, D_MODEL)
    return (y_prompt, y_sample, ret_states[0], ret_states[1],
            jnp.transpose(na_caches[0], (0, 1, 4, 2, 3)), jnp.transpose(na_caches[1], (0, 1, 4, 2, 3)))
```

```python
import functools

import numpy as np
import jax
import jax.numpy as jnp
from jax import lax
from jax.experimental import pallas as pl
from jax.experimental.pallas import tpu as pltpu
from jax.experimental.pallas import tpu_sc as plsc

F32 = jnp.float32
BF16 = jnp.bfloat16
I32 = jnp.int32
U32 = jnp.uint32

D_MODEL = 1024
BATCH = 32
SEQ = 256
DEPTH = 4
DEC_BATCH = 2
DEC_SEQ = 1024
PAST_LEN = 512
GRID_W = 64
EPS = 1e-6
ROPE_BASE = 10000.0
RET_HEADS = 4
RET_DK = D_MODEL // RET_HEADS
RET_DV = 2 * RET_DK
RET_QK_W = RET_HEADS * RET_DK
RET_V_W = RET_HEADS * RET_DV
NA_HEADS = 16
NA_HD = D_MODEL // NA_HEADS
NA_KH = 8
NA_KW = 16
N_EXPERTS = 32
TOP_K = 4
D_EXPERT = D_MODEL
SWIGLU_LIMIT = 7.0
SWIGLU_ALPHA = 1.702
N_RET_LAYERS = (DEPTH + 1) // 2
N_NA_LAYERS = DEPTH // 2

N_PROMPT = BATCH * SEQ
N_SAMPLE = DEC_BATCH * DEC_SEQ
N_TOK = N_PROMPT + N_SAMPLE
N_GROUPS = 1 + DEC_BATCH
MOD_ROWS = 8
LANES = 128

TM = 256
RET_CHUNK = 256
MOE_TM = 256
MOE_ROWS = N_TOK * TOP_K + N_EXPERTS * MOE_TM
MOE_TILES = MOE_ROWS // MOE_TM
MOE_TPS = 4
ROUTER_TM = 512
PERM_W = 256
VMEM_LIMIT = 56 * 1024 * 1024


def _cparams(n_axes, vmem=VMEM_LIMIT):
    return pltpu.CompilerParams(dimension_semantics=("arbitrary",) * n_axes, vmem_limit_bytes=vmem)


def _group_of_tile(i, tm):
    return jnp.maximum((i * tm) // DEC_SEQ - (N_PROMPT // DEC_SEQ - 1), 0)


def _sigmoid(x):
    return 1.0 / (1.0 + jnp.exp(-x))


def _silu(x):
    return x * _sigmoid(x)


def _log_sigmoid(x):
    return jnp.minimum(x, 0.0) - jnp.log(1.0 + jnp.exp(-jnp.abs(x)))


def _norm_mod(x, g, shift, scale):
    y = x * lax.rsqrt(jnp.mean(x * x, axis=-1, keepdims=True) + EPS)
    return (y * g) * (1.0 + scale) + shift


def _dot(a, b):
    return jnp.dot(a, b, preferred_element_type=F32)


HALF = D_MODEL // 2
HI_MASK = 0xFFFF0000


def _pack_rows(x):
    lo = pltpu.bitcast(x[:, :HALF].astype(BF16).astype(F32), U32)
    hi = pltpu.bitcast(x[:, HALF:].astype(BF16).astype(F32), U32)
    return (lo >> 16) | (hi & jnp.uint32(HI_MASK))


def _unpack_rows(p):
    return pltpu.bitcast(p << 16, F32), pltpu.bitcast(p & jnp.uint32(HI_MASK), F32)


def _dot_nt(a, b):
    return lax.dot_general(a, b, (((1,), (1,)), ((), ())), preferred_element_type=F32)


def _dot_tn(a, b):
    return lax.dot_general(a, b, (((0,), (0,)), ((), ())), preferred_element_type=F32)


ADA_TN = 1536


def _adaln_body(cond_ref, w_ref, b_ref, o_ref):
    s = _silu(cond_ref[...]).astype(BF16)
    o_ref[...] = _dot(s, w_ref[...].astype(BF16)) + b_ref[...]


def _adaln(cond, ada_w, ada_b):
    n_out = 6 * D_MODEL
    return pl.pallas_call(
        _adaln_body,
        grid=(DEPTH, n_out // ADA_TN),
        in_specs=[
            pl.BlockSpec((MOD_ROWS, D_MODEL), lambda l, j: (0, 0)),
            pl.BlockSpec((None, D_MODEL, ADA_TN), lambda l, j: (l, 0, j)),
            pl.BlockSpec((None, 1, ADA_TN), lambda l, j: (l, 0, j)),
        ],
        out_specs=pl.BlockSpec((None, MOD_ROWS, ADA_TN), lambda l, j: (l, 0, j)),
        out_shape=jax.ShapeDtypeStruct((DEPTH, MOD_ROWS, n_out), F32),
        compiler_params=_cparams(2),
        name="adaln",
    )(cond, ada_w, ada_b.reshape(DEPTH, 1, n_out))


CAST_TR = 256


def _cast_body(w_ref, o_ref):
    o_ref[...] = w_ref[...].astype(BF16)


def _cast_layer_bf16(w, j):
    _, r, c = w.shape
    return pl.pallas_call(
        _cast_body,
        grid=(r // CAST_TR,),
        in_specs=[pl.BlockSpec((None, CAST_TR, c), lambda i: (j, i, 0))],
        out_specs=pl.BlockSpec((CAST_TR, c), lambda i: (i, 0)),
        out_shape=jax.ShapeDtypeStruct((r, c), BF16),
        compiler_params=_cparams(1),
        name="cast_bf16",
    )(w)


def _x_inputs(x):
    n_pt = N_PROMPT // TM
    xa, xb, off = (x[0], x[1], 0) if isinstance(x, tuple) else (x, x, n_pt)
    specs = [pl.BlockSpec((TM, D_MODEL), lambda i: (jnp.minimum(i, n_pt - 1), 0)),
             pl.BlockSpec((TM, D_MODEL), lambda i: (off + jnp.maximum(i - n_pt, 0), 0))]
    return specs, [xa, xb]


def _x_tile(xa_ref, xb_ref):
    return jnp.where(pl.program_id(0) < N_PROMPT // TM, xa_ref[...], xb_ref[...])


def _weight_input(w, layer):
    if layer is None:
        return pl.BlockSpec(w.shape, lambda i: (0, 0)), []
    _, r, c = w.shape
    spec = pl.BlockSpec((None, r, c), lambda i: (layer, 0, 0), pipeline_mode=pl.Buffered(1))
    return spec, [pltpu.VMEM((r, c), BF16)]


def _weight_ref(w_ref, scratch):
    if not scratch:
        return w_ref

    @pl.when(pl.program_id(0) == 0)
    def _():
        scratch[0][...] = w_ref[...].astype(BF16)

    return scratch[0]


def _inproj_body(xa_ref, xb_ref, g_ref, mod_ref, w_ref, *rest, cache_layer, first_cache, n_scratch):
    scratch = rest[len(rest) - n_scratch:]
    rest = rest[:len(rest) - n_scratch]
    o_ref = rest[-3] if cache_layer is not None else rest[-1]
    i = pl.program_id(0)
    w = _weight_ref(w_ref, scratch)
    h = _norm_mod(_x_tile(xa_ref, xb_ref), g_ref[...], mod_ref[0:1, :], mod_ref[1:2, :]).astype(BF16)
    for c in range(o_ref.shape[1] // D_MODEL):
        sl = slice(c * D_MODEL, (c + 1) * D_MODEL)
        r = _dot(h, w[:, sl])
        o_ref[:, sl] = r.astype(o_ref.dtype)
        if cache_layer is not None and c >= 1:
            cache_ref = rest[-3 + c]

            @pl.when(i < BATCH)
            def _(r=r, cache_ref=cache_ref):
                r_t = r.T.reshape(NA_HEADS, NA_HD, SEQ)
                if first_cache:
                    for l in range(N_NA_LAYERS):
                        cache_ref[l] = r_t if l == cache_layer else jnp.zeros_like(r_t)
                else:
                    cache_ref[...] = r_t


def _inproj(x, g, mod, w, w_layer=None, cache_layer=None, prev_caches=None, out_dtype=F32):
    n_out = w.shape[-1]
    x_specs, x_args = _x_inputs(x)
    w_spec, scratch = _weight_input(w, w_layer)
    in_specs = x_specs + [
        pl.BlockSpec((1, D_MODEL), lambda i: (0, 0)),
        pl.BlockSpec((None, MOD_ROWS, D_MODEL), lambda i: (_group_of_tile(i, TM), 0, 0)),
        w_spec,
    ]
    out_specs = [pl.BlockSpec((TM, n_out), lambda i: (i, 0))]
    out_shape = [jax.ShapeDtypeStruct((N_TOK, n_out), out_dtype)]
    args = x_args + [g.reshape(1, D_MODEL), mod, w]
    aliases = {}
    first_cache = prev_caches is None
    if cache_layer is not None:
        assert TM == SEQ and n_out == 3 * D_MODEL
        cache = jax.ShapeDtypeStruct((BATCH, N_NA_LAYERS, NA_HEADS, NA_HD, SEQ), F32)
        if first_cache:
            spec = pl.BlockSpec((None, N_NA_LAYERS, NA_HEADS, NA_HD, SEQ),
                                lambda i: (jnp.minimum(i, BATCH - 1), 0, 0, 0, 0))
        else:
            spec = pl.BlockSpec((None, None, NA_HEADS, NA_HD, SEQ),
                                lambda i: (jnp.minimum(i, BATCH - 1), cache_layer, 0, 0, 0))
            in_specs += [pl.BlockSpec(memory_space=pl.ANY)] * 2
            aliases = {len(args): 1, len(args) + 1: 2}
            args += list(prev_caches)
        out_specs += [spec, spec]
        out_shape += [cache, cache]
    outs = pl.pallas_call(
        functools.partial(_inproj_body, cache_layer=cache_layer, first_cache=first_cache, n_scratch=len(scratch)),
        grid=(N_TOK // TM,),
        in_specs=in_specs,
        out_specs=out_specs,
        out_shape=out_shape,
        scratch_shapes=scratch,
        input_output_aliases=aliases,
        compiler_params=_cparams(1),
        name="inproj",
    )(*args)
    return outs[0] if cache_layer is None else outs


def _decay_terms(dec_ref, h, c):
    lgf = _log_sigmoid(jnp.zeros((c, c), F32) + dec_ref[0, h])
    lgb = _log_sigmoid(jnp.zeros((c, c), F32) + dec_ref[1, h])
    row = lax.broadcasted_iota(I32, (c, c), 0)
    col = lax.broadcasted_iota(I32, (c, c), 1)
    diff = (row - col).astype(F32)
    lower = diff >= 0
    upper = diff <= 0
    mask = (jnp.where(lower, jnp.exp(jnp.where(lower, diff, 0.0) * lgf), 0.0)
            + jnp.where(upper, jnp.exp(jnp.where(upper, -diff, 0.0) * lgb), 0.0))
    pos = lax.broadcasted_iota(I32, (c, 1), 0).astype(F32)
    lgf1 = _log_sigmoid(jnp.zeros((c, 1), F32) + dec_ref[0, h])
    lgb1 = _log_sigmoid(jnp.zeros((c, 1), F32) + dec_ref[1, h])
    lgf0 = _log_sigmoid(jnp.zeros((1, 1), F32) + dec_ref[0, h])
    lgb0 = _log_sigmoid(jnp.zeros((1, 1), F32) + dec_ref[1, h])
    terms = dict(
        mask=mask,
        q_f=jnp.exp((pos + 1.0) * lgf1), k_f=jnp.exp((c - 1.0 - pos) * lgf1), c_f=jnp.exp(c * lgf0),
        q_b=jnp.exp((c - pos) * lgb1), k_b=jnp.exp(pos * lgb1), c_b=jnp.exp(c * lgb0),
    )
    return terms


def _group_norm_gate(o, g):
    o = o * lax.rsqrt(jnp.mean(o * o, axis=-1, keepdims=True) + EPS)
    return (_silu(g) * o).astype(BF16)


def _ret_prompt_body(dec_ref, q_ref, k_ref, v_ref, g_ref, *rest, layer, first):
    o_ref, sf_ref, sb_ref = rest[-3:]
    for h in range(RET_HEADS):
        t = _decay_terms(dec_ref, h, SEQ)
        qk_cols = slice(h * RET_DK, (h + 1) * RET_DK)
        v_cols = slice(h * RET_DV, (h + 1) * RET_DV)
        q = q_ref[:, qk_cols].astype(F32)
        k = k_ref[:, qk_cols].astype(F32) * (RET_DK ** -0.5)
        v = v_ref[:, v_cols].astype(BF16)
        scores = _dot_nt(q.astype(BF16), k.astype(BF16)) * t["mask"]
        o = _dot(scores.astype(BF16), v)
        o_ref[:, v_cols] = _group_norm_gate(o, g_ref[:, v_cols].astype(F32))
        for ref, k_dec in ((sf_ref, t["k_f"]), (sb_ref, t["k_b"])):
            s = _dot_tn((k * k_dec).astype(BF16), v)
            if first:
                for l in range(N_RET_LAYERS):
                    ref[l, h] = s if l == layer else jnp.zeros_like(s)
            else:
                ref[h] = s


def _ret_prompt(proj, decay, layer, prev_states=None):
    assert 2 * RET_QK_W == RET_V_W
    state = jax.ShapeDtypeStruct((BATCH, N_RET_LAYERS, RET_HEADS, RET_DK, RET_DV), F32)
    in_specs = [
        pl.BlockSpec(memory_space=pltpu.SMEM),
        pl.BlockSpec((SEQ, RET_QK_W), lambda b: (b, 0)),
        pl.BlockSpec((SEQ, RET_QK_W), lambda b: (b, 1)),
        pl.BlockSpec((SEQ, RET_V_W), lambda b: (b, 1)),
        pl.BlockSpec((SEQ, RET_V_W), lambda b: (b, 2)),
    ]
    args = [decay, proj, proj, proj, proj]
    first = prev_states is None
    aliases = {}
    if first:
        s_spec = pl.BlockSpec((None, N_RET_LAYERS, RET_HEADS, RET_DK, RET_DV), lambda b: (b, 0, 0, 0, 0))
    else:
        s_spec = pl.BlockSpec((None, None, RET_HEADS, RET_DK, RET_DV), lambda b: (b, layer, 0, 0, 0))
        in_specs += [pl.BlockSpec(memory_space=pl.ANY)] * 2
        aliases = {len(args): 1, len(args) + 1: 2}
        args += list(prev_states)
    return pl.pallas_call(
        functools.partial(_ret_prompt_body, layer=layer, first=first),
        grid=(BATCH,),
        in_specs=in_specs,
        out_specs=[pl.BlockSpec((SEQ, RET_V_W), lambda b: (b, 0)), s_spec, s_spec],
        out_shape=[jax.ShapeDtypeStruct((N_PROMPT, RET_V_W), BF16), state, state],
        input_output_aliases=aliases,
        compiler_params=_cparams(1),
        name="ret_prompt",
    )(*args)


def _rope(x, cos, sin_signed):
    n = x.shape[-1]
    lane = lax.broadcasted_iota(I32, x.shape, 1)
    swapped = jnp.where(lane % 2 == 0, pltpu.roll(x, n - 1, 1), pltpu.roll(x, 1, 1))
    return x * cos + swapped * sin_signed


def _ret_sample_body(dec_ref, q_ref, k_ref, v_ref, g_ref, cos_ref, sin_ref, s0f_ref, s0b_ref, o_ref,
                     sf_scr, sb_scr):
    h = pl.program_id(1)
    c = RET_CHUNK
    n_chunks = DEC_SEQ // c
    t = _decay_terms(dec_ref, h, c)

    def chunk(ref, i):
        return ref[i * c:(i + 1) * c, :].astype(F32)

    def qk(i):
        q = _rope(chunk(q_ref, i), chunk(cos_ref, i), chunk(sin_ref, i))
        k = _rope(chunk(k_ref, i) * (RET_DK ** -0.5), chunk(cos_ref, i), chunk(sin_ref, i))
        return q, k

    s = s0f_ref[...]
    for i in range(n_chunks):
        sf_scr[i] = s
        if i + 1 < n_chunks:
            _, k = qk(i)
            s = s * t["c_f"] + _dot_tn((k * t["k_f"]).astype(BF16), chunk(v_ref, i).astype(BF16))
    s = s0b_ref[...]
    for i in reversed(range(n_chunks)):
        sb_scr[i] = s
        if i > 0:
            _, k = qk(i)
            s = s * t["c_b"] + _dot_tn((k * t["k_b"]).astype(BF16), chunk(v_ref, i).astype(BF16))

    for i in range(n_chunks):
        q, k = qk(i)
        scores = _dot_nt(q.astype(BF16), k.astype(BF16)) * t["mask"]
        o = _dot(scores.astype(BF16), chunk(v_ref, i).astype(BF16))
        o = o + _dot((q * t["q_f"]).astype(BF16), sf_scr[i].astype(BF16))
        o = o + _dot((q * t["q_b"]).astype(BF16), sb_scr[i].astype(BF16))
        o_ref[i * c:(i + 1) * c, :] = _group_norm_gate(o, chunk(g_ref, i))


def _ret_sample(proj, decay, cos, sin_signed, s0f, s0b, j):
    kq = RET_QK_W // RET_DK
    kv = 2 * RET_QK_W // RET_DV
    kg = kv + RET_HEADS
    rb = N_PROMPT // DEC_SEQ
    n_chunks = DEC_SEQ // RET_CHUNK
    return pl.pallas_call(
        _ret_sample_body,
        grid=(DEC_BATCH, RET_HEADS),
        in_specs=[
            pl.BlockSpec(memory_space=pltpu.SMEM),
            pl.BlockSpec((DEC_SEQ, RET_DK), lambda b, h: (rb + b, h)),
            pl.BlockSpec((DEC_SEQ, RET_DK), lambda b, h: (rb + b, kq + h)),
            pl.BlockSpec((DEC_SEQ, RET_DV), lambda b, h: (rb + b, kv + h)),
            pl.BlockSpec((DEC_SEQ, RET_DV), lambda b, h: (rb + b, kg + h)),
            pl.BlockSpec((DEC_SEQ, RET_DK), lambda b, h: (0, 0)),
            pl.BlockSpec((DEC_SEQ, RET_DK), lambda b, h: (0, 0)),
            pl.BlockSpec((None, None, None, RET_DK, RET_DV), lambda b, h: (b, j, h, 0, 0)),
            pl.BlockSpec((None, None, None, RET_DK, RET_DV), lambda b, h: (b, j, h, 0, 0)),
        ],
        out_specs=pl.BlockSpec((DEC_SEQ, RET_DV), lambda b, h: (b, h)),
        out_shape=jax.ShapeDtypeStruct((N_SAMPLE, RET_V_W), BF16),
        scratch_shapes=[pltpu.VMEM((n_chunks, RET_DK, RET_DV), F32),
                        pltpu.VMEM((n_chunks, RET_DK, RET_DV), F32)],
        compiler_params=_cparams(2),
        name="ret_sample",
    )(decay, proj, proj, proj, proj, cos, sin_signed, s0f, s0b)


NA_HPB = LANES // NA_HD


def _head_lanes(rows, hh):
    lane = lax.broadcasted_iota(I32, (rows, LANES), 1)
    return jnp.logical_and(lane >= hh * NA_HD, lane < (hh + 1) * NA_HD)


def _with_ones(v):
    return jnp.concatenate([v, jnp.ones(v.shape, v.dtype)], axis=-1)


def _ctx_attn_body(q_ref, k_ref, v_ref, o_ref):
    outs = []
    for p in range(NA_HEADS // NA_HPB):
        sl = slice(p * LANES, (p + 1) * LANES)
        q2 = q_ref[:, sl] * (NA_HD ** -0.5)
        k2 = k_ref[:, sl].astype(BF16)
        v2 = _with_ones(v_ref[:, sl].astype(BF16))
        o_pair = None
        for hh in range(NA_HPB):
            mine = _head_lanes(SEQ, hh)
            s = _dot_nt(jnp.where(mine, q2, 0.0).astype(BF16), k2)
            e = jnp.exp(s - s.max(axis=-1, keepdims=True)).astype(BF16)
            acc = _dot(e, v2)
            o_h = acc[:, :LANES] / acc[:, LANES:]
            o_pair = o_h if o_pair is None else jnp.where(mine, o_h, o_pair)
        outs.append(o_pair)
    o_ref[...] = jnp.concatenate(outs, axis=-1).astype(BF16)


def _ctx_attn(proj):
    return pl.pallas_call(
        _ctx_attn_body,
        grid=(BATCH,),
        in_specs=[
            pl.BlockSpec((SEQ, D_MODEL), lambda b: (b, 0)),
            pl.BlockSpec((SEQ, D_MODEL), lambda b: (b, 1)),
            pl.BlockSpec((SEQ, D_MODEL), lambda b: (b, 2)),
        ],
        out_specs=pl.BlockSpec((SEQ, D_MODEL), lambda b: (b, 0)),
        out_shape=jax.ShapeDtypeStruct((N_PROMPT, D_MODEL), BF16),
        compiler_params=_cparams(1),
        name="ctx_attn",
    )(proj, proj, proj)


NA_QT = 256
NA_WIN = NA_KH * GRID_W


def _na_attn_body(q_ref, k_ref, v_ref, ck_ref, cv_ref, tbl_ref, o_ref):
    t = pl.program_id(1)
    rows_per_tile = NA_QT // GRID_W
    n_rows = DEC_SEQ // GRID_W
    q2 = q_ref[...] * (NA_HD ** -0.5)
    q = [jnp.where(_head_lanes(NA_QT, hh), q2, 0.0).astype(BF16) for hh in range(NA_HPB)]
    s_ctx = _dot_nt(jnp.concatenate(q, axis=0), ck_ref[...].astype(BF16))
    acc_win = []
    e_ctx = []
    for u in range(rows_per_tile):
        qr = t * rows_per_tile + u
        rs = jnp.clip(qr - NA_KH // 2, 0, n_rows - NA_KH)
        i0 = rs - qr + (NA_KH - 1)
        start = pl.multiple_of(rs * GRID_W, GRID_W)
        kw = k_ref[pl.ds(start, NA_WIN), :].astype(BF16)
        vw = _with_ones(v_ref[pl.ds(start, NA_WIN), :].astype(BF16))
        rows = slice(u * GRID_W, (u + 1) * GRID_W)
        bias = jnp.concatenate(
            [jnp.concatenate([tbl_ref[hh, i0 + 2 * m] for m in range(NA_KH // 2)], axis=-1) for hh in range(NA_HPB)],
            axis=0)
        sw = _dot_nt(jnp.concatenate([q[hh][rows] for hh in range(NA_HPB)], axis=0), kw) + bias
        sc = jnp.concatenate([s_ctx[hh * NA_QT + u * GRID_W:hh * NA_QT + (u + 1) * GRID_W] for hh in range(NA_HPB)],
                             axis=0)
        m = jnp.maximum(sw.max(axis=-1, keepdims=True), sc.max(axis=-1, keepdims=True))
        acc_win.append(_dot(jnp.exp(sw - m).astype(BF16), vw))
        e_ctx.append(jnp.exp(sc - m).astype(BF16))
    acc = jnp.concatenate(acc_win, axis=0) + _dot(jnp.concatenate(e_ctx, axis=0), _with_ones(cv_ref[...].astype(BF16)))
    o = acc[:, :LANES] / acc[:, LANES:]
    outs = []
    for u in range(rows_per_tile):
        base = u * NA_HPB * GRID_W
        o_u = o[base:base + GRID_W]
        for hh in range(1, NA_HPB):
            o_u = jnp.where(_head_lanes(GRID_W, hh), o[base + hh * GRID_W:base + (hh + 1) * GRID_W], o_u)
        outs.append(o_u)
    o_ref[...] = jnp.concatenate(outs, axis=0).astype(BF16)


def _na_attn(proj, cache_k, cache_v, tbl, j):
    n_qt = DEC_SEQ // NA_QT
    rb_q = N_PROMPT // NA_QT
    rb_kv = N_PROMPT // DEC_SEQ
    cb = D_MODEL // LANES
    ck = cache_k.reshape(DEC_BATCH, -1, PAST_LEN, D_MODEL)
    cv = cache_v.reshape(DEC_BATCH, -1, PAST_LEN, D_MODEL)
    return pl.pallas_call(
        _na_attn_body,
        grid=(cb, n_qt, DEC_BATCH),
        in_specs=[
            pl.BlockSpec((NA_QT, LANES), lambda p, t, b: (rb_q + b * n_qt + t, p)),
            pl.BlockSpec((DEC_SEQ, LANES), lambda p, t, b: (rb_kv + b, cb + p)),
            pl.BlockSpec((DEC_SEQ, LANES), lambda p, t, b: (rb_kv + b, 2 * cb + p)),
            pl.BlockSpec((None, None, PAST_LEN, LANES), lambda p, t, b: (b, j, 0, p)),
            pl.BlockSpec((None, None, PAST_LEN, LANES), lambda p, t, b: (b, j, 0, p)),
            pl.BlockSpec((None, NA_HPB, 2 * NA_KH - 2, GRID_W, 2 * GRID_W), lambda p, t, b: (j, p, 0, 0, 0)),
        ],
        out_specs=pl.BlockSpec((NA_QT, LANES), lambda p, t, b: (b * n_qt + t, p)),
        out_shape=jax.ShapeDtypeStruct((N_SAMPLE, D_MODEL), BF16),
        compiler_params=_cparams(3),
        name="na_attn",
    )(proj, proj, proj, ck, cv, tbl)


def _na_bias_table(rpb):
    c = np.arange(GRID_W)
    cs = np.clip(c - NA_KW // 2, 0, GRID_W - NA_KW)
    col_valid = (c[None, :] >= cs[:, None]) & (c[None, :] < cs[:, None] + NA_KW)
    col_off = np.clip(c[None, :] - c[:, None] + NA_KW - 1, 0, 2 * NA_KW - 2)
    sel_c = (col_off[..., None] == np.arange(2 * NA_KW - 1)).astype(np.float32)
    t = jnp.einsum("lhij,qkj->lhiqk", rpb, sel_c, precision=lax.Precision.HIGHEST)
    t = jnp.where(col_valid, t, -jnp.inf)
    return jnp.concatenate([t[:, :, :-1], t[:, :, 1:]], axis=-1)


def _outproj_body(ap_ref, as_ref, w_ref, xa_ref, xb_ref, g_ref, mod_ref, wr_ref, br_ref, xo_ref, tok_ref, lg_ref,
                  *scratch):
    i = pl.program_id(0)
    a = jnp.where(i < N_PROMPT // TM, ap_ref[...], as_ref[...])
    x = _x_tile(xa_ref, xb_ref) + mod_ref[2:3, :] * _dot(a, _weight_ref(w_ref, scratch)[...])
    xo_ref[...] = x
    tok = _norm_mod(x, g_ref[...], mod_ref[3:4, :], mod_ref[4:5, :])
    tok_ref[...] = _pack_rows(tok)
    hi = tok.astype(BF16)
    lo = (tok - hi.astype(F32)).astype(BF16)
    wr = wr_ref[...]
    wr_hi = wr.astype(BF16)
    wr_lo = (wr - wr_hi.astype(F32)).astype(BF16)
    hi_terms = _dot(hi, jnp.concatenate([wr_hi, wr_lo], axis=-1))
    lg_ref[...] = hi_terms[:, :LANES] + _dot(lo, wr_hi) + hi_terms[:, LANES:] + br_ref[...]


def _outproj(a_prompt, a_sample, w, w_layer, x, g, mod, wr, br):
    k_in = a_prompt.shape[1]
    n_pt = N_PROMPT // TM
    x_specs, x_args = _x_inputs(x)
    w_spec, scratch = _weight_input(w, w_layer)
    return pl.pallas_call(
        _outproj_body,
        grid=(N_TOK // TM,),
        in_specs=[
            pl.BlockSpec((TM, k_in), lambda i: (jnp.minimum(i, n_pt - 1), 0)),
            pl.BlockSpec((TM, k_in), lambda i: (jnp.maximum(i - n_pt, 0), 0)),
            w_spec,
            *x_specs,
            pl.BlockSpec((1, D_MODEL), lambda i: (0, 0)),
            pl.BlockSpec((None, MOD_ROWS, D_MODEL), lambda i: (_group_of_tile(i, TM), 0, 0)),
            pl.BlockSpec((D_MODEL, LANES), lambda i: (0, 0)),
            pl.BlockSpec((1, LANES), lambda i: (0, 0)),
        ],
        out_specs=[
            pl.BlockSpec((TM, D_MODEL), lambda i: (i, 0)),
            pl.BlockSpec((TM, HALF), lambda i: (i, 0)),
            pl.BlockSpec((TM, LANES), lambda i: (i, 0)),
        ],
        out_shape=[
            jax.ShapeDtypeStruct((N_TOK, D_MODEL), F32),
            jax.ShapeDtypeStruct((N_TOK, HALF), U32),
            jax.ShapeDtypeStruct((N_TOK, LANES), F32),
        ],
        scratch_shapes=scratch,
        compiler_params=_cparams(1),
        name="outproj",
    )(a_prompt, a_sample, w, *x_args, g.reshape(1, D_MODEL), mod, wr, br)


def _router_body(lg_ref, idx_ref, gate_ref, rank_ref, cnt_ref, carry):
    i = pl.program_id(0)

    @pl.when(i == 0)
    def _():
        carry[...] = jnp.zeros_like(carry)

    lane = lax.broadcasted_iota(I32, (ROUTER_TM, LANES), 1)
    l = jnp.where(lane < N_EXPERTS, lg_ref[...], -jnp.inf)
    vals, idxs, hots = [], [], []
    for _ in range(TOP_K):
        m = l.max(axis=-1, keepdims=True)
        idx = jnp.where(l == m, lane, LANES).min(axis=-1, keepdims=True)
        hot = lane == idx
        l = jnp.where(hot, -jnp.inf, l)
        vals.append(m)
        idxs.append(idx)
        hots.append(hot)
    es = [jnp.exp(v - vals[0]) for v in vals]
    denom = es[0] + es[1] + es[2] + es[3]
    hot_all = (hots[0] | hots[1] | hots[2] | hots[3]).astype(F32)
    r = lax.broadcasted_iota(I32, (ROUTER_TM, ROUTER_TM), 0)
    c = lax.broadcasted_iota(I32, (ROUTER_TM, ROUTER_TM), 1)
    before = _dot((c < r).astype(BF16), hot_all.astype(BF16)) + carry[...]
    idx_out = jnp.zeros((ROUTER_TM, LANES), I32)
    gate_out = jnp.zeros((ROUTER_TM, LANES), F32)
    rank_out = jnp.zeros((ROUTER_TM, LANES), I32)
    for k in range(TOP_K):
        rank_k = jnp.where(hots[k], before, 0.0).sum(axis=-1, keepdims=True).astype(I32)
        idx_out = jnp.where(lane == k, idxs[k], idx_out)
        gate_out = jnp.where(lane == k, es[k] / denom, gate_out)
        rank_out = jnp.where(lane == k, rank_k, rank_out)
    idx_ref[...] = idx_out
    gate_ref[...] = gate_out
    rank_ref[...] = rank_out
    carry[...] = carry[...] + hot_all.sum(axis=0, keepdims=True)
    cnt_ref[...] = carry[...].astype(I32)


def _router(logits):
    spec = pl.BlockSpec((ROUTER_TM, LANES), lambda i: (i, 0))
    return pl.pallas_call(
        _router_body,
        grid=(N_TOK // ROUTER_TM,),
        in_specs=[spec],
        out_specs=[spec, spec, spec, pl.BlockSpec((1, LANES), lambda i: (0, 0))],
        out_shape=[
            jax.ShapeDtypeStruct((N_TOK, LANES), I32),
            jax.ShapeDtypeStruct((N_TOK, LANES), F32),
            jax.ShapeDtypeStruct((N_TOK, LANES), I32),
            jax.ShapeDtypeStruct((1, LANES), I32),
        ],
        scratch_shapes=[pltpu.VMEM((1, LANES), F32)],
        compiler_params=_cparams(1),
        name="router",
    )(logits)


SC_CORES = 2
SC_SUBCORES = 16
SC_WORKERS = SC_CORES * SC_SUBCORES
SC_CH = 32
SC_SCATTER_BUFS = 5
SC_GATHER_BUFS = 4


def _sc_mesh():
    return plsc.VectorSubcoreMesh(core_axis_name="c", subcore_axis_name="s",
                                  num_cores=SC_CORES, num_subcores=SC_SUBCORES)


def _sc_worker():
    return lax.axis_index("s") * SC_CORES + lax.axis_index("c")


def _dispatch(tok, pos_km):
    n, d = tok.shape
    n_ch = n // (SC_WORKERS * SC_CH)
    nb = SC_SCATTER_BUFS
    assert n_ch * SC_WORKERS * SC_CH == n and n_ch % nb == 0

    @functools.partial(
        pl.kernel, mesh=_sc_mesh(), out_type=jax.ShapeDtypeStruct((MOE_ROWS, d), tok.dtype),
        scratch_types=[pltpu.VMEM((n_ch, TOP_K, SC_CH), I32), pltpu.VMEM((nb, SC_CH, d), tok.dtype),
                       pltpu.SemaphoreType.DMA((nb,)), pltpu.SemaphoreType.DMA((nb,))])
    def scatter_rows(tok_hbm, pos_hbm, out_hbm, idx_v, rows_v, load_sem, scat_sem):
        wid = _sc_worker()
        base = wid * (n_ch * SC_CH)
        pltpu.sync_copy(pos_hbm.at[wid], idx_v)

        def load(g, slot):
            rows = pl.ds(pl.multiple_of(base + g * SC_CH, 8), SC_CH)
            return pltpu.make_async_copy(tok_hbm.at[rows], rows_v.at[slot], load_sem.at[slot])

        def scatter(g, k, slot):
            return pltpu.make_async_copy(rows_v.at[slot], out_hbm.at[idx_v.at[g, k]], scat_sem.at[slot])

        for s in range(nb - 1):
            load(s, s).start()

        @pl.loop(0, n_ch, step=nb)
        def _(g0):
            for s in range(nb):
                g = g0 + s
                reuse = (s + nb - 1) % nb

                @pl.when(g >= 1)
                def _():
                    for k in range(TOP_K):
                        scatter(g - 1, k, reuse).wait()

                @pl.when(g + nb - 1 < n_ch)
                def _():
                    load(g + nb - 1, reuse).start()

                load(g, s).wait()
                for k in range(TOP_K):
                    scatter(g, k, s).start()

        for k in range(TOP_K):
            scatter(n_ch - 1, k, (n_ch - 1) % nb).wait()

    pos = pos_km.reshape(TOP_K, SC_WORKERS, n_ch, SC_CH).transpose(1, 2, 0, 3)
    return scatter_rows(tok, pos)


def _gather_rows(table, idx):
    b, d = idx.shape[0], table.shape[1]
    n_ch = b // (SC_WORKERS * SC_CH)
    nb = SC_GATHER_BUFS
    assert n_ch * SC_WORKERS * SC_CH == b and n_ch % nb == 0

    @functools.partial(
        pl.kernel, mesh=_sc_mesh(), out_type=jax.ShapeDtypeStruct((b, d), table.dtype),
        scratch_types=[pltpu.VMEM((n_ch, SC_CH), I32), pltpu.VMEM((nb, SC_CH, d), table.dtype),
                       pltpu.SemaphoreType.DMA((nb,)), pltpu.SemaphoreType.DMA((nb,))])
    def gather_rows(table_hbm, idx_hbm, out_hbm, idx_v, rows_v, gather_sem, write_sem):
        wid = _sc_worker()
        base = wid * (n_ch * SC_CH)
        pltpu.sync_copy(idx_hbm.at[wid], idx_v)

        def gather(g, slot):
            return pltpu.make_async_copy(table_hbm.at[idx_v.at[g]], rows_v.at[slot], gather_sem.at[slot])

        def write_out(g, slot):
            rows = pl.ds(pl.multiple_of(base + g * SC_CH, 8), SC_CH)
            return pltpu.make_async_copy(rows_v.at[slot], out_hbm.at[rows], write_sem.at[slot])

        for s in range(nb - 1):
            gather(s, s).start()

        @pl.loop(0, n_ch, step=nb)
        def _(g0):
            for s in range(nb):
                g = g0 + s
                reuse = (s + nb - 1) % nb

                @pl.when(g >= 1)
                def _():
                    write_out(g - 1, reuse).wait()

                @pl.when(g + nb - 1 < n_ch)
                def _():
                    gather(g + nb - 1, reuse).start()

                gather(g, s).wait()
                write_out(g, s).start()

        write_out(n_ch - 1, (n_ch - 1) % nb).wait()

    return gather_rows(table, idx.reshape(SC_WORKERS, n_ch, SC_CH))


def _moe_body(te_ref, tv_ref, slot_ref, nxt_ref, x_ref, w1_hbm, w2_hbm, perm_ref, *rest, layer):
    b_refs = rest[:2 * MOE_TPS]
    o_ref, w1_buf, w2_buf, w1_scr, w2_scr, sem = rest[2 * MOE_TPS:]
    for sub in range(MOE_TPS):
        rows = slice(sub * MOE_TM, (sub + 1) * MOE_TM)
        _moe_tile(te_ref, tv_ref, slot_ref, nxt_ref, pl.program_id(0) * MOE_TPS + sub, x_ref.at[rows], w1_hbm,
                  b_refs[2 * sub], w2_hbm, b_refs[2 * sub + 1], perm_ref, o_ref.at[rows],
                  w1_buf, w2_buf, w1_scr, w2_scr, sem, layer)


def _moe_tile(te_ref, tv_ref, slot_ref, nxt_ref, t, x_ref, w1_hbm, b1_ref, w2_hbm, b2_ref, perm_ref, o_ref,
              w1_buf, w2_buf, w1_scr, w2_scr, sem, layer):
    e = te_ref[t]
    slot = slot_ref[t]
    nxt = nxt_ref[t]
    first = jnp.logical_or(t == 0, e != te_ref[jnp.maximum(t - 1, 0)])

    def fetch(expert, s):
        return (pltpu.make_async_copy(w1_hbm.at[layer, expert], w1_buf.at[s], sem.at[s, 0]),
                pltpu.make_async_copy(w2_hbm.at[layer, expert], w2_buf.at[s], sem.at[s, 1]))

    @pl.when(t == 0)
    def _():
        for cp in fetch(e, slot):
            cp.start()

    @pl.when(jnp.logical_and(tv_ref[t] == 1, first))
    def _():
        for cp in fetch(e, slot):
            cp.wait()

        @pl.when(nxt != e)
        def _():
            for cp in fetch(nxt, 1 - slot):
                cp.start()

        w2_scr[...] = w2_buf[slot].astype(BF16)
        half = PERM_W // 2
        for c in range(2 * D_EXPERT // PERM_W):
            blk = _dot(w1_buf[slot, :, c * PERM_W:(c + 1) * PERM_W].astype(BF16), perm_ref[...]).astype(BF16)
            w1_scr[:, c * half:(c + 1) * half] = blk[:, :half]
            w1_scr[:, D_EXPERT + c * half:D_EXPERT + (c + 1) * half] = blk[:, half:]

    @pl.when(tv_ref[t] == 0)
    def _():
        o_ref[...] = jnp.zeros_like(o_ref)

    @pl.when(tv_ref[t] == 1)
    def _():
        x_lo, x_hi = _unpack_rows(x_ref[...])
        x = jnp.concatenate([x_lo.astype(BF16), x_hi.astype(BF16)], axis=-1)
        hh = _dot(x, w1_scr[...]) + b1_ref[...]
        h_glu = jnp.minimum(hh[:, :D_EXPERT], SWIGLU_LIMIT)
        h_lin = jnp.clip(hh[:, D_EXPERT:], -SWIGLU_LIMIT, SWIGLU_LIMIT)
        a = h_glu * _sigmoid(SWIGLU_ALPHA * h_glu) * (h_lin + 1.0)
        o_ref[...] = _pack_rows(_dot(a.astype(BF16), w2_scr[...]) + b2_ref[...])


def _deinterleave_bias_body(b_ref, perm_ref, o_ref):
    half = PERM_W // 2
    for c in range(2 * D_EXPERT // PERM_W):
        r = b_ref[:, c * PERM_W:(c + 1) * PERM_W]
        acc = jnp.zeros(r.shape, F32)
        for _ in range(3):
            piece = r.astype(BF16)
            acc = acc + _dot(piece, perm_ref[...])
            r = r - piece.astype(F32)
        o_ref[:, c * half:(c + 1) * half] = acc[:, :half]
        o_ref[:, D_EXPERT + c * half:D_EXPERT + (c + 1) * half] = acc[:, half:]


def _deinterleave_bias(b1, perm):
    return pl.pallas_call(
        _deinterleave_bias_body,
        out_shape=jax.ShapeDtypeStruct(b1.shape, F32),
        compiler_params=_cparams(0),
        name="deinterleave_bias",
    )(b1, perm)


def _moe_experts(tile_expert, tile_valid, tile_slot, tile_next, xs, w1, b1_split, w2, b2, perm, layer):
    bias_specs = []
    for sub in range(MOE_TPS):
        bias_specs += [
            pl.BlockSpec((None, None, 1, 2 * D_EXPERT), lambda s, te, *_, sub=sub: (layer, te[s * MOE_TPS + sub], 0, 0)),
            pl.BlockSpec((None, None, 1, D_MODEL), lambda s, te, *_, sub=sub: (layer, te[s * MOE_TPS + sub], 0, 0)),
        ]
    grid_spec = pltpu.PrefetchScalarGridSpec(
        num_scalar_prefetch=4,
        grid=(MOE_TILES // MOE_TPS,),
        in_specs=[
            pl.BlockSpec((MOE_TPS * MOE_TM, HALF), lambda s, *_: (s, 0)),
            pl.BlockSpec(memory_space=pl.ANY),
            pl.BlockSpec(memory_space=pl.ANY),
            pl.BlockSpec((PERM_W, PERM_W), lambda s, *_: (0, 0)),
            *bias_specs,
        ],
        out_specs=pl.BlockSpec((MOE_TPS * MOE_TM, HALF), lambda s, *_: (s, 0)),
        scratch_shapes=[
            pltpu.VMEM((2, D_MODEL, 2 * D_EXPERT), F32),
            pltpu.VMEM((2, D_EXPERT, D_MODEL), F32),
            pltpu.VMEM((D_MODEL, 2 * D_EXPERT), BF16),
            pltpu.VMEM((D_EXPERT, D_MODEL), BF16),
            pltpu.SemaphoreType.DMA((2, 2)),
        ],
    )
    return pl.pallas_call(
        functools.partial(_moe_body, layer=layer),
        grid_spec=grid_spec,
        out_shape=jax.ShapeDtypeStruct((MOE_ROWS, HALF), U32),
        compiler_params=_cparams(1),
        name="moe_experts",
    )(tile_expert, tile_valid, tile_slot, tile_next, xs, w1, w2, perm, *([b1_split, b2] * MOE_TPS))


COMBINE_TM = 512


def _combine_body(*refs, final):
    y_refs = refs[:TOP_K]
    gate_ref, x_ref, mod_ref, g_ref = refs[TOP_K:TOP_K + 4]
    gates = gate_ref[...]
    y_lo, y_hi = (gates[:, 0:1] * h for h in _unpack_rows(y_refs[0][...]))
    for k in range(1, TOP_K):
        lo, hi = _unpack_rows(y_refs[k][...])
        y_lo = y_lo + gates[:, k:k + 1] * lo
        y_hi = y_hi + gates[:, k:k + 1] * hi
    x = x_ref[...] + mod_ref[5:6, :] * jnp.concatenate([y_lo, y_hi], axis=-1)
    if not final:
        refs[-1][...] = x
        return
    x = (x * lax.rsqrt(jnp.mean(x * x, axis=-1, keepdims=True) + EPS)) * g_ref[...]
    prompt_ref, sample_ref = refs[-2:]
    is_prompt = pl.program_id(0) < N_PROMPT // COMBINE_TM

    @pl.when(is_prompt)
    def _():
        prompt_ref[...] = x

    @pl.when(jnp.logical_not(is_prompt))
    def _():
        sample_ref[...] = x


def _combine(y_km, gates, x, mod, g_final, final):
    n_t = N_TOK // COMBINE_TM
    n_pt = N_PROMPT // COMBINE_TM
    y_specs = [pl.BlockSpec((COMBINE_TM, HALF), lambda i, k=k: (k * n_t + i, 0)) for k in range(TOP_K)]
    if final:
        out_specs = [pl.BlockSpec((COMBINE_TM, D_MODEL), lambda i: (jnp.minimum(i, n_pt - 1), 0)),
                     pl.BlockSpec((COMBINE_TM, D_MODEL), lambda i: (jnp.maximum(i - n_pt, 0), 0))]
        out_shape = [jax.ShapeDtypeStruct((N_PROMPT, D_MODEL), F32), jax.ShapeDtypeStruct((N_SAMPLE, D_MODEL), F32)]
    else:
        out_specs = pl.BlockSpec((COMBINE_TM, D_MODEL), lambda i: (i, 0))
        out_shape = jax.ShapeDtypeStruct((N_TOK, D_MODEL), F32)
    return pl.pallas_call(
        functools.partial(_combine_body, final=final),
        grid=(n_t,),
        in_specs=y_specs + [
            pl.BlockSpec((COMBINE_TM, LANES), lambda i: (i, 0)),
            pl.BlockSpec((COMBINE_TM, D_MODEL), lambda i: (i, 0)),
            pl.BlockSpec((None, MOD_ROWS, D_MODEL), lambda i: (_group_of_tile(i, COMBINE_TM), 0, 0)),
            pl.BlockSpec((1, D_MODEL), lambda i: (0, 0)),
        ],
        out_specs=out_specs,
        out_shape=out_shape,
        compiler_params=_cparams(1),
        name="combine",
    )(*([y_km] * TOP_K), gates, x, mod, g_final.reshape(1, D_MODEL))


def _routing_tables(idx, rank, counts):
    counts = counts[0, :N_EXPERTS]
    padded = ((counts + MOE_TM - 1) // MOE_TM) * MOE_TM
    ends = jnp.cumsum(padded)
    offsets = ends - padded
    idx = idx[:, :TOP_K]
    pos = rank[:, :TOP_K] + jnp.sum(
        jnp.where(idx[..., None] == jnp.arange(N_EXPERTS, dtype=I32), offsets, 0), axis=-1)
    pos_km = pos.T.astype(I32)
    tile_start = jnp.arange(MOE_TILES, dtype=I32) * MOE_TM
    tile_valid = (tile_start < ends[-1]).astype(I32)
    tile_expert = jnp.sum((tile_start[:, None] >= ends[None, :]).astype(I32), axis=-1)
    last_expert = jnp.sum((ends[-1] - 1 >= ends).astype(I32))
    tile_expert = jnp.where(tile_valid == 1, tile_expert, last_expert).astype(I32)
    e_ids = jnp.arange(N_EXPERTS, dtype=I32)
    owns = counts > 0
    ordinal = jnp.cumsum(owns.astype(I32)) - 1
    later = jnp.where(owns[None, :] & (e_ids[None, :] > e_ids[:, None]), e_ids[None, :], N_EXPERTS)
    nxt = jnp.min(later, axis=1)
    nxt = jnp.where(nxt == N_EXPERTS, e_ids, nxt)
    onehot = (tile_expert[:, None] == e_ids[None, :]).astype(I32)
    tile_slot = jnp.sum(onehot * (ordinal % 2)[None, :], axis=1).astype(I32)
    tile_next = jnp.sum(onehot * nxt[None, :], axis=1).astype(I32)
    return pos_km, tile_expert, tile_valid, tile_slot, tile_next


def _rope_tables():
    t = np.arange(DEC_SEQ)
    row = (t // GRID_W).astype(np.float32)
    col = (t % GRID_W).astype(np.float32)
    d_axis = RET_DK // 2
    inv = jnp.asarray(ROPE_BASE, F32) ** (-jnp.arange(0, d_axis, 2, dtype=F32) / d_axis)
    ang = jnp.concatenate([row[:, None] * inv, col[:, None] * inv], axis=-1)
    cos = jnp.repeat(jnp.cos(ang), 2, axis=-1)
    sin = jnp.repeat(jnp.sin(ang), 2, axis=-1)
    sign = jnp.asarray(np.tile(np.array([-1.0, 1.0], np.float32), RET_DK // 2))
    return cos, sin * sign


def _deinterleave_matrix():
    p = np.zeros((PERM_W, PERM_W), np.float32)
    half = PERM_W // 2
    for j in range(half):
        p[2 * j, j] = 1.0
        p[2 * j + 1, half + j] = 1.0
    return jnp.asarray(p, BF16)


def kernel(x_prompt, x_sample, state_ret_fwd, state_ret_bwd, cache_na_k, cache_na_v, c, c_ctx, ada_w, ada_b, norm_mix, norm_ffn, norm_final, ret_w_in, ret_w_out, ret_decay_fwd, ret_decay_bwd, na_w_in, na_w_out, na_rpb, moe_w_router, moe_b_router, moe_w1, moe_b1, moe_w2, moe_b2):
    x = (x_prompt.reshape(N_PROMPT, D_MODEL), x_sample.reshape(N_SAMPLE, D_MODEL))

    cond = jnp.zeros((MOD_ROWS, D_MODEL), F32).at[0].set(c_ctx).at[1:1 + DEC_BATCH].set(c)
    mod_all = _adaln(cond, ada_w, ada_b)
    mod_all = mod_all[:, :N_GROUPS].reshape(DEPTH, N_GROUPS, 6, D_MODEL)
    mod_all = jnp.pad(mod_all, ((0, 0), (0, 0), (0, MOD_ROWS - 6), (0, 0)))

    cos, sin_signed = _rope_tables()
    perm = _deinterleave_matrix()
    wr_all = jnp.pad(moe_w_router, ((0, 0), (0, 0), (0, LANES - N_EXPERTS)))
    br_all = jnp.pad(moe_b_router, ((0, 0), (0, LANES - N_EXPERTS))).reshape(DEPTH, 1, LANES)
    b1_split = _deinterleave_bias(moe_b1.reshape(DEPTH * N_EXPERTS, 2 * D_EXPERT), perm)
    b1_split = b1_split.reshape(DEPTH, N_EXPERTS, 1, 2 * D_EXPERT)
    b2_all = moe_b2.reshape(DEPTH, N_EXPERTS, 1, D_MODEL)
    na_tbl = _na_bias_table(na_rpb)
    ret_states = None
    na_caches = None
    for layer in range(DEPTH):
        mod = mod_all[layer]
        j = layer // 2
        if layer % 2 == 0:
            proj = _inproj(x, norm_mix[layer], mod, _cast_layer_bf16(ret_w_in, j), out_dtype=BF16)
            decay = jnp.stack([ret_decay_fwd[j], ret_decay_bwd[j]])
            a_p, s_f, s_b = _ret_prompt(proj, decay, j, ret_states)
            ret_states = (s_f, s_b)
            a_s = _ret_sample(proj, decay, cos, sin_signed, state_ret_fwd, state_ret_bwd, j)
            w_out = ret_w_out
        else:
            proj, ck, cv = _inproj(x, norm_mix[layer], mod, na_w_in, j, j, na_caches)
            na_caches = (ck, cv)
            a_p = _ctx_attn(proj)
            a_s = _na_attn(proj, cache_na_k, cache_na_v, na_tbl, j)
            w_out = na_w_out

        x, tok, logits = _outproj(a_p, a_s, w_out, j, x, norm_ffn[layer], mod, wr_all[layer], br_all[layer])

        idx, gates, rank, counts = _router(logits)
        pos_km, tile_expert, tile_valid, tile_slot, tile_next = _routing_tables(idx, rank, counts)
        xs = _dispatch(tok, pos_km)
        ys = _moe_experts(tile_expert, tile_valid, tile_slot, tile_next, xs, moe_w1, b1_split, moe_w2, b2_all,
                          perm, layer)
        y_km = _gather_rows(ys, pos_km.reshape(-1))
        x = _combine(y_km, gates, x, mod, norm_final, final=(layer == DEPTH - 1))

    y_prompt = x[0].reshape(BATCH, SEQ, D_MODEL)
    y_sample = x[1].reshape(DEC_BATCH, DEC_SEQ, D_MODEL)
    return (y_prompt, y_sample, ret_states[0], ret_states[1],
            jnp.transpose(na_caches[0], (0, 1, 4, 2, 3)), jnp.transpose(na_caches[1], (0, 1, 4, 2, 3)))
```

```python
import functools

import numpy as np
import jax
import jax.numpy as jnp
from jax import lax
from jax.experimental import pallas as pl
from jax.experimental.pallas import tpu as pltpu
from jax.experimental.pallas import tpu_sc as plsc

F32 = jnp.float32
BF16 = jnp.bfloat16
I32 = jnp.int32
U32 = jnp.uint32

D_MODEL = 1024
BATCH = 32
SEQ = 256
DEPTH = 4
DEC_BATCH = 2
DEC_SEQ = 1024
PAST_LEN = 512
GRID_W = 64
EPS = 1e-6
ROPE_BASE = 10000.0
RET_HEADS = 4
RET_DK = D_MODEL // RET_HEADS
RET_DV = 2 * RET_DK
RET_QK_W = RET_HEADS * RET_DK
RET_V_W = RET_HEADS * RET_DV
NA_HEADS = 16
NA_HD = D_MODEL // NA_HEADS
NA_KH = 8
NA_KW = 16
N_EXPERTS = 32
TOP_K = 4
D_EXPERT = D_MODEL
SWIGLU_LIMIT = 7.0
SWIGLU_ALPHA = 1.702
N_RET_LAYERS = (DEPTH + 1) // 2
N_NA_LAYERS = DEPTH // 2

N_PROMPT = BATCH * SEQ
N_SAMPLE = DEC_BATCH * DEC_SEQ
N_TOK = N_PROMPT + N_SAMPLE
N_GROUPS = 1 + DEC_BATCH
MOD_ROWS = 8
LANES = 128

TM = 256
RET_CHUNK = 256
MOE_TM = 256
MOE_ROWS = N_TOK * TOP_K + N_EXPERTS * MOE_TM
MOE_TILES = MOE_ROWS // MOE_TM
MOE_TPS = 2
ROUTER_TM = 512
PERM_W = 256
VMEM_LIMIT = 56 * 1024 * 1024


def _cparams(n_axes, vmem=VMEM_LIMIT):
    return pltpu.CompilerParams(dimension_semantics=("arbitrary",) * n_axes, vmem_limit_bytes=vmem)


def _group_of_tile(i, tm):
    return jnp.maximum((i * tm) // DEC_SEQ - (N_PROMPT // DEC_SEQ - 1), 0)


def _sigmoid(x):
    return 1.0 / (1.0 + jnp.exp(-x))


def _silu(x):
    return x * _sigmoid(x)


def _log_sigmoid(x):
    return jnp.minimum(x, 0.0) - jnp.log(1.0 + jnp.exp(-jnp.abs(x)))


def _norm_mod(x, g, shift, scale):
    y = x * lax.rsqrt(jnp.mean(x * x, axis=-1, keepdims=True) + EPS)
    return (y * g) * (1.0 + scale) + shift


def _dot(a, b):
    return jnp.dot(a, b, preferred_element_type=F32)


HALF = D_MODEL // 2
HI_MASK = 0xFFFF0000


def _pack_rows(x):
    lo = pltpu.bitcast(x[:, :HALF].astype(BF16).astype(F32), U32)
    hi = pltpu.bitcast(x[:, HALF:].astype(BF16).astype(F32), U32)
    return (lo >> 16) | (hi & jnp.uint32(HI_MASK))


def _unpack_rows(p):
    return pltpu.bitcast(p << 16, F32), pltpu.bitcast(p & jnp.uint32(HI_MASK), F32)


def _dot_nt(a, b):
    return lax.dot_general(a, b, (((1,), (1,)), ((), ())), preferred_element_type=F32)


def _dot_tn(a, b):
    return lax.dot_general(a, b, (((0,), (0,)), ((), ())), preferred_element_type=F32)


ADA_TN = 1536


def _adaln_body(cond_ref, w_ref, b_ref, o_ref):
    s = _silu(cond_ref[...]).astype(BF16)
    o_ref[...] = _dot(s, w_ref[...].astype(BF16)) + b_ref[...]


def _adaln(cond, ada_w, ada_b):
    n_out = 6 * D_MODEL
    return pl.pallas_call(
        _adaln_body,
        grid=(DEPTH, n_out // ADA_TN),
        in_specs=[
            pl.BlockSpec((MOD_ROWS, D_MODEL), lambda l, j: (0, 0)),
            pl.BlockSpec((None, D_MODEL, ADA_TN), lambda l, j: (l, 0, j)),
            pl.BlockSpec((None, 1, ADA_TN), lambda l, j: (l, 0, j)),
        ],
        out_specs=pl.BlockSpec((None, MOD_ROWS, ADA_TN), lambda l, j: (l, 0, j)),
        out_shape=jax.ShapeDtypeStruct((DEPTH, MOD_ROWS, n_out), F32),
        compiler_params=_cparams(2),
        name="adaln",
    )(cond, ada_w, ada_b.reshape(DEPTH, 1, n_out))


CAST_TR = 256


def _cast_body(w_ref, o_ref):
    o_ref[...] = w_ref[...].astype(BF16)


def _cast_layer_bf16(w, j):
    _, r, c = w.shape
    return pl.pallas_call(
        _cast_body,
        grid=(r // CAST_TR,),
        in_specs=[pl.BlockSpec((None, CAST_TR, c), lambda i: (j, i, 0))],
        out_specs=pl.BlockSpec((CAST_TR, c), lambda i: (i, 0)),
        out_shape=jax.ShapeDtypeStruct((r, c), BF16),
        compiler_params=_cparams(1),
        name="cast_bf16",
    )(w)


def _x_inputs(x):
    n_pt = N_PROMPT // TM
    xa, xb, off = (x[0], x[1], 0) if isinstance(x, tuple) else (x, x, n_pt)
    specs = [pl.BlockSpec((TM, D_MODEL), lambda i: (jnp.minimum(i, n_pt - 1), 0)),
             pl.BlockSpec((TM, D_MODEL), lambda i: (off + jnp.maximum(i - n_pt, 0), 0))]
    return specs, [xa, xb]


def _x_tile(xa_ref, xb_ref):
    return jnp.where(pl.program_id(0) < N_PROMPT // TM, xa_ref[...], xb_ref[...])


def _weight_input(w, layer):
    if layer is None:
        return pl.BlockSpec(w.shape, lambda i: (0, 0)), []
    _, r, c = w.shape
    spec = pl.BlockSpec((None, r, c), lambda i: (layer, 0, 0), pipeline_mode=pl.Buffered(1))
    return spec, [pltpu.VMEM((r, c), BF16)]


def _weight_ref(w_ref, scratch):
    if not scratch:
        return w_ref

    @pl.when(pl.program_id(0) == 0)
    def _():
        scratch[0][...] = w_ref[...].astype(BF16)

    return scratch[0]


def _inproj_body(xa_ref, xb_ref, g_ref, mod_ref, w_ref, *rest, cache_layer, first_cache, n_scratch):
    scratch = rest[len(rest) - n_scratch:]
    rest = rest[:len(rest) - n_scratch]
    o_ref = rest[-3] if cache_layer is not None else rest[-1]
    i = pl.program_id(0)
    w = _weight_ref(w_ref, scratch)
    h = _norm_mod(_x_tile(xa_ref, xb_ref), g_ref[...], mod_ref[0:1, :], mod_ref[1:2, :]).astype(BF16)
    for c in range(o_ref.shape[1] // D_MODEL):
        sl = slice(c * D_MODEL, (c + 1) * D_MODEL)
        r = _dot(h, w[:, sl])
        o_ref[:, sl] = r.astype(o_ref.dtype)
        if cache_layer is not None and c >= 1:
            cache_ref = rest[-3 + c]

            @pl.when(i < BATCH)
            def _(r=r, cache_ref=cache_ref):
                r_t = r.T.reshape(NA_HEADS, NA_HD, SEQ)
                if first_cache:
                    for l in range(N_NA_LAYERS):
                        cache_ref[l] = r_t if l == cache_layer else jnp.zeros_like(r_t)
                else:
                    cache_ref[...] = r_t


def _inproj(x, g, mod, w, w_layer=None, cache_layer=None, prev_caches=None, out_dtype=F32):
    n_out = w.shape[-1]
    x_specs, x_args = _x_inputs(x)
    w_spec, scratch = _weight_input(w, w_layer)
    in_specs = x_specs + [
        pl.BlockSpec((1, D_MODEL), lambda i: (0, 0)),
        pl.BlockSpec((None, MOD_ROWS, D_MODEL), lambda i: (_group_of_tile(i, TM), 0, 0)),
        w_spec,
    ]
    out_specs = [pl.BlockSpec((TM, n_out), lambda i: (i, 0))]
    out_shape = [jax.ShapeDtypeStruct((N_TOK, n_out), out_dtype)]
    args = x_args + [g.reshape(1, D_MODEL), mod, w]
    aliases = {}
    first_cache = prev_caches is None
    if cache_layer is not None:
        assert TM == SEQ and n_out == 3 * D_MODEL
        cache = jax.ShapeDtypeStruct((BATCH, N_NA_LAYERS, NA_HEADS, NA_HD, SEQ), F32)
        if first_cache:
            spec = pl.BlockSpec((None, N_NA_LAYERS, NA_HEADS, NA_HD, SEQ),
                                lambda i: (jnp.minimum(i, BATCH - 1), 0, 0, 0, 0))
        else:
            spec = pl.BlockSpec((None, None, NA_HEADS, NA_HD, SEQ),
                                lambda i: (jnp.minimum(i, BATCH - 1), cache_layer, 0, 0, 0))
            in_specs += [pl.BlockSpec(memory_space=pl.ANY)] * 2
            aliases = {len(args): 1, len(args) + 1: 2}
            args += list(prev_caches)
        out_specs += [spec, spec]
        out_shape += [cache, cache]
    outs = pl.pallas_call(
        functools.partial(_inproj_body, cache_layer=cache_layer, first_cache=first_cache, n_scratch=len(scratch)),
        grid=(N_TOK // TM,),
        in_specs=in_specs,
        out_specs=out_specs,
        out_shape=out_shape,
        scratch_shapes=scratch,
        input_output_aliases=aliases,
        compiler_params=_cparams(1),
        name="inproj",
    )(*args)
    return outs[0] if cache_layer is None else outs


def _decay_terms(dec_ref, h, c):
    lgf = _log_sigmoid(jnp.zeros((c, c), F32) + dec_ref[0, h])
    lgb = _log_sigmoid(jnp.zeros((c, c), F32) + dec_ref[1, h])
    row = lax.broadcasted_iota(I32, (c, c), 0)
    col = lax.broadcasted_iota(I32, (c, c), 1)
    diff = (row - col).astype(F32)
    lower = diff >= 0
    upper = diff <= 0
    mask = (jnp.where(lower, jnp.exp(jnp.where(lower, diff, 0.0) * lgf), 0.0)
            + jnp.where(upper, jnp.exp(jnp.where(upper, -diff, 0.0) * lgb), 0.0))
    pos = lax.broadcasted_iota(I32, (c, 1), 0).astype(F32)
    lgf1 = _log_sigmoid(jnp.zeros((c, 1), F32) + dec_ref[0, h])
    lgb1 = _log_sigmoid(jnp.zeros((c, 1), F32) + dec_ref[1, h])
    lgf0 = _log_sigmoid(jnp.zeros((1, 1), F32) + dec_ref[0, h])
    lgb0 = _log_sigmoid(jnp.zeros((1, 1), F32) + dec_ref[1, h])
    terms = dict(
        mask=mask,
        q_f=jnp.exp((pos + 1.0) * lgf1), k_f=jnp.exp((c - 1.0 - pos) * lgf1), c_f=jnp.exp(c * lgf0),
        q_b=jnp.exp((c - pos) * lgb1), k_b=jnp.exp(pos * lgb1), c_b=jnp.exp(c * lgb0),
    )
    return terms


def _group_norm_gate(o, g):
    o = o * lax.rsqrt(jnp.mean(o * o, axis=-1, keepdims=True) + EPS)
    return (_silu(g) * o).astype(BF16)


def _ret_prompt_body(dec_ref, q_ref, k_ref, v_ref, g_ref, *rest, layer, first):
    o_ref, sf_ref, sb_ref = rest[-3:]
    for h in range(RET_HEADS):
        t = _decay_terms(dec_ref, h, SEQ)
        qk_cols = slice(h * RET_DK, (h + 1) * RET_DK)
        v_cols = slice(h * RET_DV, (h + 1) * RET_DV)
        q = q_ref[:, qk_cols].astype(F32)
        k = k_ref[:, qk_cols].astype(F32) * (RET_DK ** -0.5)
        v = v_ref[:, v_cols].astype(BF16)
        scores = _dot_nt(q.astype(BF16), k.astype(BF16)) * t["mask"]
        o = _dot(scores.astype(BF16), v)
        o_ref[:, v_cols] = _group_norm_gate(o, g_ref[:, v_cols].astype(F32))
        for ref, k_dec in ((sf_ref, t["k_f"]), (sb_ref, t["k_b"])):
            s = _dot_tn((k * k_dec).astype(BF16), v)
            if first:
                for l in range(N_RET_LAYERS):
                    ref[l, h] = s if l == layer else jnp.zeros_like(s)
            else:
                ref[h] = s


def _ret_prompt(proj, decay, layer, prev_states=None):
    assert 2 * RET_QK_W == RET_V_W
    state = jax.ShapeDtypeStruct((BATCH, N_RET_LAYERS, RET_HEADS, RET_DK, RET_DV), F32)
    in_specs = [
        pl.BlockSpec(memory_space=pltpu.SMEM),
        pl.BlockSpec((SEQ, RET_QK_W), lambda b: (b, 0)),
        pl.BlockSpec((SEQ, RET_QK_W), lambda b: (b, 1)),
        pl.BlockSpec((SEQ, RET_V_W), lambda b: (b, 1)),
        pl.BlockSpec((SEQ, RET_V_W), lambda b: (b, 2)),
    ]
    args = [decay, proj, proj, proj, proj]
    first = prev_states is None
    aliases = {}
    if first:
        s_spec = pl.BlockSpec((None, N_RET_LAYERS, RET_HEADS, RET_DK, RET_DV), lambda b: (b, 0, 0, 0, 0))
    else:
        s_spec = pl.BlockSpec((None, None, RET_HEADS, RET_DK, RET_DV), lambda b: (b, layer, 0, 0, 0))
        in_specs += [pl.BlockSpec(memory_space=pl.ANY)] * 2
        aliases = {len(args): 1, len(args) + 1: 2}
        args += list(prev_states)
    return pl.pallas_call(
        functools.partial(_ret_prompt_body, layer=layer, first=first),
        grid=(BATCH,),
        in_specs=in_specs,
        out_specs=[pl.BlockSpec((SEQ, RET_V_W), lambda b: (b, 0)), s_spec, s_spec],
        out_shape=[jax.ShapeDtypeStruct((N_PROMPT, RET_V_W), BF16), state, state],
        input_output_aliases=aliases,
        compiler_params=_cparams(1),
        name="ret_prompt",
    )(*args)


def _rope(x, cos, sin_signed):
    n = x.shape[-1]
    lane = lax.broadcasted_iota(I32, x.shape, 1)
    swapped = jnp.where(lane % 2 == 0, pltpu.roll(x, n - 1, 1), pltpu.roll(x, 1, 1))
    return x * cos + swapped * sin_signed


def _ret_sample_body(dec_ref, q_ref, k_ref, v_ref, g_ref, cos_ref, sin_ref, s0f_ref, s0b_ref, o_ref,
                     sf_scr, sb_scr):
    h = pl.program_id(1)
    c = RET_CHUNK
    n_chunks = DEC_SEQ // c
    t = _decay_terms(dec_ref, h, c)

    def chunk(ref, i):
        return ref[i * c:(i + 1) * c, :].astype(F32)

    def qk(i):
        q = _rope(chunk(q_ref, i), chunk(cos_ref, i), chunk(sin_ref, i))
        k = _rope(chunk(k_ref, i) * (RET_DK ** -0.5), chunk(cos_ref, i), chunk(sin_ref, i))
        return q, k

    s = s0f_ref[...]
    for i in range(n_chunks):
        sf_scr[i] = s
        if i + 1 < n_chunks:
            _, k = qk(i)
            s = s * t["c_f"] + _dot_tn((k * t["k_f"]).astype(BF16), chunk(v_ref, i).astype(BF16))
    s = s0b_ref[...]
    for i in reversed(range(n_chunks)):
        sb_scr[i] = s
        if i > 0:
            _, k = qk(i)
            s = s * t["c_b"] + _dot_tn((k * t["k_b"]).astype(BF16), chunk(v_ref, i).astype(BF16))

    for i in range(n_chunks):
        q, k = qk(i)
        scores = _dot_nt(q.astype(BF16), k.astype(BF16)) * t["mask"]
        o = _dot(scores.astype(BF16), chunk(v_ref, i).astype(BF16))
        o = o + _dot((q * t["q_f"]).astype(BF16), sf_scr[i].astype(BF16))
        o = o + _dot((q * t["q_b"]).astype(BF16), sb_scr[i].astype(BF16))
        o_ref[i * c:(i + 1) * c, :] = _group_norm_gate(o, chunk(g_ref, i))


def _ret_sample(proj, decay, cos, sin_signed, s0f, s0b, j):
    kq = RET_QK_W // RET_DK
    kv = 2 * RET_QK_W // RET_DV
    kg = kv + RET_HEADS
    rb = N_PROMPT // DEC_SEQ
    n_chunks = DEC_SEQ // RET_CHUNK
    return pl.pallas_call(
        _ret_sample_body,
        grid=(DEC_BATCH, RET_HEADS),
        in_specs=[
            pl.BlockSpec(memory_space=pltpu.SMEM),
            pl.BlockSpec((DEC_SEQ, RET_DK), lambda b, h: (rb + b, h)),
            pl.BlockSpec((DEC_SEQ, RET_DK), lambda b, h: (rb + b, kq + h)),
            pl.BlockSpec((DEC_SEQ, RET_DV), lambda b, h: (rb + b, kv + h)),
            pl.BlockSpec((DEC_SEQ, RET_DV), lambda b, h: (rb + b, kg + h)),
            pl.BlockSpec((DEC_SEQ, RET_DK), lambda b, h: (0, 0)),
            pl.BlockSpec((DEC_SEQ, RET_DK), lambda b, h: (0, 0)),
            pl.BlockSpec((None, None, None, RET_DK, RET_DV), lambda b, h: (b, j, h, 0, 0)),
            pl.BlockSpec((None, None, None, RET_DK, RET_DV), lambda b, h: (b, j, h, 0, 0)),
        ],
        out_specs=pl.BlockSpec((DEC_SEQ, RET_DV), lambda b, h: (b, h)),
        out_shape=jax.ShapeDtypeStruct((N_SAMPLE, RET_V_W), BF16),
        scratch_shapes=[pltpu.VMEM((n_chunks, RET_DK, RET_DV), F32),
                        pltpu.VMEM((n_chunks, RET_DK, RET_DV), F32)],
        compiler_params=_cparams(2),
        name="ret_sample",
    )(decay, proj, proj, proj, proj, cos, sin_signed, s0f, s0b)


NA_HPB = LANES // NA_HD


def _head_lanes(rows, hh):
    lane = lax.broadcasted_iota(I32, (rows, LANES), 1)
    return jnp.logical_and(lane >= hh * NA_HD, lane < (hh + 1) * NA_HD)


def _with_ones(v):
    return jnp.concatenate([v, jnp.ones(v.shape, v.dtype)], axis=-1)


def _ctx_attn_body(q_ref, k_ref, v_ref, o_ref):
    outs = []
    for p in range(NA_HEADS // NA_HPB):
        sl = slice(p * LANES, (p + 1) * LANES)
        q2 = q_ref[:, sl] * (NA_HD ** -0.5)
        k2 = k_ref[:, sl].astype(BF16)
        v2 = _with_ones(v_ref[:, sl].astype(BF16))
        o_pair = None
        for hh in range(NA_HPB):
            mine = _head_lanes(SEQ, hh)
            s = _dot_nt(jnp.where(mine, q2, 0.0).astype(BF16), k2)
            e = jnp.exp(s - s.max(axis=-1, keepdims=True)).astype(BF16)
            acc = _dot(e, v2)
            o_h = acc[:, :LANES] / acc[:, LANES:]
            o_pair = o_h if o_pair is None else jnp.where(mine, o_h, o_pair)
        outs.append(o_pair)
    o_ref[...] = jnp.concatenate(outs, axis=-1).astype(BF16)


def _ctx_attn(proj):
    return pl.pallas_call(
        _ctx_attn_body,
        grid=(BATCH,),
        in_specs=[
            pl.BlockSpec((SEQ, D_MODEL), lambda b: (b, 0)),
            pl.BlockSpec((SEQ, D_MODEL), lambda b: (b, 1)),
            pl.BlockSpec((SEQ, D_MODEL), lambda b: (b, 2)),
        ],
        out_specs=pl.BlockSpec((SEQ, D_MODEL), lambda b: (b, 0)),
        out_shape=jax.ShapeDtypeStruct((N_PROMPT, D_MODEL), BF16),
        compiler_params=_cparams(1),
        name="ctx_attn",
    )(proj, proj, proj)


NA_QT = 256
NA_WIN = NA_KH * GRID_W


def _na_attn_body(q_ref, k_ref, v_ref, ck_ref, cv_ref, tbl_ref, o_ref):
    t = pl.program_id(1)
    rows_per_tile = NA_QT // GRID_W
    n_rows = DEC_SEQ // GRID_W
    q2 = q_ref[...] * (NA_HD ** -0.5)
    q = [jnp.where(_head_lanes(NA_QT, hh), q2, 0.0).astype(BF16) for hh in range(NA_HPB)]
    s_ctx = _dot_nt(jnp.concatenate(q, axis=0), ck_ref[...].astype(BF16))
    acc_win = []
    e_ctx = []
    for u in range(rows_per_tile):
        qr = t * rows_per_tile + u
        rs = jnp.clip(qr - NA_KH // 2, 0, n_rows - NA_KH)
        i0 = rs - qr + (NA_KH - 1)
        start = pl.multiple_of(rs * GRID_W, GRID_W)
        kw = k_ref[pl.ds(start, NA_WIN), :].astype(BF16)
        vw = _with_ones(v_ref[pl.ds(start, NA_WIN), :].astype(BF16))
        rows = slice(u * GRID_W, (u + 1) * GRID_W)
        bias = jnp.concatenate(
            [jnp.concatenate([tbl_ref[hh, i0 + 2 * m] for m in range(NA_KH // 2)], axis=-1) for hh in range(NA_HPB)],
            axis=0)
        sw = _dot_nt(jnp.concatenate([q[hh][rows] for hh in range(NA_HPB)], axis=0), kw) + bias
        sc = jnp.concatenate([s_ctx[hh * NA_QT + u * GRID_W:hh * NA_QT + (u + 1) * GRID_W] for hh in range(NA_HPB)],
                             axis=0)
        m = jnp.maximum(sw.max(axis=-1, keepdims=True), sc.max(axis=-1, keepdims=True))
        acc_win.append(_dot(jnp.exp(sw - m).astype(BF16), vw))
        e_ctx.append(jnp.exp(sc - m).astype(BF16))
    acc = jnp.concatenate(acc_win, axis=0) + _dot(jnp.concatenate(e_ctx, axis=0), _with_ones(cv_ref[...].astype(BF16)))
    o = acc[:, :LANES] / acc[:, LANES:]
    outs = []
    for u in range(rows_per_tile):
        base = u * NA_HPB * GRID_W
        o_u = o[base:base + GRID_W]
        for hh in range(1, NA_HPB):
            o_u = jnp.where(_head_lanes(GRID_W, hh), o[base + hh * GRID_W:base + (hh + 1) * GRID_W], o_u)
        outs.append(o_u)
    o_ref[...] = jnp.concatenate(outs, axis=0).astype(BF16)


def _na_attn(proj, cache_k, cache_v, tbl, j):
    n_qt = DEC_SEQ // NA_QT
    rb_q = N_PROMPT // NA_QT
    rb_kv = N_PROMPT // DEC_SEQ
    cb = D_MODEL // LANES
    ck = cache_k.reshape(DEC_BATCH, -1, PAST_LEN, D_MODEL)
    cv = cache_v.reshape(DEC_BATCH, -1, PAST_LEN, D_MODEL)
    return pl.pallas_call(
        _na_attn_body,
        grid=(cb, n_qt, DEC_BATCH),
        in_specs=[
            pl.BlockSpec((NA_QT, LANES), lambda p, t, b: (rb_q + b * n_qt + t, p)),
            pl.BlockSpec((DEC_SEQ, LANES), lambda p, t, b: (rb_kv + b, cb + p)),
            pl.BlockSpec((DEC_SEQ, LANES), lambda p, t, b: (rb_kv + b, 2 * cb + p)),
            pl.BlockSpec((None, None, PAST_LEN, LANES), lambda p, t, b: (b, j, 0, p)),
            pl.BlockSpec((None, None, PAST_LEN, LANES), lambda p, t, b: (b, j, 0, p)),
            pl.BlockSpec((None, NA_HPB, 2 * NA_KH - 2, GRID_W, 2 * GRID_W), lambda p, t, b: (j, p, 0, 0, 0)),
        ],
        out_specs=pl.BlockSpec((NA_QT, LANES), lambda p, t, b: (b * n_qt + t, p)),
        out_shape=jax.ShapeDtypeStruct((N_SAMPLE, D_MODEL), BF16),
        compiler_params=_cparams(3),
        name="na_attn",
    )(proj, proj, proj, ck, cv, tbl)


def _na_bias_table(rpb):
    c = np.arange(GRID_W)
    cs = np.clip(c - NA_KW // 2, 0, GRID_W - NA_KW)
    col_valid = (c[None, :] >= cs[:, None]) & (c[None, :] < cs[:, None] + NA_KW)
    col_off = np.clip(c[None, :] - c[:, None] + NA_KW - 1, 0, 2 * NA_KW - 2)
    sel_c = (col_off[..., None] == np.arange(2 * NA_KW - 1)).astype(np.float32)
    t = jnp.einsum("lhij,qkj->lhiqk", rpb, sel_c, precision=lax.Precision.HIGHEST)
    t = jnp.where(col_valid, t, -jnp.inf)
    return jnp.concatenate([t[:, :, :-1], t[:, :, 1:]], axis=-1)


def _outproj_body(ap_ref, as_ref, w_ref, xa_ref, xb_ref, g_ref, mod_ref, wr_ref, br_ref, xo_ref, tok_ref, lg_ref,
                  *scratch):
    i = pl.program_id(0)
    a = jnp.where(i < N_PROMPT // TM, ap_ref[...], as_ref[...])
    x = _x_tile(xa_ref, xb_ref) + mod_ref[2:3, :] * _dot(a, _weight_ref(w_ref, scratch)[...])
    xo_ref[...] = x
    tok = _norm_mod(x, g_ref[...], mod_ref[3:4, :], mod_ref[4:5, :])
    tok_ref[...] = _pack_rows(tok)
    hi = tok.astype(BF16)
    lo = (tok - hi.astype(F32)).astype(BF16)
    wr = wr_ref[...]
    wr_hi = wr.astype(BF16)
    wr_lo = (wr - wr_hi.astype(F32)).astype(BF16)
    hi_terms = _dot(hi, jnp.concatenate([wr_hi, wr_lo], axis=-1))
    lg_ref[...] = hi_terms[:, :LANES] + _dot(lo, wr_hi) + hi_terms[:, LANES:] + br_ref[...]


def _outproj(a_prompt, a_sample, w, w_layer, x, g, mod, wr, br):
    k_in = a_prompt.shape[1]
    n_pt = N_PROMPT // TM
    x_specs, x_args = _x_inputs(x)
    w_spec, scratch = _weight_input(w, w_layer)
    return pl.pallas_call(
        _outproj_body,
        grid=(N_TOK // TM,),
        in_specs=[
            pl.BlockSpec((TM, k_in), lambda i: (jnp.minimum(i, n_pt - 1), 0)),
            pl.BlockSpec((TM, k_in), lambda i: (jnp.maximum(i - n_pt, 0), 0)),
            w_spec,
            *x_specs,
            pl.BlockSpec((1, D_MODEL), lambda i: (0, 0)),
            pl.BlockSpec((None, MOD_ROWS, D_MODEL), lambda i: (_group_of_tile(i, TM), 0, 0)),
            pl.BlockSpec((D_MODEL, LANES), lambda i: (0, 0)),
            pl.BlockSpec((1, LANES), lambda i: (0, 0)),
        ],
        out_specs=[
            pl.BlockSpec((TM, D_MODEL), lambda i: (i, 0)),
            pl.BlockSpec((TM, HALF), lambda i: (i, 0)),
            pl.BlockSpec((TM, LANES), lambda i: (i, 0)),
        ],
        out_shape=[
            jax.ShapeDtypeStruct((N_TOK, D_MODEL), F32),
            jax.ShapeDtypeStruct((N_TOK, HALF), U32),
            jax.ShapeDtypeStruct((N_TOK, LANES), F32),
        ],
        scratch_shapes=scratch,
        compiler_params=_cparams(1),
        name="outproj",
    )(a_prompt, a_sample, w, *x_args, g.reshape(1, D_MODEL), mod, wr, br)


def _router_body(lg_ref, idx_ref, gate_ref, rank_ref, cnt_ref, carry):
    i = pl.program_id(0)

    @pl.when(i == 0)
    def _():
        carry[...] = jnp.zeros_like(carry)

    lane = lax.broadcasted_iota(I32, (ROUTER_TM, LANES), 1)
    l = jnp.where(lane < N_EXPERTS, lg_ref[...], -jnp.inf)
    vals, idxs, hots = [], [], []
    for _ in range(TOP_K):
        m = l.max(axis=-1, keepdims=True)
        idx = jnp.where(l == m, lane, LANES).min(axis=-1, keepdims=True)
        hot = lane == idx
        l = jnp.where(hot, -jnp.inf, l)
        vals.append(m)
        idxs.append(idx)
        hots.append(hot)
    es = [jnp.exp(v - vals[0]) for v in vals]
    denom = es[0] + es[1] + es[2] + es[3]
    hot_all = (hots[0] | hots[1] | hots[2] | hots[3]).astype(F32)
    r = lax.broadcasted_iota(I32, (ROUTER_TM, ROUTER_TM), 0)
    c = lax.broadcasted_iota(I32, (ROUTER_TM, ROUTER_TM), 1)
    before = _dot((c < r).astype(BF16), hot_all.astype(BF16)) + carry[...]
    idx_out = jnp.zeros((ROUTER_TM, LANES), I32)
    gate_out = jnp.zeros((ROUTER_TM, LANES), F32)
    rank_out = jnp.zeros((ROUTER_TM, LANES), I32)
    for k in range(TOP_K):
        rank_k = jnp.where(hots[k], before, 0.0).sum(axis=-1, keepdims=True).astype(I32)
        idx_out = jnp.where(lane == k, idxs[k], idx_out)
        gate_out = jnp.where(lane == k, es[k] / denom, gate_out)
        rank_out = jnp.where(lane == k, rank_k, rank_out)
    idx_ref[...] = idx_out
    gate_ref[...] = gate_out
    rank_ref[...] = rank_out
    carry[...] = carry[...] + hot_all.sum(axis=0, keepdims=True)
    cnt_ref[...] = carry[...].astype(I32)


def _router(logits):
    spec = pl.BlockSpec((ROUTER_TM, LANES), lambda i: (i, 0))
    return pl.pallas_call(
        _router_body,
        grid=(N_TOK // ROUTER_TM,),
        in_specs=[spec],
        out_specs=[spec, spec, spec, pl.BlockSpec((1, LANES), lambda i: (0, 0))],
        out_shape=[
            jax.ShapeDtypeStruct((N_TOK, LANES), I32),
            jax.ShapeDtypeStruct((N_TOK, LANES), F32),
            jax.ShapeDtypeStruct((N_TOK, LANES), I32),
            jax.ShapeDtypeStruct((1, LANES), I32),
        ],
        scratch_shapes=[pltpu.VMEM((1, LANES), F32)],
        compiler_params=_cparams(1),
        name="router",
    )(logits)


SC_CORES = 2
SC_SUBCORES = 16
SC_WORKERS = SC_CORES * SC_SUBCORES
SC_CH = 32
SC_SCATTER_BUFS = 5
SC_GATHER_BUFS = 4


def _sc_mesh():
    return plsc.VectorSubcoreMesh(core_axis_name="c", subcore_axis_name="s",
                                  num_cores=SC_CORES, num_subcores=SC_SUBCORES)


def _sc_worker():
    return lax.axis_index("s") * SC_CORES + lax.axis_index("c")


def _dispatch(tok, pos_km):
    n, d = tok.shape
    n_ch = n // (SC_WORKERS * SC_CH)
    nb = SC_SCATTER_BUFS
    assert n_ch * SC_WORKERS * SC_CH == n and n_ch % nb == 0

    @functools.partial(
        pl.kernel, mesh=_sc_mesh(), out_type=jax.ShapeDtypeStruct((MOE_ROWS, d), tok.dtype),
        scratch_types=[pltpu.VMEM((n_ch, TOP_K, SC_CH), I32), pltpu.VMEM((nb, SC_CH, d), tok.dtype),
                       pltpu.SemaphoreType.DMA((nb,)), pltpu.SemaphoreType.DMA((nb,))])
    def scatter_rows(tok_hbm, pos_hbm, out_hbm, idx_v, rows_v, load_sem, scat_sem):
        wid = _sc_worker()
        base = wid * (n_ch * SC_CH)
        pltpu.sync_copy(pos_hbm.at[wid], idx_v)

        def load(g, slot):
            rows = pl.ds(pl.multiple_of(base + g * SC_CH, 8), SC_CH)
            return pltpu.make_async_copy(tok_hbm.at[rows], rows_v.at[slot], load_sem.at[slot])

        def scatter(g, k, slot):
            return pltpu.make_async_copy(rows_v.at[slot], out_hbm.at[idx_v.at[g, k]], scat_sem.at[slot])

        for s in range(nb - 1):
            load(s, s).start()

        @pl.loop(0, n_ch, step=nb)
        def _(g0):
            for s in range(nb):
                g = g0 + s
                reuse = (s + nb - 1) % nb

                @pl.when(g >= 1)
                def _():
                    for k in range(TOP_K):
                        scatter(g - 1, k, reuse).wait()

                @pl.when(g + nb - 1 < n_ch)
                def _():
                    load(g + nb - 1, reuse).start()

                load(g, s).wait()
                for k in range(TOP_K):
                    scatter(g, k, s).start()

        for k in range(TOP_K):
            scatter(n_ch - 1, k, (n_ch - 1) % nb).wait()

    pos = pos_km.reshape(TOP_K, SC_WORKERS, n_ch, SC_CH).transpose(1, 2, 0, 3)
    return scatter_rows(tok, pos)


def _gather_rows(table, idx):
    b, d = idx.shape[0], table.shape[1]
    n_ch = b // (SC_WORKERS * SC_CH)
    nb = SC_GATHER_BUFS
    assert n_ch * SC_WORKERS * SC_CH == b and n_ch % nb == 0

    @functools.partial(
        pl.kernel, mesh=_sc_mesh(), out_type=jax.ShapeDtypeStruct((b, d), table.dtype),
        scratch_types=[pltpu.VMEM((n_ch, SC_CH), I32), pltpu.VMEM((nb, SC_CH, d), table.dtype),
                       pltpu.SemaphoreType.DMA((nb,)), pltpu.SemaphoreType.DMA((nb,))])
    def gather_rows(table_hbm, idx_hbm, out_hbm, idx_v, rows_v, gather_sem, write_sem):
        wid = _sc_worker()
        base = wid * (n_ch * SC_CH)
        pltpu.sync_copy(idx_hbm.at[wid], idx_v)

        def gather(g, slot):
            return pltpu.make_async_copy(table_hbm.at[idx_v.at[g]], rows_v.at[slot], gather_sem.at[slot])

        def write_out(g, slot):
            rows = pl.ds(pl.multiple_of(base + g * SC_CH, 8), SC_CH)
            return pltpu.make_async_copy(rows_v.at[slot], out_hbm.at[rows], write_sem.at[slot])

        for s in range(nb - 1):
            gather(s, s).start()

        @pl.loop(0, n_ch, step=nb)
        def _(g0):
            for s in range(nb):
                g = g0 + s
                reuse = (s + nb - 1) % nb

                @pl.when(g >= 1)
                def _():
                    write_out(g - 1, reuse).wait()

                @pl.when(g + nb - 1 < n_ch)
                def _():
                    gather(g + nb - 1, reuse).start()

                gather(g, s).wait()
                write_out(g, s).start()

        write_out(n_ch - 1, (n_ch - 1) % nb).wait()

    return gather_rows(table, idx.reshape(SC_WORKERS, n_ch, SC_CH))


def _moe_body(te_ref, tv_ref, slot_ref, nxt_ref, x_ref, w1_hbm, w2_hbm, perm_ref, *rest, layer):
    b_refs = rest[:2 * MOE_TPS]
    o_ref, w1_buf, w2_buf, w1_scr, w2_scr, sem = rest[2 * MOE_TPS:]
    for sub in range(MOE_TPS):
        rows = slice(sub * MOE_TM, (sub + 1) * MOE_TM)
        _moe_tile(te_ref, tv_ref, slot_ref, nxt_ref, pl.program_id(0) * MOE_TPS + sub, x_ref.at[rows], w1_hbm,
                  b_refs[2 * sub], w2_hbm, b_refs[2 * sub + 1], perm_ref, o_ref.at[rows],
                  w1_buf, w2_buf, w1_scr, w2_scr, sem, layer)


def _moe_tile(te_ref, tv_ref, slot_ref, nxt_ref, t, x_ref, w1_hbm, b1_ref, w2_hbm, b2_ref, perm_ref, o_ref,
              w1_buf, w2_buf, w1_scr, w2_scr, sem, layer):
    e = te_ref[t]
    slot = slot_ref[t]
    nxt = nxt_ref[t]
    first = jnp.logical_or(t == 0, e != te_ref[jnp.maximum(t - 1, 0)])

    def fetch(expert, s):
        return (pltpu.make_async_copy(w1_hbm.at[layer, expert], w1_buf.at[s], sem.at[s, 0]),
                pltpu.make_async_copy(w2_hbm.at[layer, expert], w2_buf.at[s], sem.at[s, 1]))

    @pl.when(t == 0)
    def _():
        for cp in fetch(e, slot):
            cp.start()

    @pl.when(jnp.logical_and(tv_ref[t] == 1, first))
    def _():
        for cp in fetch(e, slot):
            cp.wait()

        @pl.when(nxt != e)
        def _():
            for cp in fetch(nxt, 1 - slot):
                cp.start()

        w2_scr[...] = w2_buf[slot].astype(BF16)
        half = PERM_W // 2
        for c in range(2 * D_EXPERT // PERM_W):
            blk = _dot(w1_buf[slot, :, c * PERM_W:(c + 1) * PERM_W].astype(BF16), perm_ref[...]).astype(BF16)
            w1_scr[:, c * half:(c + 1) * half] = blk[:, :half]
            w1_scr[:, D_EXPERT + c * half:D_EXPERT + (c + 1) * half] = blk[:, half:]

    @pl.when(tv_ref[t] == 0)
    def _():
        o_ref[...] = jnp.zeros_like(o_ref)

    @pl.when(tv_ref[t] == 1)
    def _():
        x_lo, x_hi = _unpack_rows(x_ref[...])
        x = jnp.concatenate([x_lo.astype(BF16), x_hi.astype(BF16)], axis=-1)
        hh = _dot(x, w1_scr[...]) + b1_ref[...]
        h_glu = jnp.minimum(hh[:, :D_EXPERT], SWIGLU_LIMIT)
        h_lin = jnp.clip(hh[:, D_EXPERT:], -SWIGLU_LIMIT, SWIGLU_LIMIT)
        a = h_glu * _sigmoid(SWIGLU_ALPHA * h_glu) * (h_lin + 1.0)
        o_ref[...] = _pack_rows(_dot(a.astype(BF16), w2_scr[...]) + b2_ref[...])


def _deinterleave_bias_body(b_ref, perm_ref, o_ref):
    half = PERM_W // 2
    for c in range(2 * D_EXPERT // PERM_W):
        r = b_ref[:, c * PERM_W:(c + 1) * PERM_W]
        acc = jnp.zeros(r.shape, F32)
        for _ in range(3):
            piece = r.astype(BF16)
            acc = acc + _dot(piece, perm_ref[...])
            r = r - piece.astype(F32)
        o_ref[:, c * half:(c + 1) * half] = acc[:, :half]
        o_ref[:, D_EXPERT + c * half:D_EXPERT + (c + 1) * half] = acc[:, half:]


def _deinterleave_bias(b1, perm):
    return pl.pallas_call(
        _deinterleave_bias_body,
        out_shape=jax.ShapeDtypeStruct(b1.shape, F32),
        compiler_params=_cparams(0),
        name="deinterleave_bias",
    )(b1, perm)


def _moe_experts(tile_expert, tile_valid, tile_slot, tile_next, xs, w1, b1_split, w2, b2, perm, layer):
    bias_specs = []
    for sub in range(MOE_TPS):
        bias_specs += [
            pl.BlockSpec((None, None, 1, 2 * D_EXPERT), lambda s, te, *_, sub=sub: (layer, te[s * MOE_TPS + sub], 0, 0)),
            pl.BlockSpec((None, None, 1, D_MODEL), lambda s, te, *_, sub=sub: (layer, te[s * MOE_TPS + sub], 0, 0)),
        ]
    grid_spec = pltpu.PrefetchScalarGridSpec(
        num_scalar_prefetch=4,
        grid=(MOE_TILES // MOE_TPS,),
        in_specs=[
            pl.BlockSpec((MOE_TPS * MOE_TM, HALF), lambda s, *_: (s, 0)),
            pl.BlockSpec(memory_space=pl.ANY),
            pl.BlockSpec(memory_space=pl.ANY),
            pl.BlockSpec((PERM_W, PERM_W), lambda s, *_: (0, 0)),
            *bias_specs,
        ],
        out_specs=pl.BlockSpec((MOE_TPS * MOE_TM, HALF), lambda s, *_: (s, 0)),
        scratch_shapes=[
            pltpu.VMEM((2, D_MODEL, 2 * D_EXPERT), F32),
            pltpu.VMEM((2, D_EXPERT, D_MODEL), F32),
            pltpu.VMEM((D_MODEL, 2 * D_EXPERT), BF16),
            pltpu.VMEM((D_EXPERT, D_MODEL), BF16),
            pltpu.SemaphoreType.DMA((2, 2)),
        ],
    )
    return pl.pallas_call(
        functools.partial(_moe_body, layer=layer),
        grid_spec=grid_spec,
        out_shape=jax.ShapeDtypeStruct((MOE_ROWS, HALF), U32),
        compiler_params=_cparams(1),
        name="moe_experts",
    )(tile_expert, tile_valid, tile_slot, tile_next, xs, w1, w2, perm, *([b1_split, b2] * MOE_TPS))


COMBINE_TM = 512


def _combine_body(*refs, final):
    y_refs = refs[:TOP_K]
    gate_ref, x_ref, mod_ref, g_ref = refs[TOP_K:TOP_K + 4]
    gates = gate_ref[...]
    y_lo, y_hi = (gates[:, 0:1] * h for h in _unpack_rows(y_refs[0][...]))
    for k in range(1, TOP_K):
        lo, hi = _unpack_rows(y_refs[k][...])
        y_lo = y_lo + gates[:, k:k + 1] * lo
        y_hi = y_hi + gates[:, k:k + 1] * hi
    x = x_ref[...] + mod_ref[5:6, :] * jnp.concatenate([y_lo, y_hi], axis=-1)
    if not final:
        refs[-1][...] = x
        return
    x = (x * lax.rsqrt(jnp.mean(x * x, axis=-1, keepdims=True) + EPS)) * g_ref[...]
    prompt_ref, sample_ref = refs[-2:]
    is_prompt = pl.program_id(0) < N_PROMPT // COMBINE_TM

    @pl.when(is_prompt)
    def _():
        prompt_ref[...] = x

    @pl.when(jnp.logical_not(is_prompt))
    def _():
        sample_ref[...] = x


def _combine(y_km, gates, x, mod, g_final, final):
    n_t = N_TOK // COMBINE_TM
    n_pt = N_PROMPT // COMBINE_TM
    y_specs = [pl.BlockSpec((COMBINE_TM, HALF), lambda i, k=k: (k * n_t + i, 0)) for k in range(TOP_K)]
    if final:
        out_specs = [pl.BlockSpec((COMBINE_TM, D_MODEL), lambda i: (jnp.minimum(i, n_pt - 1), 0)),
                     pl.BlockSpec((COMBINE_TM, D_MODEL), lambda i: (jnp.maximum(i - n_pt, 0), 0))]
        out_shape = [jax.ShapeDtypeStruct((N_PROMPT, D_MODEL), F32), jax.ShapeDtypeStruct((N_SAMPLE, D_MODEL), F32)]
    else:
        out_specs = pl.BlockSpec((COMBINE_TM, D_MODEL), lambda i: (i, 0))
        out_shape = jax.ShapeDtypeStruct((N_TOK, D_MODEL), F32)
    return pl.pallas_call(
        functools.partial(_combine_body, final=final),
        grid=(n_t,),
        in_specs=y_specs + [
            pl.BlockSpec((COMBINE_TM, LANES), lambda i: (i, 0)),
            pl.BlockSpec((COMBINE_TM, D_MODEL), lambda i: (i, 0)),
            pl.BlockSpec((None, MOD_ROWS, D_MODEL), lambda i: (_group_of_tile(i, COMBINE_TM), 0, 0)),
            pl.BlockSpec((1, D_MODEL), lambda i: (0, 0)),
        ],
        out_specs=out_specs,
        out_shape=out_shape,
        compiler_params=_cparams(1),
        name="combine",
    )(*([y_km] * TOP_K), gates, x, mod, g_final.reshape(1, D_MODEL))


def _routing_tables(idx, rank, counts):
    counts = counts[0, :N_EXPERTS]
    padded = ((counts + MOE_TM - 1) // MOE_TM) * MOE_TM
    ends = jnp.cumsum(padded)
    offsets = ends - padded
    idx = idx[:, :TOP_K]
    pos = rank[:, :TOP_K] + jnp.sum(
        jnp.where(idx[..., None] == jnp.arange(N_EXPERTS, dtype=I32), offsets, 0), axis=-1)
    pos_km = pos.T.astype(I32)
    tile_start = jnp.arange(MOE_TILES, dtype=I32) * MOE_TM
    tile_valid = (tile_start < ends[-1]).astype(I32)
    tile_expert = jnp.sum((tile_start[:, None] >= ends[None, :]).astype(I32), axis=-1)
    last_expert = jnp.sum((ends[-1] - 1 >= ends).astype(I32))
    tile_expert = jnp.where(tile_valid == 1, tile_expert, last_expert).astype(I32)
    e_ids = jnp.arange(N_EXPERTS, dtype=I32)
    owns = counts > 0
    ordinal = jnp.cumsum(owns.astype(I32)) - 1
    later = jnp.where(owns[None, :] & (e_ids[None, :] > e_ids[:, None]), e_ids[None, :], N_EXPERTS)
    nxt = jnp.min(later, axis=1)
    nxt = jnp.where(nxt == N_EXPERTS, e_ids, nxt)
    onehot = (tile_expert[:, None] == e_ids[None, :]).astype(I32)
    tile_slot = jnp.sum(onehot * (ordinal % 2)[None, :], axis=1).astype(I32)
    tile_next = jnp.sum(onehot * nxt[None, :], axis=1).astype(I32)
    return pos_km, tile_expert, tile_valid, tile_slot, tile_next


def _rope_tables():
    t = np.arange(DEC_SEQ)
    row = (t // GRID_W).astype(np.float32)
    col = (t % GRID_W).astype(np.float32)
    d_axis = RET_DK // 2
    inv = jnp.asarray(ROPE_BASE, F32) ** (-jnp.arange(0, d_axis, 2, dtype=F32) / d_axis)
    ang = jnp.concatenate([row[:, None] * inv, col[:, None] * inv], axis=-1)
    cos = jnp.repeat(jnp.cos(ang), 2, axis=-1)
    sin = jnp.repeat(jnp.sin(ang), 2, axis=-1)
    sign = jnp.asarray(np.tile(np.array([-1.0, 1.0], np.float32), RET_DK // 2))
    return cos, sin * sign


def _deinterleave_matrix():
    p = np.zeros((PERM_W, PERM_W), np.float32)
    half = PERM_W // 2
    for j in range(half):
        p[2 * j, j] = 1.0
        p[2 * j + 1, half + j] = 1.0
    return jnp.asarray(p, BF16)


def kernel(x_prompt, x_sample, state_ret_fwd, state_ret_bwd, cache_na_k, cache_na_v, c, c_ctx, ada_w, ada_b, norm_mix, norm_ffn, norm_final, ret_w_in, ret_w_out, ret_decay_fwd, ret_decay_bwd, na_w_in, na_w_out, na_rpb, moe_w_router, moe_b_router, moe_w1, moe_b1, moe_w2, moe_b2):
    x = (x_prompt.reshape(N_PROMPT, D_MODEL), x_sample.reshape(N_SAMPLE, D_MODEL))

    cond = jnp.zeros((MOD_ROWS, D_MODEL), F32).at[0].set(c_ctx).at[1:1 + DEC_BATCH].set(c)
    mod_all = _adaln(cond, ada_w, ada_b)
    mod_all = mod_all[:, :N_GROUPS].reshape(DEPTH, N_GROUPS, 6, D_MODEL)
    mod_all = jnp.pad(mod_all, ((0, 0), (0, 0), (0, MOD_ROWS - 6), (0, 0)))

    cos, sin_signed = _rope_tables()
    perm = _deinterleave_matrix()
    wr_all = jnp.pad(moe_w_router, ((0, 0), (0, 0), (0, LANES - N_EXPERTS)))
    br_all = jnp.pad(moe_b_router, ((0, 0), (0, LANES - N_EXPERTS))).reshape(DEPTH, 1, LANES)
    b1_split = _deinterleave_bias(moe_b1.reshape(DEPTH * N_EXPERTS, 2 * D_EXPERT), perm)
    b1_split = b1_split.reshape(DEPTH, N_EXPERTS, 1, 2 * D_EXPERT)
    b2_all = moe_b2.reshape(DEPTH, N_EXPERTS, 1, D_MODEL)
    na_tbl = _na_bias_table(na_rpb)
    ret_states = None
    na_caches = None
    for layer in range(DEPTH):
        mod = mod_all[layer]
        j = layer // 2
        if layer % 2 == 0:
            proj = _inproj(x, norm_mix[layer], mod, _cast_layer_bf16(ret_w_in, j), out_dtype=BF16)
            decay = jnp.stack([ret_decay_fwd[j], ret_decay_bwd[j]])
            a_p, s_f, s_b = _ret_prompt(proj, decay, j, ret_states)
            ret_states = (s_f, s_b)
            a_s = _ret_sample(proj, decay, cos, sin_signed, state_ret_fwd, state_ret_bwd, j)
            w_out = ret_w_out
        else:
            proj, ck, cv = _inproj(x, norm_mix[layer], mod, na_w_in, j, j, na_caches)
            na_caches = (ck, cv)
            a_p = _ctx_attn(proj)
            a_s = _na_attn(proj, cache_na_k, cache_na_v, na_tbl, j)
            w_out = na_w_out

        x, tok, logits = _outproj(a_p, a_s, w_out, j, x, norm_ffn[layer], mod, wr_all[layer], br_all[layer])

        idx, gates, rank, counts = _router(logits)
        pos_km, tile_expert, tile_valid, tile_slot, tile_next = _routing_tables(idx, rank, counts)
        xs = _dispatch(tok, pos_km)
        ys = _moe_experts(tile_expert, tile_valid, tile_slot, tile_next, xs, moe_w1, b1_split, moe_w2, b2_all,
                          perm, layer)
        y_km = _gather_rows(ys, pos_km.reshape(-1))
        x = _combine(y_km, gates, x, mod, norm_final, final=(layer == DEPTH - 1))

    y_prompt = x[0].reshape(BATCH, SEQ, D_MODEL)
    y_sample = x[1].reshape(DEC_BATCH, DEC_SEQ, D_MODEL)
    return (y_prompt, y_sample, ret_states[0], ret_states[1],
            jnp.transpose(na_caches[0], (0, 1, 4, 2, 3)), jnp.transpose(na_caches[1], (0, 1, 4, 2, 3)))
```

```python
import functools

import numpy as np
import jax
import jax.numpy as jnp
from jax import lax
from jax.experimental import pallas as pl
from jax.experimental.pallas import tpu as pltpu
from jax.experimental.pallas import tpu_sc as plsc

F32 = jnp.float32
BF16 = jnp.bfloat16
I32 = jnp.int32
U32 = jnp.uint32

D_MODEL = 1024
BATCH = 32
SEQ = 256
DEPTH = 4
DEC_BATCH = 2
DEC_SEQ = 1024
PAST_LEN = 512
GRID_W = 64
EPS = 1e-6
ROPE_BASE = 10000.0
RET_HEADS = 4
RET_DK = D_MODEL // RET_HEADS
RET_DV = 2 * RET_DK
RET_QK_W = RET_HEADS * RET_DK
RET_V_W = RET_HEADS * RET_DV
NA_HEADS = 16
NA_HD = D_MODEL // NA_HEADS
NA_KH = 8
NA_KW = 16
N_EXPERTS = 32
TOP_K = 4
D_EXPERT = D_MODEL
SWIGLU_LIMIT = 7.0
SWIGLU_ALPHA = 1.702
N_RET_LAYERS = (DEPTH + 1) // 2
N_NA_LAYERS = DEPTH // 2

N_PROMPT = BATCH * SEQ
N_SAMPLE = DEC_BATCH * DEC_SEQ
N_TOK = N_PROMPT + N_SAMPLE
N_GROUPS = 1 + DEC_BATCH
MOD_ROWS = 8
LANES = 128

TM = 256
RET_CHUNK = 256
MOE_TM = 256
MOE_ROWS = N_TOK * TOP_K + N_EXPERTS * MOE_TM
MOE_TILES = MOE_ROWS // MOE_TM
MOE_TPS = 4
ROUTER_TM = 512
PERM_W = 256
VMEM_LIMIT = 56 * 1024 * 1024


def _cparams(n_axes, vmem=VMEM_LIMIT):
    return pltpu.CompilerParams(dimension_semantics=("arbitrary",) * n_axes, vmem_limit_bytes=vmem)


def _group_of_tile(i, tm):
    return jnp.maximum((i * tm) // DEC_SEQ - (N_PROMPT // DEC_SEQ - 1), 0)


def _sigmoid(x):
    return 1.0 / (1.0 + jnp.exp(-x))


def _silu(x):
    return x * _sigmoid(x)


def _log_sigmoid(x):
    return jnp.minimum(x, 0.0) - jnp.log(1.0 + jnp.exp(-jnp.abs(x)))


def _norm_mod(x, g, shift, scale):
    y = x * lax.rsqrt(jnp.mean(x * x, axis=-1, keepdims=True) + EPS)
    return (y * g) * (1.0 + scale) + shift


def _dot(a, b):
    return jnp.dot(a, b, preferred_element_type=F32)


HALF = D_MODEL // 2
HI_MASK = 0xFFFF0000


def _pack_rows(x):
    lo = pltpu.bitcast(x[:, :HALF].astype(BF16).astype(F32), U32)
    hi = pltpu.bitcast(x[:, HALF:].astype(BF16).astype(F32), U32)
    return (lo >> 16) | (hi & jnp.uint32(HI_MASK))


def _unpack_rows(p):
    return pltpu.bitcast(p << 16, F32), pltpu.bitcast(p & jnp.uint32(HI_MASK), F32)


def _dot_nt(a, b):
    return lax.dot_general(a, b, (((1,), (1,)), ((), ())), preferred_element_type=F32)


def _dot_tn(a, b):
    return lax.dot_general(a, b, (((0,), (0,)), ((), ())), preferred_element_type=F32)


ADA_TN = 1536


def _adaln_body(cond_ref, w_ref, b_ref, o_ref):
    s = _silu(cond_ref[...]).astype(BF16)
    o_ref[...] = _dot(s, w_ref[...].astype(BF16)) + b_ref[...]


def _adaln(cond, ada_w, ada_b):
    n_out = 6 * D_MODEL
    return pl.pallas_call(
        _adaln_body,
        grid=(DEPTH, n_out // ADA_TN),
        in_specs=[
            pl.BlockSpec((MOD_ROWS, D_MODEL), lambda l, j: (0, 0)),
            pl.BlockSpec((None, D_MODEL, ADA_TN), lambda l, j: (l, 0, j)),
            pl.BlockSpec((None, 1, ADA_TN), lambda l, j: (l, 0, j)),
        ],
        out_specs=pl.BlockSpec((None, MOD_ROWS, ADA_TN), lambda l, j: (l, 0, j)),
        out_shape=jax.ShapeDtypeStruct((DEPTH, MOD_ROWS, n_out), F32),
        compiler_params=_cparams(2),
        name="adaln",
    )(cond, ada_w, ada_b.reshape(DEPTH, 1, n_out))


CAST_TR = 256


def _cast_body(w_ref, o_ref):
    o_ref[...] = w_ref[...].astype(BF16)


def _cast_layer_bf16(w, j):
    _, r, c = w.shape
    return pl.pallas_call(
        _cast_body,
        grid=(r // CAST_TR,),
        in_specs=[pl.BlockSpec((None, CAST_TR, c), lambda i: (j, i, 0))],
        out_specs=pl.BlockSpec((CAST_TR, c), lambda i: (i, 0)),
        out_shape=jax.ShapeDtypeStruct((r, c), BF16),
        compiler_params=_cparams(1),
        name="cast_bf16",
    )(w)


def _x_inputs(x):
    n_pt = N_PROMPT // TM
    xa, xb, off = (x[0], x[1], 0) if isinstance(x, tuple) else (x, x, n_pt)
    specs = [pl.BlockSpec((TM, D_MODEL), lambda i: (jnp.minimum(i, n_pt - 1), 0)),
             pl.BlockSpec((TM, D_MODEL), lambda i: (off + jnp.maximum(i - n_pt, 0), 0))]
    return specs, [xa, xb]


def _x_tile(xa_ref, xb_ref):
    return jnp.where(pl.program_id(0) < N_PROMPT // TM, xa_ref[...], xb_ref[...])


def _weight_input(w, layer):
    if layer is None:
        return pl.BlockSpec(w.shape, lambda i: (0, 0)), []
    _, r, c = w.shape
    spec = pl.BlockSpec((None, r, c), lambda i: (layer, 0, 0), pipeline_mode=pl.Buffered(1))
    return spec, [pltpu.VMEM((r, c), BF16)]


def _weight_ref(w_ref, scratch):
    if not scratch:
        return w_ref

    @pl.when(pl.program_id(0) == 0)
    def _():
        scratch[0][...] = w_ref[...].astype(BF16)

    return scratch[0]


def _inproj_body(xa_ref, xb_ref, g_ref, mod_ref, w_ref, *rest, cache_layer, first_cache, n_scratch):
    scratch = rest[len(rest) - n_scratch:]
    rest = rest[:len(rest) - n_scratch]
    o_ref = rest[-3] if cache_layer is not None else rest[-1]
    i = pl.program_id(0)
    w = _weight_ref(w_ref, scratch)
    h = _norm_mod(_x_tile(xa_ref, xb_ref), g_ref[...], mod_ref[0:1, :], mod_ref[1:2, :]).astype(BF16)
    for c in range(o_ref.shape[1] // D_MODEL):
        sl = slice(c * D_MODEL, (c + 1) * D_MODEL)
        r = _dot(h, w[:, sl])
        o_ref[:, sl] = r.astype(o_ref.dtype)
        if cache_layer is not None and c >= 1:
            cache_ref = rest[-3 + c]

            @pl.when(i < BATCH)
            def _(r=r, cache_ref=cache_ref):
                r_t = r.T.reshape(NA_HEADS, NA_HD, SEQ)
                if first_cache:
                    for l in range(N_NA_LAYERS):
                        cache_ref[l] = r_t if l == cache_layer else jnp.zeros_like(r_t)
                else:
                    cache_ref[...] = r_t


def _inproj(x, g, mod, w, w_layer=None, cache_layer=None, prev_caches=None, out_dtype=F32):
    n_out = w.shape[-1]
    x_specs, x_args = _x_inputs(x)
    w_spec, scratch = _weight_input(w, w_layer)
    in_specs = x_specs + [
        pl.BlockSpec((1, D_MODEL), lambda i: (0, 0)),
        pl.BlockSpec((None, MOD_ROWS, D_MODEL), lambda i: (_group_of_tile(i, TM), 0, 0)),
        w_spec,
    ]
    out_specs = [pl.BlockSpec((TM, n_out), lambda i: (i, 0))]
    out_shape = [jax.ShapeDtypeStruct((N_TOK, n_out), out_dtype)]
    args = x_args + [g.reshape(1, D_MODEL), mod, w]
    aliases = {}
    first_cache = prev_caches is None
    if cache_layer is not None:
        assert TM == SEQ and n_out == 3 * D_MODEL
        cache = jax.ShapeDtypeStruct((BATCH, N_NA_LAYERS, NA_HEADS, NA_HD, SEQ), F32)
        if first_cache:
            spec = pl.BlockSpec((None, N_NA_LAYERS, NA_HEADS, NA_HD, SEQ),
                                lambda i: (jnp.minimum(i, BATCH - 1), 0, 0, 0, 0))
        else:
            spec = pl.BlockSpec((None, None, NA_HEADS, NA_HD, SEQ),
                                lambda i: (jnp.minimum(i, BATCH - 1), cache_layer, 0, 0, 0))
            in_specs += [pl.BlockSpec(memory_space=pl.ANY)] * 2
            aliases = {len(args): 1, len(args) + 1: 2}
            args += list(prev_caches)
        out_specs += [spec, spec]
        out_shape += [cache, cache]
    outs = pl.pallas_call(
        functools.partial(_inproj_body, cache_layer=cache_layer, first_cache=first_cache, n_scratch=len(scratch)),
        grid=(N_TOK // TM,),
        in_specs=in_specs,
        out_specs=out_specs,
        out_shape=out_shape,
        scratch_shapes=scratch,
        input_output_aliases=aliases,
        compiler_params=_cparams(1),
        name="inproj",
    )(*args)
    return outs[0] if cache_layer is None else outs


def _decay_terms(dec_ref, h, c):
    lgf = _log_sigmoid(jnp.zeros((c, c), F32) + dec_ref[0, h])
    lgb = _log_sigmoid(jnp.zeros((c, c), F32) + dec_ref[1, h])
    row = lax.broadcasted_iota(I32, (c, c), 0)
    col = lax.broadcasted_iota(I32, (c, c), 1)
    diff = (row - col).astype(F32)
    lower = diff >= 0
    upper = diff <= 0
    mask = (jnp.where(lower, jnp.exp(jnp.where(lower, diff, 0.0) * lgf), 0.0)
            + jnp.where(upper, jnp.exp(jnp.where(upper, -diff, 0.0) * lgb), 0.0))
    pos = lax.broadcasted_iota(I32, (c, 1), 0).astype(F32)
    lgf1 = _log_sigmoid(jnp.zeros((c, 1), F32) + dec_ref[0, h])
    lgb1 = _log_sigmoid(jnp.zeros((c, 1), F32) + dec_ref[1, h])
    lgf0 = _log_sigmoid(jnp.zeros((1, 1), F32) + dec_ref[0, h])
    lgb0 = _log_sigmoid(jnp.zeros((1, 1), F32) + dec_ref[1, h])
    terms = dict(
        mask=mask,
        q_f=jnp.exp((pos + 1.0) * lgf1), k_f=jnp.exp((c - 1.0 - pos) * lgf1), c_f=jnp.exp(c * lgf0),
        q_b=jnp.exp((c - pos) * lgb1), k_b=jnp.exp(pos * lgb1), c_b=jnp.exp(c * lgb0),
    )
    return terms


def _group_norm_gate(o, g):
    o = o * lax.rsqrt(jnp.mean(o * o, axis=-1, keepdims=True) + EPS)
    return (_silu(g) * o).astype(BF16)


def _ret_prompt_body(dec_ref, q_ref, k_ref, v_ref, g_ref, *rest, layer, first):
    o_ref, sf_ref, sb_ref = rest[-3:]
    for h in range(RET_HEADS):
        t = _decay_terms(dec_ref, h, SEQ)
        qk_cols = slice(h * RET_DK, (h + 1) * RET_DK)
        v_cols = slice(h * RET_DV, (h + 1) * RET_DV)
        q = q_ref[:, qk_cols].astype(F32)
        k = k_ref[:, qk_cols].astype(F32) * (RET_DK ** -0.5)
        v = v_ref[:, v_cols].astype(BF16)
        scores = _dot_nt(q.astype(BF16), k.astype(BF16)) * t["mask"]
        o = _dot(scores.astype(BF16), v)
        o_ref[:, v_cols] = _group_norm_gate(o, g_ref[:, v_cols].astype(F32))
        for ref, k_dec in ((sf_ref, t["k_f"]), (sb_ref, t["k_b"])):
            s = _dot_tn((k * k_dec).astype(BF16), v)
            if first:
                for l in range(N_RET_LAYERS):
                    ref[l, h] = s if l == layer else jnp.zeros_like(s)
            else:
                ref[h] = s


def _ret_prompt(proj, decay, layer, prev_states=None):
    assert 2 * RET_QK_W == RET_V_W
    state = jax.ShapeDtypeStruct((BATCH, N_RET_LAYERS, RET_HEADS, RET_DK, RET_DV), F32)
    in_specs = [
        pl.BlockSpec(memory_space=pltpu.SMEM),
        pl.BlockSpec((SEQ, RET_QK_W), lambda b: (b, 0)),
        pl.BlockSpec((SEQ, RET_QK_W), lambda b: (b, 1)),
        pl.BlockSpec((SEQ, RET_V_W), lambda b: (b, 1)),
        pl.BlockSpec((SEQ, RET_V_W), lambda b: (b, 2)),
    ]
    args = [decay, proj, proj, proj, proj]
    first = prev_states is None
    aliases = {}
    if first:
        s_spec = pl.BlockSpec((None, N_RET_LAYERS, RET_HEADS, RET_DK, RET_DV), lambda b: (b, 0, 0, 0, 0))
    else:
        s_spec = pl.BlockSpec((None, None, RET_HEADS, RET_DK, RET_DV), lambda b: (b, layer, 0, 0, 0))
        in_specs += [pl.BlockSpec(memory_space=pl.ANY)] * 2
        aliases = {len(args): 1, len(args) + 1: 2}
        args += list(prev_states)
    return pl.pallas_call(
        functools.partial(_ret_prompt_body, layer=layer, first=first),
        grid=(BATCH,),
        in_specs=in_specs,
        out_specs=[pl.BlockSpec((SEQ, RET_V_W), lambda b: (b, 0)), s_spec, s_spec],
        out_shape=[jax.ShapeDtypeStruct((N_PROMPT, RET_V_W), BF16), state, state],
        input_output_aliases=aliases,
        compiler_params=_cparams(1),
        name="ret_prompt",
    )(*args)


def _rope(x, cos, sin_signed):
    n = x.shape[-1]
    lane = lax.broadcasted_iota(I32, x.shape, 1)
    swapped = jnp.where(lane % 2 == 0, pltpu.roll(x, n - 1, 1), pltpu.roll(x, 1, 1))
    return x * cos + swapped * sin_signed


def _ret_sample_body(dec_ref, q_ref, k_ref, v_ref, g_ref, cos_ref, sin_ref, s0f_ref, s0b_ref, o_ref,
                     sf_scr, sb_scr):
    h = pl.program_id(1)
    c = RET_CHUNK
    n_chunks = DEC_SEQ // c
    t = _decay_terms(dec_ref, h, c)

    def chunk(ref, i):
        return ref[i * c:(i + 1) * c, :].astype(F32)

    def qk(i):
        q = _rope(chunk(q_ref, i), chunk(cos_ref, i), chunk(sin_ref, i))
        k = _rope(chunk(k_ref, i) * (RET_DK ** -0.5), chunk(cos_ref, i), chunk(sin_ref, i))
        return q, k

    s = s0f_ref[...]
    for i in range(n_chunks):
        sf_scr[i] = s
        if i + 1 < n_chunks:
            _, k = qk(i)
            s = s * t["c_f"] + _dot_tn((k * t["k_f"]).astype(BF16), chunk(v_ref, i).astype(BF16))
    s = s0b_ref[...]
    for i in reversed(range(n_chunks)):
        sb_scr[i] = s
        if i > 0:
            _, k = qk(i)
            s = s * t["c_b"] + _dot_tn((k * t["k_b"]).astype(BF16), chunk(v_ref, i).astype(BF16))

    for i in range(n_chunks):
        q, k = qk(i)
        scores = _dot_nt(q.astype(BF16), k.astype(BF16)) * t["mask"]
        o = _dot(scores.astype(BF16), chunk(v_ref, i).astype(BF16))
        o = o + _dot((q * t["q_f"]).astype(BF16), sf_scr[i].astype(BF16))
        o = o + _dot((q * t["q_b"]).astype(BF16), sb_scr[i].astype(BF16))
        o_ref[i * c:(i + 1) * c, :] = _group_norm_gate(o, chunk(g_ref, i))


def _ret_sample(proj, decay, cos, sin_signed, s0f, s0b, j):
    kq = RET_QK_W // RET_DK
    kv = 2 * RET_QK_W // RET_DV
    kg = kv + RET_HEADS
    rb = N_PROMPT // DEC_SEQ
    n_chunks = DEC_SEQ // RET_CHUNK
    return pl.pallas_call(
        _ret_sample_body,
        grid=(DEC_BATCH, RET_HEADS),
        in_specs=[
            pl.BlockSpec(memory_space=pltpu.SMEM),
            pl.BlockSpec((DEC_SEQ, RET_DK), lambda b, h: (rb + b, h)),
            pl.BlockSpec((DEC_SEQ, RET_DK), lambda b, h: (rb + b, kq + h)),
            pl.BlockSpec((DEC_SEQ, RET_DV), lambda b, h: (rb + b, kv + h)),
            pl.BlockSpec((DEC_SEQ, RET_DV), lambda b, h: (rb + b, kg + h)),
            pl.BlockSpec((DEC_SEQ, RET_DK), lambda b, h: (0, 0)),
            pl.BlockSpec((DEC_SEQ, RET_DK), lambda b, h: (0, 0)),
            pl.BlockSpec((None, None, None, RET_DK, RET_DV), lambda b, h: (b, j, h, 0, 0)),
            pl.BlockSpec((None, None, None, RET_DK, RET_DV), lambda b, h: (b, j, h, 0, 0)),
        ],
        out_specs=pl.BlockSpec((DEC_SEQ, RET_DV), lambda b, h: (b, h)),
        out_shape=jax.ShapeDtypeStruct((N_SAMPLE, RET_V_W), BF16),
        scratch_shapes=[pltpu.VMEM((n_chunks, RET_DK, RET_DV), F32),
                        pltpu.VMEM((n_chunks, RET_DK, RET_DV), F32)],
        compiler_params=_cparams(2),
        name="ret_sample",
    )(decay, proj, proj, proj, proj, cos, sin_signed, s0f, s0b)


NA_HPB = LANES // NA_HD


def _head_lanes(rows, hh):
    lane = lax.broadcasted_iota(I32, (rows, LANES), 1)
    return jnp.logical_and(lane >= hh * NA_HD, lane < (hh + 1) * NA_HD)


def _with_ones(v):
    return jnp.concatenate([v, jnp.ones(v.shape, v.dtype)], axis=-1)


def _ctx_attn_body(q_ref, k_ref, v_ref, o_ref):
    outs = []
    for p in range(NA_HEADS // NA_HPB):
        sl = slice(p * LANES, (p + 1) * LANES)
        q2 = q_ref[:, sl] * (NA_HD ** -0.5)
        k2 = k_ref[:, sl].astype(BF16)
        v2 = _with_ones(v_ref[:, sl].astype(BF16))
        o_pair = None
        for hh in range(NA_HPB):
            mine = _head_lanes(SEQ, hh)
            s = _dot_nt(jnp.where(mine, q2, 0.0).astype(BF16), k2)
            e = jnp.exp(s - s.max(axis=-1, keepdims=True)).astype(BF16)
            acc = _dot(e, v2)
            o_h = acc[:, :LANES] / acc[:, LANES:]
            o_pair = o_h if o_pair is None else jnp.where(mine, o_h, o_pair)
        outs.append(o_pair)
    o_ref[...] = jnp.concatenate(outs, axis=-1).astype(BF16)


def _ctx_attn(proj):
    return pl.pallas_call(
        _ctx_attn_body,
        grid=(BATCH,),
        in_specs=[
            pl.BlockSpec((SEQ, D_MODEL), lambda b: (b, 0)),
            pl.BlockSpec((SEQ, D_MODEL), lambda b: (b, 1)),
            pl.BlockSpec((SEQ, D_MODEL), lambda b: (b, 2)),
        ],
        out_specs=pl.BlockSpec((SEQ, D_MODEL), lambda b: (b, 0)),
        out_shape=jax.ShapeDtypeStruct((N_PROMPT, D_MODEL), BF16),
        compiler_params=_cparams(1),
        name="ctx_attn",
    )(proj, proj, proj)


NA_QT = 512
NA_WIN = NA_KH * GRID_W


def _na_attn_body(q_ref, k_ref, v_ref, ck_ref, cv_ref, tbl_ref, o_ref):
    t = pl.program_id(1)
    rows_per_tile = NA_QT // GRID_W
    n_rows = DEC_SEQ // GRID_W
    q2 = q_ref[...] * (NA_HD ** -0.5)
    q = [jnp.where(_head_lanes(NA_QT, hh), q2, 0.0).astype(BF16) for hh in range(NA_HPB)]
    s_ctx = _dot_nt(jnp.concatenate(q, axis=0), ck_ref[...].astype(BF16))
    acc_win = []
    e_ctx = []
    for u in range(rows_per_tile):
        qr = t * rows_per_tile + u
        rs = jnp.clip(qr - NA_KH // 2, 0, n_rows - NA_KH)
        i0 = rs - qr + (NA_KH - 1)
        start = pl.multiple_of(rs * GRID_W, GRID_W)
        kw = k_ref[pl.ds(start, NA_WIN), :].astype(BF16)
        vw = _with_ones(v_ref[pl.ds(start, NA_WIN), :].astype(BF16))
        rows = slice(u * GRID_W, (u + 1) * GRID_W)
        bias = jnp.concatenate(
            [jnp.concatenate([tbl_ref[hh, i0 + 2 * m] for m in range(NA_KH // 2)], axis=-1) for hh in range(NA_HPB)],
            axis=0)
        sw = _dot_nt(jnp.concatenate([q[hh][rows] for hh in range(NA_HPB)], axis=0), kw) + bias
        sc = jnp.concatenate([s_ctx[hh * NA_QT + u * GRID_W:hh * NA_QT + (u + 1) * GRID_W] for hh in range(NA_HPB)],
                             axis=0)
        m = jnp.maximum(sw.max(axis=-1, keepdims=True), sc.max(axis=-1, keepdims=True))
        acc_win.append(_dot(jnp.exp(sw - m).astype(BF16), vw))
        e_ctx.append(jnp.exp(sc - m).astype(BF16))
    acc = jnp.concatenate(acc_win, axis=0) + _dot(jnp.concatenate(e_ctx, axis=0), _with_ones(cv_ref[...].astype(BF16)))
    o = acc[:, :LANES] / acc[:, LANES:]
    outs = []
    for u in range(rows_per_tile):
        base = u * NA_HPB * GRID_W
        o_u = o[base:base + GRID_W]
        for hh in range(1, NA_HPB):
            o_u = jnp.where(_head_lanes(GRID_W, hh), o[base + hh * GRID_W:base + (hh + 1) * GRID_W], o_u)
        outs.append(o_u)
    o_ref[...] = jnp.concatenate(outs, axis=0).astype(BF16)


def _na_attn(proj, cache_k, cache_v, tbl, j):
    n_qt = DEC_SEQ // NA_QT
    rb_q = N_PROMPT // NA_QT
    rb_kv = N_PROMPT // DEC_SEQ
    cb = D_MODEL // LANES
    ck = cache_k.reshape(DEC_BATCH, -1, PAST_LEN, D_MODEL)
    cv = cache_v.reshape(DEC_BATCH, -1, PAST_LEN, D_MODEL)
    return pl.pallas_call(
        _na_attn_body,
        grid=(cb, n_qt, DEC_BATCH),
        in_specs=[
            pl.BlockSpec((NA_QT, LANES), lambda p, t, b: (rb_q + b * n_qt + t, p)),
            pl.BlockSpec((DEC_SEQ, LANES), lambda p, t, b: (rb_kv + b, cb + p)),
            pl.BlockSpec((DEC_SEQ, LANES), lambda p, t, b: (rb_kv + b, 2 * cb + p)),
            pl.BlockSpec((None, None, PAST_LEN, LANES), lambda p, t, b: (b, j, 0, p)),
            pl.BlockSpec((None, None, PAST_LEN, LANES), lambda p, t, b: (b, j, 0, p)),
            pl.BlockSpec((None, NA_HPB, 2 * NA_KH - 2, GRID_W, 2 * GRID_W), lambda p, t, b: (j, p, 0, 0, 0)),
        ],
        out_specs=pl.BlockSpec((NA_QT, LANES), lambda p, t, b: (b * n_qt + t, p)),
        out_shape=jax.ShapeDtypeStruct((N_SAMPLE, D_MODEL), BF16),
        compiler_params=_cparams(3),
        name="na_attn",
    )(proj, proj, proj, ck, cv, tbl)


def _na_bias_table(rpb):
    c = np.arange(GRID_W)
    cs = np.clip(c - NA_KW // 2, 0, GRID_W - NA_KW)
    col_valid = (c[None, :] >= cs[:, None]) & (c[None, :] < cs[:, None] + NA_KW)
    col_off = np.clip(c[None, :] - c[:, None] + NA_KW - 1, 0, 2 * NA_KW - 2)
    sel_c = (col_off[..., None] == np.arange(2 * NA_KW - 1)).astype(np.float32)
    t = jnp.einsum("lhij,qkj->lhiqk", rpb, sel_c, precision=lax.Precision.HIGHEST)
    t = jnp.where(col_valid, t, -jnp.inf)
    return jnp.concatenate([t[:, :, :-1], t[:, :, 1:]], axis=-1)


def _outproj_body(ap_ref, as_ref, w_ref, xa_ref, xb_ref, g_ref, mod_ref, wr_ref, br_ref, xo_ref, tok_ref, lg_ref,
                  *scratch):
    i = pl.program_id(0)
    a = jnp.where(i < N_PROMPT // TM, ap_ref[...], as_ref[...])
    x = _x_tile(xa_ref, xb_ref) + mod_ref[2:3, :] * _dot(a, _weight_ref(w_ref, scratch)[...])
    xo_ref[...] = x
    tok = _norm_mod(x, g_ref[...], mod_ref[3:4, :], mod_ref[4:5, :])
    tok_ref[...] = _pack_rows(tok)
    hi = tok.astype(BF16)
    lo = (tok - hi.astype(F32)).astype(BF16)
    wr = wr_ref[...]
    wr_hi = wr.astype(BF16)
    wr_lo = (wr - wr_hi.astype(F32)).astype(BF16)
    hi_terms = _dot(hi, jnp.concatenate([wr_hi, wr_lo], axis=-1))
    lg_ref[...] = hi_terms[:, :LANES] + _dot(lo, wr_hi) + hi_terms[:, LANES:] + br_ref[...]


def _outproj(a_prompt, a_sample, w, w_layer, x, g, mod, wr, br):
    k_in = a_prompt.shape[1]
    n_pt = N_PROMPT // TM
    x_specs, x_args = _x_inputs(x)
    w_spec, scratch = _weight_input(w, w_layer)
    return pl.pallas_call(
        _outproj_body,
        grid=(N_TOK // TM,),
        in_specs=[
            pl.BlockSpec((TM, k_in), lambda i: (jnp.minimum(i, n_pt - 1), 0)),
            pl.BlockSpec((TM, k_in), lambda i: (jnp.maximum(i - n_pt, 0), 0)),
            w_spec,
            *x_specs,
            pl.BlockSpec((1, D_MODEL), lambda i: (0, 0)),
            pl.BlockSpec((None, MOD_ROWS, D_MODEL), lambda i: (_group_of_tile(i, TM), 0, 0)),
            pl.BlockSpec((D_MODEL, LANES), lambda i: (0, 0)),
            pl.BlockSpec((1, LANES), lambda i: (0, 0)),
        ],
        out_specs=[
            pl.BlockSpec((TM, D_MODEL), lambda i: (i, 0)),
            pl.BlockSpec((TM, HALF), lambda i: (i, 0)),
            pl.BlockSpec((TM, LANES), lambda i: (i, 0)),
        ],
        out_shape=[
            jax.ShapeDtypeStruct((N_TOK, D_MODEL), F32),
            jax.ShapeDtypeStruct((N_TOK, HALF), U32),
            jax.ShapeDtypeStruct((N_TOK, LANES), F32),
        ],
        scratch_shapes=scratch,
        compiler_params=_cparams(1),
        name="outproj",
    )(a_prompt, a_sample, w, *x_args, g.reshape(1, D_MODEL), mod, wr, br)


def _router_body(lg_ref, idx_ref, gate_ref, rank_ref, cnt_ref, carry):
    i = pl.program_id(0)

    @pl.when(i == 0)
    def _():
        carry[...] = jnp.zeros_like(carry)

    lane = lax.broadcasted_iota(I32, (ROUTER_TM, LANES), 1)
    l = jnp.where(lane < N_EXPERTS, lg_ref[...], -jnp.inf)
    vals, idxs, hots = [], [], []
    for _ in range(TOP_K):
        m = l.max(axis=-1, keepdims=True)
        idx = jnp.where(l == m, lane, LANES).min(axis=-1, keepdims=True)
        hot = lane == idx
        l = jnp.where(hot, -jnp.inf, l)
        vals.append(m)
        idxs.append(idx)
        hots.append(hot)
    es = [jnp.exp(v - vals[0]) for v in vals]
    denom = es[0] + es[1] + es[2] + es[3]
    hot_all = (hots[0] | hots[1] | hots[2] | hots[3]).astype(F32)
    r = lax.broadcasted_iota(I32, (ROUTER_TM, ROUTER_TM), 0)
    c = lax.broadcasted_iota(I32, (ROUTER_TM, ROUTER_TM), 1)
    before = _dot((c < r).astype(BF16), hot_all.astype(BF16)) + carry[...]
    idx_out = jnp.zeros((ROUTER_TM, LANES), I32)
    gate_out = jnp.zeros((ROUTER_TM, LANES), F32)
    rank_out = jnp.zeros((ROUTER_TM, LANES), I32)
    for k in range(TOP_K):
        rank_k = jnp.where(hots[k], before, 0.0).sum(axis=-1, keepdims=True).astype(I32)
        idx_out = jnp.where(lane == k, idxs[k], idx_out)
        gate_out = jnp.where(lane == k, es[k] / denom, gate_out)
        rank_out = jnp.where(lane == k, rank_k, rank_out)
    idx_ref[...] = idx_out
    gate_ref[...] = gate_out
    rank_ref[...] = rank_out
    carry[...] = carry[...] + hot_all.sum(axis=0, keepdims=True)
    cnt_ref[...] = carry[...].astype(I32)


def _router(logits):
    spec = pl.BlockSpec((ROUTER_TM, LANES), lambda i: (i, 0))
    return pl.pallas_call(
        _router_body,
        grid=(N_TOK // ROUTER_TM,),
        in_specs=[spec],
        out_specs=[spec, spec, spec, pl.BlockSpec((1, LANES), lambda i: (0, 0))],
        out_shape=[
            jax.ShapeDtypeStruct((N_TOK, LANES), I32),
            jax.ShapeDtypeStruct((N_TOK, LANES), F32),
            jax.ShapeDtypeStruct((N_TOK, LANES), I32),
            jax.ShapeDtypeStruct((1, LANES), I32),
        ],
        scratch_shapes=[pltpu.VMEM((1, LANES), F32)],
        compiler_params=_cparams(1),
        name="router",
    )(logits)


SC_CORES = 2
SC_SUBCORES = 16
SC_WORKERS = SC_CORES * SC_SUBCORES
SC_CH = 32
SC_SCATTER_BUFS = 5
SC_GATHER_BUFS = 4


def _sc_mesh():
    return plsc.VectorSubcoreMesh(core_axis_name="c", subcore_axis_name="s",
                                  num_cores=SC_CORES, num_subcores=SC_SUBCORES)


def _sc_worker():
    return lax.axis_index("s") * SC_CORES + lax.axis_index("c")


def _dispatch(tok, pos_km):
    n, d = tok.shape
    n_ch = n // (SC_WORKERS * SC_CH)
    nb = SC_SCATTER_BUFS
    assert n_ch * SC_WORKERS * SC_CH == n and n_ch % nb == 0

    @functools.partial(
        pl.kernel, mesh=_sc_mesh(), out_type=jax.ShapeDtypeStruct((MOE_ROWS, d), tok.dtype),
        scratch_types=[pltpu.VMEM((n_ch, TOP_K, SC_CH), I32), pltpu.VMEM((nb, SC_CH, d), tok.dtype),
                       pltpu.SemaphoreType.DMA((nb,)), pltpu.SemaphoreType.DMA((nb,))])
    def scatter_rows(tok_hbm, pos_hbm, out_hbm, idx_v, rows_v, load_sem, scat_sem):
        wid = _sc_worker()
        base = wid * (n_ch * SC_CH)
        pltpu.sync_copy(pos_hbm.at[wid], idx_v)

        def load(g, slot):
            rows = pl.ds(pl.multiple_of(base + g * SC_CH, 8), SC_CH)
            return pltpu.make_async_copy(tok_hbm.at[rows], rows_v.at[slot], load_sem.at[slot])

        def scatter(g, k, slot):
            return pltpu.make_async_copy(rows_v.at[slot], out_hbm.at[idx_v.at[g, k]], scat_sem.at[slot])

        for s in range(nb - 1):
            load(s, s).start()

        @pl.loop(0, n_ch, step=nb)
        def _(g0):
            for s in range(nb):
                g = g0 + s
                reuse = (s + nb - 1) % nb

                @pl.when(g >= 1)
                def _():
                    for k in range(TOP_K):
                        scatter(g - 1, k, reuse).wait()

                @pl.when(g + nb - 1 < n_ch)
                def _():
                    load(g + nb - 1, reuse).start()

                load(g, s).wait()
                for k in range(TOP_K):
                    scatter(g, k, s).start()

        for k in range(TOP_K):
            scatter(n_ch - 1, k, (n_ch - 1) % nb).wait()

    pos = pos_km.reshape(TOP_K, SC_WORKERS, n_ch, SC_CH).transpose(1, 2, 0, 3)
    return scatter_rows(tok, pos)


def _gather_rows(table, idx):
    b, d = idx.shape[0], table.shape[1]
    n_ch = b // (SC_WORKERS * SC_CH)
    nb = SC_GATHER_BUFS
    assert n_ch * SC_WORKERS * SC_CH == b and n_ch % nb == 0

    @functools.partial(
        pl.kernel, mesh=_sc_mesh(), out_type=jax.ShapeDtypeStruct((b, d), table.dtype),
        scratch_types=[pltpu.VMEM((n_ch, SC_CH), I32), pltpu.VMEM((nb, SC_CH, d), table.dtype),
                       pltpu.SemaphoreType.DMA((nb,)), pltpu.SemaphoreType.DMA((nb,))])
    def gather_rows(table_hbm, idx_hbm, out_hbm, idx_v, rows_v, gather_sem, write_sem):
        wid = _sc_worker()
        base = wid * (n_ch * SC_CH)
        pltpu.sync_copy(idx_hbm.at[wid], idx_v)

        def gather(g, slot):
            return pltpu.make_async_copy(table_hbm.at[idx_v.at[g]], rows_v.at[slot], gather_sem.at[slot])

        def write_out(g, slot):
            rows = pl.ds(pl.multiple_of(base + g * SC_CH, 8), SC_CH)
            return pltpu.make_async_copy(rows_v.at[slot], out_hbm.at[rows], write_sem.at[slot])

        for s in range(nb - 1):
            gather(s, s).start()

        @pl.loop(0, n_ch, step=nb)
        def _(g0):
            for s in range(nb):
                g = g0 + s
                reuse = (s + nb - 1) % nb

                @pl.when(g >= 1)
                def _():
                    write_out(g - 1, reuse).wait()

                @pl.when(g + nb - 1 < n_ch)
                def _():
                    gather(g + nb - 1, reuse).start()

                gather(g, s).wait()
                write_out(g, s).start()

        write_out(n_ch - 1, (n_ch - 1) % nb).wait()

    return gather_rows(table, idx.reshape(SC_WORKERS, n_ch, SC_CH))


def _moe_body(te_ref, tv_ref, slot_ref, nxt_ref, x_ref, w1_hbm, w2_hbm, perm_ref, *rest, layer):
    b_refs = rest[:2 * MOE_TPS]
    o_ref, w1_buf, w2_buf, w1_scr, w2_scr, sem = rest[2 * MOE_TPS:]
    for sub in range(MOE_TPS):
        rows = slice(sub * MOE_TM, (sub + 1) * MOE_TM)
        _moe_tile(te_ref, tv_ref, slot_ref, nxt_ref, pl.program_id(0) * MOE_TPS + sub, x_ref.at[rows], w1_hbm,
                  b_refs[2 * sub], w2_hbm, b_refs[2 * sub + 1], perm_ref, o_ref.at[rows],
                  w1_buf, w2_buf, w1_scr, w2_scr, sem, layer)


def _moe_tile(te_ref, tv_ref, slot_ref, nxt_ref, t, x_ref, w1_hbm, b1_ref, w2_hbm, b2_ref, perm_ref, o_ref,
              w1_buf, w2_buf, w1_scr, w2_scr, sem, layer):
    e = te_ref[t]
    slot = slot_ref[t]
    nxt = nxt_ref[t]
    first = jnp.logical_or(t == 0, e != te_ref[jnp.maximum(t - 1, 0)])

    def fetch(expert, s):
        return (pltpu.make_async_copy(w1_hbm.at[layer, expert], w1_buf.at[s], sem.at[s, 0]),
                pltpu.make_async_copy(w2_hbm.at[layer, expert], w2_buf.at[s], sem.at[s, 1]))

    @pl.when(t == 0)
    def _():
        for cp in fetch(e, slot):
            cp.start()

    @pl.when(jnp.logical_and(tv_ref[t] == 1, first))
    def _():
        for cp in fetch(e, slot):
            cp.wait()

        @pl.when(nxt != e)
        def _():
            for cp in fetch(nxt, 1 - slot):
                cp.start()

        w2_scr[...] = w2_buf[slot].astype(BF16)
        half = PERM_W // 2
        for c in range(2 * D_EXPERT // PERM_W):
            blk = _dot(w1_buf[slot, :, c * PERM_W:(c + 1) * PERM_W].astype(BF16), perm_ref[...]).astype(BF16)
            w1_scr[:, c * half:(c + 1) * half] = blk[:, :half]
            w1_scr[:, D_EXPERT + c * half:D_EXPERT + (c + 1) * half] = blk[:, half:]

    @pl.when(tv_ref[t] == 0)
    def _():
        o_ref[...] = jnp.zeros_like(o_ref)

    @pl.when(tv_ref[t] == 1)
    def _():
        x_lo, x_hi = _unpack_rows(x_ref[...])
        x = jnp.concatenate([x_lo.astype(BF16), x_hi.astype(BF16)], axis=-1)
        hh = _dot(x, w1_scr[...]) + b1_ref[...]
        h_glu = jnp.minimum(hh[:, :D_EXPERT], SWIGLU_LIMIT)
        h_lin = jnp.clip(hh[:, D_EXPERT:], -SWIGLU_LIMIT, SWIGLU_LIMIT)
        a = h_glu * _sigmoid(SWIGLU_ALPHA * h_glu) * (h_lin + 1.0)
        o_ref[...] = _pack_rows(_dot(a.astype(BF16), w2_scr[...]) + b2_ref[...])


def _deinterleave_bias_body(b_ref, perm_ref, o_ref):
    half = PERM_W // 2
    for c in range(2 * D_EXPERT // PERM_W):
        r = b_ref[:, c * PERM_W:(c + 1) * PERM_W]
        acc = jnp.zeros(r.shape, F32)
        for _ in range(3):
            piece = r.astype(BF16)
            acc = acc + _dot(piece, perm_ref[...])
            r = r - piece.astype(F32)
        o_ref[:, c * half:(c + 1) * half] = acc[:, :half]
        o_ref[:, D_EXPERT + c * half:D_EXPERT + (c + 1) * half] = acc[:, half:]


def _deinterleave_bias(b1, perm):
    return pl.pallas_call(
        _deinterleave_bias_body,
        out_shape=jax.ShapeDtypeStruct(b1.shape, F32),
        compiler_params=_cparams(0),
        name="deinterleave_bias",
    )(b1, perm)


def _moe_experts(tile_expert, tile_valid, tile_slot, tile_next, xs, w1, b1_split, w2, b2, perm, layer):
    bias_specs = []
    for sub in range(MOE_TPS):
        bias_specs += [
            pl.BlockSpec((None, None, 1, 2 * D_EXPERT), lambda s, te, *_, sub=sub: (layer, te[s * MOE_TPS + sub], 0, 0)),
            pl.BlockSpec((None, None, 1, D_MODEL), lambda s, te, *_, sub=sub: (layer, te[s * MOE_TPS + sub], 0, 0)),
        ]
    grid_spec = pltpu.PrefetchScalarGridSpec(
        num_scalar_prefetch=4,
        grid=(MOE_TILES // MOE_TPS,),
        in_specs=[
            pl.BlockSpec((MOE_TPS * MOE_TM, HALF), lambda s, *_: (s, 0)),
            pl.BlockSpec(memory_space=pl.ANY),
            pl.BlockSpec(memory_space=pl.ANY),
            pl.BlockSpec((PERM_W, PERM_W), lambda s, *_: (0, 0)),
            *bias_specs,
        ],
        out_specs=pl.BlockSpec((MOE_TPS * MOE_TM, HALF), lambda s, *_: (s, 0)),
        scratch_shapes=[
            pltpu.VMEM((2, D_MODEL, 2 * D_EXPERT), F32),
            pltpu.VMEM((2, D_EXPERT, D_MODEL), F32),
            pltpu.VMEM((D_MODEL, 2 * D_EXPERT), BF16),
            pltpu.VMEM((D_EXPERT, D_MODEL), BF16),
            pltpu.SemaphoreType.DMA((2, 2)),
        ],
    )
    return pl.pallas_call(
        functools.partial(_moe_body, layer=layer),
        grid_spec=grid_spec,
        out_shape=jax.ShapeDtypeStruct((MOE_ROWS, HALF), U32),
        compiler_params=_cparams(1),
        name="moe_experts",
    )(tile_expert, tile_valid, tile_slot, tile_next, xs, w1, w2, perm, *([b1_split, b2] * MOE_TPS))


COMBINE_TM = 512


def _combine_body(*refs, final):
    y_refs = refs[:TOP_K]
    gate_ref, x_ref, mod_ref, g_ref = refs[TOP_K:TOP_K + 4]
    gates = gate_ref[...]
    y_lo, y_hi = (gates[:, 0:1] * h for h in _unpack_rows(y_refs[0][...]))
    for k in range(1, TOP_K):
        lo, hi = _unpack_rows(y_refs[k][...])
        y_lo = y_lo + gates[:, k:k + 1] * lo
        y_hi = y_hi + gates[:, k:k + 1] * hi
    x = x_ref[...] + mod_ref[5:6, :] * jnp.concatenate([y_lo, y_hi], axis=-1)
    if not final:
        refs[-1][...] = x
        return
    x = (x * lax.rsqrt(jnp.mean(x * x, axis=-1, keepdims=True) + EPS)) * g_ref[...]
    prompt_ref, sample_ref = refs[-2:]
    is_prompt = pl.program_id(0) < N_PROMPT // COMBINE_TM

    @pl.when(is_prompt)
    def _():
        prompt_ref[...] = x

    @pl.when(jnp.logical_not(is_prompt))
    def _():
        sample_ref[...] = x


def _combine(y_km, gates, x, mod, g_final, final):
    n_t = N_TOK // COMBINE_TM
    n_pt = N_PROMPT // COMBINE_TM
    y_specs = [pl.BlockSpec((COMBINE_TM, HALF), lambda i, k=k: (k * n_t + i, 0)) for k in range(TOP_K)]
    if final:
        out_specs = [pl.BlockSpec((COMBINE_TM, D_MODEL), lambda i: (jnp.minimum(i, n_pt - 1), 0)),
                     pl.BlockSpec((COMBINE_TM, D_MODEL), lambda i: (jnp.maximum(i - n_pt, 0), 0))]
        out_shape = [jax.ShapeDtypeStruct((N_PROMPT, D_MODEL), F32), jax.ShapeDtypeStruct((N_SAMPLE, D_MODEL), F32)]
    else:
        out_specs = pl.BlockSpec((COMBINE_TM, D_MODEL), lambda i: (i, 0))
        out_shape = jax.ShapeDtypeStruct((N_TOK, D_MODEL), F32)
    return pl.pallas_call(
        functools.partial(_combine_body, final=final),
        grid=(n_t,),
        in_specs=y_specs + [
            pl.BlockSpec((COMBINE_TM, LANES), lambda i: (i, 0)),
            pl.BlockSpec((COMBINE_TM, D_MODEL), lambda i: (i, 0)),
            pl.BlockSpec((None, MOD_ROWS, D_MODEL), lambda i: (_group_of_tile(i, COMBINE_TM), 0, 0)),
            pl.BlockSpec((1, D_MODEL), lambda i: (0, 0)),
        ],
        out_specs=out_specs,
        out_shape=out_shape,
        compiler_params=_cparams(1),
        name="combine",
    )(*([y_km] * TOP_K), gates, x, mod, g_final.reshape(1, D_MODEL))


def _routing_tables(idx, rank, counts):
    counts = counts[0, :N_EXPERTS]
    padded = ((counts + MOE_TM - 1) // MOE_TM) * MOE_TM
    ends = jnp.cumsum(padded)
    offsets = ends - padded
    idx = idx[:, :TOP_K]
    pos = rank[:, :TOP_K] + jnp.sum(
        jnp.where(idx[..., None] == jnp.arange(N_EXPERTS, dtype=I32), offsets, 0), axis=-1)
    pos_km = pos.T.astype(I32)
    tile_start = jnp.arange(MOE_TILES, dtype=I32) * MOE_TM
    tile_valid = (tile_start < ends[-1]).astype(I32)
    tile_expert = jnp.sum((tile_start[:, None] >= ends[None, :]).astype(I32), axis=-1)
    last_expert = jnp.sum((ends[-1] - 1 >= ends).astype(I32))
    tile_expert = jnp.where(tile_valid == 1, tile_expert, last_expert).astype(I32)
    e_ids = jnp.arange(N_EXPERTS, dtype=I32)
    owns = counts > 0
    ordinal = jnp.cumsum(owns.astype(I32)) - 1
    later = jnp.where(owns[None, :] & (e_ids[None, :] > e_ids[:, None]), e_ids[None, :], N_EXPERTS)
    nxt = jnp.min(later, axis=1)
    nxt = jnp.where(nxt == N_EXPERTS, e_ids, nxt)
    onehot = (tile_expert[:, None] == e_ids[None, :]).astype(I32)
    tile_slot = jnp.sum(onehot * (ordinal % 2)[None, :], axis=1).astype(I32)
    tile_next = jnp.sum(onehot * nxt[None, :], axis=1).astype(I32)
    return pos_km, tile_expert, tile_valid, tile_slot, tile_next


def _rope_tables():
    t = np.arange(DEC_SEQ)
    row = (t // GRID_W).astype(np.float32)
    col = (t % GRID_W).astype(np.float32)
    d_axis = RET_DK // 2
    inv = jnp.asarray(ROPE_BASE, F32) ** (-jnp.arange(0, d_axis, 2, dtype=F32) / d_axis)
    ang = jnp.concatenate([row[:, None] * inv, col[:, None] * inv], axis=-1)
    cos = jnp.repeat(jnp.cos(ang), 2, axis=-1)
    sin = jnp.repeat(jnp.sin(ang), 2, axis=-1)
    sign = jnp.asarray(np.tile(np.array([-1.0, 1.0], np.float32), RET_DK // 2))
    return cos, sin * sign


def _deinterleave_matrix():
    p = np.zeros((PERM_W, PERM_W), np.float32)
    half = PERM_W // 2
    for j in range(half):
        p[2 * j, j] = 1.0
        p[2 * j + 1, half + j] = 1.0
    return jnp.asarray(p, BF16)


def kernel(x_prompt, x_sample, state_ret_fwd, state_ret_bwd, cache_na_k, cache_na_v, c, c_ctx, ada_w, ada_b, norm_mix, norm_ffn, norm_final, ret_w_in, ret_w_out, ret_decay_fwd, ret_decay_bwd, na_w_in, na_w_out, na_rpb, moe_w_router, moe_b_router, moe_w1, moe_b1, moe_w2, moe_b2):
    x = (x_prompt.reshape(N_PROMPT, D_MODEL), x_sample.reshape(N_SAMPLE, D_MODEL))

    cond = jnp.zeros((MOD_ROWS, D_MODEL), F32).at[0].set(c_ctx).at[1:1 + DEC_BATCH].set(c)
    mod_all = _adaln(cond, ada_w, ada_b)
    mod_all = mod_all[:, :N_GROUPS].reshape(DEPTH, N_GROUPS, 6, D_MODEL)
    mod_all = jnp.pad(mod_all, ((0, 0), (0, 0), (0, MOD_ROWS - 6), (0, 0)))

    cos, sin_signed = _rope_tables()
    perm = _deinterleave_matrix()
    wr_all = jnp.pad(moe_w_router, ((0, 0), (0, 0), (0, LANES - N_EXPERTS)))
    br_all = jnp.pad(moe_b_router, ((0, 0), (0, LANES - N_EXPERTS))).reshape(DEPTH, 1, LANES)
    b1_split = _deinterleave_bias(moe_b1.reshape(DEPTH * N_EXPERTS, 2 * D_EXPERT), perm)
    b1_split = b1_split.reshape(DEPTH, N_EXPERTS, 1, 2 * D_EXPERT)
    b2_all = moe_b2.reshape(DEPTH, N_EXPERTS, 1, D_MODEL)
    na_tbl = _na_bias_table(na_rpb)
    ret_states = None
    na_caches = None
    for layer in range(DEPTH):
        mod = mod_all[layer]
        j = layer // 2
        if layer % 2 == 0:
            proj = _inproj(x, norm_mix[layer], mod, _cast_layer_bf16(ret_w_in, j), out_dtype=BF16)
            decay = jnp.stack([ret_decay_fwd[j], ret_decay_bwd[j]])
            a_p, s_f, s_b = _ret_prompt(proj, decay, j, ret_states)
            ret_states = (s_f, s_b)
            a_s = _ret_sample(proj, decay, cos, sin_signed, state_ret_fwd, state_ret_bwd, j)
            w_out = ret_w_out
        else:
            proj, ck, cv = _inproj(x, norm_mix[layer], mod, na_w_in, j, j, na_caches)
            na_caches = (ck, cv)
            a_p = _ctx_attn(proj)
            a_s = _na_attn(proj, cache_na_k, cache_na_v, na_tbl, j)
            w_out = na_w_out

        x, tok, logits = _outproj(a_p, a_s, w_out, j, x, norm_ffn[layer], mod, wr_all[layer], br_all[layer])

        idx, gates, rank, counts = _router(logits)
        pos_km, tile_expert, tile_valid, tile_slot, tile_next = _routing_tables(idx, rank, counts)
        xs = _dispatch(tok, pos_km)
        ys = _moe_experts(tile_expert, tile_valid, tile_slot, tile_next, xs, moe_w1, b1_split, moe_w2, b2_all,
                          perm, layer)
        y_km = _gather_rows(ys, pos_km.reshape(-1))
        x = _combine(y_km, gates, x, mod, norm_final, final=(layer == DEPTH - 1))

    y_prompt = x[0].reshape(BATCH, SEQ, D_MODEL)
    y_sample = x[1].reshape(DEC_BATCH, DEC_SEQ, D_MODEL)
    return (y_prompt, y_sample, ret_states[0], ret_states[1],
            jnp.transpose(na_caches[0], (0, 1, 4, 2, 3)), jnp.transpose(na_caches[1], (0, 1, 4, 2, 3)))
```

```python
import functools

import numpy as np
import jax
import jax.numpy as jnp
from jax import lax
from jax.experimental import pallas as pl
from jax.experimental.pallas import tpu as pltpu
from jax.experimental.pallas import tpu_sc as plsc

F32 = jnp.float32
BF16 = jnp.bfloat16
I32 = jnp.int32
U32 = jnp.uint32

D_MODEL = 1024
BATCH = 32
SEQ = 256
DEPTH = 4
DEC_BATCH = 2
DEC_SEQ = 1024
PAST_LEN = 512
GRID_W = 64
EPS = 1e-6
ROPE_BASE = 10000.0
RET_HEADS = 4
RET_DK = D_MODEL // RET_HEADS
RET_DV = 2 * RET_DK
RET_QK_W = RET_HEADS * RET_DK
RET_V_W = RET_HEADS * RET_DV
NA_HEADS = 16
NA_HD = D_MODEL // NA_HEADS
NA_KH = 8
NA_KW = 16
N_EXPERTS = 32
TOP_K = 4
D_EXPERT = D_MODEL
SWIGLU_LIMIT = 7.0
SWIGLU_ALPHA = 1.702
N_RET_LAYERS = (DEPTH + 1) // 2
N_NA_LAYERS = DEPTH // 2

N_PROMPT = BATCH * SEQ
N_SAMPLE = DEC_BATCH * DEC_SEQ
N_TOK = N_PROMPT + N_SAMPLE
N_GROUPS = 1 + DEC_BATCH
MOD_ROWS = 8
LANES = 128

TM = 256
RET_CHUNK = 256
MOE_TM = 256
MOE_ROWS = N_TOK * TOP_K + N_EXPERTS * MOE_TM
MOE_TILES = MOE_ROWS // MOE_TM
MOE_TPS = 2
ROUTER_TM = 512
PERM_W = 256
VMEM_LIMIT = 56 * 1024 * 1024


def _cparams(n_axes, vmem=VMEM_LIMIT):
    return pltpu.CompilerParams(dimension_semantics=("arbitrary",) * n_axes, vmem_limit_bytes=vmem)


def _group_of_tile(i, tm):
    return jnp.maximum((i * tm) // DEC_SEQ - (N_PROMPT // DEC_SEQ - 1), 0)


def _sigmoid(x):
    return 1.0 / (1.0 + jnp.exp(-x))


def _silu(x):
    return x * _sigmoid(x)


def _log_sigmoid(x):
    return jnp.minimum(x, 0.0) - jnp.log(1.0 + jnp.exp(-jnp.abs(x)))


def _norm_mod(x, g, shift, scale):
    y = x * lax.rsqrt(jnp.mean(x * x, axis=-1, keepdims=True) + EPS)
    return (y * g) * (1.0 + scale) + shift


def _dot(a, b):
    return jnp.dot(a, b, preferred_element_type=F32)


HALF = D_MODEL // 2
HI_MASK = 0xFFFF0000


def _pack_rows(x):
    lo = pltpu.bitcast(x[:, :HALF].astype(BF16).astype(F32), U32)
    hi = pltpu.bitcast(x[:, HALF:].astype(BF16).astype(F32), U32)
    return (lo >> 16) | (hi & jnp.uint32(HI_MASK))


def _unpack_rows(p):
    return pltpu.bitcast(p << 16, F32), pltpu.bitcast(p & jnp.uint32(HI_MASK), F32)


def _dot_nt(a, b):
    return lax.dot_general(a, b, (((1,), (1,)), ((), ())), preferred_element_type=F32)


def _dot_tn(a, b):
    return lax.dot_general(a, b, (((0,), (0,)), ((), ())), preferred_element_type=F32)


ADA_TN = 1536


def _adaln_body(cond_ref, w_ref, b_ref, o_ref):
    s = _silu(cond_ref[...]).astype(BF16)
    o_ref[...] = _dot(s, w_ref[...].astype(BF16)) + b_ref[...]


def _adaln(cond, ada_w, ada_b):
    n_out = 6 * D_MODEL
    return pl.pallas_call(
        _adaln_body,
        grid=(DEPTH, n_out // ADA_TN),
        in_specs=[
            pl.BlockSpec((MOD_ROWS, D_MODEL), lambda l, j: (0, 0)),
            pl.BlockSpec((None, D_MODEL, ADA_TN), lambda l, j: (l, 0, j)),
            pl.BlockSpec((None, 1, ADA_TN), lambda l, j: (l, 0, j)),
        ],
        out_specs=pl.BlockSpec((None, MOD_ROWS, ADA_TN), lambda l, j: (l, 0, j)),
        out_shape=jax.ShapeDtypeStruct((DEPTH, MOD_ROWS, n_out), F32),
        compiler_params=_cparams(2),
        name="adaln",
    )(cond, ada_w, ada_b.reshape(DEPTH, 1, n_out))


CAST_TR = 256


def _cast_body(w_ref, o_ref):
    o_ref[...] = w_ref[...].astype(BF16)


def _cast_layer_bf16(w, j):
    _, r, c = w.shape
    return pl.pallas_call(
        _cast_body,
        grid=(r // CAST_TR,),
        in_specs=[pl.BlockSpec((None, CAST_TR, c), lambda i: (j, i, 0))],
        out_specs=pl.BlockSpec((CAST_TR, c), lambda i: (i, 0)),
        out_shape=jax.ShapeDtypeStruct((r, c), BF16),
        compiler_params=_cparams(1),
        name="cast_bf16",
    )(w)


def _x_inputs(x):
    n_pt = N_PROMPT // TM
    xa, xb, off = (x[0], x[1], 0) if isinstance(x, tuple) else (x, x, n_pt)
    specs = [pl.BlockSpec((TM, D_MODEL), lambda i: (jnp.minimum(i, n_pt - 1), 0)),
             pl.BlockSpec((TM, D_MODEL), lambda i: (off + jnp.maximum(i - n_pt, 0), 0))]
    return specs, [xa, xb]


def _x_tile(xa_ref, xb_ref):
    return jnp.where(pl.program_id(0) < N_PROMPT // TM, xa_ref[...], xb_ref[...])


def _weight_input(w, layer):
    if layer is None:
        return pl.BlockSpec(w.shape, lambda i: (0, 0)), []
    _, r, c = w.shape
    spec = pl.BlockSpec((None, r, c), lambda i: (layer, 0, 0), pipeline_mode=pl.Buffered(1))
    return spec, [pltpu.VMEM((r, c), BF16)]


def _weight_ref(w_ref, scratch):
    if not scratch:
        return w_ref

    @pl.when(pl.program_id(0) == 0)
    def _():
        scratch[0][...] = w_ref[...].astype(BF16)

    return scratch[0]


def _inproj_body(xa_ref, xb_ref, g_ref, mod_ref, w_ref, *rest, cache_layer, first_cache, n_scratch):
    scratch = rest[len(rest) - n_scratch:]
    rest = rest[:len(rest) - n_scratch]
    o_ref = rest[-3] if cache_layer is not None else rest[-1]
    i = pl.program_id(0)
    w = _weight_ref(w_ref, scratch)
    h = _norm_mod(_x_tile(xa_ref, xb_ref), g_ref[...], mod_ref[0:1, :], mod_ref[1:2, :]).astype(BF16)
    for c in range(o_ref.shape[1] // D_MODEL):
        sl = slice(c * D_MODEL, (c + 1) * D_MODEL)
        r = _dot(h, w[:, sl])
        o_ref[:, sl] = r.astype(o_ref.dtype)
        if cache_layer is not None and c >= 1:
            cache_ref = rest[-3 + c]

            @pl.when(i < BATCH)
            def _(r=r, cache_ref=cache_ref):
                r_t = r.T.reshape(NA_HEADS, NA_HD, SEQ)
                if first_cache:
                    for l in range(N_NA_LAYERS):
                        cache_ref[l] = r_t if l == cache_layer else jnp.zeros_like(r_t)
                else:
                    cache_ref[...] = r_t


def _inproj(x, g, mod, w, w_layer=None, cache_layer=None, prev_caches=None, out_dtype=F32):
    n_out = w.shape[-1]
    x_specs, x_args = _x_inputs(x)
    w_spec, scratch = _weight_input(w, w_layer)
    in_specs = x_specs + [
        pl.BlockSpec((1, D_MODEL), lambda i: (0, 0)),
        pl.BlockSpec((None, MOD_ROWS, D_MODEL), lambda i: (_group_of_tile(i, TM), 0, 0)),
        w_spec,
    ]
    out_specs = [pl.BlockSpec((TM, n_out), lambda i: (i, 0))]
    out_shape = [jax.ShapeDtypeStruct((N_TOK, n_out), out_dtype)]
    args = x_args + [g.reshape(1, D_MODEL), mod, w]
    aliases = {}
    first_cache = prev_caches is None
    if cache_layer is not None:
        assert TM == SEQ and n_out == 3 * D_MODEL
        cache = jax.ShapeDtypeStruct((BATCH, N_NA_LAYERS, NA_HEADS, NA_HD, SEQ), F32)
        if first_cache:
            spec = pl.BlockSpec((None, N_NA_LAYERS, NA_HEADS, NA_HD, SEQ),
                                lambda i: (jnp.minimum(i, BATCH - 1), 0, 0, 0, 0))
        else:
            spec = pl.BlockSpec((None, None, NA_HEADS, NA_HD, SEQ),
                                lambda i: (jnp.minimum(i, BATCH - 1), cache_layer, 0, 0, 0))
            in_specs += [pl.BlockSpec(memory_space=pl.ANY)] * 2
            aliases = {len(args): 1, len(args) + 1: 2}
            args += list(prev_caches)
        out_specs += [spec, spec]
        out_shape += [cache, cache]
    outs = pl.pallas_call(
        functools.partial(_inproj_body, cache_layer=cache_layer, first_cache=first_cache, n_scratch=len(scratch)),
        grid=(N_TOK // TM,),
        in_specs=in_specs,
        out_specs=out_specs,
        out_shape=out_shape,
        scratch_shapes=scratch,
        input_output_aliases=aliases,
        compiler_params=_cparams(1),
        name="inproj",
    )(*args)
    return outs[0] if cache_layer is None else outs


def _decay_terms(dec_ref, h, c):
    lgf = _log_sigmoid(jnp.zeros((c, c), F32) + dec_ref[0, h])
    lgb = _log_sigmoid(jnp.zeros((c, c), F32) + dec_ref[1, h])
    row = lax.broadcasted_iota(I32, (c, c), 0)
    col = lax.broadcasted_iota(I32, (c, c), 1)
    diff = (row - col).astype(F32)
    lower = diff >= 0
    upper = diff <= 0
    mask = (jnp.where(lower, jnp.exp(jnp.where(lower, diff, 0.0) * lgf), 0.0)
            + jnp.where(upper, jnp.exp(jnp.where(upper, -diff, 0.0) * lgb), 0.0))
    pos = lax.broadcasted_iota(I32, (c, 1), 0).astype(F32)
    lgf1 = _log_sigmoid(jnp.zeros((c, 1), F32) + dec_ref[0, h])
    lgb1 = _log_sigmoid(jnp.zeros((c, 1), F32) + dec_ref[1, h])
    lgf0 = _log_sigmoid(jnp.zeros((1, 1), F32) + dec_ref[0, h])
    lgb0 = _log_sigmoid(jnp.zeros((1, 1), F32) + dec_ref[1, h])
    terms = dict(
        mask=mask,
        q_f=jnp.exp((pos + 1.0) * lgf1), k_f=jnp.exp((c - 1.0 - pos) * lgf1), c_f=jnp.exp(c * lgf0),
        q_b=jnp.exp((c - pos) * lgb1), k_b=jnp.exp(pos * lgb1), c_b=jnp.exp(c * lgb0),
    )
    return terms


def _group_norm_gate(o, g):
    o = o * lax.rsqrt(jnp.mean(o * o, axis=-1, keepdims=True) + EPS)
    return (_silu(g) * o).astype(BF16)


def _ret_prompt_body(dec_ref, q_ref, k_ref, v_ref, g_ref, *rest, layer, first):
    o_ref, sf_ref, sb_ref = rest[-3:]
    for h in range(RET_HEADS):
        t = _decay_terms(dec_ref, h, SEQ)
        qk_cols = slice(h * RET_DK, (h + 1) * RET_DK)
        v_cols = slice(h * RET_DV, (h + 1) * RET_DV)
        q = q_ref[:, qk_cols].astype(F32)
        k = k_ref[:, qk_cols].astype(F32) * (RET_DK ** -0.5)
        v = v_ref[:, v_cols].astype(BF16)
        scores = _dot_nt(q.astype(BF16), k.astype(BF16)) * t["mask"]
        o = _dot(scores.astype(BF16), v)
        o_ref[:, v_cols] = _group_norm_gate(o, g_ref[:, v_cols].astype(F32))
        for ref, k_dec in ((sf_ref, t["k_f"]), (sb_ref, t["k_b"])):
            s = _dot_tn((k * k_dec).astype(BF16), v)
            if first:
                for l in range(N_RET_LAYERS):
                    ref[l, h] = s if l == layer else jnp.zeros_like(s)
            else:
                ref[h] = s


def _ret_prompt(proj, decay, layer, prev_states=None):
    assert 2 * RET_QK_W == RET_V_W
    state = jax.ShapeDtypeStruct((BATCH, N_RET_LAYERS, RET_HEADS, RET_DK, RET_DV), F32)
    in_specs = [
        pl.BlockSpec(memory_space=pltpu.SMEM),
        pl.BlockSpec((SEQ, RET_QK_W), lambda b: (b, 0)),
        pl.BlockSpec((SEQ, RET_QK_W), lambda b: (b, 1)),
        pl.BlockSpec((SEQ, RET_V_W), lambda b: (b, 1)),
        pl.BlockSpec((SEQ, RET_V_W), lambda b: (b, 2)),
    ]
    args = [decay, proj, proj, proj, proj]
    first = prev_states is None
    aliases = {}
    if first:
        s_spec = pl.BlockSpec((None, N_RET_LAYERS, RET_HEADS, RET_DK, RET_DV), lambda b: (b, 0, 0, 0, 0))
    else:
        s_spec = pl.BlockSpec((None, None, RET_HEADS, RET_DK, RET_DV), lambda b: (b, layer, 0, 0, 0))
        in_specs += [pl.BlockSpec(memory_space=pl.ANY)] * 2
        aliases = {len(args): 1, len(args) + 1: 2}
        args += list(prev_states)
    return pl.pallas_call(
        functools.partial(_ret_prompt_body, layer=layer, first=first),
        grid=(BATCH,),
        in_specs=in_specs,
        out_specs=[pl.BlockSpec((SEQ, RET_V_W), lambda b: (b, 0)), s_spec, s_spec],
        out_shape=[jax.ShapeDtypeStruct((N_PROMPT, RET_V_W), BF16), state, state],
        input_output_aliases=aliases,
        compiler_params=_cparams(1),
        name="ret_prompt",
    )(*args)


def _rope(x, cos, sin_signed):
    n = x.shape[-1]
    lane = lax.broadcasted_iota(I32, x.shape, 1)
    swapped = jnp.where(lane % 2 == 0, pltpu.roll(x, n - 1, 1), pltpu.roll(x, 1, 1))
    return x * cos + swapped * sin_signed


def _ret_sample_body(dec_ref, q_ref, k_ref, v_ref, g_ref, cos_ref, sin_ref, s0f_ref, s0b_ref, o_ref,
                     sf_scr, sb_scr):
    h = pl.program_id(1)
    c = RET_CHUNK
    n_chunks = DEC_SEQ // c
    t = _decay_terms(dec_ref, h, c)

    def chunk(ref, i):
        return ref[i * c:(i + 1) * c, :].astype(F32)

    def qk(i):
        q = _rope(chunk(q_ref, i), chunk(cos_ref, i), chunk(sin_ref, i))
        k = _rope(chunk(k_ref, i) * (RET_DK ** -0.5), chunk(cos_ref, i), chunk(sin_ref, i))
        return q, k

    s = s0f_ref[...]
    for i in range(n_chunks):
        sf_scr[i] = s
        if i + 1 < n_chunks:
            _, k = qk(i)
            s = s * t["c_f"] + _dot_tn((k * t["k_f"]).astype(BF16), chunk(v_ref, i).astype(BF16))
    s = s0b_ref[...]
    for i in reversed(range(n_chunks)):
        sb_scr[i] = s
        if i > 0:
            _, k = qk(i)
            s = s * t["c_b"] + _dot_tn((k * t["k_b"]).astype(BF16), chunk(v_ref, i).astype(BF16))

    for i in range(n_chunks):
        q, k = qk(i)
        scores = _dot_nt(q.astype(BF16), k.astype(BF16)) * t["mask"]
        o = _dot(scores.astype(BF16), chunk(v_ref, i).astype(BF16))
        o = o + _dot((q * t["q_f"]).astype(BF16), sf_scr[i].astype(BF16))
        o = o + _dot((q * t["q_b"]).astype(BF16), sb_scr[i].astype(BF16))
        o_ref[i * c:(i + 1) * c, :] = _group_norm_gate(o, chunk(g_ref, i))


def _ret_sample(proj, decay, cos, sin_signed, s0f, s0b, j):
    kq = RET_QK_W // RET_DK
    kv = 2 * RET_QK_W // RET_DV
    kg = kv + RET_HEADS
    rb = N_PROMPT // DEC_SEQ
    n_chunks = DEC_SEQ // RET_CHUNK
    return pl.pallas_call(
        _ret_sample_body,
        grid=(DEC_BATCH, RET_HEADS),
        in_specs=[
            pl.BlockSpec(memory_space=pltpu.SMEM),
            pl.BlockSpec((DEC_SEQ, RET_DK), lambda b, h: (rb + b, h)),
            pl.BlockSpec((DEC_SEQ, RET_DK), lambda b, h: (rb + b, kq + h)),
            pl.BlockSpec((DEC_SEQ, RET_DV), lambda b, h: (rb + b, kv + h)),
            pl.BlockSpec((DEC_SEQ, RET_DV), lambda b, h: (rb + b, kg + h)),
            pl.BlockSpec((DEC_SEQ, RET_DK), lambda b, h: (0, 0)),
            pl.BlockSpec((DEC_SEQ, RET_DK), lambda b, h: (0, 0)),
            pl.BlockSpec((None, None, None, RET_DK, RET_DV), lambda b, h: (b, j, h, 0, 0)),
            pl.BlockSpec((None, None, None, RET_DK, RET_DV), lambda b, h: (b, j, h, 0, 0)),
        ],
        out_specs=pl.BlockSpec((DEC_SEQ, RET_DV), lambda b, h: (b, h)),
        out_shape=jax.ShapeDtypeStruct((N_SAMPLE, RET_V_W), BF16),
        scratch_shapes=[pltpu.VMEM((n_chunks, RET_DK, RET_DV), F32),
                        pltpu.VMEM((n_chunks, RET_DK, RET_DV), F32)],
        compiler_params=_cparams(2),
        name="ret_sample",
    )(decay, proj, proj, proj, proj, cos, sin_signed, s0f, s0b)


NA_HPB = LANES // NA_HD


def _head_lanes(rows, hh):
    lane = lax.broadcasted_iota(I32, (rows, LANES), 1)
    return jnp.logical_and(lane >= hh * NA_HD, lane < (hh + 1) * NA_HD)


def _with_ones(v):
    return jnp.concatenate([v, jnp.ones(v.shape, v.dtype)], axis=-1)


def _ctx_attn_body(q_ref, k_ref, v_ref, o_ref):
    outs = []
    for p in range(NA_HEADS // NA_HPB):
        sl = slice(p * LANES, (p + 1) * LANES)
        q2 = q_ref[:, sl] * (NA_HD ** -0.5)
        k2 = k_ref[:, sl].astype(BF16)
        v2 = _with_ones(v_ref[:, sl].astype(BF16))
        o_pair = None
        for hh in range(NA_HPB):
            mine = _head_lanes(SEQ, hh)
            s = _dot_nt(jnp.where(mine, q2, 0.0).astype(BF16), k2)
            e = jnp.exp(s - s.max(axis=-1, keepdims=True)).astype(BF16)
            acc = _dot(e, v2)
            o_h = acc[:, :LANES] / acc[:, LANES:]
            o_pair = o_h if o_pair is None else jnp.where(mine, o_h, o_pair)
        outs.append(o_pair)
    o_ref[...] = jnp.concatenate(outs, axis=-1).astype(BF16)


def _ctx_attn(proj):
    return pl.pallas_call(
        _ctx_attn_body,
        grid=(BATCH,),
        in_specs=[
            pl.BlockSpec((SEQ, D_MODEL), lambda b: (b, 0)),
            pl.BlockSpec((SEQ, D_MODEL), lambda b: (b, 1)),
            pl.BlockSpec((SEQ, D_MODEL), lambda b: (b, 2)),
        ],
        out_specs=pl.BlockSpec((SEQ, D_MODEL), lambda b: (b, 0)),
        out_shape=jax.ShapeDtypeStruct((N_PROMPT, D_MODEL), BF16),
        compiler_params=_cparams(1),
        name="ctx_attn",
    )(proj, proj, proj)


NA_QT = 1024
NA_WIN = NA_KH * GRID_W


def _na_attn_body(q_ref, k_ref, v_ref, ck_ref, cv_ref, tbl_ref, o_ref):
    t = pl.program_id(1)
    rows_per_tile = NA_QT // GRID_W
    n_rows = DEC_SEQ // GRID_W
    q2 = q_ref[...] * (NA_HD ** -0.5)
    q = [jnp.where(_head_lanes(NA_QT, hh), q2, 0.0).astype(BF16) for hh in range(NA_HPB)]
    s_ctx = _dot_nt(jnp.concatenate(q, axis=0), ck_ref[...].astype(BF16))
    acc_win = []
    e_ctx = []
    for u in range(rows_per_tile):
        qr = t * rows_per_tile + u
        rs = jnp.clip(qr - NA_KH // 2, 0, n_rows - NA_KH)
        i0 = rs - qr + (NA_KH - 1)
        start = pl.multiple_of(rs * GRID_W, GRID_W)
        kw = k_ref[pl.ds(start, NA_WIN), :].astype(BF16)
        vw = _with_ones(v_ref[pl.ds(start, NA_WIN), :].astype(BF16))
        rows = slice(u * GRID_W, (u + 1) * GRID_W)
        bias = jnp.concatenate(
            [jnp.concatenate([tbl_ref[hh, i0 + 2 * m] for m in range(NA_KH // 2)], axis=-1) for hh in range(NA_HPB)],
            axis=0)
        sw = _dot_nt(jnp.concatenate([q[hh][rows] for hh in range(NA_HPB)], axis=0), kw) + bias
        sc = jnp.concatenate([s_ctx[hh * NA_QT + u * GRID_W:hh * NA_QT + (u + 1) * GRID_W] for hh in range(NA_HPB)],
                             axis=0)
        m = jnp.maximum(sw.max(axis=-1, keepdims=True), sc.max(axis=-1, keepdims=True))
        acc_win.append(_dot(jnp.exp(sw - m).astype(BF16), vw))
        e_ctx.append(jnp.exp(sc - m).astype(BF16))
    acc = jnp.concatenate(acc_win, axis=0) + _dot(jnp.concatenate(e_ctx, axis=0), _with_ones(cv_ref[...].astype(BF16)))
    o = acc[:, :LANES] / acc[:, LANES:]
    outs = []
    for u in range(rows_per_tile):
        base = u * NA_HPB * GRID_W
        o_u = o[base:base + GRID_W]
        for hh in range(1, NA_HPB):
            o_u = jnp.where(_head_lanes(GRID_W, hh), o[base + hh * GRID_W:base + (hh + 1) * GRID_W], o_u)
        outs.append(o_u)
    o_ref[...] = jnp.concatenate(outs, axis=0).astype(BF16)


def _na_attn(proj, cache_k, cache_v, tbl, j):
    n_qt = DEC_SEQ // NA_QT
    rb_q = N_PROMPT // NA_QT
    rb_kv = N_PROMPT // DEC_SEQ
    cb = D_MODEL // LANES
    ck = cache_k.reshape(DEC_BATCH, -1, PAST_LEN, D_MODEL)
    cv = cache_v.reshape(DEC_BATCH, -1, PAST_LEN, D_MODEL)
    return pl.pallas_call(
        _na_attn_body,
        grid=(cb, n_qt, DEC_BATCH),
        in_specs=[
            pl.BlockSpec((NA_QT, LANES), lambda p, t, b: (rb_q + b * n_qt + t, p)),
            pl.BlockSpec((DEC_SEQ, LANES), lambda p, t, b: (rb_kv + b, cb + p)),
            pl.BlockSpec((DEC_SEQ, LANES), lambda p, t, b: (rb_kv + b, 2 * cb + p)),
            pl.BlockSpec((None, None, PAST_LEN, LANES), lambda p, t, b: (b, j, 0, p)),
            pl.BlockSpec((None, None, PAST_LEN, LANES), lambda p, t, b: (b, j, 0, p)),
            pl.BlockSpec((None, NA_HPB, 2 * NA_KH - 2, GRID_W, 2 * GRID_W), lambda p, t, b: (j, p, 0, 0, 0)),
        ],
        out_specs=pl.BlockSpec((NA_QT, LANES), lambda p, t, b: (b * n_qt + t, p)),
        out_shape=jax.ShapeDtypeStruct((N_SAMPLE, D_MODEL), BF16),
        compiler_params=_cparams(3),
        name="na_attn",
    )(proj, proj, proj, ck, cv, tbl)


def _na_bias_table(rpb):
    c = np.arange(GRID_W)
    cs = np.clip(c - NA_KW // 2, 0, GRID_W - NA_KW)
    col_valid = (c[None, :] >= cs[:, None]) & (c[None, :] < cs[:, None] + NA_KW)
    col_off = np.clip(c[None, :] - c[:, None] + NA_KW - 1, 0, 2 * NA_KW - 2)
    sel_c = (col_off[..., None] == np.arange(2 * NA_KW - 1)).astype(np.float32)
    t = jnp.einsum("lhij,qkj->lhiqk", rpb, sel_c, precision=lax.Precision.HIGHEST)
    t = jnp.where(col_valid, t, -jnp.inf)
    return jnp.concatenate([t[:, :, :-1], t[:, :, 1:]], axis=-1)


def _outproj_body(ap_ref, as_ref, w_ref, xa_ref, xb_ref, g_ref, mod_ref, wr_ref, br_ref, xo_ref, tok_ref, lg_ref,
                  *scratch):
    i = pl.program_id(0)
    a = jnp.where(i < N_PROMPT // TM, ap_ref[...], as_ref[...])
    x = _x_tile(xa_ref, xb_ref) + mod_ref[2:3, :] * _dot(a, _weight_ref(w_ref, scratch)[...])
    xo_ref[...] = x
    tok = _norm_mod(x, g_ref[...], mod_ref[3:4, :], mod_ref[4:5, :])
    tok_ref[...] = _pack_rows(tok)
    hi = tok.astype(BF16)
    lo = (tok - hi.astype(F32)).astype(BF16)
    wr = wr_ref[...]
    wr_hi = wr.astype(BF16)
    wr_lo = (wr - wr_hi.astype(F32)).astype(BF16)
    hi_terms = _dot(hi, jnp.concatenate([wr_hi, wr_lo], axis=-1))
    lg_ref[...] = hi_terms[:, :LANES] + _dot(lo, wr_hi) + hi_terms[:, LANES:] + br_ref[...]


def _outproj(a_prompt, a_sample, w, w_layer, x, g, mod, wr, br):
    k_in = a_prompt.shape[1]
    n_pt = N_PROMPT // TM
    x_specs, x_args = _x_inputs(x)
    w_spec, scratch = _weight_input(w, w_layer)
    return pl.pallas_call(
        _outproj_body,
        grid=(N_TOK // TM,),
        in_specs=[
            pl.BlockSpec((TM, k_in), lambda i: (jnp.minimum(i, n_pt - 1), 0)),
            pl.BlockSpec((TM, k_in), lambda i: (jnp.maximum(i - n_pt, 0), 0)),
            w_spec,
            *x_specs,
            pl.BlockSpec((1, D_MODEL), lambda i: (0, 0)),
            pl.BlockSpec((None, MOD_ROWS, D_MODEL), lambda i: (_group_of_tile(i, TM), 0, 0)),
            pl.BlockSpec((D_MODEL, LANES), lambda i: (0, 0)),
            pl.BlockSpec((1, LANES), lambda i: (0, 0)),
        ],
        out_specs=[
            pl.BlockSpec((TM, D_MODEL), lambda i: (i, 0)),
            pl.BlockSpec((TM, HALF), lambda i: (i, 0)),
            pl.BlockSpec((TM, LANES), lambda i: (i, 0)),
        ],
        out_shape=[
            jax.ShapeDtypeStruct((N_TOK, D_MODEL), F32),
            jax.ShapeDtypeStruct((N_TOK, HALF), U32),
            jax.ShapeDtypeStruct((N_TOK, LANES), F32),
        ],
        scratch_shapes=scratch,
        compiler_params=_cparams(1),
        name="outproj",
    )(a_prompt, a_sample, w, *x_args, g.reshape(1, D_MODEL), mod, wr, br)


def _router_body(lg_ref, idx_ref, gate_ref, rank_ref, cnt_ref, carry):
    i = pl.program_id(0)

    @pl.when(i == 0)
    def _():
        carry[...] = jnp.zeros_like(carry)

    lane = lax.broadcasted_iota(I32, (ROUTER_TM, LANES), 1)
    l = jnp.where(lane < N_EXPERTS, lg_ref[...], -jnp.inf)
    vals, idxs, hots = [], [], []
    for _ in range(TOP_K):
        m = l.max(axis=-1, keepdims=True)
        idx = jnp.where(l == m, lane, LANES).min(axis=-1, keepdims=True)
        hot = lane == idx
        l = jnp.where(hot, -jnp.inf, l)
        vals.append(m)
        idxs.append(idx)
        hots.append(hot)
    es = [jnp.exp(v - vals[0]) for v in vals]
    denom = es[0] + es[1] + es[2] + es[3]
    hot_all = (hots[0] | hots[1] | hots[2] | hots[3]).astype(F32)
    r = lax.broadcasted_iota(I32, (ROUTER_TM, ROUTER_TM), 0)
    c = lax.broadcasted_iota(I32, (ROUTER_TM, ROUTER_TM), 1)
    before = _dot((c < r).astype(BF16), hot_all.astype(BF16)) + carry[...]
    idx_out = jnp.zeros((ROUTER_TM, LANES), I32)
    gate_out = jnp.zeros((ROUTER_TM, LANES), F32)
    rank_out = jnp.zeros((ROUTER_TM, LANES), I32)
    for k in range(TOP_K):
        rank_k = jnp.where(hots[k], before, 0.0).sum(axis=-1, keepdims=True).astype(I32)
        idx_out = jnp.where(lane == k, idxs[k], idx_out)
        gate_out = jnp.where(lane == k, es[k] / denom, gate_out)
        rank_out = jnp.where(lane == k, rank_k, rank_out)
    idx_ref[...] = idx_out
    gate_ref[...] = gate_out
    rank_ref[...] = rank_out
    carry[...] = carry[...] + hot_all.sum(axis=0, keepdims=True)
    cnt_ref[...] = carry[...].astype(I32)


def _router(logits):
    spec = pl.BlockSpec((ROUTER_TM, LANES), lambda i: (i, 0))
    return pl.pallas_call(
        _router_body,
        grid=(N_TOK // ROUTER_TM,),
        in_specs=[spec],
        out_specs=[spec, spec, spec, pl.BlockSpec((1, LANES), lambda i: (0, 0))],
        out_shape=[
            jax.ShapeDtypeStruct((N_TOK, LANES), I32),
            jax.ShapeDtypeStruct((N_TOK, LANES), F32),
            jax.ShapeDtypeStruct((N_TOK, LANES), I32),
            jax.ShapeDtypeStruct((1, LANES), I32),
        ],
        scratch_shapes=[pltpu.VMEM((1, LANES), F32)],
        compiler_params=_cparams(1),
        name="router",
    )(logits)


SC_CORES = 2
SC_SUBCORES = 16
SC_WORKERS = SC_CORES * SC_SUBCORES
SC_CH = 32
SC_SCATTER_BUFS = 5
SC_GATHER_BUFS = 4


def _sc_mesh():
    return plsc.VectorSubcoreMesh(core_axis_name="c", subcore_axis_name="s",
                                  num_cores=SC_CORES, num_subcores=SC_SUBCORES)


def _sc_worker():
    return lax.axis_index("s") * SC_CORES + lax.axis_index("c")


def _dispatch(tok, pos_km):
    n, d = tok.shape
    n_ch = n // (SC_WORKERS * SC_CH)
    nb = SC_SCATTER_BUFS
    assert n_ch * SC_WORKERS * SC_CH == n and n_ch % nb == 0

    @functools.partial(
        pl.kernel, mesh=_sc_mesh(), out_type=jax.ShapeDtypeStruct((MOE_ROWS, d), tok.dtype),
        scratch_types=[pltpu.VMEM((n_ch, TOP_K, SC_CH), I32), pltpu.VMEM((nb, SC_CH, d), tok.dtype),
                       pltpu.SemaphoreType.DMA((nb,)), pltpu.SemaphoreType.DMA((nb,))])
    def scatter_rows(tok_hbm, pos_hbm, out_hbm, idx_v, rows_v, load_sem, scat_sem):
        wid = _sc_worker()
        base = wid * (n_ch * SC_CH)
        pltpu.sync_copy(pos_hbm.at[wid], idx_v)

        def load(g, slot):
            rows = pl.ds(pl.multiple_of(base + g * SC_CH, 8), SC_CH)
            return pltpu.make_async_copy(tok_hbm.at[rows], rows_v.at[slot], load_sem.at[slot])

        def scatter(g, k, slot):
            return pltpu.make_async_copy(rows_v.at[slot], out_hbm.at[idx_v.at[g, k]], scat_sem.at[slot])

        for s in range(nb - 1):
            load(s, s).start()

        @pl.loop(0, n_ch, step=nb)
        def _(g0):
            for s in range(nb):
                g = g0 + s
                reuse = (s + nb - 1) % nb

                @pl.when(g >= 1)
                def _():
                    for k in range(TOP_K):
                        scatter(g - 1, k, reuse).wait()

                @pl.when(g + nb - 1 < n_ch)
                def _():
                    load(g + nb - 1, reuse).start()

                load(g, s).wait()
                for k in range(TOP_K):
                    scatter(g, k, s).start()

        for k in range(TOP_K):
            scatter(n_ch - 1, k, (n_ch - 1) % nb).wait()

    pos = pos_km.reshape(TOP_K, SC_WORKERS, n_ch, SC_CH).transpose(1, 2, 0, 3)
    return scatter_rows(tok, pos)


def _gather_rows(table, idx):
    b, d = idx.shape[0], table.shape[1]
    n_ch = b // (SC_WORKERS * SC_CH)
    nb = SC_GATHER_BUFS
    assert n_ch * SC_WORKERS * SC_CH == b and n_ch % nb == 0

    @functools.partial(
        pl.kernel, mesh=_sc_mesh(), out_type=jax.ShapeDtypeStruct((b, d), table.dtype),
        scratch_types=[pltpu.VMEM((n_ch, SC_CH), I32), pltpu.VMEM((nb, SC_CH, d), table.dtype),
                       pltpu.SemaphoreType.DMA((nb,)), pltpu.SemaphoreType.DMA((nb,))])
    def gather_rows(table_hbm, idx_hbm, out_hbm, idx_v, rows_v, gather_sem, write_sem):
        wid = _sc_worker()
        base = wid * (n_ch * SC_CH)
        pltpu.sync_copy(idx_hbm.at[wid], idx_v)

        def gather(g, slot):
            return pltpu.make_async_copy(table_hbm.at[idx_v.at[g]], rows_v.at[slot], gather_sem.at[slot])

        def write_out(g, slot):
            rows = pl.ds(pl.multiple_of(base + g * SC_CH, 8), SC_CH)
            return pltpu.make_async_copy(rows_v.at[slot], out_hbm.at[rows], write_sem.at[slot])

        for s in range(nb - 1):
            gather(s, s).start()

        @pl.loop(0, n_ch, step=nb)
        def _(g0):
            for s in range(nb):
                g = g0 + s
                reuse = (s + nb - 1) % nb

                @pl.when(g >= 1)
                def _():
                    write_out(g - 1, reuse).wait()

                @pl.when(g + nb - 1 < n_ch)
                def _():
                    gather(g + nb - 1, reuse).start()

                gather(g, s).wait()
                write_out(g, s).start()

        write_out(n_ch - 1, (n_ch - 1) % nb).wait()

    return gather_rows(table, idx.reshape(SC_WORKERS, n_ch, SC_CH))


def _moe_body(te_ref, tv_ref, slot_ref, nxt_ref, x_ref, w1_hbm, w2_hbm, perm_ref, *rest, layer):
    b_refs = rest[:2 * MOE_TPS]
    o_ref, w1_buf, w2_buf, w1_scr, w2_scr, sem = rest[2 * MOE_TPS:]
    for sub in range(MOE_TPS):
        rows = slice(sub * MOE_TM, (sub + 1) * MOE_TM)
        _moe_tile(te_ref, tv_ref, slot_ref, nxt_ref, pl.program_id(0) * MOE_TPS + sub, x_ref.at[rows], w1_hbm,
                  b_refs[2 * sub], w2_hbm, b_refs[2 * sub + 1], perm_ref, o_ref.at[rows],
                  w1_buf, w2_buf, w1_scr, w2_scr, sem, layer)


def _moe_tile(te_ref, tv_ref, slot_ref, nxt_ref, t, x_ref, w1_hbm, b1_ref, w2_hbm, b2_ref, perm_ref, o_ref,
              w1_buf, w2_buf, w1_scr, w2_scr, sem, layer):
    e = te_ref[t]
    slot = slot_ref[t]
    nxt = nxt_ref[t]
    first = jnp.logical_or(t == 0, e != te_ref[jnp.maximum(t - 1, 0)])

    def fetch(expert, s):
        return (pltpu.make_async_copy(w1_hbm.at[layer, expert], w1_buf.at[s], sem.at[s, 0]),
                pltpu.make_async_copy(w2_hbm.at[layer, expert], w2_buf.at[s], sem.at[s, 1]))

    @pl.when(t == 0)
    def _():
        for cp in fetch(e, slot):
            cp.start()

    @pl.when(jnp.logical_and(tv_ref[t] == 1, first))
    def _():
        for cp in fetch(e, slot):
            cp.wait()

        @pl.when(nxt != e)
        def _():
            for cp in fetch(nxt, 1 - slot):
                cp.start()

        w2_scr[...] = w2_buf[slot].astype(BF16)
        half = PERM_W // 2
        for c in range(2 * D_EXPERT // PERM_W):
            blk = _dot(w1_buf[slot, :, c * PERM_W:(c + 1) * PERM_W].astype(BF16), perm_ref[...]).astype(BF16)
            w1_scr[:, c * half:(c + 1) * half] = blk[:, :half]
            w1_scr[:, D_EXPERT + c * half:D_EXPERT + (c + 1) * half] = blk[:, half:]

    @pl.when(tv_ref[t] == 0)
    def _():
        o_ref[...] = jnp.zeros_like(o_ref)

    @pl.when(tv_ref[t] == 1)
    def _():
        x_lo, x_hi = _unpack_rows(x_ref[...])
        x = jnp.concatenate([x_lo.astype(BF16), x_hi.astype(BF16)], axis=-1)
        hh = _dot(x, w1_scr[...]) + b1_ref[...]
        h_glu = jnp.minimum(hh[:, :D_EXPERT], SWIGLU_LIMIT)
        h_lin = jnp.clip(hh[:, D_EXPERT:], -SWIGLU_LIMIT, SWIGLU_LIMIT)
        a = h_glu * _sigmoid(SWIGLU_ALPHA * h_glu) * (h_lin + 1.0)
        o_ref[...] = _pack_rows(_dot(a.astype(BF16), w2_scr[...]) + b2_ref[...])


def _deinterleave_bias_body(b_ref, perm_ref, o_ref):
    half = PERM_W // 2
    for c in range(2 * D_EXPERT // PERM_W):
        r = b_ref[:, c * PERM_W:(c + 1) * PERM_W]
        acc = jnp.zeros(r.shape, F32)
        for _ in range(3):
            piece = r.astype(BF16)
            acc = acc + _dot(piece, perm_ref[...])
            r = r - piece.astype(F32)
        o_ref[:, c * half:(c + 1) * half] = acc[:, :half]
        o_ref[:, D_EXPERT + c * half:D_EXPERT + (c + 1) * half] = acc[:, half:]


def _deinterleave_bias(b1, perm):
    return pl.pallas_call(
        _deinterleave_bias_body,
        out_shape=jax.ShapeDtypeStruct(b1.shape, F32),
        compiler_params=_cparams(0),
        name="deinterleave_bias",
    )(b1, perm)


def _moe_experts(tile_expert, tile_valid, tile_slot, tile_next, xs, w1, b1_split, w2, b2, perm, layer):
    bias_specs = []
    for sub in range(MOE_TPS):
        bias_specs += [
            pl.BlockSpec((None, None, 1, 2 * D_EXPERT), lambda s, te, *_, sub=sub: (layer, te[s * MOE_TPS + sub], 0, 0)),
            pl.BlockSpec((None, None, 1, D_MODEL), lambda s, te, *_, sub=sub: (layer, te[s * MOE_TPS + sub], 0, 0)),
        ]
    grid_spec = pltpu.PrefetchScalarGridSpec(
        num_scalar_prefetch=4,
        grid=(MOE_TILES // MOE_TPS,),
        in_specs=[
            pl.BlockSpec((MOE_TPS * MOE_TM, HALF), lambda s, *_: (s, 0)),
            pl.BlockSpec(memory_space=pl.ANY),
            pl.BlockSpec(memory_space=pl.ANY),
            pl.BlockSpec((PERM_W, PERM_W), lambda s, *_: (0, 0)),
            *bias_specs,
        ],
        out_specs=pl.BlockSpec((MOE_TPS * MOE_TM, HALF), lambda s, *_: (s, 0)),
        scratch_shapes=[
            pltpu.VMEM((2, D_MODEL, 2 * D_EXPERT), F32),
            pltpu.VMEM((2, D_EXPERT, D_MODEL), F32),
            pltpu.VMEM((D_MODEL, 2 * D_EXPERT), BF16),
            pltpu.VMEM((D_EXPERT, D_MODEL), BF16),
            pltpu.SemaphoreType.DMA((2, 2)),
        ],
    )
    return pl.pallas_call(
        functools.partial(_moe_body, layer=layer),
        grid_spec=grid_spec,
        out_shape=jax.ShapeDtypeStruct((MOE_ROWS, HALF), U32),
        compiler_params=_cparams(1),
        name="moe_experts",
    )(tile_expert, tile_valid, tile_slot, tile_next, xs, w1, w2, perm, *([b1_split, b2] * MOE_TPS))


COMBINE_TM = 512


def _combine_body(*refs, final):
    y_refs = refs[:TOP_K]
    gate_ref, x_ref, mod_ref, g_ref = refs[TOP_K:TOP_K + 4]
    gates = gate_ref[...]
    y_lo, y_hi = (gates[:, 0:1] * h for h in _unpack_rows(y_refs[0][...]))
    for k in range(1, TOP_K):
        lo, hi = _unpack_rows(y_refs[k][...])
        y_lo = y_lo + gates[:, k:k + 1] * lo
        y_hi = y_hi + gates[:, k:k + 1] * hi
    x = x_ref[...] + mod_ref[5:6, :] * jnp.concatenate([y_lo, y_hi], axis=-1)
    if not final:
        refs[-1][...] = x
        return
    x = (x * lax.rsqrt(jnp.mean(x * x, axis=-1, keepdims=True) + EPS)) * g_ref[...]
    prompt_ref, sample_ref = refs[-2:]
    is_prompt = pl.program_id(0) < N_PROMPT // COMBINE_TM

    @pl.when(is_prompt)
    def _():
        prompt_ref[...] = x

    @pl.when(jnp.logical_not(is_prompt))
    def _():
        sample_ref[...] = x


def _combine(y_km, gates, x, mod, g_final, final):
    n_t = N_TOK // COMBINE_TM
    n_pt = N_PROMPT // COMBINE_TM
    y_specs = [pl.BlockSpec((COMBINE_TM, HALF), lambda i, k=k: (k * n_t + i, 0)) for k in range(TOP_K)]
    if final:
        out_specs = [pl.BlockSpec((COMBINE_TM, D_MODEL), lambda i: (jnp.minimum(i, n_pt - 1), 0)),
                     pl.BlockSpec((COMBINE_TM, D_MODEL), lambda i: (jnp.maximum(i - n_pt, 0), 0))]
        out_shape = [jax.ShapeDtypeStruct((N_PROMPT, D_MODEL), F32), jax.ShapeDtypeStruct((N_SAMPLE, D_MODEL), F32)]
    else:
        out_specs = pl.BlockSpec((COMBINE_TM, D_MODEL), lambda i: (i, 0))
        out_shape = jax.ShapeDtypeStruct((N_TOK, D_MODEL), F32)
    return pl.pallas_call(
        functools.partial(_combine_body, final=final),
        grid=(n_t,),
        in_specs=y_specs + [
            pl.BlockSpec((COMBINE_TM, LANES), lambda i: (i, 0)),
            pl.BlockSpec((COMBINE_TM, D_MODEL), lambda i: (i, 0)),
            pl.BlockSpec((None, MOD_ROWS, D_MODEL), lambda i: (_group_of_tile(i, COMBINE_TM), 0, 0)),
            pl.BlockSpec((1, D_MODEL), lambda i: (0, 0)),
        ],
        out_specs=out_specs,
        out_shape=out_shape,
        compiler_params=_cparams(1),
        name="combine",
    )(*([y_km] * TOP_K), gates, x, mod, g_final.reshape(1, D_MODEL))


def _routing_tables(idx, rank, counts):
    counts = counts[0, :N_EXPERTS]
    padded = ((counts + MOE_TM - 1) // MOE_TM) * MOE_TM
    ends = jnp.cumsum(padded)
    offsets = ends - padded
    idx = idx[:, :TOP_K]
    pos = rank[:, :TOP_K] + jnp.sum(
        jnp.where(idx[..., None] == jnp.arange(N_EXPERTS, dtype=I32), offsets, 0), axis=-1)
    pos_km = pos.T.astype(I32)
    tile_start = jnp.arange(MOE_TILES, dtype=I32) * MOE_TM
    tile_valid = (tile_start < ends[-1]).astype(I32)
    tile_expert = jnp.sum((tile_start[:, None] >= ends[None, :]).astype(I32), axis=-1)
    last_expert = jnp.sum((ends[-1] - 1 >= ends).astype(I32))
    tile_expert = jnp.where(tile_valid == 1, tile_expert, last_expert).astype(I32)
    e_ids = jnp.arange(N_EXPERTS, dtype=I32)
    owns = counts > 0
    ordinal = jnp.cumsum(owns.astype(I32)) - 1
    later = jnp.where(owns[None, :] & (e_ids[None, :] > e_ids[:, None]), e_ids[None, :], N_EXPERTS)
    nxt = jnp.min(later, axis=1)
    nxt = jnp.where(nxt == N_EXPERTS, e_ids, nxt)
    onehot = (tile_expert[:, None] == e_ids[None, :]).astype(I32)
    tile_slot = jnp.sum(onehot * (ordinal % 2)[None, :], axis=1).astype(I32)
    tile_next = jnp.sum(onehot * nxt[None, :], axis=1).astype(I32)
    return pos_km, tile_expert, tile_valid, tile_slot, tile_next


def _rope_tables():
    t = np.arange(DEC_SEQ)
    row = (t // GRID_W).astype(np.float32)
    col = (t % GRID_W).astype(np.float32)
    d_axis = RET_DK // 2
    inv = jnp.asarray(ROPE_BASE, F32) ** (-jnp.arange(0, d_axis, 2, dtype=F32) / d_axis)
    ang = jnp.concatenate([row[:, None] * inv, col[:, None] * inv], axis=-1)
    cos = jnp.repeat(jnp.cos(ang), 2, axis=-1)
    sin = jnp.repeat(jnp.sin(ang), 2, axis=-1)
    sign = jnp.asarray(np.tile(np.array([-1.0, 1.0], np.float32), RET_DK // 2))
    return cos, sin * sign


def _deinterleave_matrix():
    p = np.zeros((PERM_W, PERM_W), np.float32)
    half = PERM_W // 2
    for j in range(half):
        p[2 * j, j] = 1.0
        p[2 * j + 1, half + j] = 1.0
    return jnp.asarray(p, BF16)


def kernel(x_prompt, x_sample, state_ret_fwd, state_ret_bwd, cache_na_k, cache_na_v, c, c_ctx, ada_w, ada_b, norm_mix, norm_ffn, norm_final, ret_w_in, ret_w_out, ret_decay_fwd, ret_decay_bwd, na_w_in, na_w_out, na_rpb, moe_w_router, moe_b_router, moe_w1, moe_b1, moe_w2, moe_b2):
    x = (x_prompt.reshape(N_PROMPT, D_MODEL), x_sample.reshape(N_SAMPLE, D_MODEL))

    cond = jnp.zeros((MOD_ROWS, D_MODEL), F32).at[0].set(c_ctx).at[1:1 + DEC_BATCH].set(c)
    mod_all = _adaln(cond, ada_w, ada_b)
    mod_all = mod_all[:, :N_GROUPS].reshape(DEPTH, N_GROUPS, 6, D_MODEL)
    mod_all = jnp.pad(mod_all, ((0, 0), (0, 0), (0, MOD_ROWS - 6), (0, 0)))

    cos, sin_signed = _rope_tables()
    perm = _deinterleave_matrix()
    wr_all = jnp.pad(moe_w_router, ((0, 0), (0, 0), (0, LANES - N_EXPERTS)))
    br_all = jnp.pad(moe_b_router, ((0, 0), (0, LANES - N_EXPERTS))).reshape(DEPTH, 1, LANES)
    b1_split = _deinterleave_bias(moe_b1.reshape(DEPTH * N_EXPERTS, 2 * D_EXPERT), perm)
    b1_split = b1_split.reshape(DEPTH, N_EXPERTS, 1, 2 * D_EXPERT)
    b2_all = moe_b2.reshape(DEPTH, N_EXPERTS, 1, D_MODEL)
    na_tbl = _na_bias_table(na_rpb)
    ret_states = None
    na_caches = None
    for layer in range(DEPTH):
        mod = mod_all[layer]
        j = layer // 2
        if layer % 2 == 0:
            proj = _inproj(x, norm_mix[layer], mod, _cast_layer_bf16(ret_w_in, j), out_dtype=BF16)
            decay = jnp.stack([ret_decay_fwd[j], ret_decay_bwd[j]])
            a_p, s_f, s_b = _ret_prompt(proj, decay, j, ret_states)
            ret_states = (s_f, s_b)
            a_s = _ret_sample(proj, decay, cos, sin_signed, state_ret_fwd, state_ret_bwd, j)
            w_out = ret_w_out
        else:
            proj, ck, cv = _inproj(x, norm_mix[layer], mod, na_w_in, j, j, na_caches)
            na_caches = (ck, cv)
            a_p = _ctx_attn(proj)
            a_s = _na_attn(proj, cache_na_k, cache_na_v, na_tbl, j)
            w_out = na_w_out

        x, tok, logits = _outproj(a_p, a_s, w_out, j, x, norm_ffn[layer], mod, wr_all[layer], br_all[layer])

        idx, gates, rank, counts = _router(logits)
        pos_km, tile_expert, tile_valid, tile_slot, tile_next = _routing_tables(idx, rank, counts)
        xs = _dispatch(tok, pos_km)
        ys = _moe_experts(tile_expert, tile_valid, tile_slot, tile_next, xs, moe_w1, b1_split, moe_w2, b2_all,
                          perm, layer)
        y_km = _gather_rows(ys, pos_km.reshape(-1))
        x = _combine(y_km, gates, x, mod, norm_final, final=(layer == DEPTH - 1))

    y_prompt = x[0].reshape(BATCH, SEQ, D_MODEL)
    y_sample = x[1].reshape(DEC_BATCH, DEC_SEQ, D_MODEL)
    return (y_prompt, y_sample, ret_states[0], ret_states[1],
            jnp.transpose(na_caches[0], (0, 1, 4, 2, 3)), jnp.transpose(na_caches[1], (0, 1, 4, 2, 3)))
```
